```python
import math
import jax, jax.numpy as jnp
from jax import lax
import numpy as np

D_MODEL = 1024
BATCH = 8
SEQ = 2048
DEPTH = 2
DEC_BATCH = 128
DEC_SEQ = 1
PAST_LEN = 16384
PAGE_SIZE = 128

D_MIX = D_MODEL
N_MIXERS = 4
W_GROUP = D_MIX // N_MIXERS
N_HEADS = 4
HEAD_DIM = W_GROUP // N_HEADS
GDN_CONV = 4
SC_CONV = 3
GLA_RANK = 16
GLA_NORMALIZER = 16.0
CHUNK = 64
D_FF = 2816
N_EXPERTS = 8
TOP_K = 2
D_FF_EXPERT = 1408
N_DENSE = (DEPTH + 1) // 2
N_MOE = DEPTH // 2
EPS = 1e-6
N_STATE = 7

SPLIT_SIZES = (
    [W_GROUP] * 4 + [N_HEADS] * 2 +
    [W_GROUP] * 4 + [N_HEADS] * 2 +
    [W_GROUP] * 3 +
    [W_GROUP] * 4 + [GLA_RANK]
)
D_IN = sum(SPLIT_SIZES)

kernel_name = 'hybrid_mlstm_gdn_conv_gla_step'


def _split(z, sizes):
    idx, acc = [], 0
    for s in sizes[:-1]:
        acc += s
        idx.append(acc)
    return jnp.split(z, idx, axis=-1)


def _rmsnorm(x, g):
    xf = x.astype(jnp.float32)
    xf = xf * lax.rsqrt(jnp.mean(xf * xf, axis=-1, keepdims=True) + EPS)
    return (xf * g.astype(jnp.float32)).astype(x.dtype)


def _headnorm(o, w):
    o = o * lax.rsqrt(jnp.mean(o * o, axis=-1, keepdims=True) + EPS)
    b, h, l, d = o.shape
    return o.transpose(0, 2, 1, 3).reshape(b, l, h * d) * w.astype(jnp.float32)


def _l2norm(x):
    return x * lax.rsqrt(jnp.sum(x * x, axis=-1, keepdims=True) + EPS)


def _causal_conv(u, buf, w):
    width = w.shape[-1]
    L = u.shape[1]
    w = w.astype(u.dtype)
    xp = jnp.concatenate([buf, u], axis=1)
    y = xp[:, 0:L] * w[:, 0]
    for j in range(1, width):
        y = y + xp[:, j:j + L] * w[:, j]
    return y, xp[:, L:]


def _to_chunks(a, c):
    b, h, L = a.shape[:3]
    a = a.reshape(b, h, L // c, c, *a.shape[3:])
    return jnp.moveaxis(a, 2, 0)


def _from_chunks(a):
    a = jnp.moveaxis(a, 0, 2)
    return a.reshape(a.shape[0], a.shape[1], -1, a.shape[-1])


def _mlstm(q, k, v, ig, fg, C0, n0, m0):
    L = q.shape[2]
    c = math.gcd(L, CHUNK)
    k = k * HEAD_DIM ** -0.5
    logf = jax.nn.log_sigmoid(fg)
    incl = jnp.tril(jnp.ones((c, c), dtype=bool))

    def step(carry, inp):
        C, n, m = carry
        qc, kc, vc, ic, lf = inp
        b = jnp.cumsum(lf, axis=-1)
        a = b + m[..., None]
        D = jnp.where(incl, b[..., :, None] - b[..., None, :] + ic[..., None, :], -jnp.inf)
        mt = jnp.maximum(a, jnp.max(D, axis=-1))
        w_int = jnp.exp(a - mt)
        s = jnp.einsum('bhtd,bhsd->bhts', qc, kc) * jnp.exp(D - mt[..., None])
        num = w_int[..., None] * jnp.einsum('bhtd,bhde->bhte', qc, C) + jnp.einsum('bhts,bhse->bhte', s, vc)
        den = w_int * jnp.einsum('bhtd,bhd->bht', qc, n) + jnp.sum(s, axis=-1)
        h = num / jnp.maximum(jnp.abs(den), jnp.exp(-mt))[..., None]
        a_end = b[..., -1] + m
        g = b[..., -1:] - b + ic
        m_new = jnp.maximum(a_end, jnp.max(g, axis=-1))
        wg = jnp.exp(g - m_new[..., None])
        dec = jnp.exp(a_end - m_new)
        C_new = dec[..., None, None] * C + jnp.einsum('bhs,bhsd,bhse->bhde', wg, kc, vc)
        n_new = dec[..., None] * n + jnp.einsum('bhs,bhsd->bhd', wg, kc)
        return (C_new, n_new, m_new), h

    (C, n, m), hs = lax.scan(step, (C0, n0, m0),
                             (_to_chunks(q, c), _to_chunks(k, c), _to_chunks(v, c),
                              _to_chunks(ig, c), _to_chunks(logf, c)))
    return _from_chunks(hs), C, n, m


def _gated_delta(q, k, v, g, beta, S0):
    L = q.shape[2]
    c = math.gcd(L, CHUNK)
    q = q * HEAD_DIM ** -0.5
    dv = v.shape[-1]
    incl = jnp.tril(jnp.ones((c, c), dtype=bool))
    strict = jnp.tril(jnp.ones((c, c), dtype=bool), k=-1)

    def step(S, inp):
        qc, kc, vc, gc, bc = inp
        G = jnp.cumsum(gc, axis=-1)
        Lw = jnp.exp(jnp.where(incl, G[..., :, None] - G[..., None, :], -jnp.inf))
        kk = jnp.einsum('bhtd,bhsd->bhts', kc, kc)
        A = jnp.where(strict, bc[..., :, None] * kk * Lw, 0.0)
        rhs = jnp.concatenate([bc[..., None] * vc, (bc * jnp.exp(G))[..., None] * kc], axis=-1)
        sol = lax.linalg.triangular_solve(A, rhs, left_side=True, lower=True, unit_diagonal=True)
        U, W = sol[..., :dv], sol[..., dv:]
        Vn = U - jnp.einsum('bhtd,bhde->bhte', W, S)
        qk = jnp.einsum('bhtd,bhsd->bhts', qc, kc) * Lw
        o = jnp.einsum('bhtd,bhde->bhte', qc * jnp.exp(G)[..., None], S) + jnp.einsum('bhts,bhse->bhte', qk, Vn)
        G_last = G[..., -1]
        S_new = jnp.exp(G_last)[..., None, None] * S + jnp.einsum(
            'bhsd,bhse->bhde', kc * jnp.exp(G_last[..., None] - G)[..., None], Vn)
        return S_new, o

    S, os_ = lax.scan(step, S0, (_to_chunks(q, c), _to_chunks(k, c), _to_chunks(v, c),
                                 _to_chunks(g, c), _to_chunks(beta, c)))
    return _from_chunks(os_), S


def _gla(q, k, v, gk, S0):
    L = q.shape[2]
    c = math.gcd(L, CHUNK)
    q = q * HEAD_DIM ** -0.5
    incl = jnp.tril(jnp.ones((c, c), dtype=bool))[..., None]

    def step(S, inp):
        qc, kc, vc, gc = inp
        G = jnp.cumsum(gc, axis=2)
        dec = jnp.exp(jnp.where(incl, G[:, :, :, None, :] - G[:, :, None, :, :], -jnp.inf))
        A = jnp.einsum('bhtd,bhsd,bhtsd->bhts', qc, kc, dec)
        o = jnp.einsum('bhtd,bhde->bhte', qc * jnp.exp(G), S) + jnp.einsum('bhts,bhse->bhte', A, vc)
        G_last = G[:, :, -1]
        S_new = jnp.exp(G_last)[..., None] * S + jnp.einsum(
            'bhsd,bhse->bhde', kc * jnp.exp(G_last[:, :, None] - G), vc)
        return S_new, o

    S, os_ = lax.scan(step, S0, (_to_chunks(q, c), _to_chunks(k, c), _to_chunks(v, c), _to_chunks(gk, c)))
    return _from_chunks(os_), S


def _mix(h, st, w_in, mlstm_b_i, mlstm_b_f, mlstm_norm, gdn_conv_w, gdn_a_log, gdn_dt_bias, gdn_norm,
         sc_conv_w, gla_w_gate, gla_b_gate, gla_norm, w_out):
    f32 = jnp.float32
    dt = h.dtype
    bn, L = h.shape[:2]
    C0, n0, m0, Sg0, gbuf0, sbuf0, Sl0 = [s.astype(f32) for s in st]
    z = jnp.einsum('bld,de->ble', h, w_in).astype(f32)
    (mq, mk, mv, mo, mi, mf, gq, gk, gv, gz, ga, gb, sb, sc, sh, lq, lk, lv, lr, lg) = _split(z, SPLIT_SIZES)

    def heads(a):
        return a.reshape(bn, L, N_HEADS, -1).transpose(0, 2, 1, 3)

    def per_head(a):
        return a.transpose(0, 2, 1)

    hm, C1, n1, m1 = _mlstm(heads(mq), heads(mk), heads(mv), per_head(mi + mlstm_b_i.astype(f32)),
                            per_head(mf + mlstm_b_f.astype(f32)), C0, n0, m0)
    y_a = jax.nn.sigmoid(mo) * _headnorm(hm, mlstm_norm)

    qkv, gbuf1 = _causal_conv(jnp.concatenate([gq, gk, gv], axis=-1), gbuf0, gdn_conv_w)
    cq, ck, cv = jnp.split(jax.nn.silu(qkv), 3, axis=-1)
    decay = -jnp.exp(gdn_a_log.astype(f32)) * jax.nn.softplus(ga + gdn_dt_bias.astype(f32))
    ho, Sg1 = _gated_delta(_l2norm(heads(cq)), _l2norm(heads(ck)), heads(cv), per_head(decay),
                           per_head(jax.nn.sigmoid(gb)), Sg0)
    y_b = _headnorm(ho, gdn_norm) * jax.nn.silu(gz)

    cu, sbuf1 = _causal_conv(sc * sh, sbuf0, sc_conv_w)
    y_c = sb * cu

    gate_log = jax.nn.log_sigmoid(lg @ gla_w_gate.astype(f32) + gla_b_gate.astype(f32)) / GLA_NORMALIZER
    hl, Sl1 = _gla(heads(lq), heads(lk), heads(lv), heads(gate_log), Sl0)
    y_d = _headnorm(hl, gla_norm) * jax.nn.silu(lr)

    y = jnp.concatenate([y_a, y_b, y_c, y_d], axis=-1).astype(dt)
    y = jnp.einsum('ble,ed->bld', y, w_out)
    new = (C1, n1, m1, Sg1, gbuf1, sbuf1, Sl1)
    return y, [s.astype(dt) for s in new]


def _swiglu(h, wg, wu, wd):
    return (jax.nn.silu(h @ wg) * (h @ wu)) @ wd


def _moe(h, w_router, b_router, w_g, w_u, w_d):
    logits = (h @ w_router).astype(jnp.float32) + b_router.astype(jnp.float32)
    probs = jax.nn.softmax(logits, axis=-1)
    topv, topi = lax.top_k(probs, TOP_K)
    topv = topv / jnp.sum(topv, axis=-1, keepdims=True)
    gates = jnp.sum(jax.nn.one_hot(topi, N_EXPERTS, dtype=jnp.float32) * topv[..., None], axis=-2)
    y = jnp.zeros(h.shape, jnp.float32)
    for e in range(N_EXPERTS):
        y = y + gates[..., e:e + 1] * _swiglu(h, w_g[e], w_u[e], w_d[e]).astype(jnp.float32)
    return y.astype(h.dtype)


def _trunk(x, states, params):
    (w_in, g_mix, mlstm_b_i, mlstm_b_f, mlstm_norm, gdn_conv_w, gdn_a_log, gdn_dt_bias, gdn_norm,
     sc_conv_w, gla_w_gate, gla_b_gate, gla_norm, w_out, g_ffn, ffn_w_gate, ffn_w_up, ffn_w_down,
     moe_w_router, moe_b_router, moe_w_gate, moe_w_up, moe_w_down, g_final) = params
    per_layer = []
    for l in range(DEPTH):
        h = _rmsnorm(x, g_mix[l])
        y, st = _mix(h, [s[l] for s in states], w_in[l], mlstm_b_i[l], mlstm_b_f[l], mlstm_norm[l],
                     gdn_conv_w[l], gdn_a_log[l], gdn_dt_bias[l], gdn_norm[l], sc_conv_w[l],
                     gla_w_gate[l], gla_b_gate[l], gla_norm[l], w_out[l])
        x = x + y
        h = _rmsnorm(x, g_ffn[l])
        j = l // 2
        if l % 2 == 0:
            x = x + _swiglu(h, ffn_w_gate[j], ffn_w_up[j], ffn_w_down[j])
        else:
            x = x + _moe(h, moe_w_router[j], moe_b_router[j], moe_w_gate[j], moe_w_up[j], moe_w_down[j])
        per_layer.append(st)
    new_states = [jnp.stack([st[i] for st in per_layer]) for i in range(N_STATE)]
    return _rmsnorm(x, g_final), new_states


def _zero_states(bn, dtype):
    return [jnp.zeros((DEPTH, bn, N_HEADS, HEAD_DIM, HEAD_DIM), dtype),
            jnp.zeros((DEPTH, bn, N_HEADS, HEAD_DIM), dtype),
            jnp.zeros((DEPTH, bn, N_HEADS), dtype),
            jnp.zeros((DEPTH, bn, N_HEADS, HEAD_DIM, HEAD_DIM), dtype),
            jnp.zeros((DEPTH, bn, GDN_CONV - 1, 3 * W_GROUP), dtype),
            jnp.zeros((DEPTH, bn, SC_CONV - 1, W_GROUP), dtype),
            jnp.zeros((DEPTH, bn, N_HEADS, HEAD_DIM, HEAD_DIM), dtype)]


def setup_inputs(seed: int = 0) -> dict:
    key = jax.random.key(seed)
    keys = jax.random.split(key, 64)
    counter = [0]
    f32 = jnp.float32

    def nk():
        counter[0] += 1
        return keys[counter[0] - 1]

    def nrm(shape, scale):
        return scale * jax.random.normal(nk(), shape, f32)

    def gain(shape):
        return 1.0 + nrm(shape, 0.02)

    inp = {}
    inp['x_prompt'] = nrm((BATCH, SEQ, D_MODEL), 1.0)
    inp['x_sample'] = nrm((DEC_BATCH, DEC_SEQ, D_MODEL), 1.0)
    inp['state_mlstm_C'] = nrm((DEPTH, DEC_BATCH, N_HEADS, HEAD_DIM, HEAD_DIM), 1.0)
    inp['state_mlstm_n'] = nrm((DEPTH, DEC_BATCH, N_HEADS, HEAD_DIM), 1.0)
    inp['state_mlstm_m'] = nrm((DEPTH, DEC_BATCH, N_HEADS), 1.0)
    inp['state_gdn_S'] = nrm((DEPTH, DEC_BATCH, N_HEADS, HEAD_DIM, HEAD_DIM), 0.1)
    inp['state_gdn_conv'] = nrm((DEPTH, DEC_BATCH, GDN_CONV - 1, 3 * W_GROUP), 1.0)
    inp['state_sc_conv'] = nrm((DEPTH, DEC_BATCH, SC_CONV - 1, W_GROUP), 1.0)
    inp['state_gla_S'] = nrm((DEPTH, DEC_BATCH, N_HEADS, HEAD_DIM, HEAD_DIM), 0.5)
    inp['w_in'] = nrm((DEPTH, D_MODEL, D_IN), D_MODEL ** -0.5)
    inp['g_mix'] = gain((DEPTH, D_MODEL))
    inp['mlstm_b_i'] = nrm((DEPTH, N_HEADS), 0.1)
    inp['mlstm_b_f'] = jnp.linspace(3.0, 6.0, N_HEADS, dtype=f32)[None, :] + nrm((DEPTH, N_HEADS), 0.1)
    inp['mlstm_norm'] = gain((DEPTH, W_GROUP))
    inp['gdn_conv_w'] = nrm((DEPTH, 3 * W_GROUP, GDN_CONV), GDN_CONV ** -0.5)
    inp['gdn_a_log'] = jnp.log(jax.random.uniform(nk(), (DEPTH, N_HEADS), f32, 1.0, 16.0))
    dt0 = jnp.exp(jax.random.uniform(nk(), (DEPTH, N_HEADS), f32, math.log(1e-3), math.log(1e-1)))
    inp['gdn_dt_bias'] = dt0 + jnp.log(-jnp.expm1(-dt0))
    inp['gdn_norm'] = gain((DEPTH, W_GROUP))
    inp['sc_conv_w'] = nrm((DEPTH, W_GROUP, SC_CONV), SC_CONV ** -0.5)
    inp['gla_w_gate'] = nrm((DEPTH, GLA_RANK, W_GROUP), GLA_RANK ** -0.5)
    inp['gla_b_gate'] = nrm((DEPTH, W_GROUP), 0.1)
    inp['gla_norm'] = gain((DEPTH, W_GROUP))
    inp['w_out'] = nrm((DEPTH, D_MIX, D_MODEL), D_MIX ** -0.5)
    inp['g_ffn'] = gain((DEPTH, D_MODEL))
    inp['ffn_w_gate'] = nrm((N_DENSE, D_MODEL, D_FF), D_MODEL ** -0.5)
    inp['ffn_w_up'] = nrm((N_DENSE, D_MODEL, D_FF), D_MODEL ** -0.5)
    inp['ffn_w_down'] = nrm((N_DENSE, D_FF, D_MODEL), D_FF ** -0.5)
    inp['moe_w_router'] = nrm((N_MOE, D_MODEL, N_EXPERTS), D_MODEL ** -0.5)
    inp['moe_b_router'] = nrm((N_MOE, N_EXPERTS), 0.01)
    inp['moe_w_gate'] = nrm((N_MOE, N_EXPERTS, D_MODEL, D_FF_EXPERT), D_MODEL ** -0.5)
    inp['moe_w_up'] = nrm((N_MOE, N_EXPERTS, D_MODEL, D_FF_EXPERT), D_MODEL ** -0.5)
    inp['moe_w_down'] = nrm((N_MOE, N_EXPERTS, D_FF_EXPERT, D_MODEL), D_FF_EXPERT ** -0.5)
    inp['g_final'] = gain((D_MODEL,))
    return inp


def reference(x_prompt, x_sample, state_mlstm_C, state_mlstm_n, state_mlstm_m, state_gdn_S, state_gdn_conv,
              state_sc_conv, state_gla_S, w_in, g_mix, mlstm_b_i, mlstm_b_f, mlstm_norm, gdn_conv_w, gdn_a_log,
              gdn_dt_bias, gdn_norm, sc_conv_w, gla_w_gate, gla_b_gate, gla_norm, w_out, g_ffn, ffn_w_gate,
              ffn_w_up, ffn_w_down, moe_w_router, moe_b_router, moe_w_gate, moe_w_up, moe_w_down, g_final):
    params = (w_in, g_mix, mlstm_b_i, mlstm_b_f, mlstm_norm, gdn_conv_w, gdn_a_log, gdn_dt_bias, gdn_norm,
              sc_conv_w, gla_w_gate, gla_b_gate, gla_norm, w_out, g_ffn, ffn_w_gate, ffn_w_up, ffn_w_down,
              moe_w_router, moe_b_router, moe_w_gate, moe_w_up, moe_w_down, g_final)
    y_prompt, sp = _trunk(x_prompt, _zero_states(x_prompt.shape[0], x_prompt.dtype), params)
    ss_in = [state_mlstm_C, state_mlstm_n, state_mlstm_m, state_gdn_S, state_gdn_conv, state_sc_conv, state_gla_S]
    y_sample, ss = _trunk(x_sample, ss_in, params)
    p_mC, p_mn, p_mm, p_gS, p_gconv, p_sconv, p_lS = sp
    s_mC, s_mn, s_mm, s_gS, s_gconv, s_sconv, s_lS = ss
    return (y_prompt, y_sample, p_mC, s_mC, p_mn, s_mn, p_mm, s_mm, p_gS, s_gS,
            p_gconv, s_gconv, p_sconv, s_sconv, p_lS, s_lS)
```

```python
import functools

import numpy as np
import jax
import jax.numpy as jnp
from jax import lax
from jax.experimental import pallas as pl
from jax.experimental.pallas import tpu as pltpu

F32 = jnp.float32
BF16 = jnp.bfloat16

D_MODEL = 1024
W_GROUP = 256
N_HEADS = 4
HEAD_DIM = 64
CHUNK = 64
GLA_RANK = 16
GLA_NORMALIZER = 16.0
N_EXPERTS = 8
EPS = 1e-6
Q_SCALE = HEAD_DIM ** -0.5

VMEM_LIMIT = 56 * 1024 * 1024
LANES = 128

SPLIT_SIZES = ([W_GROUP] * 4 + [N_HEADS] * 2 + [W_GROUP] * 4 + [N_HEADS] * 2 +
               [W_GROUP] * 3 + [W_GROUP] * 4 + [GLA_RANK])
_OFF = np.concatenate([[0], np.cumsum(SPLIT_SIZES)])
_BIG = [0, 1, 2, 3, 6, 7, 8, 9, 12, 13, 14, 15, 16, 17, 18]
_SMALL = [4, 5, 10, 11, 19]
MQ, MK, MV, MO = 0, 256, 512, 768
GQ, GK, GV, GZ = 1024, 1280, 1536, 1792
SB, SC, SH = 2048, 2304, 2560
LQ, LK, LV, LR = 2816, 3072, 3328, 3584
SM = 3840
NP = SM + LANES
N_SMALL = 32


def _cparams(sem):
    return pltpu.CompilerParams(dimension_semantics=sem, vmem_limit_bytes=VMEM_LIMIT)


def _softplus(x):
    return jnp.maximum(x, 0.0) + jnp.log1p(jnp.exp(-jnp.abs(x)))


def _logsigmoid(x):
    return -_softplus(-x)


def _sigmoid(x):
    return 1.0 / (1.0 + jnp.exp(-x))


def _silu(x):
    return x * _sigmoid(x)


def _rms(x, g):
    return x * lax.rsqrt(jnp.mean(x * x, axis=-1, keepdims=True) + EPS) * g


def _mm(a, b):
    return jnp.dot(a.astype(BF16), b.astype(BF16), preferred_element_type=F32)


def _mm_nt(a, b):
    return lax.dot_general(a.astype(BF16), b.astype(BF16), (((1,), (1,)), ((), ())),
                           preferred_element_type=F32)


def _mm_tn(a, b):
    return lax.dot_general(a.astype(BF16), b.astype(BF16), (((0,), (0,)), ((), ())),
                           preferred_element_type=F32)


def _mm_hi(a, b):
    return jnp.dot(a, b, precision=lax.Precision.HIGHEST, preferred_element_type=F32)


def _split3(x):
    x1 = x.astype(BF16)
    r1 = x - x1.astype(F32)
    x2 = r1.astype(BF16)
    x3 = (r1 - x2.astype(F32)).astype(BF16)
    return x1, x2, x3


def _sel(m, x):
    x1, x2, x3 = _split3(x)
    d = lambda p: jnp.dot(m, p, preferred_element_type=F32)
    return d(x1) + d(x2) + d(x3)


def _sel_r(x, m):
    x1, x2, x3 = _split3(x)
    d = lambda p: jnp.dot(p, m, preferred_element_type=F32)
    return d(x1) + d(x2) + d(x3)


def _sel_nt(m, x):
    x1, x2, x3 = _split3(x)
    d = lambda p: lax.dot_general(m, p, (((1,), (1,)), ((), ())), preferred_element_type=F32)
    return d(x1) + d(x2) + d(x3)


def _iota2(shape, dim):
    return lax.broadcasted_iota(jnp.int32, shape, dim)


def _headnorm(o, w):
    return o * lax.rsqrt(jnp.mean(o * o, axis=-1, keepdims=True) + EPS) * w


def _inproj_kernel(x_ref, g_ref, w_ref, wst_ref, z_ref, zt_ref):
    hb = _rms(x_ref[...], g_ref[...]).astype(BF16)
    z_ref[...] = jnp.dot(hb, w_ref[...], preferred_element_type=F32)
    zt = lax.dot_general(wst_ref[...], hb, (((1,), (1,)), ((), ())), preferred_element_type=F32)
    for c in range(zt_ref.shape[0]):
        zt_ref[c] = zt[:, c * CHUNK:(c + 1) * CHUNK]


def _inproj(x, g, w, wst, tm):
    t = x.shape[0]
    return pl.pallas_call(
        _inproj_kernel,
        grid=(t // tm,),
        in_specs=[pl.BlockSpec((tm, D_MODEL), lambda i: (i, 0)),
                  pl.BlockSpec((1, D_MODEL), lambda i: (0, 0)),
                  pl.BlockSpec((D_MODEL, NP), lambda i: (0, 0)),
                  pl.BlockSpec((N_SMALL, D_MODEL), lambda i: (0, 0))],
        out_specs=[pl.BlockSpec((tm, NP), lambda i: (i, 0)),
                   pl.BlockSpec((tm // CHUNK, N_SMALL, CHUNK), lambda i: (i, 0, 0))],
        out_shape=[jax.ShapeDtypeStruct((t, NP), F32),
                   jax.ShapeDtypeStruct((t // CHUNK, N_SMALL, CHUNK), F32)],
        compiler_params=_cparams(("parallel",)),
        name="inproj",
    )(x, g, w, wst)


def _inproj_s_kernel(x_ref, g_ref, w_ref, z_ref):
    hb = _rms(x_ref[...], g_ref[...]).astype(BF16)
    z_ref[...] = jnp.dot(hb, w_ref[...], preferred_element_type=F32)


def _inproj_s(x, g, w):
    t = x.shape[0]
    return pl.pallas_call(
        _inproj_s_kernel,
        out_shape=jax.ShapeDtypeStruct((t, NP), F32),
        compiler_params=pltpu.CompilerParams(vmem_limit_bytes=VMEM_LIMIT),
        name="inproj_s",
    )(x, g, w)


def _gate_act(pre, idx, neg_a):
    lf = _logsigmoid(pre)
    dec = neg_a * _softplus(pre)
    beta = _sigmoid(pre)
    return jnp.where(idx < 4, pre,
                     jnp.where(idx < 8, lf,
                               jnp.where(idx < 12, dec,
                                         jnp.where(idx < 16, beta, 0.0))))


def _mix_kernel(z_ref, zt_ref, bcol_ref, acol_ref, brow_ref, arow_ref, mnorm_ref, gnorm_ref,
                lnorm_ref, gcw_ref, scw_ref, wg_ref, bg_ref,
                y_ref, c_out, n_out, m_out, sg_out, gconv_out, sconv_out, sl_out,
                caug_ref, mst_ref, sg_ref, slt_ref, xg_ref, xs_ref, qkv_ref, glog_ref, act_ref,
                *, lb_rows):
    lb = pl.program_id(1)
    nlb = pl.num_programs(1)
    n_chunks = lb_rows // CHUNK

    @pl.when(lb == 0)
    def _():
        caug_ref[...] = jnp.zeros_like(caug_ref)
        mst_ref[...] = jnp.zeros_like(mst_ref)
        sg_ref[...] = jnp.zeros_like(sg_ref)
        slt_ref[...] = jnp.zeros_like(slt_ref)
        xg_ref[0:8, :] = jnp.zeros((8, 3 * W_GROUP), F32)
        xs_ref[0:8, :] = jnp.zeros((8, W_GROUP), F32)

    small = z_ref[:, SM:SM + LANES]
    lane = _iota2((1, LANES), 1)
    neg_a = -jnp.exp(acol_ref[...])
    act_ref[...] = _gate_act(small + bcol_ref[...], lane, neg_a)
    glin = jnp.dot(small.astype(BF16), wg_ref[...], preferred_element_type=F32) + bg_ref[...]
    glog_ref[...] = _logsigmoid(glin) * (1.0 / GLA_NORMALIZER)

    xg_ref[8:, :] = z_ref[:, GQ:GQ + 3 * W_GROUP]
    conv = xg_ref[5:5 + lb_rows, :] * gcw_ref[0:1, :]
    for j in range(1, 4):
        conv = conv + xg_ref[5 + j:5 + j + lb_rows, :] * gcw_ref[j:j + 1, :]
    conv = _silu(conv)
    hsum = (_iota2((W_GROUP, W_GROUP), 0) // HEAD_DIM ==
            _iota2((W_GROUP, W_GROUP), 1) // HEAD_DIM).astype(BF16)

    def l2n(x):
        sq = x * x
        hi = sq.astype(BF16)
        lo = (sq - hi.astype(F32)).astype(BF16)
        ss = (jnp.dot(hi, hsum, preferred_element_type=F32) +
              jnp.dot(lo, hsum, preferred_element_type=F32))
        return x * lax.rsqrt(ss + EPS)

    qkv_ref[:, 0:W_GROUP] = l2n(conv[:, 0:W_GROUP]) * Q_SCALE
    qkv_ref[:, W_GROUP:2 * W_GROUP] = l2n(conv[:, W_GROUP:2 * W_GROUP])
    qkv_ref[:, 2 * W_GROUP:] = conv[:, 2 * W_GROUP:]

    xs_ref[8:, :] = z_ref[:, SC:SC + W_GROUP] * z_ref[:, SH:SH + W_GROUP]
    cu = xs_ref[6:6 + lb_rows, :] * scw_ref[0:1, :]
    for j in range(1, 3):
        cu = cu + xs_ref[6 + j:6 + j + lb_rows, :] * scw_ref[j:j + 1, :]
    y_ref[:, 2 * W_GROUP:3 * W_GROUP] = (z_ref[:, SB:SB + W_GROUP] * cu).astype(y_ref.dtype)

    ti = _iota2((CHUNK, CHUNK), 0)
    si = _iota2((CHUNK, CHUNK), 1)
    incl = ti >= si
    strict = ti > si
    eye = ti == si
    tril = incl.astype(BF16)
    triu = (ti <= si).astype(BF16)
    eye_f = eye.astype(F32)
    blk16 = (ti // 16) == (si // 16)
    blk32 = (ti // 32) == (si // 32)
    levels = (32, 16, 8, 4, 2)
    lvl_mask = {n: ((ti // (2 * n)) == (si // (2 * n))) & ((ti // n) > (si // n))
                for n in levels + (1,)}
    rows = _iota2((2 * len(levels) * CHUNK, CHUNK), 0)
    cols = _iota2((2 * len(levels) * CHUNK, CHUNK), 1)
    r_in = rows % CHUNK
    sel_stack = jnp.zeros((2 * len(levels) * CHUNK, CHUNK), F32)
    for i, n in enumerate(levels):
        blk = rows // CHUNK
        tgt_e = (r_in // n) * n + (n - 1)
        tgt_p = (r_in // n) * n - 1
        sel_stack = jnp.where((blk == 2 * i) & (cols == tgt_e), 1.0, sel_stack)
        sel_stack = jnp.where((blk == 2 * i + 1) & (cols == tgt_p), 1.0, sel_stack)
    sel_stack = sel_stack.astype(BF16)
    sub = _iota2((N_SMALL, 1), 0)
    neg_a_row = -jnp.exp(arow_ref[...])
    ones_col = (_iota2((CHUNK, LANES - HEAD_DIM), 1) == 0).astype(F32)

    def chunk(c, carry):
        r0 = pl.multiple_of(c * CHUNK, CHUNK)
        rs = pl.ds(r0, CHUNK)
        act = act_ref[rs, :]
        act_t = _gate_act(zt_ref[c] + brow_ref[...], sub, neg_a_row)
        cs_col = _sel(tril, act)
        cs_row = _sel_r(act_t, triu)

        zq = z_ref[rs, MQ:MQ + W_GROUP]
        zk = z_ref[rs, MK:MK + W_GROUP] * Q_SCALE
        zv = z_ref[rs, MV:MV + W_GROUP]
        zo = z_ref[rs, MO:MO + W_GROUP]
        ya = []
        for h in range(N_HEADS):
            hs = slice(h * HEAD_DIM, (h + 1) * HEAD_DIM)
            q, k, v = zq[:, hs], zk[:, hs], zv[:, hs]
            vaug = jnp.concatenate([v, ones_col], axis=1)
            bq = cs_col[:, 4 + h:5 + h]
            br = cs_row[4 + h:5 + h, :]
            ir = act_t[h:h + 1, :]
            icol = act[:, h:h + 1]
            m_old = mst_ref[h][0:1, 0:1]
            caug = caug_ref[h]
            a = bq + m_old
            dmat = jnp.where(incl, bq - br + ir, -jnp.inf)
            mt = jnp.maximum(a, jnp.max(dmat, axis=-1, keepdims=True))
            w_int = jnp.exp(a - mt)
            s = _mm_nt(q, k) * jnp.exp(dmat - mt)
            numden = w_int * _mm(q, caug) + _mm(s, vaug)
            den = numden[:, HEAD_DIM:HEAD_DIM + 1]
            hh = numden[:, :HEAD_DIM] / jnp.maximum(jnp.abs(den), jnp.exp(-mt))
            b_last = bq[CHUNK - 1:CHUNK, :]
            a_end = b_last + m_old
            g = b_last - bq + icol
            m_new = jnp.maximum(a_end, jnp.max(g, axis=0, keepdims=True))
            wg = jnp.exp(g - m_new)
            dec = jnp.exp(a_end - m_new)
            caug_ref[h] = dec * caug + _mm_tn(wg * k, vaug)
            mst_ref[h] = jnp.broadcast_to(m_new, (8, LANES))
            ya.append(_headnorm(hh, mnorm_ref[:, hs]))
        y_a = _sigmoid(zo) * jnp.concatenate(ya, axis=1)
        y_ref[rs, 0:W_GROUP] = y_a.astype(y_ref.dtype)

        gq_all = qkv_ref[rs, 0:W_GROUP]
        gk_all = qkv_ref[rs, W_GROUP:2 * W_GROUP]
        gv_all = qkv_ref[rs, 2 * W_GROUP:]
        yb = []
        for h in range(N_HEADS):
            hs = slice(h * HEAD_DIM, (h + 1) * HEAD_DIM)
            q, k, v = gq_all[:, hs], gk_all[:, hs], gv_all[:, hs]
            gcol = cs_col[:, 8 + h:9 + h]
            grow = cs_row[8 + h:9 + h, :]
            beta = act[:, 12 + h:13 + h]
            s_old = sg_ref[h]
            lw = jnp.exp(jnp.where(incl, gcol - grow, -jnp.inf))
            amat = jnp.where(strict, beta * _mm_nt(k, k) * lw, 0.0)
            egc = jnp.exp(gcol)
            rhs = jnp.concatenate([beta * v, (beta * egc) * k], axis=1)
            ad = jnp.where(blk16, amat, 0.0)
            a2 = _mm_hi(ad, ad)
            a4 = _mm_hi(a2, a2)
            a8 = _mm_hi(a4, a4)
            tinv = eye_f - ad
            tinv = tinv + _mm_hi(tinv, a2)
            tinv = tinv + _mm_hi(tinv, a4)
            tinv = tinv + _mm_hi(tinv, a8)
            n1 = jnp.where(blk32 & jnp.logical_not(blk16), amat, 0.0)
            tinv = tinv - _mm_hi(_mm_hi(tinv, n1), tinv)
            n2 = jnp.where(blk32, 0.0, amat)
            tinv = tinv - _mm_hi(_mm_hi(tinv, n2), tinv)
            sol = _mm_hi(tinv, rhs)
            u, w = sol[:, :HEAD_DIM], sol[:, HEAD_DIM:]
            vn = u - _mm(w, s_old)
            qk = _mm_nt(q, k) * lw
            o = _mm(q * egc, s_old) + _mm(qk, vn)
            g_last = gcol[CHUNK - 1:CHUNK, :]
            sg_ref[h] = jnp.exp(g_last) * s_old + _mm_tn(k * jnp.exp(g_last - gcol), vn)
            yb.append(_headnorm(o, gnorm_ref[:, hs]))
        y_b = jnp.concatenate(yb, axis=1) * _silu(z_ref[rs, GZ:GZ + W_GROUP])
        y_ref[rs, W_GROUP:2 * W_GROUP] = y_b.astype(y_ref.dtype)

        lq = z_ref[rs, LQ:LQ + W_GROUP] * Q_SCALE
        lk = z_ref[rs, LK:LK + W_GROUP]
        lv = z_ref[rs, LV:LV + W_GROUP]
        gk = glog_ref[rs, :]
        gcum = _sel(tril, gk)
        stack = _sel(sel_stack, gcum)
        qs = {1: lq * jnp.exp(gk), 0: lq}
        ks = {1: lk, 0: lk}
        for i, n in enumerate(levels):
            g_end = stack[2 * i * CHUNK:(2 * i + 1) * CHUNK, :]
            g_prev = stack[(2 * i + 1) * CHUNK:(2 * i + 2) * CHUNK, :]
            qs[n] = lq * jnp.exp(gcum - g_prev)
            ks[n] = lk * jnp.exp(g_end - gcum)
        g_last = gcum[CHUNK - 1:CHUNK, :]
        q_dec = lq * jnp.exp(gcum)
        k_dec = lk * jnp.exp(g_last - gcum)
        e_last = jnp.exp(g_last)
        yd = []
        for h in range(N_HEADS):
            hs = slice(h * HEAD_DIM, (h + 1) * HEAD_DIM)
            st_old = slt_ref[h]
            amat = jnp.where(eye, _mm_nt(qs[0][:, hs], ks[0][:, hs]), 0.0)
            for n in levels + (1,):
                amat = amat + jnp.where(lvl_mask[n], _mm_nt(qs[n][:, hs], ks[n][:, hs]), 0.0)
            o = _mm_nt(q_dec[:, hs], st_old) + _mm(amat, lv[:, hs])
            slt_ref[h] = st_old * e_last[:, hs] + _mm_tn(lv[:, hs], k_dec[:, hs])
            yd.append(_headnorm(o, lnorm_ref[:, hs]))
        y_d = jnp.concatenate(yd, axis=1) * _silu(z_ref[rs, LR:LR + W_GROUP])
        y_ref[rs, 3 * W_GROUP:] = y_d.astype(y_ref.dtype)
        return carry

    lax.fori_loop(0, n_chunks, chunk, 0)

    gtail = xg_ref[lb_rows:lb_rows + 8, :]
    stail = xs_ref[lb_rows:lb_rows + 8, :]
    xg_ref[0:8, :] = gtail
    xs_ref[0:8, :] = stail

    @pl.when(lb == nlb - 1)
    def _():
        gconv_out[0] = gtail[5:8, :]
        sconv_out[0] = stail[6:8, :]
        pick_n = (_iota2((8, LANES), 1) == HEAD_DIM).astype(BF16)
        eye_b = eye.astype(BF16)
        lane4 = _iota2((1, N_HEADS), 1)
        m_row = jnp.zeros((1, N_HEADS), F32)
        for h in range(N_HEADS):
            caug = caug_ref[h]
            c_out[0, h] = caug[:, :HEAD_DIM]
            n_out[0, h:h + 1, :] = _sel_nt(pick_n, caug)[0:1, :]
            m_row = jnp.where(lane4 == h, mst_ref[h][0:1, 0:1], m_row)
            sg_out[0, h] = sg_ref[h]
            sl_out[0, h] = _sel_nt(eye_b, slt_ref[h])
        m_out[0] = m_row


def _mix_prompt(z, zt, p, bsz, seq, lb_rows):
    t = bsz * seq
    nlb = seq // lb_rows
    row = lambda b, l: (b * nlb + l, 0)
    const2 = lambda b, l: (0, 0)
    st4 = lambda b, l: (b, 0, 0, 0)
    st3 = lambda b, l: (b, 0, 0)
    params = [p["bcol"], p["acol"], p["brow"], p["arow"], p["mnorm"], p["gnorm"], p["lnorm"],
              p["gcw"], p["scw"], p["wg"], p["bg"]]
    in_specs = [pl.BlockSpec((lb_rows, NP), row),
                pl.BlockSpec((lb_rows // CHUNK, N_SMALL, CHUNK), lambda b, l: (b * nlb + l, 0, 0))]
    in_specs += [pl.BlockSpec(a.shape, const2) for a in params]
    hd = (N_HEADS, HEAD_DIM, HEAD_DIM)
    out_shape = [jax.ShapeDtypeStruct((t, D_MODEL), BF16),
                 jax.ShapeDtypeStruct((bsz,) + hd, F32),
                 jax.ShapeDtypeStruct((bsz, N_HEADS, HEAD_DIM), F32),
                 jax.ShapeDtypeStruct((bsz, 1, N_HEADS), F32),
                 jax.ShapeDtypeStruct((bsz,) + hd, F32),
                 jax.ShapeDtypeStruct((bsz, 3, 3 * W_GROUP), F32),
                 jax.ShapeDtypeStruct((bsz, 2, W_GROUP), F32),
                 jax.ShapeDtypeStruct((bsz,) + hd, F32)]
    out_specs = [pl.BlockSpec((lb_rows, D_MODEL), row),
                 pl.BlockSpec((1,) + hd, st4),
                 pl.BlockSpec((1, N_HEADS, HEAD_DIM), st3),
                 pl.BlockSpec((1, 1, N_HEADS), st3),
                 pl.BlockSpec((1,) + hd, st4),
                 pl.BlockSpec((1, 3, 3 * W_GROUP), st3),
                 pl.BlockSpec((1, 2, W_GROUP), st3),
                 pl.BlockSpec((1,) + hd, st4)]
    scratch = [pltpu.VMEM((N_HEADS, HEAD_DIM, LANES), F32),
               pltpu.VMEM((N_HEADS, 8, LANES), F32),
               pltpu.VMEM(hd, F32),
               pltpu.VMEM(hd, F32),
               pltpu.VMEM((lb_rows + 8, 3 * W_GROUP), F32),
               pltpu.VMEM((lb_rows + 8, W_GROUP), F32),
               pltpu.VMEM((lb_rows, 3 * W_GROUP), F32),
               pltpu.VMEM((lb_rows, W_GROUP), F32),
               pltpu.VMEM((lb_rows, LANES), F32)]
    return pl.pallas_call(
        functools.partial(_mix_kernel, lb_rows=lb_rows),
        grid=(bsz, nlb),
        in_specs=in_specs,
        out_specs=out_specs,
        out_shape=out_shape,
        scratch_shapes=scratch,
        compiler_params=_cparams(("parallel", "arbitrary")),
        name="mix_prompt",
    )(z, zt, *params)


def _spre_kernel(z_ref, gbuf_ref, sbuf_ref, bcol_ref, acol_ref, gcw_ref, scw_ref, wg_ref, bg_ref,
                 qkv_ref, glog_ref, act_ref, yc_ref, gbuf_out, sbuf_out):
    small = z_ref[:, SM:SM + LANES]
    lane = _iota2((1, LANES), 1)
    act_ref[...] = _gate_act(small + bcol_ref[...], lane, -jnp.exp(acol_ref[...]))
    glin = jnp.dot(small.astype(BF16), wg_ref[...], preferred_element_type=F32) + bg_ref[...]
    glog_ref[...] = _logsigmoid(glin) * (1.0 / GLA_NORMALIZER)

    wq = 3 * W_GROUP
    u = z_ref[:, GQ:GQ + wq]
    b0, b1, b2 = gbuf_ref[:, 0:wq], gbuf_ref[:, wq:2 * wq], gbuf_ref[:, 2 * wq:]
    conv = b0 * gcw_ref[0:1, :] + b1 * gcw_ref[1:2, :] + b2 * gcw_ref[2:3, :] + u * gcw_ref[3:4, :]
    conv = _silu(conv)
    gbuf_out[:, 0:wq] = b1
    gbuf_out[:, wq:2 * wq] = b2
    gbuf_out[:, 2 * wq:] = u
    hsum = (_iota2((W_GROUP, W_GROUP), 0) // HEAD_DIM ==
            _iota2((W_GROUP, W_GROUP), 1) // HEAD_DIM).astype(BF16)

    def l2n(x):
        return x * lax.rsqrt(_sel_r(x * x, hsum) + EPS)

    qkv_ref[:, 0:W_GROUP] = l2n(conv[:, 0:W_GROUP]) * Q_SCALE
    qkv_ref[:, W_GROUP:2 * W_GROUP] = l2n(conv[:, W_GROUP:2 * W_GROUP])
    qkv_ref[:, 2 * W_GROUP:] = conv[:, 2 * W_GROUP:]

    u2 = z_ref[:, SC:SC + W_GROUP] * z_ref[:, SH:SH + W_GROUP]
    s0, s1 = sbuf_ref[:, 0:W_GROUP], sbuf_ref[:, W_GROUP:]
    cu = s0 * scw_ref[0:1, :] + s1 * scw_ref[1:2, :] + u2 * scw_ref[2:3, :]
    yc_ref[...] = z_ref[:, SB:SB + W_GROUP] * cu
    sbuf_out[:, 0:W_GROUP] = s1
    sbuf_out[:, W_GROUP:] = u2


def _sample_pre(z, gbuf, sbuf, p):
    bn = z.shape[0]
    sd = lambda s: jax.ShapeDtypeStruct(s, F32)
    return pl.pallas_call(
        _spre_kernel,
        out_shape=[sd((bn, 3 * W_GROUP)), sd((bn, W_GROUP)), sd((bn, LANES)), sd((bn, W_GROUP)),
                   sd(gbuf.shape), sd(sbuf.shape)],
        compiler_params=pltpu.CompilerParams(vmem_limit_bytes=VMEM_LIMIT),
        name="sample_pre",
    )(z, gbuf, sbuf, p["bcol"], p["acol"], p["gcw"], p["scw"], p["wg"], p["bg"])


ZH = 14 * HEAD_DIM


def _srec_kernel(zh_ref, c_ref, n_ref, m_ref, sg_ref, sl_ref, norm_ref, ex_ref, tl_ref,
                 y_ref, c_out, n_out, m_out, sg_out, sl_out):
    zh = zh_ref[0]
    col = lambda j: zh[:, j * HEAD_DIM:(j + 1) * HEAD_DIM]
    mq, mk, mv, mo = col(0), col(1) * Q_SCALE, col(2), col(3)
    gq, gk, gv, gz = col(4), col(5), col(6), col(7)
    lq, lk, lv, lr = col(8) * Q_SCALE, col(9), col(10), col(11)
    glog = col(12)
    gates = col(13)
    ic, lf, gdec, beta = gates[:, 0:1], gates[:, 1:2], gates[:, 2:3], gates[:, 3:4]
    ex = ex_ref[...]
    tl = tl_ref[...]
    expand = lambda x: _sel_r(x, ex)
    tile = lambda x: _sel_r(x, tl)

    def reduce_d(pmat):
        acc = pmat[:, 0:LANES]
        for j in range(1, HEAD_DIM * HEAD_DIM // LANES):
            acc = acc + pmat[:, j * LANES:(j + 1) * LANES]
        return acc[:, :HEAD_DIM] + acc[:, HEAD_DIM:]

    rowsum = lambda x: jnp.sum(x, axis=-1, keepdims=True)
    nw = norm_ref[0]

    cmat = c_ref[...]
    nvec = n_ref[0]
    m_old = m_ref[0]
    a = lf + m_old
    mt = jnp.maximum(a, ic)
    w_int = jnp.exp(a - mt)
    e_i = jnp.exp(ic - mt)
    s = rowsum(mq * mk) * e_i
    num = w_int * reduce_d(expand(mq) * cmat) + s * mv
    den = w_int * rowsum(mq * nvec) + s
    hh = num / jnp.maximum(jnp.abs(den), jnp.exp(-mt))
    c_out[...] = w_int * cmat + e_i * (expand(mk) * tile(mv))
    n_out[0] = w_int * nvec + e_i * mk
    m_out[0] = mt
    y_a = _sigmoid(mo) * _headnorm(hh, nw[:, 0:HEAD_DIM])

    smat = sg_ref[...]
    eg = jnp.exp(gdec)
    kexp = expand(gk)
    vn = beta * gv - (beta * eg) * reduce_d(kexp * smat)
    o = eg * reduce_d(expand(gq) * smat) + rowsum(gq * gk) * vn
    sg_out[...] = eg * smat + kexp * tile(vn)
    y_b = _headnorm(o, nw[:, HEAD_DIM:2 * HEAD_DIM]) * _silu(gz)

    lmat = sl_ref[...]
    egk = jnp.exp(glog)
    o = reduce_d(expand(lq * egk) * lmat) + rowsum(lq * lk) * lv
    sl_out[...] = expand(egk) * lmat + expand(lk) * tile(lv)
    y_d = _headnorm(o, nw[:, 2 * HEAD_DIM:]) * _silu(lr)

    y_ref[0] = jnp.concatenate([y_a, y_b, y_d], axis=1)


def _sample_rec(zh, cst, nst, mst, sgst, slst, norms, ex, tl):
    bn = zh.shape[1]
    hh = HEAD_DIM * HEAD_DIM
    big = pl.BlockSpec((bn, hh), lambda h: (0, h))
    per3 = lambda w: pl.BlockSpec((1, bn, w), lambda h: (h, 0, 0))
    const = lambda a: pl.BlockSpec(a.shape, lambda h: (0, 0))
    sd = lambda s: jax.ShapeDtypeStruct(s, F32)
    return pl.pallas_call(
        _srec_kernel,
        grid=(N_HEADS,),
        in_specs=[per3(ZH), big, per3(HEAD_DIM), per3(1), big, big,
                  pl.BlockSpec((1, 1, 3 * HEAD_DIM), lambda h: (h, 0, 0)), const(ex), const(tl)],
        out_specs=[per3(3 * HEAD_DIM), big, per3(HEAD_DIM), per3(1), big, big],
        out_shape=[sd((N_HEADS, bn, 3 * HEAD_DIM)), sd(cst.shape), sd(nst.shape), sd(mst.shape),
                   sd(sgst.shape), sd(slst.shape)],
        compiler_params=_cparams(("parallel",)),
        name="sample_rec",
    )(zh, cst, nst, mst, sgst, slst, norms, ex, tl)


def _outproj_kernel(y_ref, w_ref, x_ref, o_ref):
    o_ref[...] = x_ref[...] + jnp.dot(y_ref[...].astype(BF16), w_ref[...],
                                      preferred_element_type=F32)


def _outproj(y, w, x, tm):
    t = x.shape[0]
    return pl.pallas_call(
        _outproj_kernel,
        grid=(t // tm,),
        in_specs=[pl.BlockSpec((tm, D_MODEL), lambda i: (i, 0)),
                  pl.BlockSpec((D_MODEL, D_MODEL), lambda i: (0, 0)),
                  pl.BlockSpec((tm, D_MODEL), lambda i: (i, 0))],
        out_specs=pl.BlockSpec((tm, D_MODEL), lambda i: (i, 0)),
        out_shape=jax.ShapeDtypeStruct((t, D_MODEL), F32),
        compiler_params=_cparams(("parallel",)),
        name="outproj",
    )(y, w, x)


def _ffn_kernel(x_ref, g_ref, wg_ref, wu_ref, wd_ref, o_ref, h_ref, acc_ref):
    j = pl.program_id(1)

    @pl.when(j == 0)
    def _():
        h_ref[...] = _rms(x_ref[...], g_ref[...]).astype(BF16)
        acc_ref[...] = jnp.zeros_like(acc_ref)

    h = h_ref[...]
    gate = jnp.dot(h, wg_ref[...], preferred_element_type=F32)
    up = jnp.dot(h, wu_ref[...], preferred_element_type=F32)
    acc_ref[...] += jnp.dot((_silu(gate) * up).astype(BF16), wd_ref[...],
                            preferred_element_type=F32)

    @pl.when(j == pl.num_programs(1) - 1)
    def _():
        o_ref[...] = x_ref[...] + acc_ref[...]


def _ffn(x, g, wg, wu, wd, tm, tf):
    t = x.shape[0]
    dff = wg.shape[1]
    return pl.pallas_call(
        _ffn_kernel,
        grid=(t // tm, dff // tf),
        in_specs=[pl.BlockSpec((tm, D_MODEL), lambda i, j: (i, 0)),
                  pl.BlockSpec((1, D_MODEL), lambda i, j: (0, 0)),
                  pl.BlockSpec((D_MODEL, tf), lambda i, j: (0, j)),
                  pl.BlockSpec((D_MODEL, tf), lambda i, j: (0, j)),
                  pl.BlockSpec((tf, D_MODEL), lambda i, j: (j, 0))],
        out_specs=pl.BlockSpec((tm, D_MODEL), lambda i, j: (i, 0)),
        out_shape=jax.ShapeDtypeStruct((t, D_MODEL), F32),
        scratch_shapes=[pltpu.VMEM((tm, D_MODEL), BF16), pltpu.VMEM((tm, D_MODEL), F32)],
        compiler_params=_cparams(("parallel", "arbitrary")),
        name="ffn",
    )(x, g, wg, wu, wd)


def _moe_kernel(x_ref, g_ref, wr_ref, br_ref, wg_ref, wu_ref, wd_ref, gf_ref, o_ref,
                h_ref, acc_ref, gates_ref):
    e = pl.program_id(1)

    @pl.when(e == 0)
    def _():
        hb = _rms(x_ref[...], g_ref[...]).astype(BF16)
        h_ref[...] = hb
        acc_ref[...] = jnp.zeros_like(acc_ref)
        lane = _iota2((1, LANES), 1)
        valid = lane < N_EXPERTS
        logits = jnp.dot(hb, wr_ref[...], preferred_element_type=F32) + br_ref[...]
        logits = jnp.where(valid, logits, -jnp.inf)
        ex = jnp.exp(logits - jnp.max(logits, axis=-1, keepdims=True))
        probs = ex / jnp.sum(ex, axis=-1, keepdims=True)
        v1 = jnp.max(probs, axis=-1, keepdims=True)
        i1 = jnp.min(jnp.where(probs == v1, lane, LANES), axis=-1, keepdims=True)
        rest = jnp.where((lane == i1) | jnp.logical_not(valid), -1.0, probs)
        v2 = jnp.max(rest, axis=-1, keepdims=True)
        i2 = jnp.min(jnp.where(rest == v2, lane, LANES), axis=-1, keepdims=True)
        tot = v1 + v2
        gates_ref[...] = jnp.where(lane == i1, v1 / tot, 0.0) + jnp.where(lane == i2, v2 / tot, 0.0)

    h = h_ref[...]
    gate = jnp.dot(h, wg_ref[0], preferred_element_type=F32)
    up = jnp.dot(h, wu_ref[0], preferred_element_type=F32)
    y = jnp.dot((_silu(gate) * up).astype(BF16), wd_ref[0], preferred_element_type=F32)
    lane = _iota2((1, LANES), 1)
    ge = jnp.sum(jnp.where(lane == e, gates_ref[...], 0.0), axis=-1, keepdims=True)
    acc_ref[...] += ge * y

    @pl.when(e == pl.num_programs(1) - 1)
    def _():
        o_ref[...] = _rms(x_ref[...] + acc_ref[...], gf_ref[...])


def _moe(x, g, wr, br, wg, wu, wd, gf, tm):
    t = x.shape[0]
    fe = wg.shape[2]
    return pl.pallas_call(
        _moe_kernel,
        grid=(t // tm, N_EXPERTS),
        in_specs=[pl.BlockSpec((tm, D_MODEL), lambda i, e: (i, 0)),
                  pl.BlockSpec((1, D_MODEL), lambda i, e: (0, 0)),
                  pl.BlockSpec((D_MODEL, LANES), lambda i, e: (0, 0)),
                  pl.BlockSpec((1, LANES), lambda i, e: (0, 0)),
                  pl.BlockSpec((1, D_MODEL, fe), lambda i, e: (e, 0, 0)),
                  pl.BlockSpec((1, D_MODEL, fe), lambda i, e: (e, 0, 0)),
                  pl.BlockSpec((1, fe, D_MODEL), lambda i, e: (e, 0, 0)),
                  pl.BlockSpec((1, D_MODEL), lambda i, e: (0, 0))],
        out_specs=pl.BlockSpec((tm, D_MODEL), lambda i, e: (i, 0)),
        out_shape=jax.ShapeDtypeStruct((t, D_MODEL), F32),
        scratch_shapes=[pltpu.VMEM((tm, D_MODEL), BF16), pltpu.VMEM((tm, D_MODEL), F32),
                        pltpu.VMEM((tm, LANES), F32)],
        compiler_params=_cparams(("parallel", "arbitrary")),
        name="moe",
    )(x, g, wr, br, wg, wu, wd, gf)


def _prep_w_in(w):
    seg = lambda i: w[:, _OFF[i]:_OFF[i + 1]]
    small = jnp.concatenate([seg(i) for i in _SMALL], axis=1)
    pad = jnp.zeros((w.shape[0], LANES - N_SMALL), w.dtype)
    wfull = jnp.concatenate([seg(i) for i in _BIG] + [small, pad], axis=1).astype(BF16)
    return wfull, small.T.astype(BF16)


def _pad_lanes(v, start):
    out = jnp.zeros((1, LANES), F32)
    return lax.dynamic_update_slice(out, v.reshape(1, -1).astype(F32), (0, start))


def _layer_params(l, mlstm_b_i, mlstm_b_f, mlstm_norm, gdn_conv_w, gdn_a_log, gdn_dt_bias, gdn_norm,
                  sc_conv_w, gla_w_gate, gla_b_gate, gla_norm):
    bias = jnp.concatenate([mlstm_b_i[l], mlstm_b_f[l], gdn_dt_bias[l]]).astype(F32)
    alog = gdn_a_log[l].astype(F32)
    bcol = _pad_lanes(bias, 0)
    acol = _pad_lanes(alog, 8)
    brow = jnp.zeros((N_SMALL, 1), F32).at[0:12, 0].set(bias)
    arow = jnp.zeros((N_SMALL, 1), F32).at[8:12, 0].set(alog)
    wg = jnp.zeros((LANES, W_GROUP), F32).at[16:16 + GLA_RANK].set(gla_w_gate[l]).astype(BF16)
    return dict(bcol=bcol, acol=acol, brow=brow, arow=arow,
                mnorm=mlstm_norm[l].reshape(1, -1), gnorm=gdn_norm[l].reshape(1, -1),
                lnorm=gla_norm[l].reshape(1, -1),
                gcw=gdn_conv_w[l].T, scw=sc_conv_w[l].T, wg=wg, bg=gla_b_gate[l].reshape(1, -1))


def _pick_tile(n, pref):
    for c in pref:
        if n % c == 0:
            return c
    return n


def _heads_major(a):
    return a.reshape(a.shape[0], N_HEADS, HEAD_DIM).transpose(1, 0, 2)


def kernel(x_prompt, x_sample, state_mlstm_C, state_mlstm_n, state_mlstm_m, state_gdn_S, state_gdn_conv,
           state_sc_conv, state_gla_S, w_in, g_mix, mlstm_b_i, mlstm_b_f, mlstm_norm, gdn_conv_w, gdn_a_log,
           gdn_dt_bias, gdn_norm, sc_conv_w, gla_w_gate, gla_b_gate, gla_norm, w_out, g_ffn, ffn_w_gate,
           ffn_w_up, ffn_w_down, moe_w_router, moe_b_router, moe_w_gate, moe_w_up, moe_w_down, g_final):
    depth = w_in.shape[0]
    bsz, seq, _ = x_prompt.shape
    bn = x_sample.shape[0]
    tp = bsz * seq
    assert x_sample.shape[1] == 1 and seq % CHUNK == 0

    xp = x_prompt.reshape(tp, D_MODEL)
    xs = x_sample.reshape(bn, D_MODEL)
    tm_p = _pick_tile(tp, (256, 128, 64))
    tm_f = _pick_tile(tp, (1024, 512, 256, 128, 64))
    tm_e = _pick_tile(tp, (512, 256, 128, 64))
    lb_rows = _pick_tile(seq, (256, 128, 64))

    eye = jnp.eye(HEAD_DIM, dtype=BF16)
    ex = jnp.repeat(eye, HEAD_DIM, axis=1)
    tl = jnp.tile(eye, (1, HEAD_DIM))

    p_states = [[] for _ in range(7)]
    s_states = [[] for _ in range(7)]
    for l in range(depth):
        p = _layer_params(l, mlstm_b_i, mlstm_b_f, mlstm_norm, gdn_conv_w, gdn_a_log, gdn_dt_bias,
                          gdn_norm, sc_conv_w, gla_w_gate, gla_b_gate, gla_norm)
        wfull, wst = _prep_w_in(w_in[l])
        wo = w_out[l].astype(BF16)
        gm = g_mix[l].reshape(1, -1)
        gf = g_ffn[l].reshape(1, -1)

        z, zt = _inproj(xp, gm, wfull, wst, tm_p)
        y, c1, n1, m1, sg1, gb1, sb1, sl1 = _mix_prompt(z, zt, p, bsz, seq, lb_rows)
        xp = _outproj(y, wo, xp, tm_p)
        for lst, v in zip(p_states, (c1, n1, m1.reshape(bsz, N_HEADS), sg1, gb1, sb1, sl1)):
            lst.append(v)

        zs = _inproj_s(xs, gm, wfull)
        gbuf = state_gdn_conv[l].reshape(bn, -1)
        sbuf = state_sc_conv[l].reshape(bn, -1)
        qkv, glog, act, yc, gbuf1, sbuf1 = _sample_pre(zs, gbuf, sbuf, p)
        gates = jnp.pad(act[:, :16].reshape(bn, 4, N_HEADS).transpose(2, 0, 1),
                        ((0, 0), (0, 0), (0, HEAD_DIM - 4)))
        parts = ([_heads_major(zs[:, o:o + W_GROUP]) for o in (MQ, MK, MV, MO)] +
                 [_heads_major(qkv[:, o:o + W_GROUP]) for o in (0, W_GROUP, 2 * W_GROUP)] +
                 [_heads_major(zs[:, o:o + W_GROUP]) for o in (GZ, LQ, LK, LV, LR)] +
                 [_heads_major(glog), gates])
        zh = jnp.concatenate(parts, axis=-1)
        norms = jnp.concatenate([_heads_major(v.reshape(1, -1)) for v in
                                 (mlstm_norm[l], gdn_norm[l], gla_norm[l])], axis=-1)
        hh = HEAD_DIM * HEAD_DIM
        ysr, c2, n2, m2, sg2, sl2 = _sample_rec(
            zh, state_mlstm_C[l].reshape(bn, N_HEADS * hh),
            state_mlstm_n[l].transpose(1, 0, 2),
            state_mlstm_m[l].T.reshape(N_HEADS, bn, 1),
            state_gdn_S[l].reshape(bn, N_HEADS * hh), state_gla_S[l].reshape(bn, N_HEADS * hh),
            norms, ex, tl)
        unhead = lambda a: a.transpose(1, 0, 2).reshape(bn, W_GROUP)
        ymix = jnp.concatenate([unhead(ysr[:, :, 0:HEAD_DIM]), unhead(ysr[:, :, HEAD_DIM:2 * HEAD_DIM]),
                                yc, unhead(ysr[:, :, 2 * HEAD_DIM:])], axis=-1)
        xs = _outproj(ymix, wo, xs, bn)
        st_shape = (bn, N_HEADS, HEAD_DIM, HEAD_DIM)
        for lst, v in zip(s_states, (c2.reshape(st_shape), n2.transpose(1, 0, 2),
                                     m2.reshape(N_HEADS, bn).T, sg2.reshape(st_shape),
                                     gbuf1.reshape(bn, 3, -1), sbuf1.reshape(bn, 2, -1),
                                     sl2.reshape(st_shape))):
            lst.append(v)

        j = l // 2
        if l % 2 == 0:
            wgt, wup, wdn = (ffn_w_gate[j].astype(BF16), ffn_w_up[j].astype(BF16),
                             ffn_w_down[j].astype(BF16))
            tf = _pick_tile(wgt.shape[1], (256, 128))
            xp = _ffn(xp, gf, wgt, wup, wdn, tm_f, tf)
            xs = _ffn(xs, gf, wgt, wup, wdn, bn, tf)
            if l == depth - 1:
                raise NotImplementedError("final norm is fused into the expert layer")
        else:
            wr = jnp.zeros((D_MODEL, LANES), F32).at[:, :N_EXPERTS].set(moe_w_router[j]).astype(BF16)
            br = _pad_lanes(moe_b_router[j], 0)
            wgt, wup, wdn = (moe_w_gate[j].astype(BF16), moe_w_up[j].astype(BF16),
                             moe_w_down[j].astype(BF16))
            last = l == depth - 1
            gfin = g_final.reshape(1, -1) if last else None
            assert last, "expert layer is expected to be the last layer"
            xp = _moe(xp, gf, wr, br, wgt, wup, wdn, gfin, tm_e)
            xs = _moe(xs, gf, wr, br, wgt, wup, wdn, gfin, bn)

    y_prompt = xp.reshape(bsz, seq, D_MODEL)
    y_sample = xs.reshape(bn, 1, D_MODEL)
    sp = [jnp.stack(v) for v in p_states]
    ss = [jnp.stack(v) for v in s_states]
    return (y_prompt, y_sample, sp[0], ss[0], sp[1], ss[1], sp[2], ss[2], sp[3], ss[3],
            sp[4], ss[4], sp[5], ss[5], sp[6], ss[6])
```

```python
import functools

import numpy as np
import jax
import jax.numpy as jnp
from jax import lax
from jax.experimental import pallas as pl
from jax.experimental.pallas import tpu as pltpu

F32 = jnp.float32
BF16 = jnp.bfloat16

D_MODEL = 1024
W_GROUP = 256
N_HEADS = 4
HEAD_DIM = 64
CHUNK = 64
GLA_RANK = 16
GLA_NORMALIZER = 16.0
N_EXPERTS = 8
EPS = 1e-6
Q_SCALE = HEAD_DIM ** -0.5

VMEM_LIMIT = 56 * 1024 * 1024
LANES = 128

SPLIT_SIZES = ([W_GROUP] * 4 + [N_HEADS] * 2 + [W_GROUP] * 4 + [N_HEADS] * 2 +
               [W_GROUP] * 3 + [W_GROUP] * 4 + [GLA_RANK])
_OFF = np.concatenate([[0], np.cumsum(SPLIT_SIZES)])
_BIG = [0, 1, 2, 3, 6, 7, 8, 9, 12, 13, 14, 15, 16, 17, 18]
_SMALL = [4, 5, 10, 11, 19]
_GATES = [4, 5, 10, 11]
MQ, MK, MV, MO = 0, 256, 512, 768
GQ, GK, GV, GZ = 1024, 1280, 1536, 1792
SB, SC, SH = 2048, 2304, 2560
LQ, LK, LV, LR = 2816, 3072, 3328, 3584
SM = 3840
NP = SM + LANES
N_SMALL = 32
N_GATE_ROWS = 16
GX_I, GX_B, GX_G, GX_BETA, GX_PM = 0, 256, 512, 768, 1024
GLA_LEVELS = (32, 16, 8, 4, 2)


def _cparams(sem):
    return pltpu.CompilerParams(dimension_semantics=sem, vmem_limit_bytes=VMEM_LIMIT)


def _log1pexp_negabs(x):
    return jnp.log(1.0 + jnp.exp(-jnp.abs(x)))


def _softplus(x):
    return jnp.maximum(x, 0.0) + _log1pexp_negabs(x)


def _logsigmoid(x):
    return -(jnp.maximum(-x, 0.0) + _log1pexp_negabs(x))


def _sigmoid(x):
    return 1.0 / (1.0 + jnp.exp(-x))


def _silu(x):
    return x * _sigmoid(x)


def _rms(x, g):
    return x * lax.rsqrt(jnp.mean(x * x, axis=-1, keepdims=True) + EPS) * g


def _dot(a, b):
    return jnp.dot(a, b, preferred_element_type=F32)


def _dot_nt(a, b):
    return lax.dot_general(a, b, (((1,), (1,)), ((), ())), preferred_element_type=F32)


def _dot_tn(a, b):
    return lax.dot_general(a, b, (((0,), (0,)), ((), ())), preferred_element_type=F32)


def _split2(x):
    x1 = x.astype(BF16)
    return x1, (x - x1.astype(F32)).astype(BF16)


def _split3(x):
    x1 = x.astype(BF16)
    r1 = x - x1.astype(F32)
    x2 = r1.astype(BF16)
    x3 = (r1 - x2.astype(F32)).astype(BF16)
    return x1, x2, x3


def _sel(m, x):
    x1, x2, x3 = _split3(x)
    return _dot(m, x1) + _dot(m, x2) + _dot(m, x3)


def _sel_r(x, m):
    n = x.shape[0]
    if n % 8:
        x1, x2, x3 = _split3(x)
        return _dot(x1, m) + _dot(x2, m) + _dot(x3, m)
    r = _dot(jnp.concatenate(_split3(x), axis=0), m)
    return r[0:n] + r[n:2 * n] + r[2 * n:3 * n]


def _sel_nt(m, x):
    x1, x2, x3 = _split3(x)
    return _dot_nt(m, x1) + _dot_nt(m, x2) + _dot_nt(m, x3)


def _mm3(x, w):
    n = x.shape[0]
    x1, x2 = _split2(x)
    w1, w2 = _split2(w)
    r = _dot(jnp.concatenate([x1, x2], axis=0), w1)
    return r[0:n] + r[n:] + _dot(x1, w2)


def _iota2(shape, dim):
    return lax.broadcasted_iota(jnp.int32, shape, dim)


def _inproj_kernel(x_ref, g_ref, w_ref, wst_ref, z_ref, zt_ref):
    hb = _rms(x_ref[...], g_ref[...]).astype(BF16)
    z_ref[...] = _dot(hb, w_ref[...])
    zt = _dot_nt(wst_ref[...], hb)
    for c in range(zt_ref.shape[0]):
        for h in range(N_HEADS):
            zt_ref[c, :, h * HEAD_DIM:(h + 1) * HEAD_DIM] = (
                zt[h * 4:(h + 1) * 4, c * CHUNK:(c + 1) * CHUNK])


def _inproj(x, g, w, wst, tm):
    t = x.shape[0]
    return pl.pallas_call(
        _inproj_kernel,
        grid=(t // tm,),
        in_specs=[pl.BlockSpec((tm, D_MODEL), lambda i: (i, 0)),
                  pl.BlockSpec((1, D_MODEL), lambda i: (0, 0)),
                  pl.BlockSpec((D_MODEL, NP), lambda i: (0, 0)),
                  pl.BlockSpec((N_GATE_ROWS, D_MODEL), lambda i: (0, 0))],
        out_specs=[pl.BlockSpec((tm, NP), lambda i: (i, 0)),
                   pl.BlockSpec((tm // CHUNK, 4, W_GROUP), lambda i: (i, 0, 0))],
        out_shape=[jax.ShapeDtypeStruct((t, NP), F32),
                   jax.ShapeDtypeStruct((t // CHUNK, 4, W_GROUP), F32)],
        compiler_params=_cparams(("parallel",)),
        name="inproj",
    )(x, g, w, wst)


def _gate_act(pre, idx, neg_a):
    tail = _log1pexp_negabs(pre)
    lf = -(jnp.maximum(-pre, 0.0) + tail)
    dec = neg_a * (jnp.maximum(pre, 0.0) + tail)
    beta = _sigmoid(pre)
    return jnp.where(idx < 4, pre,
                     jnp.where(idx < 8, lf,
                               jnp.where(idx < 12, dec,
                                         jnp.where(idx < 16, beta, 0.0))))


def _mix_kernel(z_ref, zt_ref, bcol_ref, acol_ref, brow_ref, arow_ref, mnorm_ref, gnorm_ref,
                lnorm_ref, gcw_ref, scw_ref, wg_ref, bg_ref,
                bdmask_ref, bdtriu_ref, bdtril_ref, selexp_ref, selstack_ref, eye64_ref,
                y_ref, c_out, n_out, m_out, sg_out, gconv_out, sconv_out, sl_out,
                c_ref, n_ref, m_ref, sg_ref, slt_ref, xg_ref, xs_ref, qkv_ref, gcum_ref, glog_ref,
                gx_ref, rw_ref, *, lb_rows):
    lb = pl.program_id(1)
    nlb = pl.num_programs(1)
    n_chunks = lb_rows // CHUNK
    wd = W_GROUP

    @pl.when(lb == 0)
    def _():
        c_ref[...] = jnp.zeros_like(c_ref)
        n_ref[...] = jnp.zeros_like(n_ref)
        m_ref[...] = jnp.zeros_like(m_ref)
        sg_ref[...] = jnp.zeros_like(sg_ref)
        slt_ref[...] = jnp.zeros_like(slt_ref)
        xg_ref[0:8, :] = jnp.zeros((8, 3 * wd), F32)
        xs_ref[0:8, :] = jnp.zeros((8, wd), F32)

    ti = _iota2((CHUNK, wd), 0)
    si = _iota2((CHUNK, wd), 1) % HEAD_DIM
    incl = ti >= si
    strict = ti > si
    eye = ti == si
    blk16 = (ti // 16) == (si // 16)
    blk32 = (ti // 32) == (si // 32)
    eye_f = eye.astype(F32)
    lvl_mask = {n: ((ti // (2 * n)) == (si // (2 * n))) & ((ti // n) > (si // n))
                for n in GLA_LEVELS + (1,)}
    bdmask = bdmask_ref[...]
    bdtriu = bdtriu_ref[...]
    bdtril = bdtril_ref[...]
    head_of_lane = _iota2((1, wd), 1) // HEAD_DIM
    pos_of_lane = _iota2((8, wd), 1) % HEAD_DIM

    def bd(x):
        xb = x.astype(BF16)
        return jnp.concatenate([xb, xb, xb, xb], axis=0) * bdmask

    def unbd(m):
        out = m[3 * HEAD_DIM:, :]
        for h in (2, 1, 0):
            out = jnp.where(head_of_lane == h, m[h * HEAD_DIM:(h + 1) * HEAD_DIM, :], out)
        return out

    def mm(a, bmat):
        return _dot(a.astype(BF16), bmat)

    def mm_nt(a, bmat):
        return _dot_nt(a.astype(BF16), bmat)

    def mm3_bd(a, b):
        a1, a2 = _split2(a)
        b1, b2 = _split2(b)
        r = _dot(jnp.concatenate([a1, a2], axis=0), bd(b1))
        return r[0:CHUNK] + r[CHUNK:] + _dot(a1, bd(b2))

    def hsum(x):
        x1, x2 = _split2(x)
        r = _dot(jnp.concatenate([x1, x2], axis=0), bdmask)
        return r[0:x.shape[0]] + r[x.shape[0]:]

    def headnorm(o, w):
        return o * lax.rsqrt(hsum(o * o) * (1.0 / HEAD_DIM) + EPS) * w

    small = z_ref[:, SM:SM + LANES]
    lane = _iota2((1, LANES), 1)
    act = _gate_act(small + bcol_ref[...], lane, -jnp.exp(acol_ref[...]))
    selexp = selexp_ref[...]
    gx4 = _sel_r(act, selexp)
    gx_ref[:, GX_I:GX_I + wd] = gx4[:, 0:wd]
    gx_ref[:, GX_BETA:GX_BETA + wd] = gx4[:, 3 * wd:]
    gx_ref[:, GX_B:GX_B + 2 * wd] = _sel(bdtril, gx4[:, wd:3 * wd])
    glin = _dot(small.astype(BF16), wg_ref[...]) + bg_ref[...]
    glog = _logsigmoid(glin) * (1.0 / GLA_NORMALIZER)
    glog_ref[...] = glog
    gcum_ref[...] = _sel(bdtril, glog)

    kind = 4 * _iota2((4, 1), 0)
    neg_a_row = -jnp.exp(arow_ref[...])
    for c in range(n_chunks):
        at = _gate_act(zt_ref[c] + brow_ref[...], kind, neg_a_row)
        cst = _sel_r(at, bdtriu)
        c_row = at[0:1, :] - cst[1:2, :]
        pm = jnp.broadcast_to(c_row, (8, wd))
        for sh in (1, 2, 4, 8, 16, 32):
            pm = jnp.where(pos_of_lane >= sh, jnp.maximum(pm, pltpu.roll(pm, sh, axis=1)), pm)
        rw_ref[c, 0:1, :] = c_row
        rw_ref[c, 1:2, :] = cst[2:3, :]
        diag = jnp.where(eye, jnp.broadcast_to(pm[0:1, :], (CHUNK, wd)), 0.0)
        gx_ref[c * CHUNK:(c + 1) * CHUNK, GX_PM:GX_PM + wd] = _sel_r(diag, bdmask)

    xg_ref[8:, :] = z_ref[:, GQ:GQ + 3 * wd]
    conv = xg_ref[5:5 + lb_rows, :] * gcw_ref[0:1, :]
    for j in range(1, 4):
        conv = conv + xg_ref[5 + j:5 + j + lb_rows, :] * gcw_ref[j:j + 1, :]
    conv = _silu(conv)

    def l2n(x):
        return x * lax.rsqrt(hsum(x * x) + EPS)

    qkv_ref[:, 0:wd] = l2n(conv[:, 0:wd]) * Q_SCALE
    qkv_ref[:, wd:2 * wd] = l2n(conv[:, wd:2 * wd])
    qkv_ref[:, 2 * wd:] = conv[:, 2 * wd:]

    xs_ref[8:, :] = z_ref[:, SC:SC + wd] * z_ref[:, SH:SH + wd]
    cu = xs_ref[6:6 + lb_rows, :] * scw_ref[0:1, :]
    for j in range(1, 3):
        cu = cu + xs_ref[6 + j:6 + j + lb_rows, :] * scw_ref[j:j + 1, :]
    y_ref[:, 2 * wd:3 * wd] = (z_ref[:, SB:SB + wd] * cu).astype(y_ref.dtype)

    sel_stack = selstack_ref[...]
    eye64 = eye64_ref[...]

    def chunk(c, carry):
        r0 = pl.multiple_of(c * CHUNK, CHUNK)
        rs = pl.ds(r0, CHUNK)
        rw = rw_ref[c]

        q = z_ref[rs, MQ:MQ + wd]
        k = z_ref[rs, MK:MK + wd] * Q_SCALE
        v = z_ref[rs, MV:MV + wd]
        ic_e = gx_ref[rs, GX_I:GX_I + wd]
        b_e = gx_ref[rs, GX_B:GX_B + wd]
        pm_e = gx_ref[rs, GX_PM:GX_PM + wd]
        c_row = rw[0:1, :]
        m_old = m_ref[0:1, :]
        cmat = c_ref[...]
        nrow = n_ref[0:1, :]
        mx = jnp.maximum(m_old, pm_e)
        w_int = jnp.exp(m_old - mx)
        p = jnp.where(incl, jnp.exp(c_row - mx), 0.0)
        s = mm_nt(q, bd(k)) * p
        num = w_int * mm(q, bd(cmat)) + mm(s, bd(v))
        den = hsum(w_int * (q * nrow) + s)
        hh = num / jnp.maximum(jnp.abs(den), jnp.exp(-(b_e + mx)))
        mx_last = mx[CHUNK - 1:CHUNK, :]
        wgk = jnp.exp((ic_e - b_e) - mx_last) * k
        dec = jnp.exp(m_old - mx_last)
        c_ref[...] = dec * cmat + unbd(_dot_tn(wgk.astype(BF16), v.astype(BF16)))
        n_ref[...] = jnp.broadcast_to(dec * nrow + jnp.sum(wgk, axis=0, keepdims=True), (8, wd))
        m_ref[...] = jnp.broadcast_to(b_e[CHUNK - 1:CHUNK, :] + mx_last, (8, wd))
        y_a = _sigmoid(z_ref[rs, MO:MO + wd]) * headnorm(hh, mnorm_ref[...])
        y_ref[rs, 0:wd] = y_a.astype(y_ref.dtype)

        q = qkv_ref[rs, 0:wd]
        k = qkv_ref[rs, wd:2 * wd]
        v = qkv_ref[rs, 2 * wd:]
        g_e = gx_ref[rs, GX_G:GX_G + wd]
        beta = gx_ref[rs, GX_BETA:GX_BETA + wd]
        g_row = rw[1:2, :]
        s_old = sg_ref[...]
        bdk = bd(k)
        lw = jnp.where(incl, jnp.exp(g_e - g_row), 0.0)
        amat = jnp.where(strict, beta * mm_nt(k, bdk) * lw, 0.0)
        egc = jnp.exp(g_e)
        ad = jnp.where(blk16, amat, 0.0)
        a2 = mm3_bd(ad, ad)
        a4 = mm3_bd(a2, a2)
        a8 = mm3_bd(a4, a4)
        tinv = eye_f - ad
        tinv = tinv + mm3_bd(tinv, a2)
        tinv = tinv + mm3_bd(tinv, a4)
        tinv = tinv + mm3_bd(tinv, a8)
        n1 = jnp.where(blk32 & jnp.logical_not(blk16), amat, 0.0)
        tinv = tinv - mm3_bd(mm3_bd(tinv, n1), tinv)
        n2 = jnp.where(blk32, 0.0, amat)
        tinv = tinv - mm3_bd(mm3_bd(tinv, n2), tinv)
        u = mm3_bd(tinv, beta * v)
        w = mm3_bd(tinv, (beta * egc) * k)
        bds = bd(s_old)
        vn = u - mm(w, bds)
        qk = mm_nt(q, bdk) * lw
        o = mm(q * egc, bds) + mm(qk, bd(vn))
        g_last = g_e[CHUNK - 1:CHUNK, :]
        kdec = k * jnp.exp(g_last - g_e)
        sg_ref[...] = jnp.exp(g_last) * s_old + unbd(_dot_tn(kdec.astype(BF16), vn.astype(BF16)))
        y_b = headnorm(o, gnorm_ref[...]) * _silu(z_ref[rs, GZ:GZ + wd])
        y_ref[rs, wd:2 * wd] = y_b.astype(y_ref.dtype)

        q = z_ref[rs, LQ:LQ + wd] * Q_SCALE
        k = z_ref[rs, LK:LK + wd]
        v = z_ref[rs, LV:LV + wd]
        gk = glog_ref[rs, :]
        gcum = gcum_ref[rs, :]
        stack = _sel(sel_stack, gcum)
        st_old = slt_ref[...]
        bdk = bd(k)
        amat = jnp.where(eye, mm_nt(q, bdk), 0.0)
        amat = amat + jnp.where(lvl_mask[1], mm_nt(q * jnp.exp(gk), bdk), 0.0)
        for i, n in enumerate(GLA_LEVELS):
            g_end = stack[2 * i * CHUNK:(2 * i + 1) * CHUNK, :]
            g_prev = stack[(2 * i + 1) * CHUNK:(2 * i + 2) * CHUNK, :]
            qn = q * jnp.exp(gcum - g_prev)
            kn = k * jnp.exp(g_end - gcum)
            amat = amat + jnp.where(lvl_mask[n], mm_nt(qn, bd(kn)), 0.0)
        g_last = gcum[CHUNK - 1:CHUNK, :]
        o = mm_nt(q * jnp.exp(gcum), bd(st_old)) + mm(amat, bd(v))
        kdec = k * jnp.exp(g_last - gcum)
        slt_ref[...] = st_old * jnp.exp(g_last) + unbd(_dot_tn(v.astype(BF16), kdec.astype(BF16)))
        y_d = headnorm(o, lnorm_ref[...]) * _silu(z_ref[rs, LR:LR + wd])
        y_ref[rs, 3 * wd:] = y_d.astype(y_ref.dtype)
        return carry

    lax.fori_loop(0, n_chunks, chunk, 0)

    gtail = xg_ref[lb_rows:lb_rows + 8, :]
    stail = xs_ref[lb_rows:lb_rows + 8, :]
    xg_ref[0:8, :] = gtail
    xs_ref[0:8, :] = stail

    @pl.when(lb == nlb - 1)
    def _():
        gconv_out[0] = gtail[5:8, :]
        sconv_out[0] = stail[6:8, :]
        lane4 = _iota2((1, N_HEADS), 1)
        m_row = jnp.zeros((1, N_HEADS), F32)
        for h in range(N_HEADS):
            hs = slice(h * HEAD_DIM, (h + 1) * HEAD_DIM)
            c_out[0, h] = c_ref[:, hs]
            n_out[0, h:h + 1, :] = n_ref[0:1, hs]
            m_row = jnp.where(lane4 == h, m_ref[0:1, h * HEAD_DIM:h * HEAD_DIM + 1], m_row)
            sg_out[0, h] = sg_ref[:, hs]
            sl_out[0, h] = _sel_nt(eye64, slt_ref[:, hs])
        m_out[0] = m_row


def _mix_consts(lb_rows):
    wd = W_GROUP
    r, c = np.indices((wd, wd))
    same_head = (r // HEAD_DIM) == (c // HEAD_DIM)
    bdmask = same_head
    bdtriu = same_head & ((r % HEAD_DIM) <= (c % HEAD_DIM))
    rl, cl = np.indices((lb_rows, lb_rows))
    bdtril = ((rl // CHUNK) == (cl // CHUNK)) & (rl >= cl)
    rr, cc = np.indices((LANES, 4 * wd))
    selexp = rr == 4 * (cc // wd) + (cc % wd) // HEAD_DIM
    rows, cols = np.indices((2 * len(GLA_LEVELS) * CHUNK, CHUNK))
    r_in, blk = rows % CHUNK, rows // CHUNK
    sel_stack = np.zeros(rows.shape, bool)
    for i, n in enumerate(GLA_LEVELS):
        sel_stack |= (blk == 2 * i) & (cols == (r_in // n) * n + (n - 1))
        sel_stack |= (blk == 2 * i + 1) & (cols == (r_in // n) * n - 1)
    eye64 = np.eye(CHUNK, dtype=bool)
    return [jnp.asarray(m, dtype=BF16) for m in (bdmask, bdtriu, bdtril, selexp, sel_stack, eye64)]


def _mix_prompt(z, zt, p, bsz, seq, lb_rows):
    t = bsz * seq
    nlb = seq // lb_rows
    row = lambda b, l: (b * nlb + l, 0)
    const2 = lambda b, l: (0, 0)
    st4 = lambda b, l: (b, 0, 0, 0)
    st3 = lambda b, l: (b, 0, 0)
    params = [p["bcol"], p["acol"], p["brow"], p["arow"], p["mnorm"], p["gnorm"], p["lnorm"],
              p["gcw"], p["scw"], p["wg"], p["bg"]] + _mix_consts(lb_rows)
    in_specs = [pl.BlockSpec((lb_rows, NP), row),
                pl.BlockSpec((lb_rows // CHUNK, 4, W_GROUP), lambda b, l: (b * nlb + l, 0, 0))]
    in_specs += [pl.BlockSpec(a.shape, const2) for a in params]
    hd = (N_HEADS, HEAD_DIM, HEAD_DIM)
    out_shape = [jax.ShapeDtypeStruct((t, D_MODEL), BF16),
                 jax.ShapeDtypeStruct((bsz,) + hd, F32),
                 jax.ShapeDtypeStruct((bsz, N_HEADS, HEAD_DIM), F32),
                 jax.ShapeDtypeStruct((bsz, 1, N_HEADS), F32),
                 jax.ShapeDtypeStruct((bsz,) + hd, F32),
                 jax.ShapeDtypeStruct((bsz, 3, 3 * W_GROUP), F32),
                 jax.ShapeDtypeStruct((bsz, 2, W_GROUP), F32),
                 jax.ShapeDtypeStruct((bsz,) + hd, F32)]
    out_specs = [pl.BlockSpec((lb_rows, D_MODEL), row),
                 pl.BlockSpec((1,) + hd, st4),
                 pl.BlockSpec((1, N_HEADS, HEAD_DIM), st3),
                 pl.BlockSpec((1, 1, N_HEADS), st3),
                 pl.BlockSpec((1,) + hd, st4),
                 pl.BlockSpec((1, 3, 3 * W_GROUP), st3),
                 pl.BlockSpec((1, 2, W_GROUP), st3),
                 pl.BlockSpec((1,) + hd, st4)]
    scratch = [pltpu.VMEM((HEAD_DIM, W_GROUP), F32),
               pltpu.VMEM((8, W_GROUP), F32),
               pltpu.VMEM((8, W_GROUP), F32),
               pltpu.VMEM((HEAD_DIM, W_GROUP), F32),
               pltpu.VMEM((HEAD_DIM, W_GROUP), F32),
               pltpu.VMEM((lb_rows + 8, 3 * W_GROUP), F32),
               pltpu.VMEM((lb_rows + 8, W_GROUP), F32),
               pltpu.VMEM((lb_rows, 3 * W_GROUP), F32),
               pltpu.VMEM((lb_rows, W_GROUP), F32),
               pltpu.VMEM((lb_rows, W_GROUP), F32),
               pltpu.VMEM((lb_rows, 5 * W_GROUP), F32),
               pltpu.VMEM((lb_rows // CHUNK, 8, W_GROUP), F32)]
    return pl.pallas_call(
        functools.partial(_mix_kernel, lb_rows=lb_rows),
        grid=(bsz, nlb),
        in_specs=in_specs,
        out_specs=out_specs,
        out_shape=out_shape,
        scratch_shapes=scratch,
        compiler_params=_cparams(("parallel", "arbitrary")),
        name="mix_prompt",
    )(z, zt, *params)


def _inproj_s_kernel(x_ref, g_ref, w_ref, z_ref):
    h = _rms(x_ref[...], g_ref[...])
    n = w_ref.shape[1]
    step = 4 * LANES
    for j in range(0, n, step):
        cs = slice(j, min(j + step, n))
        z_ref[:, cs] = _mm3(h, w_ref[:, cs])


def _inproj_s(x, g, w):
    return pl.pallas_call(
        _inproj_s_kernel,
        out_shape=jax.ShapeDtypeStruct((x.shape[0], w.shape[1]), F32),
        compiler_params=pltpu.CompilerParams(vmem_limit_bytes=VMEM_LIMIT),
        name="inproj_s",
    )(x, g, w)


def _spre_kernel(z_ref, gbuf_ref, sbuf_ref, bcol_ref, acol_ref, gcw_ref, scw_ref, wg_ref, bg_ref,
                 qkv_ref, glog_ref, act_ref, yc_ref, gbuf_out, sbuf_out):
    small = z_ref[:, SM:SM + LANES]
    lane = _iota2((1, LANES), 1)
    act_ref[...] = _gate_act(small + bcol_ref[...], lane, -jnp.exp(acol_ref[...]))
    glin = _mm3(small, wg_ref[...]) + bg_ref[...]
    glog_ref[...] = _logsigmoid(glin) * (1.0 / GLA_NORMALIZER)

    wq = 3 * W_GROUP
    u = z_ref[:, GQ:GQ + wq]
    b0, b1, b2 = gbuf_ref[:, 0:wq], gbuf_ref[:, wq:2 * wq], gbuf_ref[:, 2 * wq:]
    conv = b0 * gcw_ref[0:1, :] + b1 * gcw_ref[1:2, :] + b2 * gcw_ref[2:3, :] + u * gcw_ref[3:4, :]
    conv = _silu(conv)
    gbuf_out[:, 0:wq] = b1
    gbuf_out[:, wq:2 * wq] = b2
    gbuf_out[:, 2 * wq:] = u
    hsum = (_iota2((W_GROUP, W_GROUP), 0) // HEAD_DIM ==
            _iota2((W_GROUP, W_GROUP), 1) // HEAD_DIM).astype(BF16)

    def l2n(x):
        return x * lax.rsqrt(_sel_r(x * x, hsum) + EPS)

    qkv_ref[:, 0:W_GROUP] = l2n(conv[:, 0:W_GROUP]) * Q_SCALE
    qkv_ref[:, W_GROUP:2 * W_GROUP] = l2n(conv[:, W_GROUP:2 * W_GROUP])
    qkv_ref[:, 2 * W_GROUP:] = conv[:, 2 * W_GROUP:]

    u2 = z_ref[:, SC:SC + W_GROUP] * z_ref[:, SH:SH + W_GROUP]
    s0, s1 = sbuf_ref[:, 0:W_GROUP], sbuf_ref[:, W_GROUP:]
    cu = s0 * scw_ref[0:1, :] + s1 * scw_ref[1:2, :] + u2 * scw_ref[2:3, :]
    yc_ref[...] = z_ref[:, SB:SB + W_GROUP] * cu
    sbuf_out[:, 0:W_GROUP] = s1
    sbuf_out[:, W_GROUP:] = u2


def _sample_pre(z, gbuf, sbuf, p):
    bn = z.shape[0]
    sd = lambda s: jax.ShapeDtypeStruct(s, F32)
    return pl.pallas_call(
        _spre_kernel,
        out_shape=[sd((bn, 3 * W_GROUP)), sd((bn, W_GROUP)), sd((bn, LANES)), sd((bn, W_GROUP)),
                   sd(gbuf.shape), sd(sbuf.shape)],
        compiler_params=pltpu.CompilerParams(vmem_limit_bytes=VMEM_LIMIT),
        name="sample_pre",
    )(z, gbuf, sbuf, p["bcol"], p["acol"], p["gcw"], p["scw"], p["wg32"], p["bg"])


ZH = 14 * HEAD_DIM


def _srec_kernel(zh_ref, c_ref, n_ref, m_ref, sg_ref, sl_ref, norm_ref, ex_ref, tl_ref,
                 y_ref, c_out, n_out, m_out, sg_out, sl_out):
    zh = zh_ref[0]
    col = lambda j: zh[:, j * HEAD_DIM:(j + 1) * HEAD_DIM]
    mq, mk, mv, mo = col(0), col(1) * Q_SCALE, col(2), col(3)
    gq, gk, gv, gz = col(4), col(5), col(6), col(7)
    lq, lk, lv, lr = col(8) * Q_SCALE, col(9), col(10), col(11)
    glog = col(12)
    gates = col(13)
    ic, lf, gdec, beta = gates[:, 0:1], gates[:, 1:2], gates[:, 2:3], gates[:, 3:4]
    ex = ex_ref[...]
    tl = tl_ref[...]
    expand = lambda x: _sel_r(x, ex)
    tile = lambda x: _sel_r(x, tl)

    def reduce_d(pmat):
        acc = pmat[:, 0:LANES]
        for j in range(1, HEAD_DIM * HEAD_DIM // LANES):
            acc = acc + pmat[:, j * LANES:(j + 1) * LANES]
        return acc[:, :HEAD_DIM] + acc[:, HEAD_DIM:]

    def headnorm(o, w):
        return o * lax.rsqrt(jnp.mean(o * o, axis=-1, keepdims=True) + EPS) * w

    rowsum = lambda x: jnp.sum(x, axis=-1, keepdims=True)
    nw = norm_ref[0]

    cmat = c_ref[...]
    nvec = n_ref[0]
    m_old = m_ref[0]
    a = lf + m_old
    mt = jnp.maximum(a, ic)
    w_int = jnp.exp(a - mt)
    e_i = jnp.exp(ic - mt)
    s = rowsum(mq * mk) * e_i
    num = w_int * reduce_d(expand(mq) * cmat) + s * mv
    den = w_int * rowsum(mq * nvec) + s
    hh = num / jnp.maximum(jnp.abs(den), jnp.exp(-mt))
    c_out[...] = w_int * cmat + e_i * (expand(mk) * tile(mv))
    n_out[0] = w_int * nvec + e_i * mk
    m_out[0] = mt
    y_a = _sigmoid(mo) * headnorm(hh, nw[:, 0:HEAD_DIM])

    smat = sg_ref[...]
    eg = jnp.exp(gdec)
    kexp = expand(gk)
    vn = beta * gv - (beta * eg) * reduce_d(kexp * smat)
    o = eg * reduce_d(expand(gq) * smat) + rowsum(gq * gk) * vn
    sg_out[...] = eg * smat + kexp * tile(vn)
    y_b = headnorm(o, nw[:, HEAD_DIM:2 * HEAD_DIM]) * _silu(gz)

    lmat = sl_ref[...]
    egk = jnp.exp(glog)
    o = reduce_d(expand(lq * egk) * lmat) + rowsum(lq * lk) * lv
    sl_out[...] = expand(egk) * lmat + expand(lk) * tile(lv)
    y_d = headnorm(o, nw[:, 2 * HEAD_DIM:]) * _silu(lr)

    y_ref[0] = jnp.concatenate([y_a, y_b, y_d], axis=1)


def _sample_rec(zh, cst, nst, mst, sgst, slst, norms, ex, tl):
    bn = zh.shape[1]
    hh = HEAD_DIM * HEAD_DIM
    big = pl.BlockSpec((bn, hh), lambda h: (0, h))
    per3 = lambda w: pl.BlockSpec((1, bn, w), lambda h: (h, 0, 0))
    const = lambda a: pl.BlockSpec(a.shape, lambda h: (0, 0))
    sd = lambda s: jax.ShapeDtypeStruct(s, F32)
    return pl.pallas_call(
        _srec_kernel,
        grid=(N_HEADS,),
        in_specs=[per3(ZH), big, per3(HEAD_DIM), per3(1), big, big,
                  pl.BlockSpec((1, 1, 3 * HEAD_DIM), lambda h: (h, 0, 0)), const(ex), const(tl)],
        out_specs=[per3(3 * HEAD_DIM), big, per3(HEAD_DIM), per3(1), big, big],
        out_shape=[sd((N_HEADS, bn, 3 * HEAD_DIM)), sd(cst.shape), sd(nst.shape), sd(mst.shape),
                   sd(sgst.shape), sd(slst.shape)],
        compiler_params=_cparams(("parallel",)),
        name="sample_rec",
    )(zh, cst, nst, mst, sgst, slst, norms, ex, tl)


def _outproj_s_kernel(y_ref, w_ref, x_ref, o_ref):
    o_ref[...] = x_ref[...] + _mm3(y_ref[...], w_ref[...])


def _outproj_s(y, w, x):
    return pl.pallas_call(
        _outproj_s_kernel,
        out_shape=jax.ShapeDtypeStruct(x.shape, F32),
        compiler_params=pltpu.CompilerParams(vmem_limit_bytes=VMEM_LIMIT),
        name="outproj_s",
    )(y, w, x)


def _ffn_s_kernel(x_ref, g_ref, wg_ref, wu_ref, wd_ref, gf_ref, o_ref, h_ref, acc_ref, *, final_norm):
    j = pl.program_id(0)

    @pl.when(j == 0)
    def _():
        h_ref[...] = _rms(x_ref[...], g_ref[...])
        acc_ref[...] = jnp.zeros_like(acc_ref)

    h = h_ref[...]
    acc_ref[...] += _mm3(_silu(_mm3(h, wg_ref[...])) * _mm3(h, wu_ref[...]), wd_ref[...])

    @pl.when(j == pl.num_programs(0) - 1)
    def _():
        out = x_ref[...] + acc_ref[...]
        o_ref[...] = _rms(out, gf_ref[...]) if final_norm else out


def _ffn_s(x, g, wg, wu, wd, gf, tf, final_norm):
    t = x.shape[0]
    dff = wg.shape[1]
    full = lambda j: (0, 0)
    return pl.pallas_call(
        functools.partial(_ffn_s_kernel, final_norm=final_norm),
        grid=(dff // tf,),
        in_specs=[pl.BlockSpec((t, D_MODEL), full),
                  pl.BlockSpec((1, D_MODEL), full),
                  pl.BlockSpec((D_MODEL, tf), lambda j: (0, j)),
                  pl.BlockSpec((D_MODEL, tf), lambda j: (0, j)),
                  pl.BlockSpec((tf, D_MODEL), lambda j: (j, 0)),
                  pl.BlockSpec((1, D_MODEL), full)],
        out_specs=pl.BlockSpec((t, D_MODEL), full),
        out_shape=jax.ShapeDtypeStruct((t, D_MODEL), F32),
        scratch_shapes=[pltpu.VMEM((t, D_MODEL), F32), pltpu.VMEM((t, D_MODEL), F32)],
        compiler_params=_cparams(("arbitrary",)),
        name="ffn_s",
    )(x, g, wg, wu, wd, gf)


def _outproj_kernel(y_ref, w_ref, x_ref, o_ref):
    o_ref[...] = x_ref[...] + _dot(y_ref[...], w_ref[...])


def _outproj(y, w, x, tm):
    t = x.shape[0]
    return pl.pallas_call(
        _outproj_kernel,
        grid=(t // tm,),
        in_specs=[pl.BlockSpec((tm, D_MODEL), lambda i: (i, 0)),
                  pl.BlockSpec((D_MODEL, D_MODEL), lambda i: (0, 0)),
                  pl.BlockSpec((tm, D_MODEL), lambda i: (i, 0))],
        out_specs=pl.BlockSpec((tm, D_MODEL), lambda i: (i, 0)),
        out_shape=jax.ShapeDtypeStruct((t, D_MODEL), F32),
        compiler_params=_cparams(("parallel",)),
        name="outproj",
    )(y, w, x)


def _ffn_kernel(x_ref, g_ref, wg_ref, wu_ref, wd_ref, gf_ref, o_ref, h_ref, acc_ref, *, final_norm):
    j = pl.program_id(1)

    @pl.when(j == 0)
    def _():
        h_ref[...] = _rms(x_ref[...], g_ref[...]).astype(BF16)
        acc_ref[...] = jnp.zeros_like(acc_ref)

    h = h_ref[...]
    acc_ref[...] += _dot((_silu(_dot(h, wg_ref[...])) * _dot(h, wu_ref[...])).astype(BF16), wd_ref[...])

    @pl.when(j == pl.num_programs(1) - 1)
    def _():
        out = x_ref[...] + acc_ref[...]
        o_ref[...] = _rms(out, gf_ref[...]) if final_norm else out


def _ffn(x, g, wg, wu, wd, gf, tm, tf, final_norm):
    t = x.shape[0]
    dff = wg.shape[1]
    return pl.pallas_call(
        functools.partial(_ffn_kernel, final_norm=final_norm),
        grid=(t // tm, dff // tf),
        in_specs=[pl.BlockSpec((tm, D_MODEL), lambda i, j: (i, 0)),
                  pl.BlockSpec((1, D_MODEL), lambda i, j: (0, 0)),
                  pl.BlockSpec((D_MODEL, tf), lambda i, j: (0, j)),
                  pl.BlockSpec((D_MODEL, tf), lambda i, j: (0, j)),
                  pl.BlockSpec((tf, D_MODEL), lambda i, j: (j, 0)),
                  pl.BlockSpec((1, D_MODEL), lambda i, j: (0, 0))],
        out_specs=pl.BlockSpec((tm, D_MODEL), lambda i, j: (i, 0)),
        out_shape=jax.ShapeDtypeStruct((t, D_MODEL), F32),
        scratch_shapes=[pltpu.VMEM((tm, D_MODEL), BF16), pltpu.VMEM((tm, D_MODEL), F32)],
        compiler_params=_cparams(("parallel", "arbitrary")),
        name="ffn",
    )(x, g, wg, wu, wd, gf)


def _top2_gates(logits):
    lane = _iota2((1, LANES), 1)
    valid = lane < N_EXPERTS
    logits = jnp.where(valid, logits, -jnp.inf)
    ex = jnp.exp(logits - jnp.max(logits, axis=-1, keepdims=True))
    probs = ex / jnp.sum(ex, axis=-1, keepdims=True)
    v1 = jnp.max(probs, axis=-1, keepdims=True)
    i1 = jnp.min(jnp.where(probs == v1, lane, LANES), axis=-1, keepdims=True)
    rest = jnp.where((lane == i1) | jnp.logical_not(valid), -1.0, probs)
    v2 = jnp.max(rest, axis=-1, keepdims=True)
    i2 = jnp.min(jnp.where(rest == v2, lane, LANES), axis=-1, keepdims=True)
    tot = v1 + v2
    return jnp.where(lane == i1, v1 / tot, 0.0) + jnp.where(lane == i2, v2 / tot, 0.0)


def _moe_kernel(x_ref, g_ref, wr_ref, br_ref, wg_ref, wu_ref, wd_ref, gf_ref, o_ref,
                h_ref, acc_ref, gates_ref, *, final_norm, precise_router):
    e = pl.program_id(1)

    @pl.when(e == 0)
    def _():
        h = _rms(x_ref[...], g_ref[...])
        h_ref[...] = h.astype(BF16)
        acc_ref[...] = jnp.zeros_like(acc_ref)
        if precise_router:
            logits = _mm3(h, wr_ref[...])
        else:
            logits = _dot(h.astype(BF16), wr_ref[...].astype(BF16))
        gates_ref[...] = _top2_gates(logits + br_ref[...])

    h = h_ref[...]
    y = _dot((_silu(_dot(h, wg_ref[0])) * _dot(h, wu_ref[0])).astype(BF16), wd_ref[0])
    lane = _iota2((1, LANES), 1)
    ge = jnp.sum(jnp.where(lane == e, gates_ref[...], 0.0), axis=-1, keepdims=True)
    acc_ref[...] += ge * y

    @pl.when(e == pl.num_programs(1) - 1)
    def _():
        out = x_ref[...] + acc_ref[...]
        o_ref[...] = _rms(out, gf_ref[...]) if final_norm else out


def _moe(x, g, wr, br, wg, wu, wd, gf, tm, final_norm, precise_router):
    t = x.shape[0]
    fe = wg.shape[2]
    return pl.pallas_call(
        functools.partial(_moe_kernel, final_norm=final_norm, precise_router=precise_router),
        grid=(t // tm, N_EXPERTS),
        in_specs=[pl.BlockSpec((tm, D_MODEL), lambda i, e: (i, 0)),
                  pl.BlockSpec((1, D_MODEL), lambda i, e: (0, 0)),
                  pl.BlockSpec((D_MODEL, LANES), lambda i, e: (0, 0)),
                  pl.BlockSpec((1, LANES), lambda i, e: (0, 0)),
                  pl.BlockSpec((1, D_MODEL, fe), lambda i, e: (e, 0, 0)),
                  pl.BlockSpec((1, D_MODEL, fe), lambda i, e: (e, 0, 0)),
                  pl.BlockSpec((1, fe, D_MODEL), lambda i, e: (e, 0, 0)),
                  pl.BlockSpec((1, D_MODEL), lambda i, e: (0, 0))],
        out_specs=pl.BlockSpec((tm, D_MODEL), lambda i, e: (i, 0)),
        out_shape=jax.ShapeDtypeStruct((t, D_MODEL), F32),
        scratch_shapes=[pltpu.VMEM((tm, D_MODEL), BF16), pltpu.VMEM((tm, D_MODEL), F32),
                        pltpu.VMEM((tm, LANES), F32)],
        compiler_params=_cparams(("parallel", "arbitrary")),
        name="moe",
    )(x, g, wr, br, wg, wu, wd, gf)


def _prep_w_in(w):
    seg = lambda i: w[:, _OFF[i]:_OFF[i + 1]]
    small = jnp.concatenate([seg(i) for i in _SMALL], axis=1)
    pad = jnp.zeros((w.shape[0], LANES - N_SMALL), w.dtype)
    wfull = jnp.concatenate([seg(i) for i in _BIG] + [small, pad], axis=1).astype(BF16)
    gates = jnp.stack([seg(i) for i in _GATES], axis=-1)
    return wfull, gates.reshape(w.shape[0], N_GATE_ROWS).T.astype(BF16)


def _regroup_z(z):
    seg = lambda i: z[:, _OFF[i]:_OFF[i + 1]]
    pad = jnp.zeros((z.shape[0], LANES - N_SMALL), z.dtype)
    return jnp.concatenate([seg(i) for i in _BIG] + [seg(i) for i in _SMALL] + [pad], axis=1)


def _pad_lanes(v, start):
    out = jnp.zeros((1, LANES), F32)
    return lax.dynamic_update_slice(out, v.reshape(1, -1).astype(F32), (0, start))


def _layer_params(l, mlstm_b_i, mlstm_b_f, mlstm_norm, gdn_conv_w, gdn_a_log, gdn_dt_bias, gdn_norm,
                  sc_conv_w, gla_w_gate, gla_b_gate, gla_norm):
    bias = jnp.concatenate([mlstm_b_i[l], mlstm_b_f[l], gdn_dt_bias[l]]).astype(F32)
    alog = gdn_a_log[l].astype(F32)
    per_lane = lambda v: jnp.repeat(v, HEAD_DIM)
    zero = jnp.zeros((W_GROUP,), F32)
    brow = jnp.stack([per_lane(mlstm_b_i[l]), per_lane(mlstm_b_f[l]), per_lane(gdn_dt_bias[l]), zero])
    arow = jnp.stack([zero, zero, per_lane(alog), zero])
    wg32 = jnp.zeros((LANES, W_GROUP), F32).at[16:16 + GLA_RANK].set(gla_w_gate[l])
    return dict(bcol=_pad_lanes(bias, 0), acol=_pad_lanes(alog, 8), brow=brow, arow=arow,
                mnorm=mlstm_norm[l].reshape(1, -1), gnorm=gdn_norm[l].reshape(1, -1),
                lnorm=gla_norm[l].reshape(1, -1),
                gcw=gdn_conv_w[l].T, scw=sc_conv_w[l].T, wg=wg32.astype(BF16), wg32=wg32,
                bg=gla_b_gate[l].reshape(1, -1))


def _pick_tile(n, pref):
    for c in pref:
        if n % c == 0:
            return c
    return n


def _heads_major(a):
    return a.reshape(a.shape[0], N_HEADS, HEAD_DIM).transpose(1, 0, 2)


def kernel(x_prompt, x_sample, state_mlstm_C, state_mlstm_n, state_mlstm_m, state_gdn_S, state_gdn_conv,
           state_sc_conv, state_gla_S, w_in, g_mix, mlstm_b_i, mlstm_b_f, mlstm_norm, gdn_conv_w, gdn_a_log,
           gdn_dt_bias, gdn_norm, sc_conv_w, gla_w_gate, gla_b_gate, gla_norm, w_out, g_ffn, ffn_w_gate,
           ffn_w_up, ffn_w_down, moe_w_router, moe_b_router, moe_w_gate, moe_w_up, moe_w_down, g_final):
    depth = w_in.shape[0]
    bsz, seq, _ = x_prompt.shape
    bn = x_sample.shape[0]
    tp = bsz * seq
    assert x_sample.shape[1] == 1 and seq % CHUNK == 0

    xp = x_prompt.reshape(tp, D_MODEL)
    xs = x_sample.reshape(bn, D_MODEL)
    tm_p = _pick_tile(tp, (256, 128, 64))
    tm_f = _pick_tile(tp, (1024, 512, 256, 128, 64))
    tm_e = _pick_tile(tp, (512, 256, 128, 64))
    lb_rows = _pick_tile(seq, (256, 128, 64))

    eye = jnp.eye(HEAD_DIM, dtype=BF16)
    ex = jnp.repeat(eye, HEAD_DIM, axis=1)
    tl = jnp.tile(eye, (1, HEAD_DIM))
    gfin = g_final.reshape(1, -1)

    p_states = [[] for _ in range(7)]
    s_states = [[] for _ in range(7)]
    for l in range(depth):
        p = _layer_params(l, mlstm_b_i, mlstm_b_f, mlstm_norm, gdn_conv_w, gdn_a_log, gdn_dt_bias,
                          gdn_norm, sc_conv_w, gla_w_gate, gla_b_gate, gla_norm)
        wfull, wst = _prep_w_in(w_in[l])
        gm = g_mix[l].reshape(1, -1)
        gf = g_ffn[l].reshape(1, -1)
        last = l == depth - 1

        z, zt = _inproj(xp, gm, wfull, wst, tm_p)
        y, c1, n1, m1, sg1, gb1, sb1, sl1 = _mix_prompt(z, zt, p, bsz, seq, lb_rows)
        xp = _outproj(y, w_out[l].astype(BF16), xp, tm_p)
        for lst, v in zip(p_states, (c1, n1, m1.reshape(bsz, N_HEADS), sg1, gb1, sb1, sl1)):
            lst.append(v)

        zs = _regroup_z(_inproj_s(xs, gm, w_in[l]))
        gbuf = state_gdn_conv[l].reshape(bn, -1)
        sbuf = state_sc_conv[l].reshape(bn, -1)
        qkv, glog, act, yc, gbuf1, sbuf1 = _sample_pre(zs, gbuf, sbuf, p)
        gates = jnp.pad(act[:, :16].reshape(bn, 4, N_HEADS).transpose(2, 0, 1),
                        ((0, 0), (0, 0), (0, HEAD_DIM - 4)))
        parts = ([_heads_major(zs[:, o:o + W_GROUP]) for o in (MQ, MK, MV, MO)] +
                 [_heads_major(qkv[:, o:o + W_GROUP]) for o in (0, W_GROUP, 2 * W_GROUP)] +
                 [_heads_major(zs[:, o:o + W_GROUP]) for o in (GZ, LQ, LK, LV, LR)] +
                 [_heads_major(glog), gates])
        zh = jnp.concatenate(parts, axis=-1)
        norms = jnp.concatenate([_heads_major(v.reshape(1, -1)) for v in
                                 (mlstm_norm[l], gdn_norm[l], gla_norm[l])], axis=-1)
        hh = HEAD_DIM * HEAD_DIM
        ysr, c2, n2, m2, sg2, sl2 = _sample_rec(
            zh, state_mlstm_C[l].reshape(bn, N_HEADS * hh),
            state_mlstm_n[l].transpose(1, 0, 2),
            state_mlstm_m[l].T.reshape(N_HEADS, bn, 1),
            state_gdn_S[l].reshape(bn, N_HEADS * hh), state_gla_S[l].reshape(bn, N_HEADS * hh),
            norms, ex, tl)
        unhead = lambda a: a.transpose(1, 0, 2).reshape(bn, W_GROUP)
        ymix = jnp.concatenate([unhead(ysr[:, :, 0:HEAD_DIM]), unhead(ysr[:, :, HEAD_DIM:2 * HEAD_DIM]),
                                yc, unhead(ysr[:, :, 2 * HEAD_DIM:])], axis=-1)
        xs = _outproj_s(ymix, w_out[l], xs)
        st_shape = (bn, N_HEADS, HEAD_DIM, HEAD_DIM)
        for lst, v in zip(s_states, (c2.reshape(st_shape), n2.transpose(1, 0, 2),
                                     m2.reshape(N_HEADS, bn).T, sg2.reshape(st_shape),
                                     gbuf1.reshape(bn, 3, -1), sbuf1.reshape(bn, 2, -1),
                                     sl2.reshape(st_shape))):
            lst.append(v)

        j = l // 2
        if l % 2 == 0:
            tf = _pick_tile(ffn_w_gate.shape[2], (256, 128))
            xp = _ffn(xp, gf, ffn_w_gate[j].astype(BF16), ffn_w_up[j].astype(BF16),
                      ffn_w_down[j].astype(BF16), gfin, tm_f, tf, last)
            xs = _ffn_s(xs, gf, ffn_w_gate[j], ffn_w_up[j], ffn_w_down[j], gfin, tf, last)
        else:
            wr = jnp.zeros((D_MODEL, LANES), F32).at[:, :N_EXPERTS].set(moe_w_router[j])
            br = _pad_lanes(moe_b_router[j], 0)
            wgt, wup, wdn = (moe_w_gate[j].astype(BF16), moe_w_up[j].astype(BF16),
                             moe_w_down[j].astype(BF16))
            xp = _moe(xp, gf, wr, br, wgt, wup, wdn, gfin, tm_e, last, False)
            xs = _moe(xs, gf, wr, br, wgt, wup, wdn, gfin, bn, last, True)

    y_prompt = xp.reshape(bsz, seq, D_MODEL)
    y_sample = xs.reshape(bn, 1, D_MODEL)
    sp = [jnp.stack(v) for v in p_states]
    ss = [jnp.stack(v) for v in s_states]
    return (y_prompt, y_sample, sp[0], ss[0], sp[1], ss[1], sp[2], ss[2], sp[3], ss[3],
            sp[4], ss[4], sp[5], ss[5], sp[6], ss[6])
```

```python
import functools

import numpy as np
import jax
import jax.numpy as jnp
from jax import lax
from jax.experimental import pallas as pl
from jax.experimental.pallas import tpu as pltpu

F32 = jnp.float32
BF16 = jnp.bfloat16

D_MODEL = 1024
W_GROUP = 256
N_HEADS = 4
HEAD_DIM = 64
CHUNK = 64
GLA_RANK = 16
GLA_NORMALIZER = 16.0
N_EXPERTS = 8
EPS = 1e-6
Q_SCALE = HEAD_DIM ** -0.5

VMEM_LIMIT = 56 * 1024 * 1024
LANES = 128

SPLIT_SIZES = ([W_GROUP] * 4 + [N_HEADS] * 2 + [W_GROUP] * 4 + [N_HEADS] * 2 +
               [W_GROUP] * 3 + [W_GROUP] * 4 + [GLA_RANK])
_OFF = np.concatenate([[0], np.cumsum(SPLIT_SIZES)])
_BIG = [0, 1, 2, 3, 6, 7, 8, 9, 12, 13, 14, 15, 16, 17, 18]
_SMALL = [4, 5, 10, 11, 19]
_GATES = [4, 5, 10, 11]
MQ, MK, MV, MO = 0, 256, 512, 768
GQ, GK, GV, GZ = 1024, 1280, 1536, 1792
SB, SC, SH = 2048, 2304, 2560
LQ, LK, LV, LR = 2816, 3072, 3328, 3584
SM = 3840
NP = SM + LANES
N_SMALL = 32
N_GATE_ROWS = 16
N_ROW_KINDS = 3
GX_I, GX_B, GX_G, GX_BETA, GX_PM = 0, 256, 512, 768, 1024
GLA_LEVELS = (32, 16, 8, 4, 2)


def _cparams(sem):
    return pltpu.CompilerParams(dimension_semantics=sem, vmem_limit_bytes=VMEM_LIMIT)


def _log1pexp_negabs(x):
    return jnp.log(1.0 + jnp.exp(-jnp.abs(x)))


def _softplus(x):
    return jnp.maximum(x, 0.0) + _log1pexp_negabs(x)


def _logsigmoid(x):
    return -(jnp.maximum(-x, 0.0) + _log1pexp_negabs(x))


def _sigmoid(x):
    return 1.0 / (1.0 + jnp.exp(-x))


def _silu(x):
    return x * _sigmoid(x)


def _rms(x, g):
    return x * lax.rsqrt(jnp.mean(x * x, axis=-1, keepdims=True) + EPS) * g


def _dot(a, b):
    return jnp.dot(a, b, preferred_element_type=F32)


def _dot_nt(a, b):
    return lax.dot_general(a, b, (((1,), (1,)), ((), ())), preferred_element_type=F32)


def _dot_tn(a, b):
    return lax.dot_general(a, b, (((0,), (0,)), ((), ())), preferred_element_type=F32)


def _split2(x):
    x1 = x.astype(BF16)
    return x1, (x - x1.astype(F32)).astype(BF16)


def _split3(x):
    x1 = x.astype(BF16)
    r1 = x - x1.astype(F32)
    x2 = r1.astype(BF16)
    x3 = (r1 - x2.astype(F32)).astype(BF16)
    return x1, x2, x3


def _sel(m, x):
    x1, x2, x3 = _split3(x)
    return _dot(m, x1) + _dot(m, x2) + _dot(m, x3)


def _sel_r(x, m):
    n = x.shape[0]
    if n % 8:
        x1, x2, x3 = _split3(x)
        return _dot(x1, m) + _dot(x2, m) + _dot(x3, m)
    r = _dot(jnp.concatenate(_split3(x), axis=0), m)
    return r[0:n] + r[n:2 * n] + r[2 * n:3 * n]


def _sel_nt(m, x):
    x1, x2, x3 = _split3(x)
    return _dot_nt(m, x1) + _dot_nt(m, x2) + _dot_nt(m, x3)


def _mm3(x, w):
    n = x.shape[0]
    x1, x2 = _split2(x)
    w1, w2 = _split2(w)
    r = _dot(jnp.concatenate([x1, x2], axis=0), w1)
    return r[0:n] + r[n:] + _dot(x1, w2)


def _iota2(shape, dim):
    return lax.broadcasted_iota(jnp.int32, shape, dim)


def _inproj_kernel(x_ref, g_ref, w_ref, wst_ref, z_ref, zt_ref):
    hb = _rms(x_ref[...], g_ref[...]).astype(BF16)
    z_ref[...] = _dot(hb, w_ref[...])
    zt = _dot_nt(wst_ref[...], hb)
    n_blk, n_kind, n_chunks, _ = zt_ref.shape
    for b in range(n_blk):
        for k in range(n_kind):
            for c in range(n_chunks):
                t0 = (b * n_chunks + c) * CHUNK
                for h in range(N_HEADS):
                    zt_ref[b, k, c:c + 1, h * HEAD_DIM:(h + 1) * HEAD_DIM] = (
                        zt[k * 4 + h:k * 4 + h + 1, t0:t0 + CHUNK])


def _inproj(x, g, w, wst, tm, lb_rows):
    t = x.shape[0]
    n_chunks = lb_rows // CHUNK
    return pl.pallas_call(
        _inproj_kernel,
        grid=(t // tm,),
        in_specs=[pl.BlockSpec((tm, D_MODEL), lambda i: (i, 0)),
                  pl.BlockSpec((1, D_MODEL), lambda i: (0, 0)),
                  pl.BlockSpec((D_MODEL, NP), lambda i: (0, 0)),
                  pl.BlockSpec((N_GATE_ROWS, D_MODEL), lambda i: (0, 0))],
        out_specs=[pl.BlockSpec((tm, NP), lambda i: (i, 0)),
                   pl.BlockSpec((tm // lb_rows, N_ROW_KINDS, n_chunks, W_GROUP), lambda i: (i, 0, 0, 0))],
        out_shape=[jax.ShapeDtypeStruct((t, NP), F32),
                   jax.ShapeDtypeStruct((t // lb_rows, N_ROW_KINDS, n_chunks, W_GROUP), F32)],
        compiler_params=_cparams(("parallel",)),
        name="inproj",
    )(x, g, w, wst)


def _gate_act(pre, idx, neg_a):
    tail = _log1pexp_negabs(pre)
    lf = -(jnp.maximum(-pre, 0.0) + tail)
    dec = neg_a * (jnp.maximum(pre, 0.0) + tail)
    beta = _sigmoid(pre)
    return jnp.where(idx < 4, pre,
                     jnp.where(idx < 8, lf,
                               jnp.where(idx < 12, dec,
                                         jnp.where(idx < 16, beta, 0.0))))


def _mix_kernel(z_ref, zt_ref, bcol_ref, acol_ref, brow_ref, arow_ref, mnorm_ref, gnorm_ref,
                lnorm_ref, gcw_ref, scw_ref, wg_ref, bg_ref,
                bdmask_ref, bdtriu_ref, bdtril_ref, selexp_ref, selstack_ref, eye64_ref,
                y_ref, c_out, n_out, m_out, sg_out, gconv_out, sconv_out, sl_out,
                c_ref, n_ref, m_ref, sg_ref, slt_ref, xg_ref, xs_ref, qkv_ref, gcum_ref, glog_ref,
                gx_ref, rw_ref, *, lb_rows, nseq):
    lb = pl.program_id(1)
    nlb = pl.num_programs(1)
    n_chunks = lb_rows // CHUNK
    wd = W_GROUP
    seqs = range(nseq)

    @pl.when(lb == 0)
    def _():
        c_ref[...] = jnp.zeros_like(c_ref)
        n_ref[...] = jnp.zeros_like(n_ref)
        m_ref[...] = jnp.zeros_like(m_ref)
        sg_ref[...] = jnp.zeros_like(sg_ref)
        slt_ref[...] = jnp.zeros_like(slt_ref)
        for i in seqs:
            xg_ref[i, 0:8, :] = jnp.zeros((8, 3 * wd), F32)
            xs_ref[i, 0:8, :] = jnp.zeros((8, wd), F32)

    ti = _iota2((CHUNK, wd), 0)
    si = _iota2((CHUNK, wd), 1) % HEAD_DIM
    incl = ti >= si
    strict = ti > si
    eye = ti == si
    blk16 = (ti // 16) == (si // 16)
    blk32 = (ti // 32) == (si // 32)
    eye_f = eye.astype(F32)
    lvl_mask = {n: ((ti // (2 * n)) == (si // (2 * n))) & ((ti // n) > (si // n))
                for n in GLA_LEVELS + (1,)}
    bdmask = bdmask_ref[...]
    bdtriu = bdtriu_ref[...]
    bdtril = bdtril_ref[...]
    head_of_lane = _iota2((1, wd), 1) // HEAD_DIM
    pos_of_lane = _iota2((8, wd), 1) % HEAD_DIM

    def bd(x):
        xb = x.astype(BF16)
        return jnp.concatenate([xb, xb, xb, xb], axis=0) * bdmask

    def unbd(m):
        out = m[3 * HEAD_DIM:, :]
        for h in (2, 1, 0):
            out = jnp.where(head_of_lane == h, m[h * HEAD_DIM:(h + 1) * HEAD_DIM, :], out)
        return out

    def mm(a, bmat):
        return _dot(a.astype(BF16), bmat)

    def mm_nt(a, bmat):
        return _dot_nt(a.astype(BF16), bmat)

    def mm3_bd(a, b):
        a1, a2 = _split2(a)
        b1, b2 = _split2(b)
        r = _dot(jnp.concatenate([a1, a2], axis=0), bd(b1))
        return r[0:CHUNK] + r[CHUNK:] + _dot(a1, bd(b2))

    def hsum(x):
        x1, x2 = _split2(x)
        r = _dot(jnp.concatenate([x1, x2], axis=0), bdmask)
        return r[0:x.shape[0]] + r[x.shape[0]:]

    def headnorm(o, w):
        return o * lax.rsqrt(hsum(o * o) * (1.0 / HEAD_DIM) + EPS) * w

    lane = _iota2((1, LANES), 1)
    neg_a_col = -jnp.exp(acol_ref[...])
    neg_a_row = -jnp.exp(arow_ref[2:3, :])
    selexp = selexp_ref[...]
    sel_stack = selstack_ref[...]
    eye64 = eye64_ref[...]

    def l2n(x):
        return x * lax.rsqrt(hsum(x * x) + EPS)

    for i in seqs:
        small = z_ref[i, :, SM:SM + LANES]
        act = _gate_act(small + bcol_ref[...], lane, neg_a_col)
        gx4 = _sel_r(act, selexp)
        gx_ref[i, :, GX_I:GX_I + wd] = gx4[:, 0:wd]
        gx_ref[i, :, GX_BETA:GX_BETA + wd] = gx4[:, 3 * wd:]
        gx_ref[i, :, GX_B:GX_B + 2 * wd] = _sel(bdtril, gx4[:, wd:3 * wd])
        glin = _dot(small.astype(BF16), wg_ref[...]) + bg_ref[...]
        glog = _logsigmoid(glin) * (1.0 / GLA_NORMALIZER)
        glog_ref[i] = glog
        gcum_ref[i] = _sel(bdtril, glog)

        ic_r = zt_ref[i, 0, 0] + brow_ref[0:1, :]
        lf_r = _logsigmoid(zt_ref[i, 0, 1] + brow_ref[1:2, :])
        dec_r = neg_a_row * _softplus(zt_ref[i, 0, 2] + brow_ref[2:3, :])
        cs_r = _sel_r(jnp.concatenate([lf_r, dec_r], axis=0), bdtriu)
        c_rows = ic_r - cs_r[0:n_chunks, :]
        g_rows = cs_r[n_chunks:, :]
        pm = jnp.concatenate([c_rows] * (8 // n_chunks), axis=0)
        for sh in (1, 2, 4, 8, 16, 32):
            pm = jnp.where(pos_of_lane >= sh, jnp.maximum(pm, pltpu.roll(pm, sh, axis=1)), pm)
        for c in range(n_chunks):
            rw_ref[i, c, 0:1, :] = c_rows[c:c + 1, :]
            rw_ref[i, c, 1:2, :] = g_rows[c:c + 1, :]
        diag = jnp.concatenate(
            [jnp.where(eye, jnp.broadcast_to(pm[c:c + 1, :], (CHUNK, wd)), 0.0)
             for c in range(n_chunks)], axis=0)
        gx_ref[i, :, GX_PM:GX_PM + wd] = _sel_r(diag, bdmask)

        xg_ref[i, 8:, :] = z_ref[i, :, GQ:GQ + 3 * wd]
        conv = xg_ref[i, 5:5 + lb_rows, :] * gcw_ref[0:1, :]
        for j in range(1, 4):
            conv = conv + xg_ref[i, 5 + j:5 + j + lb_rows, :] * gcw_ref[j:j + 1, :]
        conv = _silu(conv)
        qkv_ref[i, :, 0:wd] = l2n(conv[:, 0:wd]) * Q_SCALE
        qkv_ref[i, :, wd:2 * wd] = l2n(conv[:, wd:2 * wd])
        qkv_ref[i, :, 2 * wd:] = conv[:, 2 * wd:]

        xs_ref[i, 8:, :] = z_ref[i, :, SC:SC + wd] * z_ref[i, :, SH:SH + wd]
        cu = xs_ref[i, 6:6 + lb_rows, :] * scw_ref[0:1, :]
        for j in range(1, 3):
            cu = cu + xs_ref[i, 6 + j:6 + j + lb_rows, :] * scw_ref[j:j + 1, :]
        y_ref[i, :, 2 * wd:3 * wd] = (z_ref[i, :, SB:SB + wd] * cu).astype(y_ref.dtype)

    def chunk(c, carry):
        r0 = pl.multiple_of(c * CHUNK, CHUNK)
        rs = pl.ds(r0, CHUNK)
        each = lambda f, *xs: [f(*a) for a in zip(*xs)]
        zcol = lambda o: [z_ref[i, rs, o:o + wd] for i in seqs]
        gxcol = lambda o: [gx_ref[i, rs, o:o + wd] for i in seqs]
        last = lambda xs: [x[CHUNK - 1:CHUNK, :] for x in xs]
        tn_bd = lambda a, b: each(lambda x, y: unbd(_dot_tn(x.astype(BF16), y.astype(BF16))), a, b)
        mm3s = lambda a, b: each(mm3_bd, a, b)
        rw = [rw_ref[i, c, 0:2, :] for i in seqs]

        q = zcol(MQ)
        k = [x * Q_SCALE for x in zcol(MK)]
        v = zcol(MV)
        ic_e, b_e, pm_e = gxcol(GX_I), gxcol(GX_B), gxcol(GX_PM)
        m_old = [m_ref[i, 0:1, :] for i in seqs]
        cmat = [c_ref[i] for i in seqs]
        nrow = [n_ref[i, 0:1, :] for i in seqs]
        mx = each(jnp.maximum, m_old, pm_e)
        w_int = each(lambda a, b: jnp.exp(a - b), m_old, mx)
        p = each(lambda r, m: jnp.where(incl, jnp.exp(r[0:1, :] - m), 0.0), rw, mx)
        s = each(lambda a, b, c_: mm_nt(a, bd(b)) * c_, q, k, p)
        num = each(lambda w_, a, cm, s_, v_: w_ * mm(a, bd(cm)) + mm(s_, bd(v_)), w_int, q, cmat, s, v)
        den = each(lambda w_, a, n_, s_: hsum(w_ * (a * n_) + s_), w_int, q, nrow, s)
        hh = each(lambda n_, d_, b_, m: n_ / jnp.maximum(jnp.abs(d_), jnp.exp(-(b_ + m))), num, den, b_e, mx)
        mx_last = last(mx)
        wgk = each(lambda i_, b_, ml, k_: jnp.exp((i_ - b_) - ml) * k_, ic_e, b_e, mx_last, k)
        dec = each(lambda a, b: jnp.exp(a - b), m_old, mx_last)
        c_new = each(lambda d_, cm, u_: d_ * cm + u_, dec, cmat, tn_bd(wgk, v))
        for i in seqs:
            c_ref[i] = c_new[i]
            n_ref[i] = jnp.broadcast_to(dec[i] * nrow[i] + jnp.sum(wgk[i], axis=0, keepdims=True), (8, wd))
            m_ref[i] = jnp.broadcast_to(b_e[i][CHUNK - 1:CHUNK, :] + mx_last[i], (8, wd))
        y_a = each(lambda o_, h_: _sigmoid(o_) * headnorm(h_, mnorm_ref[...]), zcol(MO), hh)
        for i in seqs:
            y_ref[i, rs, 0:wd] = y_a[i].astype(y_ref.dtype)

        q = [qkv_ref[i, rs, 0:wd] for i in seqs]
        k = [qkv_ref[i, rs, wd:2 * wd] for i in seqs]
        v = [qkv_ref[i, rs, 2 * wd:] for i in seqs]
        g_e, beta = gxcol(GX_G), gxcol(GX_BETA)
        s_old = [sg_ref[i] for i in seqs]
        bdk = each(bd, k)
        lw = each(lambda g_, r: jnp.where(incl, jnp.exp(g_ - r[1:2, :]), 0.0), g_e, rw)
        amat = each(lambda b_, k_, bk, l_: jnp.where(strict, b_ * mm_nt(k_, bk) * l_, 0.0), beta, k, bdk, lw)
        egc = each(jnp.exp, g_e)
        ad = [jnp.where(blk16, a, 0.0) for a in amat]
        a2 = mm3s(ad, ad)
        a4 = mm3s(a2, a2)
        a8 = mm3s(a4, a4)
        tinv = [eye_f - a for a in ad]
        for apow in (a2, a4, a8):
            tinv = each(lambda t_, u_: t_ + u_, tinv, mm3s(tinv, apow))
        n1 = [jnp.where(blk32 & jnp.logical_not(blk16), a, 0.0) for a in amat]
        tinv = each(lambda t_, u_: t_ - u_, tinv, mm3s(mm3s(tinv, n1), tinv))
        n2 = [jnp.where(blk32, 0.0, a) for a in amat]
        tinv = each(lambda t_, u_: t_ - u_, tinv, mm3s(mm3s(tinv, n2), tinv))
        u = mm3s(tinv, each(lambda b_, v_: b_ * v_, beta, v))
        w = mm3s(tinv, each(lambda b_, e_, k_: (b_ * e_) * k_, beta, egc, k))
        bds = each(bd, s_old)
        vn = each(lambda u_, w_, bs: u_ - mm(w_, bs), u, w, bds)
        qk = each(lambda q_, bk, l_: mm_nt(q_, bk) * l_, q, bdk, lw)
        o = each(lambda q_, e_, bs, qk_, vn_: mm(q_ * e_, bs) + mm(qk_, bd(vn_)), q, egc, bds, qk, vn)
        g_last = last(g_e)
        kdec = each(lambda k_, gl, g_: k_ * jnp.exp(gl - g_), k, g_last, g_e)
        s_new = each(lambda gl, so, u_: jnp.exp(gl) * so + u_, g_last, s_old, tn_bd(kdec, vn))
        y_b = each(lambda o_, z_: headnorm(o_, gnorm_ref[...]) * _silu(z_), o, zcol(GZ))
        for i in seqs:
            sg_ref[i] = s_new[i]
            y_ref[i, rs, wd:2 * wd] = y_b[i].astype(y_ref.dtype)

        q = [x * Q_SCALE for x in zcol(LQ)]
        k = zcol(LK)
        v = zcol(LV)
        gk = [glog_ref[i, rs, :] for i in seqs]
        gcum = [gcum_ref[i, rs, :] for i in seqs]
        stack = [_sel(sel_stack, g_) for g_ in gcum]
        st_old = [slt_ref[i] for i in seqs]
        bdk = each(bd, k)
        amat = each(lambda q_, bk: jnp.where(eye, mm_nt(q_, bk), 0.0), q, bdk)
        amat = each(lambda a_, q_, g_, bk: a_ + jnp.where(lvl_mask[1], mm_nt(q_ * jnp.exp(g_), bk), 0.0),
                    amat, q, gk, bdk)
        for j, n in enumerate(GLA_LEVELS):
            g_end = [s_[2 * j * CHUNK:(2 * j + 1) * CHUNK, :] for s_ in stack]
            g_prev = [s_[(2 * j + 1) * CHUNK:(2 * j + 2) * CHUNK, :] for s_ in stack]
            qn = each(lambda q_, gc, gp: q_ * jnp.exp(gc - gp), q, gcum, g_prev)
            kn = each(lambda k_, ge, gc: k_ * jnp.exp(ge - gc), k, g_end, gcum)
            amat = each(lambda a_, q_, k_, m=lvl_mask[n]: a_ + jnp.where(m, mm_nt(q_, bd(k_)), 0.0),
                        amat, qn, kn)
        g_last = last(gcum)
        o = each(lambda q_, gc, st, a_, v_: mm_nt(q_ * jnp.exp(gc), bd(st)) + mm(a_, bd(v_)),
                 q, gcum, st_old, amat, v)
        kdec = each(lambda k_, gl, gc: k_ * jnp.exp(gl - gc), k, g_last, gcum)
        st_new = each(lambda st, gl, u_: st * jnp.exp(gl) + u_, st_old, g_last, tn_bd(v, kdec))
        y_d = each(lambda o_, z_: headnorm(o_, lnorm_ref[...]) * _silu(z_), o, zcol(LR))
        for i in seqs:
            slt_ref[i] = st_new[i]
            y_ref[i, rs, 3 * wd:] = y_d[i].astype(y_ref.dtype)
        return carry

    lax.fori_loop(0, n_chunks, chunk, 0)

    gtail = [xg_ref[i, lb_rows:lb_rows + 8, :] for i in seqs]
    stail = [xs_ref[i, lb_rows:lb_rows + 8, :] for i in seqs]
    for i in seqs:
        xg_ref[i, 0:8, :] = gtail[i]
        xs_ref[i, 0:8, :] = stail[i]

    @pl.when(lb == nlb - 1)
    def _():
        lane4 = _iota2((1, N_HEADS), 1)
        for i in seqs:
            gconv_out[i] = gtail[i][5:8, :]
            sconv_out[i] = stail[i][6:8, :]
            m_row = jnp.zeros((1, N_HEADS), F32)
            for h in range(N_HEADS):
                hs = slice(h * HEAD_DIM, (h + 1) * HEAD_DIM)
                c_out[i, h] = c_ref[i, :, hs]
                n_out[i, h:h + 1, :] = n_ref[i, 0:1, hs]
                m_row = jnp.where(lane4 == h, m_ref[i, 0:1, h * HEAD_DIM:h * HEAD_DIM + 1], m_row)
                sg_out[i, h] = sg_ref[i, :, hs]
                sl_out[i, h] = _sel_nt(eye64, slt_ref[i, :, hs])
            m_out[i] = m_row


def _mix_consts(lb_rows):
    wd = W_GROUP
    r, c = np.indices((wd, wd))
    same_head = (r // HEAD_DIM) == (c // HEAD_DIM)
    bdmask = same_head
    bdtriu = same_head & ((r % HEAD_DIM) <= (c % HEAD_DIM))
    rl, cl = np.indices((lb_rows, lb_rows))
    bdtril = ((rl // CHUNK) == (cl // CHUNK)) & (rl >= cl)
    rr, cc = np.indices((LANES, 4 * wd))
    selexp = rr == 4 * (cc // wd) + (cc % wd) // HEAD_DIM
    rows, cols = np.indices((2 * len(GLA_LEVELS) * CHUNK, CHUNK))
    r_in, blk = rows % CHUNK, rows // CHUNK
    sel_stack = np.zeros(rows.shape, bool)
    for i, n in enumerate(GLA_LEVELS):
        sel_stack |= (blk == 2 * i) & (cols == (r_in // n) * n + (n - 1))
        sel_stack |= (blk == 2 * i + 1) & (cols == (r_in // n) * n - 1)
    eye64 = np.eye(CHUNK, dtype=bool)
    return [jnp.asarray(m, dtype=BF16) for m in (bdmask, bdtriu, bdtril, selexp, sel_stack, eye64)]


def _mix_prompt(z, zt, p, bsz, seq, lb_rows, nseq):
    nlb = seq // lb_rows
    n_chunks = lb_rows // CHUNK
    row = lambda b, l: (b, l, 0)
    const2 = lambda b, l: (0, 0)
    st4 = lambda b, l: (b, 0, 0, 0)
    st3 = lambda b, l: (b, 0, 0)
    params = [p["bcol"], p["acol"], p["brow"], p["arow"], p["mnorm"], p["gnorm"], p["lnorm"],
              p["gcw"], p["scw"], p["wg"], p["bg"]] + _mix_consts(lb_rows)
    in_specs = [pl.BlockSpec((nseq, lb_rows, NP), row),
                pl.BlockSpec((nseq, 1, N_ROW_KINDS, n_chunks, W_GROUP), lambda b, l: (b, l, 0, 0, 0))]
    in_specs += [pl.BlockSpec(a.shape, const2) for a in params]
    hd = (N_HEADS, HEAD_DIM, HEAD_DIM)
    out_shape = [jax.ShapeDtypeStruct((bsz, seq, D_MODEL), BF16),
                 jax.ShapeDtypeStruct((bsz,) + hd, F32),
                 jax.ShapeDtypeStruct((bsz, N_HEADS, HEAD_DIM), F32),
                 jax.ShapeDtypeStruct((bsz, 1, N_HEADS), F32),
                 jax.ShapeDtypeStruct((bsz,) + hd, F32),
                 jax.ShapeDtypeStruct((bsz, 3, 3 * W_GROUP), F32),
                 jax.ShapeDtypeStruct((bsz, 2, W_GROUP), F32),
                 jax.ShapeDtypeStruct((bsz,) + hd, F32)]
    out_specs = [pl.BlockSpec((nseq, lb_rows, D_MODEL), row),
                 pl.BlockSpec((nseq,) + hd, st4),
                 pl.BlockSpec((nseq, N_HEADS, HEAD_DIM), st3),
                 pl.BlockSpec((nseq, 1, N_HEADS), st3),
                 pl.BlockSpec((nseq,) + hd, st4),
                 pl.BlockSpec((nseq, 3, 3 * W_GROUP), st3),
                 pl.BlockSpec((nseq, 2, W_GROUP), st3),
                 pl.BlockSpec((nseq,) + hd, st4)]
    vm = lambda *shape: pltpu.VMEM((nseq,) + shape, F32)
    scratch = [vm(HEAD_DIM, W_GROUP),
               vm(8, W_GROUP),
               vm(8, W_GROUP),
               vm(HEAD_DIM, W_GROUP),
               vm(HEAD_DIM, W_GROUP),
               vm(lb_rows + 8, 3 * W_GROUP),
               vm(lb_rows + 8, W_GROUP),
               vm(lb_rows, 3 * W_GROUP),
               vm(lb_rows, W_GROUP),
               vm(lb_rows, W_GROUP),
               vm(lb_rows, 5 * W_GROUP),
               vm(n_chunks, 8, W_GROUP)]
    return pl.pallas_call(
        functools.partial(_mix_kernel, lb_rows=lb_rows, nseq=nseq),
        grid=(bsz // nseq, nlb),
        in_specs=in_specs,
        out_specs=out_specs,
        out_shape=out_shape,
        scratch_shapes=scratch,
        compiler_params=_cparams(("parallel", "arbitrary")),
        name="mix_prompt",
    )(z, zt, *params)


def _inproj_s_kernel(x_ref, g_ref, w_ref, z_ref):
    h = _rms(x_ref[...], g_ref[...])
    n = w_ref.shape[1]
    step = 4 * LANES
    for j in range(0, n, step):
        cs = slice(j, min(j + step, n))
        z_ref[:, cs] = _mm3(h, w_ref[:, cs])


def _inproj_s(x, g, w):
    return pl.pallas_call(
        _inproj_s_kernel,
        out_shape=jax.ShapeDtypeStruct((x.shape[0], w.shape[1]), F32),
        compiler_params=pltpu.CompilerParams(vmem_limit_bytes=VMEM_LIMIT),
        name="inproj_s",
    )(x, g, w)


def _spre_kernel(z_ref, gbuf_ref, sbuf_ref, bcol_ref, acol_ref, gcw_ref, scw_ref, wg_ref, bg_ref,
                 qkv_ref, glog_ref, act_ref, yc_ref, gbuf_out, sbuf_out):
    small = z_ref[:, SM:SM + LANES]
    lane = _iota2((1, LANES), 1)
    act_ref[...] = _gate_act(small + bcol_ref[...], lane, -jnp.exp(acol_ref[...]))
    glin = _mm3(small, wg_ref[...]) + bg_ref[...]
    glog_ref[...] = _logsigmoid(glin) * (1.0 / GLA_NORMALIZER)

    wq = 3 * W_GROUP
    u = z_ref[:, GQ:GQ + wq]
    b0, b1, b2 = gbuf_ref[:, 0:wq], gbuf_ref[:, wq:2 * wq], gbuf_ref[:, 2 * wq:]
    conv = b0 * gcw_ref[0:1, :] + b1 * gcw_ref[1:2, :] + b2 * gcw_ref[2:3, :] + u * gcw_ref[3:4, :]
    conv = _silu(conv)
    gbuf_out[:, 0:wq] = b1
    gbuf_out[:, wq:2 * wq] = b2
    gbuf_out[:, 2 * wq:] = u
    hsum = (_iota2((W_GROUP, W_GROUP), 0) // HEAD_DIM ==
            _iota2((W_GROUP, W_GROUP), 1) // HEAD_DIM).astype(BF16)

    def l2n(x):
        return x * lax.rsqrt(_sel_r(x * x, hsum) + EPS)

    qkv_ref[:, 0:W_GROUP] = l2n(conv[:, 0:W_GROUP]) * Q_SCALE
    qkv_ref[:, W_GROUP:2 * W_GROUP] = l2n(conv[:, W_GROUP:2 * W_GROUP])
    qkv_ref[:, 2 * W_GROUP:] = conv[:, 2 * W_GROUP:]

    u2 = z_ref[:, SC:SC + W_GROUP] * z_ref[:, SH:SH + W_GROUP]
    s0, s1 = sbuf_ref[:, 0:W_GROUP], sbuf_ref[:, W_GROUP:]
    cu = s0 * scw_ref[0:1, :] + s1 * scw_ref[1:2, :] + u2 * scw_ref[2:3, :]
    yc_ref[...] = z_ref[:, SB:SB + W_GROUP] * cu
    sbuf_out[:, 0:W_GROUP] = s1
    sbuf_out[:, W_GROUP:] = u2


def _sample_pre(z, gbuf, sbuf, p):
    bn = z.shape[0]
    sd = lambda s: jax.ShapeDtypeStruct(s, F32)
    return pl.pallas_call(
        _spre_kernel,
        out_shape=[sd((bn, 3 * W_GROUP)), sd((bn, W_GROUP)), sd((bn, LANES)), sd((bn, W_GROUP)),
                   sd(gbuf.shape), sd(sbuf.shape)],
        compiler_params=pltpu.CompilerParams(vmem_limit_bytes=VMEM_LIMIT),
        name="sample_pre",
    )(z, gbuf, sbuf, p["bcol"], p["acol"], p["gcw"], p["scw"], p["wg32"], p["bg"])


ZH = 14 * HEAD_DIM


def _srec_kernel(zh_ref, c_ref, n_ref, m_ref, sg_ref, sl_ref, norm_ref, ex_ref, tl_ref,
                 y_ref, c_out, n_out, m_out, sg_out, sl_out):
    zh = zh_ref[0]
    col = lambda j: zh[:, j * HEAD_DIM:(j + 1) * HEAD_DIM]
    mq, mk, mv, mo = col(0), col(1) * Q_SCALE, col(2), col(3)
    gq, gk, gv, gz = col(4), col(5), col(6), col(7)
    lq, lk, lv, lr = col(8) * Q_SCALE, col(9), col(10), col(11)
    glog = col(12)
    gates = col(13)
    ic, lf, gdec, beta = gates[:, 0:1], gates[:, 1:2], gates[:, 2:3], gates[:, 3:4]
    ex = ex_ref[...]
    tl = tl_ref[...]
    expand = lambda x: _sel_r(x, ex)
    tile = lambda x: _sel_r(x, tl)

    def reduce_d(pmat):
        acc = pmat[:, 0:LANES]
        for j in range(1, HEAD_DIM * HEAD_DIM // LANES):
            acc = acc + pmat[:, j * LANES:(j + 1) * LANES]
        return acc[:, :HEAD_DIM] + acc[:, HEAD_DIM:]

    def headnorm(o, w):
        return o * lax.rsqrt(jnp.mean(o * o, axis=-1, keepdims=True) + EPS) * w

    rowsum = lambda x: jnp.sum(x, axis=-1, keepdims=True)
    nw = norm_ref[0]

    cmat = c_ref[...]
    nvec = n_ref[0]
    m_old = m_ref[0]
    a = lf + m_old
    mt = jnp.maximum(a, ic)
    w_int = jnp.exp(a - mt)
    e_i = jnp.exp(ic - mt)
    s = rowsum(mq * mk) * e_i
    num = w_int * reduce_d(expand(mq) * cmat) + s * mv
    den = w_int * rowsum(mq * nvec) + s
    hh = num / jnp.maximum(jnp.abs(den), jnp.exp(-mt))
    c_out[...] = w_int * cmat + e_i * (expand(mk) * tile(mv))
    n_out[0] = w_int * nvec + e_i * mk
    m_out[0] = mt
    y_a = _sigmoid(mo) * headnorm(hh, nw[:, 0:HEAD_DIM])

    smat = sg_ref[...]
    eg = jnp.exp(gdec)
    kexp = expand(gk)
    vn = beta * gv - (beta * eg) * reduce_d(kexp * smat)
    o = eg * reduce_d(expand(gq) * smat) + rowsum(gq * gk) * vn
    sg_out[...] = eg * smat + kexp * tile(vn)
    y_b = headnorm(o, nw[:, HEAD_DIM:2 * HEAD_DIM]) * _silu(gz)

    lmat = sl_ref[...]
    egk = jnp.exp(glog)
    o = reduce_d(expand(lq * egk) * lmat) + rowsum(lq * lk) * lv
    sl_out[...] = expand(egk) * lmat + expand(lk) * tile(lv)
    y_d = headnorm(o, nw[:, 2 * HEAD_DIM:]) * _silu(lr)

    y_ref[0] = jnp.concatenate([y_a, y_b, y_d], axis=1)


def _sample_rec(zh, cst, nst, mst, sgst, slst, norms, ex, tl):
    bn = zh.shape[1]
    hh = HEAD_DIM * HEAD_DIM
    big = pl.BlockSpec((bn, hh), lambda h: (0, h))
    per3 = lambda w: pl.BlockSpec((1, bn, w), lambda h: (h, 0, 0))
    const = lambda a: pl.BlockSpec(a.shape, lambda h: (0, 0))
    sd = lambda s: jax.ShapeDtypeStruct(s, F32)
    return pl.pallas_call(
        _srec_kernel,
        grid=(N_HEADS,),
        in_specs=[per3(ZH), big, per3(HEAD_DIM), per3(1), big, big,
                  pl.BlockSpec((1, 1, 3 * HEAD_DIM), lambda h: (h, 0, 0)), const(ex), const(tl)],
        out_specs=[per3(3 * HEAD_DIM), big, per3(HEAD_DIM), per3(1), big, big],
        out_shape=[sd((N_HEADS, bn, 3 * HEAD_DIM)), sd(cst.shape), sd(nst.shape), sd(mst.shape),
                   sd(sgst.shape), sd(slst.shape)],
        compiler_params=_cparams(("parallel",)),
        name="sample_rec",
    )(zh, cst, nst, mst, sgst, slst, norms, ex, tl)


def _outproj_s_kernel(y_ref, w_ref, x_ref, o_ref):
    o_ref[...] = x_ref[...] + _mm3(y_ref[...], w_ref[...])


def _outproj_s(y, w, x):
    return pl.pallas_call(
        _outproj_s_kernel,
        out_shape=jax.ShapeDtypeStruct(x.shape, F32),
        compiler_params=pltpu.CompilerParams(vmem_limit_bytes=VMEM_LIMIT),
        name="outproj_s",
    )(y, w, x)


def _ffn_s_kernel(x_ref, g_ref, wg_ref, wu_ref, wd_ref, gf_ref, o_ref, h_ref, acc_ref, *, final_norm):
    j = pl.program_id(0)

    @pl.when(j == 0)
    def _():
        h_ref[...] = _rms(x_ref[...], g_ref[...])
        acc_ref[...] = jnp.zeros_like(acc_ref)

    h = h_ref[...]
    acc_ref[...] += _mm3(_silu(_mm3(h, wg_ref[...])) * _mm3(h, wu_ref[...]), wd_ref[...])

    @pl.when(j == pl.num_programs(0) - 1)
    def _():
        out = x_ref[...] + acc_ref[...]
        o_ref[...] = _rms(out, gf_ref[...]) if final_norm else out


def _ffn_s(x, g, wg, wu, wd, gf, tf, final_norm):
    t = x.shape[0]
    dff = wg.shape[1]
    full = lambda j: (0, 0)
    return pl.pallas_call(
        functools.partial(_ffn_s_kernel, final_norm=final_norm),
        grid=(dff // tf,),
        in_specs=[pl.BlockSpec((t, D_MODEL), full),
                  pl.BlockSpec((1, D_MODEL), full),
                  pl.BlockSpec((D_MODEL, tf), lambda j: (0, j)),
                  pl.BlockSpec((D_MODEL, tf), lambda j: (0, j)),
                  pl.BlockSpec((tf, D_MODEL), lambda j: (j, 0)),
                  pl.BlockSpec((1, D_MODEL), full)],
        out_specs=pl.BlockSpec((t, D_MODEL), full),
        out_shape=jax.ShapeDtypeStruct((t, D_MODEL), F32),
        scratch_shapes=[pltpu.VMEM((t, D_MODEL), F32), pltpu.VMEM((t, D_MODEL), F32)],
        compiler_params=_cparams(("arbitrary",)),
        name="ffn_s",
    )(x, g, wg, wu, wd, gf)


def _outproj_kernel(y_ref, w_ref, x_ref, o_ref):
    o_ref[...] = x_ref[...] + _dot(y_ref[...], w_ref[...])


def _outproj(y, w, x, tm):
    t = x.shape[0]
    return pl.pallas_call(
        _outproj_kernel,
        grid=(t // tm,),
        in_specs=[pl.BlockSpec((tm, D_MODEL), lambda i: (i, 0)),
                  pl.BlockSpec((D_MODEL, D_MODEL), lambda i: (0, 0)),
                  pl.BlockSpec((tm, D_MODEL), lambda i: (i, 0))],
        out_specs=pl.BlockSpec((tm, D_MODEL), lambda i: (i, 0)),
        out_shape=jax.ShapeDtypeStruct((t, D_MODEL), F32),
        compiler_params=_cparams(("parallel",)),
        name="outproj",
    )(y, w, x)


def _ffn_kernel(x_ref, g_ref, wg_ref, wu_ref, wd_ref, gf_ref, o_ref, h_ref, acc_ref, *, final_norm):
    j = pl.program_id(1)

    @pl.when(j == 0)
    def _():
        h_ref[...] = _rms(x_ref[...], g_ref[...]).astype(BF16)
        acc_ref[...] = jnp.zeros_like(acc_ref)

    h = h_ref[...]
    acc_ref[...] += _dot((_silu(_dot(h, wg_ref[...])) * _dot(h, wu_ref[...])).astype(BF16), wd_ref[...])

    @pl.when(j == pl.num_programs(1) - 1)
    def _():
        out = x_ref[...] + acc_ref[...]
        o_ref[...] = _rms(out, gf_ref[...]) if final_norm else out


def _ffn(x, g, wg, wu, wd, gf, tm, tf, final_norm):
    t = x.shape[0]
    dff = wg.shape[1]
    return pl.pallas_call(
        functools.partial(_ffn_kernel, final_norm=final_norm),
        grid=(t // tm, dff // tf),
        in_specs=[pl.BlockSpec((tm, D_MODEL), lambda i, j: (i, 0)),
                  pl.BlockSpec((1, D_MODEL), lambda i, j: (0, 0)),
                  pl.BlockSpec((D_MODEL, tf), lambda i, j: (0, j)),
                  pl.BlockSpec((D_MODEL, tf), lambda i, j: (0, j)),
                  pl.BlockSpec((tf, D_MODEL), lambda i, j: (j, 0)),
                  pl.BlockSpec((1, D_MODEL), lambda i, j: (0, 0))],
        out_specs=pl.BlockSpec((tm, D_MODEL), lambda i, j: (i, 0)),
        out_shape=jax.ShapeDtypeStruct((t, D_MODEL), F32),
        scratch_shapes=[pltpu.VMEM((tm, D_MODEL), BF16), pltpu.VMEM((tm, D_MODEL), F32)],
        compiler_params=_cparams(("parallel", "arbitrary")),
        name="ffn",
    )(x, g, wg, wu, wd, gf)


def _top2_gates(logits):
    lane = _iota2((1, LANES), 1)
    valid = lane < N_EXPERTS
    logits = jnp.where(valid, logits, -jnp.inf)
    ex = jnp.exp(logits - jnp.max(logits, axis=-1, keepdims=True))
    probs = ex / jnp.sum(ex, axis=-1, keepdims=True)
    v1 = jnp.max(probs, axis=-1, keepdims=True)
    i1 = jnp.min(jnp.where(probs == v1, lane, LANES), axis=-1, keepdims=True)
    rest = jnp.where((lane == i1) | jnp.logical_not(valid), -1.0, probs)
    v2 = jnp.max(rest, axis=-1, keepdims=True)
    i2 = jnp.min(jnp.where(rest == v2, lane, LANES), axis=-1, keepdims=True)
    tot = v1 + v2
    return jnp.where(lane == i1, v1 / tot, 0.0) + jnp.where(lane == i2, v2 / tot, 0.0)


def _moe_kernel(x_ref, g_ref, wr_ref, br_ref, wg_ref, wu_ref, wd_ref, gf_ref, o_ref,
                h_ref, acc_ref, gates_ref, *, final_norm, precise_router):
    e = pl.program_id(1)

    @pl.when(e == 0)
    def _():
        h = _rms(x_ref[...], g_ref[...])
        h_ref[...] = h.astype(BF16)
        acc_ref[...] = jnp.zeros_like(acc_ref)
        if precise_router:
            logits = _mm3(h, wr_ref[...])
        else:
            logits = _dot(h.astype(BF16), wr_ref[...].astype(BF16))
        gates_ref[...] = _top2_gates(logits + br_ref[...])

    h = h_ref[...]
    y = _dot((_silu(_dot(h, wg_ref[0])) * _dot(h, wu_ref[0])).astype(BF16), wd_ref[0])
    lane = _iota2((1, LANES), 1)
    ge = jnp.sum(jnp.where(lane == e, gates_ref[...], 0.0), axis=-1, keepdims=True)
    acc_ref[...] += ge * y

    @pl.when(e == pl.num_programs(1) - 1)
    def _():
        out = x_ref[...] + acc_ref[...]
        o_ref[...] = _rms(out, gf_ref[...]) if final_norm else out


def _moe(x, g, wr, br, wg, wu, wd, gf, tm, final_norm, precise_router):
    t = x.shape[0]
    fe = wg.shape[2]
    return pl.pallas_call(
        functools.partial(_moe_kernel, final_norm=final_norm, precise_router=precise_router),
        grid=(t // tm, N_EXPERTS),
        in_specs=[pl.BlockSpec((tm, D_MODEL), lambda i, e: (i, 0)),
                  pl.BlockSpec((1, D_MODEL), lambda i, e: (0, 0)),
                  pl.BlockSpec((D_MODEL, LANES), lambda i, e: (0, 0)),
                  pl.BlockSpec((1, LANES), lambda i, e: (0, 0)),
                  pl.BlockSpec((1, D_MODEL, fe), lambda i, e: (e, 0, 0)),
                  pl.BlockSpec((1, D_MODEL, fe), lambda i, e: (e, 0, 0)),
                  pl.BlockSpec((1, fe, D_MODEL), lambda i, e: (e, 0, 0)),
                  pl.BlockSpec((1, D_MODEL), lambda i, e: (0, 0))],
        out_specs=pl.BlockSpec((tm, D_MODEL), lambda i, e: (i, 0)),
        out_shape=jax.ShapeDtypeStruct((t, D_MODEL), F32),
        scratch_shapes=[pltpu.VMEM((tm, D_MODEL), BF16), pltpu.VMEM((tm, D_MODEL), F32),
                        pltpu.VMEM((tm, LANES), F32)],
        compiler_params=_cparams(("parallel", "arbitrary")),
        name="moe",
    )(x, g, wr, br, wg, wu, wd, gf)


def _prep_w_in(w):
    seg = lambda i: w[:, _OFF[i]:_OFF[i + 1]]
    small = jnp.concatenate([seg(i) for i in _SMALL], axis=1)
    pad = jnp.zeros((w.shape[0], LANES - N_SMALL), w.dtype)
    wfull = jnp.concatenate([seg(i) for i in _BIG] + [small, pad], axis=1).astype(BF16)
    gates = jnp.concatenate([seg(i) for i in _GATES], axis=1)
    return wfull, gates.T.astype(BF16)


def _regroup_z(z):
    seg = lambda i: z[:, _OFF[i]:_OFF[i + 1]]
    pad = jnp.zeros((z.shape[0], LANES - N_SMALL), z.dtype)
    return jnp.concatenate([seg(i) for i in _BIG] + [seg(i) for i in _SMALL] + [pad], axis=1)


def _pad_lanes(v, start):
    out = jnp.zeros((1, LANES), F32)
    return lax.dynamic_update_slice(out, v.reshape(1, -1).astype(F32), (0, start))


def _layer_params(l, mlstm_b_i, mlstm_b_f, mlstm_norm, gdn_conv_w, gdn_a_log, gdn_dt_bias, gdn_norm,
                  sc_conv_w, gla_w_gate, gla_b_gate, gla_norm):
    bias = jnp.concatenate([mlstm_b_i[l], mlstm_b_f[l], gdn_dt_bias[l]]).astype(F32)
    alog = gdn_a_log[l].astype(F32)
    per_lane = lambda v: jnp.repeat(v, HEAD_DIM)
    zero = jnp.zeros((W_GROUP,), F32)
    brow = jnp.stack([per_lane(mlstm_b_i[l]), per_lane(mlstm_b_f[l]), per_lane(gdn_dt_bias[l]), zero])
    arow = jnp.stack([zero, zero, per_lane(alog), zero])
    wg32 = jnp.zeros((LANES, W_GROUP), F32).at[16:16 + GLA_RANK].set(gla_w_gate[l])
    return dict(bcol=_pad_lanes(bias, 0), acol=_pad_lanes(alog, 8), brow=brow, arow=arow,
                mnorm=mlstm_norm[l].reshape(1, -1), gnorm=gdn_norm[l].reshape(1, -1),
                lnorm=gla_norm[l].reshape(1, -1),
                gcw=gdn_conv_w[l].T, scw=sc_conv_w[l].T, wg=wg32.astype(BF16), wg32=wg32,
                bg=gla_b_gate[l].reshape(1, -1))


def _pick_tile(n, pref):
    for c in pref:
        if n % c == 0:
            return c
    return n


def _heads_major(a):
    return a.reshape(a.shape[0], N_HEADS, HEAD_DIM).transpose(1, 0, 2)


def kernel(x_prompt, x_sample, state_mlstm_C, state_mlstm_n, state_mlstm_m, state_gdn_S, state_gdn_conv,
           state_sc_conv, state_gla_S, w_in, g_mix, mlstm_b_i, mlstm_b_f, mlstm_norm, gdn_conv_w, gdn_a_log,
           gdn_dt_bias, gdn_norm, sc_conv_w, gla_w_gate, gla_b_gate, gla_norm, w_out, g_ffn, ffn_w_gate,
           ffn_w_up, ffn_w_down, moe_w_router, moe_b_router, moe_w_gate, moe_w_up, moe_w_down, g_final):
    depth = w_in.shape[0]
    bsz, seq, _ = x_prompt.shape
    bn = x_sample.shape[0]
    tp = bsz * seq
    assert x_sample.shape[1] == 1 and seq % CHUNK == 0

    xp = x_prompt.reshape(tp, D_MODEL)
    xs = x_sample.reshape(bn, D_MODEL)
    lb_rows = _pick_tile(seq, (128, 64))
    tm_p = _pick_tile(tp, (256, 128, 64))
    tm_f = _pick_tile(tp, (1024, 512, 256, 128, 64))
    tm_e = _pick_tile(tp, (512, 256, 128, 64))
    nseq = _pick_tile(bsz, (4, 2, 1))

    eye = jnp.eye(HEAD_DIM, dtype=BF16)
    ex = jnp.repeat(eye, HEAD_DIM, axis=1)
    tl = jnp.tile(eye, (1, HEAD_DIM))
    gfin = g_final.reshape(1, -1)

    p_states = [[] for _ in range(7)]
    s_states = [[] for _ in range(7)]
    for l in range(depth):
        p = _layer_params(l, mlstm_b_i, mlstm_b_f, mlstm_norm, gdn_conv_w, gdn_a_log, gdn_dt_bias,
                          gdn_norm, sc_conv_w, gla_w_gate, gla_b_gate, gla_norm)
        wfull, wst = _prep_w_in(w_in[l])
        gm = g_mix[l].reshape(1, -1)
        gf = g_ffn[l].reshape(1, -1)
        last = l == depth - 1

        z, zt = _inproj(xp, gm, wfull, wst, tm_p, lb_rows)
        y, c1, n1, m1, sg1, gb1, sb1, sl1 = _mix_prompt(
            z.reshape(bsz, seq, NP), zt.reshape((bsz, seq // lb_rows) + zt.shape[1:]), p,
            bsz, seq, lb_rows, nseq)
        xp = _outproj(y.reshape(tp, D_MODEL), w_out[l].astype(BF16), xp, tm_p)
        for lst, v in zip(p_states, (c1, n1, m1.reshape(bsz, N_HEADS), sg1, gb1, sb1, sl1)):
            lst.append(v)

        zs = _regroup_z(_inproj_s(xs, gm, w_in[l]))
        gbuf = state_gdn_conv[l].reshape(bn, -1)
        sbuf = state_sc_conv[l].reshape(bn, -1)
        qkv, glog, act, yc, gbuf1, sbuf1 = _sample_pre(zs, gbuf, sbuf, p)
        gates = jnp.pad(act[:, :16].reshape(bn, 4, N_HEADS).transpose(2, 0, 1),
                        ((0, 0), (0, 0), (0, HEAD_DIM - 4)))
        parts = ([_heads_major(zs[:, o:o + W_GROUP]) for o in (MQ, MK, MV, MO)] +
                 [_heads_major(qkv[:, o:o + W_GROUP]) for o in (0, W_GROUP, 2 * W_GROUP)] +
                 [_heads_major(zs[:, o:o + W_GROUP]) for o in (GZ, LQ, LK, LV, LR)] +
                 [_heads_major(glog), gates])
        zh = jnp.concatenate(parts, axis=-1)
        norms = jnp.concatenate([_heads_major(v.reshape(1, -1)) for v in
                                 (mlstm_norm[l], gdn_norm[l], gla_norm[l])], axis=-1)
        hh = HEAD_DIM * HEAD_DIM
        ysr, c2, n2, m2, sg2, sl2 = _sample_rec(
            zh, state_mlstm_C[l].reshape(bn, N_HEADS * hh),
            state_mlstm_n[l].transpose(1, 0, 2),
            state_mlstm_m[l].T.reshape(N_HEADS, bn, 1),
            state_gdn_S[l].reshape(bn, N_HEADS * hh), state_gla_S[l].reshape(bn, N_HEADS * hh),
            norms, ex, tl)
        unhead = lambda a: a.transpose(1, 0, 2).reshape(bn, W_GROUP)
        ymix = jnp.concatenate([unhead(ysr[:, :, 0:HEAD_DIM]), unhead(ysr[:, :, HEAD_DIM:2 * HEAD_DIM]),
                                yc, unhead(ysr[:, :, 2 * HEAD_DIM:])], axis=-1)
        xs = _outproj_s(ymix, w_out[l], xs)
        st_shape = (bn, N_HEADS, HEAD_DIM, HEAD_DIM)
        for lst, v in zip(s_states, (c2.reshape(st_shape), n2.transpose(1, 0, 2),
                                     m2.reshape(N_HEADS, bn).T, sg2.reshape(st_shape),
                                     gbuf1.reshape(bn, 3, -1), sbuf1.reshape(bn, 2, -1),
                                     sl2.reshape(st_shape))):
            lst.append(v)

        j = l // 2
        if l % 2 == 0:
            tf = _pick_tile(ffn_w_gate.shape[2], (256, 128))
            xp = _ffn(xp, gf, ffn_w_gate[j].astype(BF16), ffn_w_up[j].astype(BF16),
                      ffn_w_down[j].astype(BF16), gfin, tm_f, tf, last)
            xs = _ffn_s(xs, gf, ffn_w_gate[j], ffn_w_up[j], ffn_w_down[j], gfin, tf, last)
        else:
            wr = jnp.zeros((D_MODEL, LANES), F32).at[:, :N_EXPERTS].set(moe_w_router[j])
            br = _pad_lanes(moe_b_router[j], 0)
            wgt, wup, wdn = (moe_w_gate[j].astype(BF16), moe_w_up[j].astype(BF16),
                             moe_w_down[j].astype(BF16))
            xp = _moe(xp, gf, wr, br, wgt, wup, wdn, gfin, tm_e, last, False)
            xs = _moe(xs, gf, wr, br, wgt, wup, wdn, gfin, bn, last, True)

    y_prompt = xp.reshape(bsz, seq, D_MODEL)
    y_sample = xs.reshape(bn, 1, D_MODEL)
    sp = [jnp.stack(v) for v in p_states]
    ss = [jnp.stack(v) for v in s_states]
    return (y_prompt, y_sample, sp[0], ss[0], sp[1], ss[1], sp[2], ss[2], sp[3], ss[3],
            sp[4], ss[4], sp[5], ss[5], sp[6], ss[6])
```

```python
import functools

import numpy as np
import jax
import jax.numpy as jnp
from jax import lax
from jax.experimental import pallas as pl
from jax.experimental.pallas import tpu as pltpu

F32 = jnp.float32
BF16 = jnp.bfloat16

D_MODEL = 1024
W_GROUP = 256
N_HEADS = 4
HEAD_DIM = 64
CHUNK = 64
GLA_RANK = 16
GLA_NORMALIZER = 16.0
N_EXPERTS = 8
EPS = 1e-6
Q_SCALE = HEAD_DIM ** -0.5

VMEM_LIMIT = 56 * 1024 * 1024
LANES = 128

SPLIT_SIZES = ([W_GROUP] * 4 + [N_HEADS] * 2 + [W_GROUP] * 4 + [N_HEADS] * 2 +
               [W_GROUP] * 3 + [W_GROUP] * 4 + [GLA_RANK])
_OFF = np.concatenate([[0], np.cumsum(SPLIT_SIZES)])
_BIG = [0, 1, 2, 3, 6, 7, 8, 9, 12, 13, 14, 15, 16, 17, 18]
_SMALL = [4, 5, 10, 11, 19]
_GATES = [4, 5, 10, 11]
MQ, MK, MV, MO = 0, 256, 512, 768
GQ, GK, GV, GZ = 1024, 1280, 1536, 1792
SB, SC, SH = 2048, 2304, 2560
LQ, LK, LV, LR = 2816, 3072, 3328, 3584
SM = 3840
NP = SM + LANES
N_SMALL = 32
N_GATE_ROWS = 16
N_ROW_KINDS = 3
GX_I, GX_B, GX_G, GX_BETA, GX_PM = 0, 256, 512, 768, 1024
GLA_LEVELS = (32, 16, 8, 4, 2)


def _cparams(sem):
    return pltpu.CompilerParams(dimension_semantics=sem, vmem_limit_bytes=VMEM_LIMIT)


def _log1pexp_negabs(x):
    return jnp.log(1.0 + jnp.exp(-jnp.abs(x)))


def _softplus(x):
    return jnp.maximum(x, 0.0) + _log1pexp_negabs(x)


def _logsigmoid(x):
    return -(jnp.maximum(-x, 0.0) + _log1pexp_negabs(x))


def _sigmoid(x):
    return 1.0 / (1.0 + jnp.exp(-x))


def _silu(x):
    return x * _sigmoid(x)


def _rms(x, g):
    return x * lax.rsqrt(jnp.mean(x * x, axis=-1, keepdims=True) + EPS) * g


def _dot(a, b):
    return jnp.dot(a, b, preferred_element_type=F32)


def _dot_nt(a, b):
    return lax.dot_general(a, b, (((1,), (1,)), ((), ())), preferred_element_type=F32)


def _dot_tn(a, b):
    return lax.dot_general(a, b, (((0,), (0,)), ((), ())), preferred_element_type=F32)


def _split2(x):
    x1 = x.astype(BF16)
    return x1, (x - x1.astype(F32)).astype(BF16)


def _split3(x):
    x1 = x.astype(BF16)
    r1 = x - x1.astype(F32)
    x2 = r1.astype(BF16)
    x3 = (r1 - x2.astype(F32)).astype(BF16)
    return x1, x2, x3


def _sel(m, x):
    x1, x2, x3 = _split3(x)
    return _dot(m, x1) + _dot(m, x2) + _dot(m, x3)


def _sel_r(x, m):
    n = x.shape[0]
    if n % 8:
        x1, x2, x3 = _split3(x)
        return _dot(x1, m) + _dot(x2, m) + _dot(x3, m)
    r = _dot(jnp.concatenate(_split3(x), axis=0), m)
    return r[0:n] + r[n:2 * n] + r[2 * n:3 * n]


def _sel_nt(m, x):
    x1, x2, x3 = _split3(x)
    return _dot_nt(m, x1) + _dot_nt(m, x2) + _dot_nt(m, x3)


def _mm3(x, w):
    n = x.shape[0]
    x1, x2 = _split2(x)
    w1, w2 = _split2(w)
    r = _dot(jnp.concatenate([x1, x2], axis=0), w1)
    return r[0:n] + r[n:] + _dot(x1, w2)


def _iota2(shape, dim):
    return lax.broadcasted_iota(jnp.int32, shape, dim)


def _inproj_kernel(x_ref, g_ref, w_ref, wst_ref, z_ref, zt_ref):
    hb = _rms(x_ref[...], g_ref[...]).astype(BF16)
    z_ref[...] = _dot(hb, w_ref[...])
    zt = _dot_nt(wst_ref[...], hb)
    n_blk, n_kind, n_chunks, _ = zt_ref.shape
    for b in range(n_blk):
        for k in range(n_kind):
            for c in range(n_chunks):
                t0 = (b * n_chunks + c) * CHUNK
                for h in range(N_HEADS):
                    zt_ref[b, k, c:c + 1, h * HEAD_DIM:(h + 1) * HEAD_DIM] = (
                        zt[k * 4 + h:k * 4 + h + 1, t0:t0 + CHUNK])


def _inproj(x, g, w, wst, tm, lb_rows):
    t = x.shape[0]
    n_chunks = lb_rows // CHUNK
    return pl.pallas_call(
        _inproj_kernel,
        grid=(t // tm,),
        in_specs=[pl.BlockSpec((tm, D_MODEL), lambda i: (i, 0)),
                  pl.BlockSpec((1, D_MODEL), lambda i: (0, 0)),
                  pl.BlockSpec((D_MODEL, NP), lambda i: (0, 0)),
                  pl.BlockSpec((N_GATE_ROWS, D_MODEL), lambda i: (0, 0))],
        out_specs=[pl.BlockSpec((tm, NP), lambda i: (i, 0)),
                   pl.BlockSpec((tm // lb_rows, N_ROW_KINDS, n_chunks, W_GROUP), lambda i: (i, 0, 0, 0))],
        out_shape=[jax.ShapeDtypeStruct((t, NP), F32),
                   jax.ShapeDtypeStruct((t // lb_rows, N_ROW_KINDS, n_chunks, W_GROUP), F32)],
        compiler_params=_cparams(("parallel",)),
        name="inproj",
    )(x, g, w, wst)


def _gate_act(pre, idx, neg_a):
    tail = _log1pexp_negabs(pre)
    lf = -(jnp.maximum(-pre, 0.0) + tail)
    dec = neg_a * (jnp.maximum(pre, 0.0) + tail)
    beta = _sigmoid(pre)
    return jnp.where(idx < 4, pre,
                     jnp.where(idx < 8, lf,
                               jnp.where(idx < 12, dec,
                                         jnp.where(idx < 16, beta, 0.0))))


def _mix_kernel(z_ref, zt_ref, bcol_ref, acol_ref, brow_ref, arow_ref, mnorm_ref, gnorm_ref,
                lnorm_ref, gcw_ref, scw_ref, wg_ref, bg_ref,
                bdmask_ref, bdtriu_ref, bdtril_ref, selexp_ref, selstack_ref, eye64_ref,
                y_ref, c_out, n_out, m_out, sg_out, gconv_out, sconv_out, sl_out,
                c_ref, n_ref, m_ref, sg_ref, slt_ref, xg_ref, xs_ref, qkv_ref, gcum_ref, glog_ref,
                gx_ref, rw_ref, *, lb_rows, nseq):
    lb = pl.program_id(1)
    nlb = pl.num_programs(1)
    n_chunks = lb_rows // CHUNK
    wd = W_GROUP
    seqs = range(nseq)

    @pl.when(lb == 0)
    def _():
        c_ref[...] = jnp.zeros_like(c_ref)
        n_ref[...] = jnp.zeros_like(n_ref)
        m_ref[...] = jnp.zeros_like(m_ref)
        sg_ref[...] = jnp.zeros_like(sg_ref)
        slt_ref[...] = jnp.zeros_like(slt_ref)
        for i in seqs:
            xg_ref[i, 0:8, :] = jnp.zeros((8, 3 * wd), F32)
            xs_ref[i, 0:8, :] = jnp.zeros((8, wd), F32)

    ti = _iota2((CHUNK, wd), 0)
    si = _iota2((CHUNK, wd), 1) % HEAD_DIM
    incl = ti >= si
    strict = ti > si
    eye = ti == si
    blk16 = (ti // 16) == (si // 16)
    blk32 = (ti // 32) == (si // 32)
    eye_f = eye.astype(F32)
    lvl_mask = {n: ((ti // (2 * n)) == (si // (2 * n))) & ((ti // n) > (si // n))
                for n in GLA_LEVELS + (1,)}
    bdmask = bdmask_ref[...]
    bdtriu = bdtriu_ref[...]
    bdtril = bdtril_ref[...]
    head_of_lane = _iota2((1, wd), 1) // HEAD_DIM
    pos_of_lane = _iota2((8, wd), 1) % HEAD_DIM

    def bd(x):
        xb = x.astype(BF16)
        return jnp.concatenate([xb, xb, xb, xb], axis=0) * bdmask

    def unbd(m):
        out = m[3 * HEAD_DIM:, :]
        for h in (2, 1, 0):
            out = jnp.where(head_of_lane == h, m[h * HEAD_DIM:(h + 1) * HEAD_DIM, :], out)
        return out

    def mm(a, bmat):
        return _dot(a.astype(BF16), bmat)

    def mm_nt(a, bmat):
        return _dot_nt(a.astype(BF16), bmat)

    def mm3_bd(a, b):
        a1, a2 = _split2(a)
        b1, b2 = _split2(b)
        r = _dot(jnp.concatenate([a1, a2], axis=0), bd(b1))
        return r[0:CHUNK] + r[CHUNK:] + _dot(a1, bd(b2))

    def hsum(x):
        x1, x2 = _split2(x)
        r = _dot(jnp.concatenate([x1, x2], axis=0), bdmask)
        return r[0:x.shape[0]] + r[x.shape[0]:]

    def headnorm(o, w):
        return o * lax.rsqrt(hsum(o * o) * (1.0 / HEAD_DIM) + EPS) * w

    lane = _iota2((1, LANES), 1)
    neg_a_col = -jnp.exp(acol_ref[...])
    neg_a_row = -jnp.exp(arow_ref[2:3, :])
    selexp = selexp_ref[...]
    sel_stack = selstack_ref[...]
    eye64 = eye64_ref[...]

    def l2n(x):
        return x * lax.rsqrt(hsum(x * x) + EPS)

    for i in seqs:
        small = z_ref[i, :, SM:SM + LANES]
        act = _gate_act(small + bcol_ref[...], lane, neg_a_col)
        gx4 = _sel_r(act, selexp)
        gx_ref[i, :, GX_I:GX_I + wd] = gx4[:, 0:wd]
        gx_ref[i, :, GX_BETA:GX_BETA + wd] = gx4[:, 3 * wd:]
        gx_ref[i, :, GX_B:GX_B + 2 * wd] = _sel(bdtril, gx4[:, wd:3 * wd])
        glin = _dot(small.astype(BF16), wg_ref[...]) + bg_ref[...]
        glog = _logsigmoid(glin) * (1.0 / GLA_NORMALIZER)
        glog_ref[i] = glog
        gcum_ref[i] = _sel(bdtril, glog)

        ic_r = zt_ref[i, 0, 0] + brow_ref[0:1, :]
        lf_r = _logsigmoid(zt_ref[i, 0, 1] + brow_ref[1:2, :])
        dec_r = neg_a_row * _softplus(zt_ref[i, 0, 2] + brow_ref[2:3, :])
        cs_r = _sel_r(jnp.concatenate([lf_r, dec_r], axis=0), bdtriu)
        c_rows = ic_r - cs_r[0:n_chunks, :]
        g_rows = cs_r[n_chunks:, :]
        pm = jnp.concatenate([c_rows] * (8 // n_chunks), axis=0)
        for sh in (1, 2, 4, 8, 16, 32):
            pm = jnp.where(pos_of_lane >= sh, jnp.maximum(pm, pltpu.roll(pm, sh, axis=1)), pm)
        for c in range(n_chunks):
            rw_ref[i, c, 0:1, :] = c_rows[c:c + 1, :]
            rw_ref[i, c, 1:2, :] = g_rows[c:c + 1, :]
        diag = jnp.concatenate(
            [jnp.where(eye, jnp.broadcast_to(pm[c:c + 1, :], (CHUNK, wd)), 0.0)
             for c in range(n_chunks)], axis=0)
        gx_ref[i, :, GX_PM:GX_PM + wd] = _sel_r(diag, bdmask)

        xg_ref[i, 8:, :] = z_ref[i, :, GQ:GQ + 3 * wd]
        conv = xg_ref[i, 5:5 + lb_rows, :] * gcw_ref[0:1, :]
        for j in range(1, 4):
            conv = conv + xg_ref[i, 5 + j:5 + j + lb_rows, :] * gcw_ref[j:j + 1, :]
        conv = _silu(conv)
        qkv_ref[i, :, 0:wd] = l2n(conv[:, 0:wd]) * Q_SCALE
        qkv_ref[i, :, wd:2 * wd] = l2n(conv[:, wd:2 * wd])
        qkv_ref[i, :, 2 * wd:] = conv[:, 2 * wd:]

        xs_ref[i, 8:, :] = z_ref[i, :, SC:SC + wd] * z_ref[i, :, SH:SH + wd]
        cu = xs_ref[i, 6:6 + lb_rows, :] * scw_ref[0:1, :]
        for j in range(1, 3):
            cu = cu + xs_ref[i, 6 + j:6 + j + lb_rows, :] * scw_ref[j:j + 1, :]
        y_ref[i, :, 2 * wd:3 * wd] = (z_ref[i, :, SB:SB + wd] * cu).astype(y_ref.dtype)

    def chunk(c, carry):
        r0 = pl.multiple_of(c * CHUNK, CHUNK)
        rs = pl.ds(r0, CHUNK)
        each = lambda f, *xs: [f(*a) for a in zip(*xs)]
        zcol = lambda o: [z_ref[i, rs, o:o + wd] for i in seqs]
        gxcol = lambda o: [gx_ref[i, rs, o:o + wd] for i in seqs]
        last = lambda xs: [x[CHUNK - 1:CHUNK, :] for x in xs]
        tn_bd = lambda a, b: each(lambda x, y: unbd(_dot_tn(x.astype(BF16), y.astype(BF16))), a, b)
        mm3s = lambda a, b: each(mm3_bd, a, b)
        rw = [rw_ref[i, c, 0:2, :] for i in seqs]

        q = zcol(MQ)
        k = [x * Q_SCALE for x in zcol(MK)]
        v = zcol(MV)
        ic_e, b_e, pm_e = gxcol(GX_I), gxcol(GX_B), gxcol(GX_PM)
        m_old = [m_ref[i, 0:1, :] for i in seqs]
        cmat = [c_ref[i] for i in seqs]
        nrow = [n_ref[i, 0:1, :] for i in seqs]
        mx = each(jnp.maximum, m_old, pm_e)
        w_int = each(lambda a, b: jnp.exp(a - b), m_old, mx)
        p = each(lambda r, m: jnp.where(incl, jnp.exp(r[0:1, :] - m), 0.0), rw, mx)
        s = each(lambda a, b, c_: mm_nt(a, bd(b)) * c_, q, k, p)
        num = each(lambda w_, a, cm, s_, v_: w_ * mm(a, bd(cm)) + mm(s_, bd(v_)), w_int, q, cmat, s, v)
        den = each(lambda w_, a, n_, s_: hsum(w_ * (a * n_) + s_), w_int, q, nrow, s)
        hh = each(lambda n_, d_, b_, m: n_ / jnp.maximum(jnp.abs(d_), jnp.exp(-(b_ + m))), num, den, b_e, mx)
        mx_last = last(mx)
        wgk = each(lambda i_, b_, ml, k_: jnp.exp((i_ - b_) - ml) * k_, ic_e, b_e, mx_last, k)
        dec = each(lambda a, b: jnp.exp(a - b), m_old, mx_last)
        c_new = each(lambda d_, cm, u_: d_ * cm + u_, dec, cmat, tn_bd(wgk, v))
        for i in seqs:
            c_ref[i] = c_new[i]
            n_ref[i] = jnp.broadcast_to(dec[i] * nrow[i] + jnp.sum(wgk[i], axis=0, keepdims=True), (8, wd))
            m_ref[i] = jnp.broadcast_to(b_e[i][CHUNK - 1:CHUNK, :] + mx_last[i], (8, wd))
        y_a = each(lambda o_, h_: _sigmoid(o_) * headnorm(h_, mnorm_ref[...]), zcol(MO), hh)
        for i in seqs:
            y_ref[i, rs, 0:wd] = y_a[i].astype(y_ref.dtype)

        q = [qkv_ref[i, rs, 0:wd] for i in seqs]
        k = [qkv_ref[i, rs, wd:2 * wd] for i in seqs]
        v = [qkv_ref[i, rs, 2 * wd:] for i in seqs]
        g_e, beta = gxcol(GX_G), gxcol(GX_BETA)
        s_old = [sg_ref[i] for i in seqs]
        bdk = each(bd, k)
        lw = each(lambda g_, r: jnp.where(incl, jnp.exp(g_ - r[1:2, :]), 0.0), g_e, rw)
        amat = each(lambda b_, k_, bk, l_: jnp.where(strict, b_ * mm_nt(k_, bk) * l_, 0.0), beta, k, bdk, lw)
        egc = each(jnp.exp, g_e)
        ad = [jnp.where(blk16, a, 0.0) for a in amat]
        a2 = mm3s(ad, ad)
        a4 = mm3s(a2, a2)
        a8 = mm3s(a4, a4)
        tinv = [eye_f - a for a in ad]
        for apow in (a2, a4, a8):
            tinv = each(lambda t_, u_: t_ + u_, tinv, mm3s(tinv, apow))
        n1 = [jnp.where(blk32 & jnp.logical_not(blk16), a, 0.0) for a in amat]
        tinv = each(lambda t_, u_: t_ - u_, tinv, mm3s(mm3s(tinv, n1), tinv))
        n2 = [jnp.where(blk32, 0.0, a) for a in amat]
        tinv = each(lambda t_, u_: t_ - u_, tinv, mm3s(mm3s(tinv, n2), tinv))
        u = mm3s(tinv, each(lambda b_, v_: b_ * v_, beta, v))
        w = mm3s(tinv, each(lambda b_, e_, k_: (b_ * e_) * k_, beta, egc, k))
        bds = each(bd, s_old)
        vn = each(lambda u_, w_, bs: u_ - mm(w_, bs), u, w, bds)
        qk = each(lambda q_, bk, l_: mm_nt(q_, bk) * l_, q, bdk, lw)
        o = each(lambda q_, e_, bs, qk_, vn_: mm(q_ * e_, bs) + mm(qk_, bd(vn_)), q, egc, bds, qk, vn)
        g_last = last(g_e)
        kdec = each(lambda k_, gl, g_: k_ * jnp.exp(gl - g_), k, g_last, g_e)
        s_new = each(lambda gl, so, u_: jnp.exp(gl) * so + u_, g_last, s_old, tn_bd(kdec, vn))
        y_b = each(lambda o_, z_: headnorm(o_, gnorm_ref[...]) * _silu(z_), o, zcol(GZ))
        for i in seqs:
            sg_ref[i] = s_new[i]
            y_ref[i, rs, wd:2 * wd] = y_b[i].astype(y_ref.dtype)

        q = [x * Q_SCALE for x in zcol(LQ)]
        k = zcol(LK)
        v = zcol(LV)
        gk = [glog_ref[i, rs, :] for i in seqs]
        gcum = [gcum_ref[i, rs, :] for i in seqs]
        stack = [_sel(sel_stack, g_) for g_ in gcum]
        st_old = [slt_ref[i] for i in seqs]
        bdk = each(bd, k)
        amat = each(lambda q_, bk: jnp.where(eye, mm_nt(q_, bk), 0.0), q, bdk)
        amat = each(lambda a_, q_, g_, bk: a_ + jnp.where(lvl_mask[1], mm_nt(q_ * jnp.exp(g_), bk), 0.0),
                    amat, q, gk, bdk)
        for j, n in enumerate(GLA_LEVELS):
            g_end = [s_[2 * j * CHUNK:(2 * j + 1) * CHUNK, :] for s_ in stack]
            g_prev = [s_[(2 * j + 1) * CHUNK:(2 * j + 2) * CHUNK, :] for s_ in stack]
            qn = each(lambda q_, gc, gp: q_ * jnp.exp(gc - gp), q, gcum, g_prev)
            kn = each(lambda k_, ge, gc: k_ * jnp.exp(ge - gc), k, g_end, gcum)
            amat = each(lambda a_, q_, k_, m=lvl_mask[n]: a_ + jnp.where(m, mm_nt(q_, bd(k_)), 0.0),
                        amat, qn, kn)
        g_last = last(gcum)
        o = each(lambda q_, gc, st, a_, v_: mm_nt(q_ * jnp.exp(gc), bd(st)) + mm(a_, bd(v_)),
                 q, gcum, st_old, amat, v)
        kdec = each(lambda k_, gl, gc: k_ * jnp.exp(gl - gc), k, g_last, gcum)
        st_new = each(lambda st, gl, u_: st * jnp.exp(gl) + u_, st_old, g_last, tn_bd(v, kdec))
        y_d = each(lambda o_, z_: headnorm(o_, lnorm_ref[...]) * _silu(z_), o, zcol(LR))
        for i in seqs:
            slt_ref[i] = st_new[i]
            y_ref[i, rs, 3 * wd:] = y_d[i].astype(y_ref.dtype)
        return carry

    lax.fori_loop(0, n_chunks, chunk, 0)

    gtail = [xg_ref[i, lb_rows:lb_rows + 8, :] for i in seqs]
    stail = [xs_ref[i, lb_rows:lb_rows + 8, :] for i in seqs]
    for i in seqs:
        xg_ref[i, 0:8, :] = gtail[i]
        xs_ref[i, 0:8, :] = stail[i]

    @pl.when(lb == nlb - 1)
    def _():
        lane4 = _iota2((1, N_HEADS), 1)
        for i in seqs:
            gconv_out[i] = gtail[i][5:8, :]
            sconv_out[i] = stail[i][6:8, :]
            m_row = jnp.zeros((1, N_HEADS), F32)
            for h in range(N_HEADS):
                hs = slice(h * HEAD_DIM, (h + 1) * HEAD_DIM)
                c_out[i, h] = c_ref[i, :, hs]
                n_out[i, h:h + 1, :] = n_ref[i, 0:1, hs]
                m_row = jnp.where(lane4 == h, m_ref[i, 0:1, h * HEAD_DIM:h * HEAD_DIM + 1], m_row)
                sg_out[i, h] = sg_ref[i, :, hs]
                sl_out[i, h] = _sel_nt(eye64, slt_ref[i, :, hs])
            m_out[i] = m_row


def _mix_consts(lb_rows):
    wd = W_GROUP
    r, c = np.indices((wd, wd))
    same_head = (r // HEAD_DIM) == (c // HEAD_DIM)
    bdmask = same_head
    bdtriu = same_head & ((r % HEAD_DIM) <= (c % HEAD_DIM))
    rl, cl = np.indices((lb_rows, lb_rows))
    bdtril = ((rl // CHUNK) == (cl // CHUNK)) & (rl >= cl)
    rr, cc = np.indices((LANES, 4 * wd))
    selexp = rr == 4 * (cc // wd) + (cc % wd) // HEAD_DIM
    rows, cols = np.indices((2 * len(GLA_LEVELS) * CHUNK, CHUNK))
    r_in, blk = rows % CHUNK, rows // CHUNK
    sel_stack = np.zeros(rows.shape, bool)
    for i, n in enumerate(GLA_LEVELS):
        sel_stack |= (blk == 2 * i) & (cols == (r_in // n) * n + (n - 1))
        sel_stack |= (blk == 2 * i + 1) & (cols == (r_in // n) * n - 1)
    eye64 = np.eye(CHUNK, dtype=bool)
    return [jnp.asarray(m, dtype=BF16) for m in (bdmask, bdtriu, bdtril, selexp, sel_stack, eye64)]


def _mix_prompt(z, zt, p, bsz, seq, lb_rows, nseq):
    nlb = seq // lb_rows
    n_chunks = lb_rows // CHUNK
    row = lambda b, l: (b, l, 0)
    const2 = lambda b, l: (0, 0)
    st4 = lambda b, l: (b, 0, 0, 0)
    st3 = lambda b, l: (b, 0, 0)
    params = [p["bcol"], p["acol"], p["brow"], p["arow"], p["mnorm"], p["gnorm"], p["lnorm"],
              p["gcw"], p["scw"], p["wg"], p["bg"]] + _mix_consts(lb_rows)
    in_specs = [pl.BlockSpec((nseq, lb_rows, NP), row),
                pl.BlockSpec((nseq, 1, N_ROW_KINDS, n_chunks, W_GROUP), lambda b, l: (b, l, 0, 0, 0))]
    in_specs += [pl.BlockSpec(a.shape, const2) for a in params]
    hd = (N_HEADS, HEAD_DIM, HEAD_DIM)
    out_shape = [jax.ShapeDtypeStruct((bsz, seq, D_MODEL), BF16),
                 jax.ShapeDtypeStruct((bsz,) + hd, F32),
                 jax.ShapeDtypeStruct((bsz, N_HEADS, HEAD_DIM), F32),
                 jax.ShapeDtypeStruct((bsz, 1, N_HEADS), F32),
                 jax.ShapeDtypeStruct((bsz,) + hd, F32),
                 jax.ShapeDtypeStruct((bsz, 3, 3 * W_GROUP), F32),
                 jax.ShapeDtypeStruct((bsz, 2, W_GROUP), F32),
                 jax.ShapeDtypeStruct((bsz,) + hd, F32)]
    out_specs = [pl.BlockSpec((nseq, lb_rows, D_MODEL), row),
                 pl.BlockSpec((nseq,) + hd, st4),
                 pl.BlockSpec((nseq, N_HEADS, HEAD_DIM), st3),
                 pl.BlockSpec((nseq, 1, N_HEADS), st3),
                 pl.BlockSpec((nseq,) + hd, st4),
                 pl.BlockSpec((nseq, 3, 3 * W_GROUP), st3),
                 pl.BlockSpec((nseq, 2, W_GROUP), st3),
                 pl.BlockSpec((nseq,) + hd, st4)]
    vm = lambda *shape: pltpu.VMEM((nseq,) + shape, F32)
    scratch = [vm(HEAD_DIM, W_GROUP),
               vm(8, W_GROUP),
               vm(8, W_GROUP),
               vm(HEAD_DIM, W_GROUP),
               vm(HEAD_DIM, W_GROUP),
               vm(lb_rows + 8, 3 * W_GROUP),
               vm(lb_rows + 8, W_GROUP),
               vm(lb_rows, 3 * W_GROUP),
               vm(lb_rows, W_GROUP),
               vm(lb_rows, W_GROUP),
               vm(lb_rows, 5 * W_GROUP),
               vm(n_chunks, 8, W_GROUP)]
    return pl.pallas_call(
        functools.partial(_mix_kernel, lb_rows=lb_rows, nseq=nseq),
        grid=(bsz // nseq, nlb),
        in_specs=in_specs,
        out_specs=out_specs,
        out_shape=out_shape,
        scratch_shapes=scratch,
        compiler_params=_cparams(("parallel", "arbitrary")),
        name="mix_prompt",
    )(z, zt, *params)


def _inproj_s_kernel(x_ref, g_ref, w_ref, z_ref):
    h = _rms(x_ref[...], g_ref[...])
    n = w_ref.shape[1]
    step = 4 * LANES
    for j in range(0, n, step):
        cs = slice(j, min(j + step, n))
        z_ref[:, cs] = _mm3(h, w_ref[:, cs])


def _inproj_s(x, g, w):
    return pl.pallas_call(
        _inproj_s_kernel,
        out_shape=jax.ShapeDtypeStruct((x.shape[0], w.shape[1]), F32),
        compiler_params=pltpu.CompilerParams(vmem_limit_bytes=VMEM_LIMIT),
        name="inproj_s",
    )(x, g, w)


def _spre_kernel(z_ref, gbuf_ref, sbuf_ref, bcol_ref, acol_ref, gcw_ref, scw_ref, wg_ref, bg_ref,
                 qkv_ref, glog_ref, act_ref, yc_ref, gbuf_out, sbuf_out):
    small = z_ref[:, SM:SM + LANES]
    lane = _iota2((1, LANES), 1)
    act_ref[...] = _gate_act(small + bcol_ref[...], lane, -jnp.exp(acol_ref[...]))
    glin = _mm3(small, wg_ref[...]) + bg_ref[...]
    glog_ref[...] = _logsigmoid(glin) * (1.0 / GLA_NORMALIZER)

    wq = 3 * W_GROUP
    u = z_ref[:, GQ:GQ + wq]
    b0, b1, b2 = gbuf_ref[:, 0:wq], gbuf_ref[:, wq:2 * wq], gbuf_ref[:, 2 * wq:]
    conv = b0 * gcw_ref[0:1, :] + b1 * gcw_ref[1:2, :] + b2 * gcw_ref[2:3, :] + u * gcw_ref[3:4, :]
    conv = _silu(conv)
    gbuf_out[:, 0:wq] = b1
    gbuf_out[:, wq:2 * wq] = b2
    gbuf_out[:, 2 * wq:] = u
    hsum = (_iota2((W_GROUP, W_GROUP), 0) // HEAD_DIM ==
            _iota2((W_GROUP, W_GROUP), 1) // HEAD_DIM).astype(BF16)

    def l2n(x):
        return x * lax.rsqrt(_sel_r(x * x, hsum) + EPS)

    qkv_ref[:, 0:W_GROUP] = l2n(conv[:, 0:W_GROUP]) * Q_SCALE
    qkv_ref[:, W_GROUP:2 * W_GROUP] = l2n(conv[:, W_GROUP:2 * W_GROUP])
    qkv_ref[:, 2 * W_GROUP:] = conv[:, 2 * W_GROUP:]

    u2 = z_ref[:, SC:SC + W_GROUP] * z_ref[:, SH:SH + W_GROUP]
    s0, s1 = sbuf_ref[:, 0:W_GROUP], sbuf_ref[:, W_GROUP:]
    cu = s0 * scw_ref[0:1, :] + s1 * scw_ref[1:2, :] + u2 * scw_ref[2:3, :]
    yc_ref[...] = z_ref[:, SB:SB + W_GROUP] * cu
    sbuf_out[:, 0:W_GROUP] = s1
    sbuf_out[:, W_GROUP:] = u2


def _sample_pre(z, gbuf, sbuf, p):
    bn = z.shape[0]
    sd = lambda s: jax.ShapeDtypeStruct(s, F32)
    return pl.pallas_call(
        _spre_kernel,
        out_shape=[sd((bn, 3 * W_GROUP)), sd((bn, W_GROUP)), sd((bn, LANES)), sd((bn, W_GROUP)),
                   sd(gbuf.shape), sd(sbuf.shape)],
        compiler_params=pltpu.CompilerParams(vmem_limit_bytes=VMEM_LIMIT),
        name="sample_pre",
    )(z, gbuf, sbuf, p["bcol"], p["acol"], p["gcw"], p["scw"], p["wg32"], p["bg"])


ZH = 14 * HEAD_DIM


def _srec_kernel(zh_ref, c_ref, n_ref, m_ref, sg_ref, sl_ref, norm_ref, ex_ref, tl_ref,
                 y_ref, c_out, n_out, m_out, sg_out, sl_out):
    zh = zh_ref[0]
    col = lambda j: zh[:, j * HEAD_DIM:(j + 1) * HEAD_DIM]
    mq, mk, mv, mo = col(0), col(1) * Q_SCALE, col(2), col(3)
    gq, gk, gv, gz = col(4), col(5), col(6), col(7)
    lq, lk, lv, lr = col(8) * Q_SCALE, col(9), col(10), col(11)
    glog = col(12)
    gates = col(13)
    ic, lf, gdec, beta = gates[:, 0:1], gates[:, 1:2], gates[:, 2:3], gates[:, 3:4]
    ex = ex_ref[...]
    tl = tl_ref[...]
    expand = lambda x: _sel_r(x, ex)
    tile = lambda x: _sel_r(x, tl)

    def reduce_d(pmat):
        acc = pmat[:, 0:LANES]
        for j in range(1, HEAD_DIM * HEAD_DIM // LANES):
            acc = acc + pmat[:, j * LANES:(j + 1) * LANES]
        return acc[:, :HEAD_DIM] + acc[:, HEAD_DIM:]

    def headnorm(o, w):
        return o * lax.rsqrt(jnp.mean(o * o, axis=-1, keepdims=True) + EPS) * w

    rowsum = lambda x: jnp.sum(x, axis=-1, keepdims=True)
    nw = norm_ref[0]

    cmat = c_ref[...]
    nvec = n_ref[0]
    m_old = m_ref[0]
    a = lf + m_old
    mt = jnp.maximum(a, ic)
    w_int = jnp.exp(a - mt)
    e_i = jnp.exp(ic - mt)
    s = rowsum(mq * mk) * e_i
    num = w_int * reduce_d(expand(mq) * cmat) + s * mv
    den = w_int * rowsum(mq * nvec) + s
    hh = num / jnp.maximum(jnp.abs(den), jnp.exp(-mt))
    c_out[...] = w_int * cmat + e_i * (expand(mk) * tile(mv))
    n_out[0] = w_int * nvec + e_i * mk
    m_out[0] = mt
    y_a = _sigmoid(mo) * headnorm(hh, nw[:, 0:HEAD_DIM])

    smat = sg_ref[...]
    eg = jnp.exp(gdec)
    kexp = expand(gk)
    vn = beta * gv - (beta * eg) * reduce_d(kexp * smat)
    o = eg * reduce_d(expand(gq) * smat) + rowsum(gq * gk) * vn
    sg_out[...] = eg * smat + kexp * tile(vn)
    y_b = headnorm(o, nw[:, HEAD_DIM:2 * HEAD_DIM]) * _silu(gz)

    lmat = sl_ref[...]
    egk = jnp.exp(glog)
    o = reduce_d(expand(lq * egk) * lmat) + rowsum(lq * lk) * lv
    sl_out[...] = expand(egk) * lmat + expand(lk) * tile(lv)
    y_d = headnorm(o, nw[:, 2 * HEAD_DIM:]) * _silu(lr)

    y_ref[0] = jnp.concatenate([y_a, y_b, y_d], axis=1)


def _sample_rec(zh, cst, nst, mst, sgst, slst, norms, ex, tl):
    bn = zh.shape[1]
    hh = HEAD_DIM * HEAD_DIM
    big = pl.BlockSpec((bn, hh), lambda h: (0, h))
    per3 = lambda w: pl.BlockSpec((1, bn, w), lambda h: (h, 0, 0))
    const = lambda a: pl.BlockSpec(a.shape, lambda h: (0, 0))
    sd = lambda s: jax.ShapeDtypeStruct(s, F32)
    return pl.pallas_call(
        _srec_kernel,
        grid=(N_HEADS,),
        in_specs=[per3(ZH), big, per3(HEAD_DIM), per3(1), big, big,
                  pl.BlockSpec((1, 1, 3 * HEAD_DIM), lambda h: (h, 0, 0)), const(ex), const(tl)],
        out_specs=[per3(3 * HEAD_DIM), big, per3(HEAD_DIM), per3(1), big, big],
        out_shape=[sd((N_HEADS, bn, 3 * HEAD_DIM)), sd(cst.shape), sd(nst.shape), sd(mst.shape),
                   sd(sgst.shape), sd(slst.shape)],
        compiler_params=_cparams(("parallel",)),
        name="sample_rec",
    )(zh, cst, nst, mst, sgst, slst, norms, ex, tl)


def _outproj_s_kernel(y_ref, w_ref, x_ref, o_ref):
    o_ref[...] = x_ref[...] + _mm3(y_ref[...], w_ref[...])


def _outproj_s(y, w, x):
    return pl.pallas_call(
        _outproj_s_kernel,
        out_shape=jax.ShapeDtypeStruct(x.shape, F32),
        compiler_params=pltpu.CompilerParams(vmem_limit_bytes=VMEM_LIMIT),
        name="outproj_s",
    )(y, w, x)


def _ffn_s_kernel(x_ref, g_ref, wg_ref, wu_ref, wd_ref, gf_ref, o_ref, h_ref, acc_ref, *, final_norm):
    j = pl.program_id(0)

    @pl.when(j == 0)
    def _():
        h_ref[...] = _rms(x_ref[...], g_ref[...])
        acc_ref[...] = jnp.zeros_like(acc_ref)

    h = h_ref[...]
    acc_ref[...] += _mm3(_silu(_mm3(h, wg_ref[...])) * _mm3(h, wu_ref[...]), wd_ref[...])

    @pl.when(j == pl.num_programs(0) - 1)
    def _():
        out = x_ref[...] + acc_ref[...]
        o_ref[...] = _rms(out, gf_ref[...]) if final_norm else out


def _ffn_s(x, g, wg, wu, wd, gf, tf, final_norm):
    t = x.shape[0]
    dff = wg.shape[1]
    full = lambda j: (0, 0)
    return pl.pallas_call(
        functools.partial(_ffn_s_kernel, final_norm=final_norm),
        grid=(dff // tf,),
        in_specs=[pl.BlockSpec((t, D_MODEL), full),
                  pl.BlockSpec((1, D_MODEL), full),
                  pl.BlockSpec((D_MODEL, tf), lambda j: (0, j)),
                  pl.BlockSpec((D_MODEL, tf), lambda j: (0, j)),
                  pl.BlockSpec((tf, D_MODEL), lambda j: (j, 0)),
                  pl.BlockSpec((1, D_MODEL), full)],
        out_specs=pl.BlockSpec((t, D_MODEL), full),
        out_shape=jax.ShapeDtypeStruct((t, D_MODEL), F32),
        scratch_shapes=[pltpu.VMEM((t, D_MODEL), F32), pltpu.VMEM((t, D_MODEL), F32)],
        compiler_params=_cparams(("arbitrary",)),
        name="ffn_s",
    )(x, g, wg, wu, wd, gf)


def _outproj_kernel(y_ref, w_ref, x_ref, o_ref):
    o_ref[...] = x_ref[...] + _dot(y_ref[...], w_ref[...])


def _outproj(y, w, x, tm):
    t = x.shape[0]
    return pl.pallas_call(
        _outproj_kernel,
        grid=(t // tm,),
        in_specs=[pl.BlockSpec((tm, D_MODEL), lambda i: (i, 0)),
                  pl.BlockSpec((D_MODEL, D_MODEL), lambda i: (0, 0)),
                  pl.BlockSpec((tm, D_MODEL), lambda i: (i, 0))],
        out_specs=pl.BlockSpec((tm, D_MODEL), lambda i: (i, 0)),
        out_shape=jax.ShapeDtypeStruct((t, D_MODEL), F32),
        compiler_params=_cparams(("parallel",)),
        name="outproj",
    )(y, w, x)


def _ffn_kernel(x_ref, g_ref, wg_ref, wu_ref, wd_ref, gf_ref, o_ref, h_ref, acc_ref, *, final_norm):
    j = pl.program_id(1)

    @pl.when(j == 0)
    def _():
        h_ref[...] = _rms(x_ref[...], g_ref[...]).astype(BF16)
        acc_ref[...] = jnp.zeros_like(acc_ref)

    h = h_ref[...]
    acc_ref[...] += _dot((_silu(_dot(h, wg_ref[...])) * _dot(h, wu_ref[...])).astype(BF16), wd_ref[...])

    @pl.when(j == pl.num_programs(1) - 1)
    def _():
        out = x_ref[...] + acc_ref[...]
        o_ref[...] = _rms(out, gf_ref[...]) if final_norm else out


def _ffn(x, g, wg, wu, wd, gf, tm, tf, final_norm):
    t = x.shape[0]
    dff = wg.shape[1]
    return pl.pallas_call(
        functools.partial(_ffn_kernel, final_norm=final_norm),
        grid=(t // tm, dff // tf),
        in_specs=[pl.BlockSpec((tm, D_MODEL), lambda i, j: (i, 0)),
                  pl.BlockSpec((1, D_MODEL), lambda i, j: (0, 0)),
                  pl.BlockSpec((D_MODEL, tf), lambda i, j: (0, j)),
                  pl.BlockSpec((D_MODEL, tf), lambda i, j: (0, j)),
                  pl.BlockSpec((tf, D_MODEL), lambda i, j: (j, 0)),
                  pl.BlockSpec((1, D_MODEL), lambda i, j: (0, 0))],
        out_specs=pl.BlockSpec((tm, D_MODEL), lambda i, j: (i, 0)),
        out_shape=jax.ShapeDtypeStruct((t, D_MODEL), F32),
        scratch_shapes=[pltpu.VMEM((tm, D_MODEL), BF16), pltpu.VMEM((tm, D_MODEL), F32)],
        compiler_params=_cparams(("parallel", "arbitrary")),
        name="ffn",
    )(x, g, wg, wu, wd, gf)


def _top2_gates(logits):
    return _top2_select(logits)[0]


def _top2_select(logits):
    lane = _iota2((1, LANES), 1)
    valid = lane < N_EXPERTS
    logits = jnp.where(valid, logits, -jnp.inf)
    ex = jnp.exp(logits - jnp.max(logits, axis=-1, keepdims=True))
    probs = ex / jnp.sum(ex, axis=-1, keepdims=True)
    v1 = jnp.max(probs, axis=-1, keepdims=True)
    i1 = jnp.min(jnp.where(probs == v1, lane, LANES), axis=-1, keepdims=True)
    rest = jnp.where((lane == i1) | jnp.logical_not(valid), -1.0, probs)
    v2 = jnp.max(rest, axis=-1, keepdims=True)
    i2 = jnp.min(jnp.where(rest == v2, lane, LANES), axis=-1, keepdims=True)
    tot = v1 + v2
    gates = jnp.where(lane == i1, v1 / tot, 0.0) + jnp.where(lane == i2, v2 / tot, 0.0)
    return gates, ((lane == i1) | (lane == i2)).astype(F32)


def _moe_kernel(x_ref, g_ref, wr_ref, br_ref, wg_ref, wu_ref, wd_ref, gf_ref, o_ref,
                h_ref, acc_ref, gates_ref, *, final_norm, precise_router):
    e = pl.program_id(1)

    @pl.when(e == 0)
    def _():
        h = _rms(x_ref[...], g_ref[...])
        h_ref[...] = h.astype(BF16)
        acc_ref[...] = jnp.zeros_like(acc_ref)
        if precise_router:
            logits = _mm3(h, wr_ref[...])
        else:
            logits = _dot(h.astype(BF16), wr_ref[...].astype(BF16))
        gates_ref[...] = _top2_gates(logits + br_ref[...])

    h = h_ref[...]
    y = _dot((_silu(_dot(h, wg_ref[0])) * _dot(h, wu_ref[0])).astype(BF16), wd_ref[0])
    lane = _iota2((1, LANES), 1)
    ge = jnp.sum(jnp.where(lane == e, gates_ref[...], 0.0), axis=-1, keepdims=True)
    acc_ref[...] += ge * y

    @pl.when(e == pl.num_programs(1) - 1)
    def _():
        out = x_ref[...] + acc_ref[...]
        o_ref[...] = _rms(out, gf_ref[...]) if final_norm else out


def _moe(x, g, wr, br, wg, wu, wd, gf, tm, final_norm, precise_router):
    t = x.shape[0]
    fe = wg.shape[2]
    return pl.pallas_call(
        functools.partial(_moe_kernel, final_norm=final_norm, precise_router=precise_router),
        grid=(t // tm, N_EXPERTS),
        in_specs=[pl.BlockSpec((tm, D_MODEL), lambda i, e: (i, 0)),
                  pl.BlockSpec((1, D_MODEL), lambda i, e: (0, 0)),
                  pl.BlockSpec((D_MODEL, LANES), lambda i, e: (0, 0)),
                  pl.BlockSpec((1, LANES), lambda i, e: (0, 0)),
                  pl.BlockSpec((1, D_MODEL, fe), lambda i, e: (e, 0, 0)),
                  pl.BlockSpec((1, D_MODEL, fe), lambda i, e: (e, 0, 0)),
                  pl.BlockSpec((1, fe, D_MODEL), lambda i, e: (e, 0, 0)),
                  pl.BlockSpec((1, D_MODEL), lambda i, e: (0, 0))],
        out_specs=pl.BlockSpec((tm, D_MODEL), lambda i, e: (i, 0)),
        out_shape=jax.ShapeDtypeStruct((t, D_MODEL), F32),
        scratch_shapes=[pltpu.VMEM((tm, D_MODEL), BF16), pltpu.VMEM((tm, D_MODEL), F32),
                        pltpu.VMEM((tm, LANES), F32)],
        compiler_params=_cparams(("parallel", "arbitrary")),
        name="moe",
    )(x, g, wr, br, wg, wu, wd, gf)


def _moe_win_kernel(x_ref, g_ref, wr_ref, br_ref, tril_ref, wg_ref, wu_ref, wd_ref, gf_ref, o_ref,
                    h_ref, gate_ref, key_ref, keyt_ref, *, final_norm, cap, tc):
    e = pl.program_id(1)
    w_rows = x_ref.shape[0]
    lane = _iota2((1, LANES), 1)

    @pl.when(e == 0)
    def _():
        x = x_ref[...]
        hb = _rms(x, g_ref[...]).astype(BF16)
        h_ref[...] = hb
        o_ref[...] = x
        gates, sel = _top2_select(_dot(hb, wr_ref[...].astype(BF16)) + br_ref[...])
        rank = _dot(tril_ref[...], sel.astype(BF16))
        key = jnp.where(sel > 0.0, rank, -1.0)
        gate_ref[...] = gates
        key_ref[...] = key
        keyt_ref[...] = key.T

    key_r = keyt_ref[pl.ds(e, 1), :]
    count = jnp.sum((key_r >= 0.0).astype(F32), axis=-1, keepdims=True)[0, 0].astype(jnp.int32)
    pick_e = (_iota2((LANES, LANES), 0) == e).astype(BF16)
    key_c = _sel_r(key_ref[...], pick_e)
    key_c = jnp.concatenate([key_c] * pl.cdiv(cap, LANES), axis=1)[:, :cap]
    gate_c = _sel_r(gate_ref[...], pick_e)
    gate_c = jnp.concatenate([gate_c] * (tc // LANES), axis=1)
    row_id = _iota2((cap, w_rows), 0).astype(F32)
    col_id = _iota2((w_rows, cap), 1).astype(F32)

    def trip(j, carry):
        base = (j * cap).astype(F32)
        gather = jnp.where(row_id + base == key_r, 1.0, 0.0).astype(BF16)
        xg = _dot(gather, h_ref[...]).astype(BF16)
        act = (_silu(_dot(xg, wg_ref[0])) * _dot(xg, wu_ref[0])).astype(BF16)
        y1, y2 = _split2(_dot(act, wd_ref[0]))
        scatter = jnp.where(col_id + base == key_c, 1.0, 0.0).astype(BF16)
        for c0 in range(0, D_MODEL, tc):
            cs = slice(c0, c0 + tc)
            o_ref[:, cs] += gate_c * (_dot(scatter, y1[:, cs]) + _dot(scatter, y2[:, cs]))
        return carry

    lax.fori_loop(0, (count + cap - 1) // cap, trip, 0)

    if final_norm:
        @pl.when(e == pl.num_programs(1) - 1)
        def _():
            o_ref[...] = _rms(o_ref[...], gf_ref[...])


def _moe_win(x, g, wr, br, wg, wu, wd, gf, w_rows, final_norm):
    t = x.shape[0]
    fe = wg.shape[2]
    cap = max(16, int(w_rows / 4 + 2.3 * (w_rows * 3 / 16) ** 0.5) // 16 * 16)
    tc = min(w_rows, 256)
    tril = jnp.asarray(np.tril(np.ones((w_rows, w_rows), np.float32), -1), dtype=BF16)
    const = lambda i, e: (0, 0)
    return pl.pallas_call(
        functools.partial(_moe_win_kernel, final_norm=final_norm, cap=cap, tc=tc),
        grid=(t // w_rows, N_EXPERTS),
        in_specs=[pl.BlockSpec((w_rows, D_MODEL), lambda i, e: (i, 0)),
                  pl.BlockSpec((1, D_MODEL), const),
                  pl.BlockSpec((D_MODEL, LANES), const),
                  pl.BlockSpec((1, LANES), const),
                  pl.BlockSpec((w_rows, w_rows), const),
                  pl.BlockSpec((1, D_MODEL, fe), lambda i, e: (e, 0, 0)),
                  pl.BlockSpec((1, D_MODEL, fe), lambda i, e: (e, 0, 0)),
                  pl.BlockSpec((1, fe, D_MODEL), lambda i, e: (e, 0, 0)),
                  pl.BlockSpec((1, D_MODEL), const)],
        out_specs=pl.BlockSpec((w_rows, D_MODEL), lambda i, e: (i, 0)),
        out_shape=jax.ShapeDtypeStruct((t, D_MODEL), F32),
        scratch_shapes=[pltpu.VMEM((w_rows, D_MODEL), BF16),
                        pltpu.VMEM((w_rows, LANES), F32),
                        pltpu.VMEM((w_rows, LANES), F32), pltpu.VMEM((LANES, w_rows), F32)],
        compiler_params=_cparams(("parallel", "arbitrary")),
        name="moe_win",
    )(x, g, wr, br, tril, wg, wu, wd, gf)


def _prep_w_in(w):
    seg = lambda i: w[:, _OFF[i]:_OFF[i + 1]]
    small = jnp.concatenate([seg(i) for i in _SMALL], axis=1)
    pad = jnp.zeros((w.shape[0], LANES - N_SMALL), w.dtype)
    wfull = jnp.concatenate([seg(i) for i in _BIG] + [small, pad], axis=1).astype(BF16)
    gates = jnp.concatenate([seg(i) for i in _GATES], axis=1)
    return wfull, gates.T.astype(BF16)


def _regroup_z(z):
    seg = lambda i: z[:, _OFF[i]:_OFF[i + 1]]
    pad = jnp.zeros((z.shape[0], LANES - N_SMALL), z.dtype)
    return jnp.concatenate([seg(i) for i in _BIG] + [seg(i) for i in _SMALL] + [pad], axis=1)


def _pad_lanes(v, start):
    out = jnp.zeros((1, LANES), F32)
    return lax.dynamic_update_slice(out, v.reshape(1, -1).astype(F32), (0, start))


def _layer_params(l, mlstm_b_i, mlstm_b_f, mlstm_norm, gdn_conv_w, gdn_a_log, gdn_dt_bias, gdn_norm,
                  sc_conv_w, gla_w_gate, gla_b_gate, gla_norm):
    bias = jnp.concatenate([mlstm_b_i[l], mlstm_b_f[l], gdn_dt_bias[l]]).astype(F32)
    alog = gdn_a_log[l].astype(F32)
    per_lane = lambda v: jnp.repeat(v, HEAD_DIM)
    zero = jnp.zeros((W_GROUP,), F32)
    brow = jnp.stack([per_lane(mlstm_b_i[l]), per_lane(mlstm_b_f[l]), per_lane(gdn_dt_bias[l]), zero])
    arow = jnp.stack([zero, zero, per_lane(alog), zero])
    wg32 = jnp.zeros((LANES, W_GROUP), F32).at[16:16 + GLA_RANK].set(gla_w_gate[l])
    return dict(bcol=_pad_lanes(bias, 0), acol=_pad_lanes(alog, 8), brow=brow, arow=arow,
                mnorm=mlstm_norm[l].reshape(1, -1), gnorm=gdn_norm[l].reshape(1, -1),
                lnorm=gla_norm[l].reshape(1, -1),
                gcw=gdn_conv_w[l].T, scw=sc_conv_w[l].T, wg=wg32.astype(BF16), wg32=wg32,
                bg=gla_b_gate[l].reshape(1, -1))


def _pick_tile(n, pref):
    for c in pref:
        if n % c == 0:
            return c
    return n


def _heads_major(a):
    return a.reshape(a.shape[0], N_HEADS, HEAD_DIM).transpose(1, 0, 2)


def kernel(x_prompt, x_sample, state_mlstm_C, state_mlstm_n, state_mlstm_m, state_gdn_S, state_gdn_conv,
           state_sc_conv, state_gla_S, w_in, g_mix, mlstm_b_i, mlstm_b_f, mlstm_norm, gdn_conv_w, gdn_a_log,
           gdn_dt_bias, gdn_norm, sc_conv_w, gla_w_gate, gla_b_gate, gla_norm, w_out, g_ffn, ffn_w_gate,
           ffn_w_up, ffn_w_down, moe_w_router, moe_b_router, moe_w_gate, moe_w_up, moe_w_down, g_final):
    depth = w_in.shape[0]
    bsz, seq, _ = x_prompt.shape
    bn = x_sample.shape[0]
    tp = bsz * seq
    assert x_sample.shape[1] == 1 and seq % CHUNK == 0

    xp = x_prompt.reshape(tp, D_MODEL)
    xs = x_sample.reshape(bn, D_MODEL)
    lb_rows = _pick_tile(seq, (128, 64))
    tm_p = _pick_tile(tp, (256, 128, 64))
    tm_f = _pick_tile(tp, (1024, 512, 256, 128, 64))
    tm_e = _pick_tile(tp, (1024, 512, 256, 128))
    nseq = _pick_tile(bsz, (4, 2, 1))

    eye = jnp.eye(HEAD_DIM, dtype=BF16)
    ex = jnp.repeat(eye, HEAD_DIM, axis=1)
    tl = jnp.tile(eye, (1, HEAD_DIM))
    gfin = g_final.reshape(1, -1)

    p_states = [[] for _ in range(7)]
    s_states = [[] for _ in range(7)]
    for l in range(depth):
        p = _layer_params(l, mlstm_b_i, mlstm_b_f, mlstm_norm, gdn_conv_w, gdn_a_log, gdn_dt_bias,
                          gdn_norm, sc_conv_w, gla_w_gate, gla_b_gate, gla_norm)
        wfull, wst = _prep_w_in(w_in[l])
        gm = g_mix[l].reshape(1, -1)
        gf = g_ffn[l].reshape(1, -1)
        last = l == depth - 1

        z, zt = _inproj(xp, gm, wfull, wst, tm_p, lb_rows)
        y, c1, n1, m1, sg1, gb1, sb1, sl1 = _mix_prompt(
            z.reshape(bsz, seq, NP), zt.reshape((bsz, seq // lb_rows) + zt.shape[1:]), p,
            bsz, seq, lb_rows, nseq)
        xp = _outproj(y.reshape(tp, D_MODEL), w_out[l].astype(BF16), xp, tm_p)
        for lst, v in zip(p_states, (c1, n1, m1.reshape(bsz, N_HEADS), sg1, gb1, sb1, sl1)):
            lst.append(v)

        zs = _regroup_z(_inproj_s(xs, gm, w_in[l]))
        gbuf = state_gdn_conv[l].reshape(bn, -1)
        sbuf = state_sc_conv[l].reshape(bn, -1)
        qkv, glog, act, yc, gbuf1, sbuf1 = _sample_pre(zs, gbuf, sbuf, p)
        gates = jnp.pad(act[:, :16].reshape(bn, 4, N_HEADS).transpose(2, 0, 1),
                        ((0, 0), (0, 0), (0, HEAD_DIM - 4)))
        parts = ([_heads_major(zs[:, o:o + W_GROUP]) for o in (MQ, MK, MV, MO)] +
                 [_heads_major(qkv[:, o:o + W_GROUP]) for o in (0, W_GROUP, 2 * W_GROUP)] +
                 [_heads_major(zs[:, o:o + W_GROUP]) for o in (GZ, LQ, LK, LV, LR)] +
                 [_heads_major(glog), gates])
        zh = jnp.concatenate(parts, axis=-1)
        norms = jnp.concatenate([_heads_major(v.reshape(1, -1)) for v in
                                 (mlstm_norm[l], gdn_norm[l], gla_norm[l])], axis=-1)
        hh = HEAD_DIM * HEAD_DIM
        ysr, c2, n2, m2, sg2, sl2 = _sample_rec(
            zh, state_mlstm_C[l].reshape(bn, N_HEADS * hh),
            state_mlstm_n[l].transpose(1, 0, 2),
            state_mlstm_m[l].T.reshape(N_HEADS, bn, 1),
            state_gdn_S[l].reshape(bn, N_HEADS * hh), state_gla_S[l].reshape(bn, N_HEADS * hh),
            norms, ex, tl)
        unhead = lambda a: a.transpose(1, 0, 2).reshape(bn, W_GROUP)
        ymix = jnp.concatenate([unhead(ysr[:, :, 0:HEAD_DIM]), unhead(ysr[:, :, HEAD_DIM:2 * HEAD_DIM]),
                                yc, unhead(ysr[:, :, 2 * HEAD_DIM:])], axis=-1)
        xs = _outproj_s(ymix, w_out[l], xs)
        st_shape = (bn, N_HEADS, HEAD_DIM, HEAD_DIM)
        for lst, v in zip(s_states, (c2.reshape(st_shape), n2.transpose(1, 0, 2),
                                     m2.reshape(N_HEADS, bn).T, sg2.reshape(st_shape),
                                     gbuf1.reshape(bn, 3, -1), sbuf1.reshape(bn, 2, -1),
                                     sl2.reshape(st_shape))):
            lst.append(v)

        j = l // 2
        if l % 2 == 0:
            tf = _pick_tile(ffn_w_gate.shape[2], (256, 128))
            xp = _ffn(xp, gf, ffn_w_gate[j].astype(BF16), ffn_w_up[j].astype(BF16),
                      ffn_w_down[j].astype(BF16), gfin, tm_f, tf, last)
            xs = _ffn_s(xs, gf, ffn_w_gate[j], ffn_w_up[j], ffn_w_down[j], gfin, tf, last)
        else:
            wr = jnp.zeros((D_MODEL, LANES), F32).at[:, :N_EXPERTS].set(moe_w_router[j])
            br = _pad_lanes(moe_b_router[j], 0)
            wgt, wup, wdn = (moe_w_gate[j].astype(BF16), moe_w_up[j].astype(BF16),
                             moe_w_down[j].astype(BF16))
            xp = _moe_win(xp, gf, wr, br, wgt, wup, wdn, gfin, tm_e, last)
            xs = _moe(xs, gf, wr, br, wgt, wup, wdn, gfin, bn, last, True)

    y_prompt = xp.reshape(bsz, seq, D_MODEL)
    y_sample = xs.reshape(bn, 1, D_MODEL)
    sp = [jnp.stack(v) for v in p_states]
    ss = [jnp.stack(v) for v in s_states]
    return (y_prompt, y_sample, sp[0], ss[0], sp[1], ss[1], sp[2], ss[2], sp[3], ss[3],
            sp[4], ss[4], sp[5], ss[5], sp[6], ss[6])
```

```python
import functools

import numpy as np
import jax
import jax.numpy as jnp
from jax import lax
from jax.experimental import pallas as pl
from jax.experimental.pallas import tpu as pltpu

F32 = jnp.float32
BF16 = jnp.bfloat16

D_MODEL = 1024
W_GROUP = 256
N_HEADS = 4
HEAD_DIM = 64
CHUNK = 64
GLA_RANK = 16
GLA_NORMALIZER = 16.0
N_EXPERTS = 8
EPS = 1e-6
Q_SCALE = HEAD_DIM ** -0.5

VMEM_LIMIT = 56 * 1024 * 1024
LANES = 128

SPLIT_SIZES = ([W_GROUP] * 4 + [N_HEADS] * 2 + [W_GROUP] * 4 + [N_HEADS] * 2 +
               [W_GROUP] * 3 + [W_GROUP] * 4 + [GLA_RANK])
_OFF = np.concatenate([[0], np.cumsum(SPLIT_SIZES)])
_BIG = [0, 1, 2, 3, 6, 7, 8, 9, 12, 13, 14, 15, 16, 17, 18]
_SMALL = [4, 5, 10, 11, 19]
_GATES = [4, 5, 10, 11]
MQ, MK, MV, MO = 0, 256, 512, 768
GQ, GK, GV, GZ = 1024, 1280, 1536, 1792
SB, SC, SH = 2048, 2304, 2560
LQ, LK, LV, LR = 2816, 3072, 3328, 3584
SM = 3840
NP = SM + LANES
N_SMALL = 32
N_GATE_ROWS = 16
N_ROW_KINDS = 3
GX_I, GX_B, GX_G, GX_BETA, GX_PM = 0, 256, 512, 768, 1024
GLA_LEVELS = (32, 16, 8, 4, 2)


def _cparams(sem):
    return pltpu.CompilerParams(dimension_semantics=sem, vmem_limit_bytes=VMEM_LIMIT)


def _log1pexp_negabs(x):
    return jnp.log(1.0 + jnp.exp(-jnp.abs(x)))


def _softplus(x):
    return jnp.maximum(x, 0.0) + _log1pexp_negabs(x)


def _logsigmoid(x):
    return -(jnp.maximum(-x, 0.0) + _log1pexp_negabs(x))


def _sigmoid(x):
    return 1.0 / (1.0 + jnp.exp(-x))


def _silu(x):
    return x * _sigmoid(x)


def _rms(x, g):
    return x * lax.rsqrt(jnp.mean(x * x, axis=-1, keepdims=True) + EPS) * g


def _dot(a, b):
    return jnp.dot(a, b, preferred_element_type=F32)


def _dot_nt(a, b):
    return lax.dot_general(a, b, (((1,), (1,)), ((), ())), preferred_element_type=F32)


def _dot_tn(a, b):
    return lax.dot_general(a, b, (((0,), (0,)), ((), ())), preferred_element_type=F32)


def _split2(x):
    x1 = x.astype(BF16)
    return x1, (x - x1.astype(F32)).astype(BF16)


def _split3(x):
    x1 = x.astype(BF16)
    r1 = x - x1.astype(F32)
    x2 = r1.astype(BF16)
    x3 = (r1 - x2.astype(F32)).astype(BF16)
    return x1, x2, x3


def _sel(m, x):
    x1, x2, x3 = _split3(x)
    return _dot(m, x1) + _dot(m, x2) + _dot(m, x3)


def _sel_r(x, m):
    n = x.shape[0]
    if n % 8:
        x1, x2, x3 = _split3(x)
        return _dot(x1, m) + _dot(x2, m) + _dot(x3, m)
    r = _dot(jnp.concatenate(_split3(x), axis=0), m)
    return r[0:n] + r[n:2 * n] + r[2 * n:3 * n]


def _sel_nt(m, x):
    x1, x2, x3 = _split3(x)
    return _dot_nt(m, x1) + _dot_nt(m, x2) + _dot_nt(m, x3)


def _mm3(x, w):
    n = x.shape[0]
    x1, x2 = _split2(x)
    w1, w2 = _split2(w)
    r = _dot(jnp.concatenate([x1, x2], axis=0), w1)
    return r[0:n] + r[n:] + _dot(x1, w2)


def _iota2(shape, dim):
    return lax.broadcasted_iota(jnp.int32, shape, dim)


def _inproj_kernel(x_ref, g_ref, w_ref, wst_ref, z_ref, zt_ref):
    hb = _rms(x_ref[...], g_ref[...]).astype(BF16)
    z_ref[...] = _dot(hb, w_ref[...])
    zt = _dot_nt(wst_ref[...], hb)
    n_blk, n_kind, n_chunks, _ = zt_ref.shape
    for b in range(n_blk):
        for k in range(n_kind):
            for c in range(n_chunks):
                t0 = (b * n_chunks + c) * CHUNK
                for h in range(N_HEADS):
                    zt_ref[b, k, c:c + 1, h * HEAD_DIM:(h + 1) * HEAD_DIM] = (
                        zt[k * 4 + h:k * 4 + h + 1, t0:t0 + CHUNK])


def _inproj(x, g, w, wst, tm, lb_rows):
    t = x.shape[0]
    n_chunks = lb_rows // CHUNK
    return pl.pallas_call(
        _inproj_kernel,
        grid=(t // tm,),
        in_specs=[pl.BlockSpec((tm, D_MODEL), lambda i: (i, 0)),
                  pl.BlockSpec((1, D_MODEL), lambda i: (0, 0)),
                  pl.BlockSpec((D_MODEL, NP), lambda i: (0, 0)),
                  pl.BlockSpec((N_GATE_ROWS, D_MODEL), lambda i: (0, 0))],
        out_specs=[pl.BlockSpec((tm, NP), lambda i: (i, 0)),
                   pl.BlockSpec((tm // lb_rows, N_ROW_KINDS, n_chunks, W_GROUP), lambda i: (i, 0, 0, 0))],
        out_shape=[jax.ShapeDtypeStruct((t, NP), F32),
                   jax.ShapeDtypeStruct((t // lb_rows, N_ROW_KINDS, n_chunks, W_GROUP), F32)],
        compiler_params=_cparams(("parallel",)),
        name="inproj",
    )(x, g, w, wst)


def _gate_act(pre, idx, neg_a):
    tail = _log1pexp_negabs(pre)
    lf = -(jnp.maximum(-pre, 0.0) + tail)
    dec = neg_a * (jnp.maximum(pre, 0.0) + tail)
    beta = _sigmoid(pre)
    return jnp.where(idx < 4, pre,
                     jnp.where(idx < 8, lf,
                               jnp.where(idx < 12, dec,
                                         jnp.where(idx < 16, beta, 0.0))))


def _mix_kernel(z_ref, zt_ref, bcol_ref, acol_ref, brow_ref, arow_ref, mnorm_ref, gnorm_ref,
                lnorm_ref, gcw_ref, scw_ref, wg_ref, bg_ref,
                bdmask_ref, bdtriu_ref, bdtril_ref, selexp_ref, selstack_ref, eye64_ref,
                y_ref, c_out, n_out, m_out, sg_out, gconv_out, sconv_out, sl_out,
                c_ref, n_ref, m_ref, sg_ref, slt_ref, xg_ref, xs_ref, qkv_ref, gcum_ref, glog_ref,
                gx_ref, rw_ref, *, lb_rows, nseq):
    lb = pl.program_id(1)
    nlb = pl.num_programs(1)
    n_chunks = lb_rows // CHUNK
    wd = W_GROUP
    seqs = range(nseq)

    @pl.when(lb == 0)
    def _():
        c_ref[...] = jnp.zeros_like(c_ref)
        n_ref[...] = jnp.zeros_like(n_ref)
        m_ref[...] = jnp.zeros_like(m_ref)
        sg_ref[...] = jnp.zeros_like(sg_ref)
        slt_ref[...] = jnp.zeros_like(slt_ref)
        for i in seqs:
            xg_ref[i, 0:8, :] = jnp.zeros((8, 3 * wd), F32)
            xs_ref[i, 0:8, :] = jnp.zeros((8, wd), F32)

    ti = _iota2((CHUNK, wd), 0)
    si = _iota2((CHUNK, wd), 1) % HEAD_DIM
    incl = ti >= si
    strict = ti > si
    eye = ti == si
    blk16 = (ti // 16) == (si // 16)
    blk32 = (ti // 32) == (si // 32)
    eye_f = eye.astype(F32)
    lvl_mask = {n: ((ti // (2 * n)) == (si // (2 * n))) & ((ti // n) > (si // n))
                for n in GLA_LEVELS + (1,)}
    bdmask = bdmask_ref[...]
    bdtriu = bdtriu_ref[...]
    bdtril = bdtril_ref[...]
    head_of_lane = _iota2((1, wd), 1) // HEAD_DIM
    pos_of_lane = _iota2((8, wd), 1) % HEAD_DIM

    def bd(x):
        xb = x.astype(BF16)
        return jnp.concatenate([xb, xb, xb, xb], axis=0) * bdmask

    def unbd(m):
        out = m[3 * HEAD_DIM:, :]
        for h in (2, 1, 0):
            out = jnp.where(head_of_lane == h, m[h * HEAD_DIM:(h + 1) * HEAD_DIM, :], out)
        return out

    def mm(a, bmat):
        return _dot(a.astype(BF16), bmat)

    def mm_nt(a, bmat):
        return _dot_nt(a.astype(BF16), bmat)

    def mm3_bd(a, b):
        a1, a2 = _split2(a)
        b1, b2 = _split2(b)
        r = _dot(jnp.concatenate([a1, a2], axis=0), bd(b1))
        return r[0:CHUNK] + r[CHUNK:] + _dot(a1, bd(b2))

    def hsum(x):
        x1, x2 = _split2(x)
        r = _dot(jnp.concatenate([x1, x2], axis=0), bdmask)
        return r[0:x.shape[0]] + r[x.shape[0]:]

    def headnorm(o, w):
        return o * lax.rsqrt(hsum(o * o) * (1.0 / HEAD_DIM) + EPS) * w

    lane = _iota2((1, LANES), 1)
    neg_a_col = -jnp.exp(acol_ref[...])
    neg_a_row = -jnp.exp(arow_ref[2:3, :])
    selexp = selexp_ref[...]
    sel_stack = selstack_ref[...]
    eye64 = eye64_ref[...]

    def l2n(x):
        return x * lax.rsqrt(hsum(x * x) + EPS)

    for i in seqs:
        small = z_ref[i, :, SM:SM + LANES]
        act = _gate_act(small + bcol_ref[...], lane, neg_a_col)
        gx4 = _sel_r(act, selexp)
        gx_ref[i, :, GX_I:GX_I + wd] = gx4[:, 0:wd]
        gx_ref[i, :, GX_BETA:GX_BETA + wd] = gx4[:, 3 * wd:]
        gx_ref[i, :, GX_B:GX_B + 2 * wd] = _sel(bdtril, gx4[:, wd:3 * wd])
        glin = _dot(small.astype(BF16), wg_ref[...]) + bg_ref[...]
        glog = _logsigmoid(glin) * (1.0 / GLA_NORMALIZER)
        glog_ref[i] = glog
        gcum_ref[i] = _sel(bdtril, glog)

        ic_r = zt_ref[i, 0, 0] + brow_ref[0:1, :]
        lf_r = _logsigmoid(zt_ref[i, 0, 1] + brow_ref[1:2, :])
        dec_r = neg_a_row * _softplus(zt_ref[i, 0, 2] + brow_ref[2:3, :])
        cs_r = _sel_r(jnp.concatenate([lf_r, dec_r], axis=0), bdtriu)
        c_rows = ic_r - cs_r[0:n_chunks, :]
        g_rows = cs_r[n_chunks:, :]
        pm = jnp.concatenate([c_rows] * (8 // n_chunks), axis=0)
        for sh in (1, 2, 4, 8, 16, 32):
            pm = jnp.where(pos_of_lane >= sh, jnp.maximum(pm, pltpu.roll(pm, sh, axis=1)), pm)
        for c in range(n_chunks):
            rw_ref[i, c, 0:1, :] = c_rows[c:c + 1, :]
            rw_ref[i, c, 1:2, :] = g_rows[c:c + 1, :]
        diag = jnp.concatenate(
            [jnp.where(eye, jnp.broadcast_to(pm[c:c + 1, :], (CHUNK, wd)), 0.0)
             for c in range(n_chunks)], axis=0)
        gx_ref[i, :, GX_PM:GX_PM + wd] = _sel_r(diag, bdmask)

        xg_ref[i, 8:, :] = z_ref[i, :, GQ:GQ + 3 * wd]
        conv = xg_ref[i, 5:5 + lb_rows, :] * gcw_ref[0:1, :]
        for j in range(1, 4):
            conv = conv + xg_ref[i, 5 + j:5 + j + lb_rows, :] * gcw_ref[j:j + 1, :]
        conv = _silu(conv)
        qkv_ref[i, :, 0:wd] = l2n(conv[:, 0:wd]) * Q_SCALE
        qkv_ref[i, :, wd:2 * wd] = l2n(conv[:, wd:2 * wd])
        qkv_ref[i, :, 2 * wd:] = conv[:, 2 * wd:]

        xs_ref[i, 8:, :] = z_ref[i, :, SC:SC + wd] * z_ref[i, :, SH:SH + wd]
        cu = xs_ref[i, 6:6 + lb_rows, :] * scw_ref[0:1, :]
        for j in range(1, 3):
            cu = cu + xs_ref[i, 6 + j:6 + j + lb_rows, :] * scw_ref[j:j + 1, :]
        y_ref[i, :, 2 * wd:3 * wd] = (z_ref[i, :, SB:SB + wd] * cu).astype(y_ref.dtype)

    def chunk(c, carry):
        r0 = pl.multiple_of(c * CHUNK, CHUNK)
        rs = pl.ds(r0, CHUNK)
        each = lambda f, *xs: [f(*a) for a in zip(*xs)]
        zcol = lambda o: [z_ref[i, rs, o:o + wd] for i in seqs]
        gxcol = lambda o: [gx_ref[i, rs, o:o + wd] for i in seqs]
        last = lambda xs: [x[CHUNK - 1:CHUNK, :] for x in xs]
        tn_bd = lambda a, b: each(lambda x, y: unbd(_dot_tn(x.astype(BF16), y.astype(BF16))), a, b)
        mm3s = lambda a, b: each(mm3_bd, a, b)
        rw = [rw_ref[i, c, 0:2, :] for i in seqs]

        q = zcol(MQ)
        k = [x * Q_SCALE for x in zcol(MK)]
        v = zcol(MV)
        ic_e, b_e, pm_e = gxcol(GX_I), gxcol(GX_B), gxcol(GX_PM)
        m_old = [m_ref[i, 0:1, :] for i in seqs]
        cmat = [c_ref[i] for i in seqs]
        nrow = [n_ref[i, 0:1, :] for i in seqs]
        mx = each(jnp.maximum, m_old, pm_e)
        w_int = each(lambda a, b: jnp.exp(a - b), m_old, mx)
        p = each(lambda r, m: jnp.where(incl, jnp.exp(r[0:1, :] - m), 0.0), rw, mx)
        s = each(lambda a, b, c_: mm_nt(a, bd(b)) * c_, q, k, p)
        num = each(lambda w_, a, cm, s_, v_: w_ * mm(a, bd(cm)) + mm(s_, bd(v_)), w_int, q, cmat, s, v)
        den = each(lambda w_, a, n_, s_: hsum(w_ * (a * n_) + s_), w_int, q, nrow, s)
        hh = each(lambda n_, d_, b_, m: n_ / jnp.maximum(jnp.abs(d_), jnp.exp(-(b_ + m))), num, den, b_e, mx)
        mx_last = last(mx)
        wgk = each(lambda i_, b_, ml, k_: jnp.exp((i_ - b_) - ml) * k_, ic_e, b_e, mx_last, k)
        dec = each(lambda a, b: jnp.exp(a - b), m_old, mx_last)
        c_new = each(lambda d_, cm, u_: d_ * cm + u_, dec, cmat, tn_bd(wgk, v))
        for i in seqs:
            c_ref[i] = c_new[i]
            n_ref[i] = jnp.broadcast_to(dec[i] * nrow[i] + jnp.sum(wgk[i], axis=0, keepdims=True), (8, wd))
            m_ref[i] = jnp.broadcast_to(b_e[i][CHUNK - 1:CHUNK, :] + mx_last[i], (8, wd))
        y_a = each(lambda o_, h_: _sigmoid(o_) * headnorm(h_, mnorm_ref[...]), zcol(MO), hh)
        for i in seqs:
            y_ref[i, rs, 0:wd] = y_a[i].astype(y_ref.dtype)

        q = [qkv_ref[i, rs, 0:wd] for i in seqs]
        k = [qkv_ref[i, rs, wd:2 * wd] for i in seqs]
        v = [qkv_ref[i, rs, 2 * wd:] for i in seqs]
        g_e, beta = gxcol(GX_G), gxcol(GX_BETA)
        s_old = [sg_ref[i] for i in seqs]
        bdk = each(bd, k)
        lw = each(lambda g_, r: jnp.where(incl, jnp.exp(g_ - r[1:2, :]), 0.0), g_e, rw)
        amat = each(lambda b_, k_, bk, l_: jnp.where(strict, b_ * mm_nt(k_, bk) * l_, 0.0), beta, k, bdk, lw)
        egc = each(jnp.exp, g_e)
        ad = [jnp.where(blk16, a, 0.0) for a in amat]
        a2 = mm3s(ad, ad)
        a4 = mm3s(a2, a2)
        a8 = mm3s(a4, a4)
        tinv = [eye_f - a for a in ad]
        for apow in (a2, a4, a8):
            tinv = each(lambda t_, u_: t_ + u_, tinv, mm3s(tinv, apow))
        n1 = [jnp.where(blk32 & jnp.logical_not(blk16), a, 0.0) for a in amat]
        tinv = each(lambda t_, u_: t_ - u_, tinv, mm3s(mm3s(tinv, n1), tinv))
        n2 = [jnp.where(blk32, 0.0, a) for a in amat]
        tinv = each(lambda t_, u_: t_ - u_, tinv, mm3s(mm3s(tinv, n2), tinv))
        u = mm3s(tinv, each(lambda b_, v_: b_ * v_, beta, v))
        w = mm3s(tinv, each(lambda b_, e_, k_: (b_ * e_) * k_, beta, egc, k))
        bds = each(bd, s_old)
        vn = each(lambda u_, w_, bs: u_ - mm(w_, bs), u, w, bds)
        qk = each(lambda q_, bk, l_: mm_nt(q_, bk) * l_, q, bdk, lw)
        o = each(lambda q_, e_, bs, qk_, vn_: mm(q_ * e_, bs) + mm(qk_, bd(vn_)), q, egc, bds, qk, vn)
        g_last = last(g_e)
        kdec = each(lambda k_, gl, g_: k_ * jnp.exp(gl - g_), k, g_last, g_e)
        s_new = each(lambda gl, so, u_: jnp.exp(gl) * so + u_, g_last, s_old, tn_bd(kdec, vn))
        y_b = each(lambda o_, z_: headnorm(o_, gnorm_ref[...]) * _silu(z_), o, zcol(GZ))
        for i in seqs:
            sg_ref[i] = s_new[i]
            y_ref[i, rs, wd:2 * wd] = y_b[i].astype(y_ref.dtype)

        q = [x * Q_SCALE for x in zcol(LQ)]
        k = zcol(LK)
        v = zcol(LV)
        gk = [glog_ref[i, rs, :] for i in seqs]
        gcum = [gcum_ref[i, rs, :] for i in seqs]
        stack = [_sel(sel_stack, g_) for g_ in gcum]
        st_old = [slt_ref[i] for i in seqs]
        bdk = each(bd, k)
        amat = each(lambda q_, bk: jnp.where(eye, mm_nt(q_, bk), 0.0), q, bdk)
        amat = each(lambda a_, q_, g_, bk: a_ + jnp.where(lvl_mask[1], mm_nt(q_ * jnp.exp(g_), bk), 0.0),
                    amat, q, gk, bdk)
        for j, n in enumerate(GLA_LEVELS):
            g_end = [s_[2 * j * CHUNK:(2 * j + 1) * CHUNK, :] for s_ in stack]
            g_prev = [s_[(2 * j + 1) * CHUNK:(2 * j + 2) * CHUNK, :] for s_ in stack]
            qn = each(lambda q_, gc, gp: q_ * jnp.exp(gc - gp), q, gcum, g_prev)
            kn = each(lambda k_, ge, gc: k_ * jnp.exp(ge - gc), k, g_end, gcum)
            amat = each(lambda a_, q_, k_, m=lvl_mask[n]: a_ + jnp.where(m, mm_nt(q_, bd(k_)), 0.0),
                        amat, qn, kn)
        g_last = last(gcum)
        o = each(lambda q_, gc, st, a_, v_: mm_nt(q_ * jnp.exp(gc), bd(st)) + mm(a_, bd(v_)),
                 q, gcum, st_old, amat, v)
        kdec = each(lambda k_, gl, gc: k_ * jnp.exp(gl - gc), k, g_last, gcum)
        st_new = each(lambda st, gl, u_: st * jnp.exp(gl) + u_, st_old, g_last, tn_bd(v, kdec))
        y_d = each(lambda o_, z_: headnorm(o_, lnorm_ref[...]) * _silu(z_), o, zcol(LR))
        for i in seqs:
            slt_ref[i] = st_new[i]
            y_ref[i, rs, 3 * wd:] = y_d[i].astype(y_ref.dtype)
        return carry

    lax.fori_loop(0, n_chunks, chunk, 0)

    gtail = [xg_ref[i, lb_rows:lb_rows + 8, :] for i in seqs]
    stail = [xs_ref[i, lb_rows:lb_rows + 8, :] for i in seqs]
    for i in seqs:
        xg_ref[i, 0:8, :] = gtail[i]
        xs_ref[i, 0:8, :] = stail[i]

    @pl.when(lb == nlb - 1)
    def _():
        lane4 = _iota2((1, N_HEADS), 1)
        for i in seqs:
            gconv_out[i] = gtail[i][5:8, :]
            sconv_out[i] = stail[i][6:8, :]
            m_row = jnp.zeros((1, N_HEADS), F32)
            for h in range(N_HEADS):
                hs = slice(h * HEAD_DIM, (h + 1) * HEAD_DIM)
                c_out[i, h] = c_ref[i, :, hs]
                n_out[i, h:h + 1, :] = n_ref[i, 0:1, hs]
                m_row = jnp.where(lane4 == h, m_ref[i, 0:1, h * HEAD_DIM:h * HEAD_DIM + 1], m_row)
                sg_out[i, h] = sg_ref[i, :, hs]
                sl_out[i, h] = _sel_nt(eye64, slt_ref[i, :, hs])
            m_out[i] = m_row


def _mix_consts(lb_rows):
    wd = W_GROUP
    r, c = np.indices((wd, wd))
    same_head = (r // HEAD_DIM) == (c // HEAD_DIM)
    bdmask = same_head
    bdtriu = same_head & ((r % HEAD_DIM) <= (c % HEAD_DIM))
    rl, cl = np.indices((lb_rows, lb_rows))
    bdtril = ((rl // CHUNK) == (cl // CHUNK)) & (rl >= cl)
    rr, cc = np.indices((LANES, 4 * wd))
    selexp = rr == 4 * (cc // wd) + (cc % wd) // HEAD_DIM
    rows, cols = np.indices((2 * len(GLA_LEVELS) * CHUNK, CHUNK))
    r_in, blk = rows % CHUNK, rows // CHUNK
    sel_stack = np.zeros(rows.shape, bool)
    for i, n in enumerate(GLA_LEVELS):
        sel_stack |= (blk == 2 * i) & (cols == (r_in // n) * n + (n - 1))
        sel_stack |= (blk == 2 * i + 1) & (cols == (r_in // n) * n - 1)
    eye64 = np.eye(CHUNK, dtype=bool)
    return [jnp.asarray(m, dtype=BF16) for m in (bdmask, bdtriu, bdtril, selexp, sel_stack, eye64)]


def _mix_prompt(z, zt, p, bsz, seq, lb_rows, nseq):
    nlb = seq // lb_rows
    n_chunks = lb_rows // CHUNK
    row = lambda b, l: (b, l, 0)
    const2 = lambda b, l: (0, 0)
    st4 = lambda b, l: (b, 0, 0, 0)
    st3 = lambda b, l: (b, 0, 0)
    params = [p["bcol"], p["acol"], p["brow"], p["arow"], p["mnorm"], p["gnorm"], p["lnorm"],
              p["gcw"], p["scw"], p["wg"], p["bg"]] + _mix_consts(lb_rows)
    in_specs = [pl.BlockSpec((nseq, lb_rows, NP), row),
                pl.BlockSpec((nseq, 1, N_ROW_KINDS, n_chunks, W_GROUP), lambda b, l: (b, l, 0, 0, 0))]
    in_specs += [pl.BlockSpec(a.shape, const2) for a in params]
    hd = (N_HEADS, HEAD_DIM, HEAD_DIM)
    out_shape = [jax.ShapeDtypeStruct((bsz, seq, D_MODEL), BF16),
                 jax.ShapeDtypeStruct((bsz,) + hd, F32),
                 jax.ShapeDtypeStruct((bsz, N_HEADS, HEAD_DIM), F32),
                 jax.ShapeDtypeStruct((bsz, 1, N_HEADS), F32),
                 jax.ShapeDtypeStruct((bsz,) + hd, F32),
                 jax.ShapeDtypeStruct((bsz, 3, 3 * W_GROUP), F32),
                 jax.ShapeDtypeStruct((bsz, 2, W_GROUP), F32),
                 jax.ShapeDtypeStruct((bsz,) + hd, F32)]
    out_specs = [pl.BlockSpec((nseq, lb_rows, D_MODEL), row),
                 pl.BlockSpec((nseq,) + hd, st4),
                 pl.BlockSpec((nseq, N_HEADS, HEAD_DIM), st3),
                 pl.BlockSpec((nseq, 1, N_HEADS), st3),
                 pl.BlockSpec((nseq,) + hd, st4),
                 pl.BlockSpec((nseq, 3, 3 * W_GROUP), st3),
                 pl.BlockSpec((nseq, 2, W_GROUP), st3),
                 pl.BlockSpec((nseq,) + hd, st4)]
    vm = lambda *shape: pltpu.VMEM((nseq,) + shape, F32)
    scratch = [vm(HEAD_DIM, W_GROUP),
               vm(8, W_GROUP),
               vm(8, W_GROUP),
               vm(HEAD_DIM, W_GROUP),
               vm(HEAD_DIM, W_GROUP),
               vm(lb_rows + 8, 3 * W_GROUP),
               vm(lb_rows + 8, W_GROUP),
               vm(lb_rows, 3 * W_GROUP),
               vm(lb_rows, W_GROUP),
               vm(lb_rows, W_GROUP),
               vm(lb_rows, 5 * W_GROUP),
               vm(n_chunks, 8, W_GROUP)]
    return pl.pallas_call(
        functools.partial(_mix_kernel, lb_rows=lb_rows, nseq=nseq),
        grid=(bsz // nseq, nlb),
        in_specs=in_specs,
        out_specs=out_specs,
        out_shape=out_shape,
        scratch_shapes=scratch,
        compiler_params=_cparams(("parallel", "arbitrary")),
        name="mix_prompt",
    )(z, zt, *params)


def _inproj_s_kernel(x_ref, g_ref, w_ref, z_ref, h_ref):
    k = pl.program_id(0)
    n_k, _, tk = h_ref.shape

    @pl.when(k == 0)
    def _():
        h = _rms(x_ref[...], g_ref[...])
        for j in range(n_k):
            h_ref[j] = h[:, j * tk:(j + 1) * tk]
        z_ref[...] = jnp.zeros_like(z_ref)

    hk = h_ref[k]
    n = w_ref.shape[1]
    step = 4 * LANES
    for j in range(0, n, step):
        cs = slice(j, min(j + step, n))
        z_ref[:, cs] += _mm3(hk, w_ref[:, cs])


def _inproj_s(x, g, w_all, layer, tk=256):
    bn = x.shape[0]
    d_in = w_all.shape[2]
    return pl.pallas_call(
        _inproj_s_kernel,
        grid=(D_MODEL // tk,),
        in_specs=[pl.BlockSpec((bn, D_MODEL), lambda k: (0, 0)),
                  pl.BlockSpec((1, D_MODEL), lambda k: (0, 0)),
                  pl.BlockSpec((None, tk, d_in), lambda k: (layer, k, 0))],
        out_specs=pl.BlockSpec((bn, d_in), lambda k: (0, 0)),
        out_shape=jax.ShapeDtypeStruct((bn, d_in), F32),
        scratch_shapes=[pltpu.VMEM((D_MODEL // tk, bn, tk), F32)],
        compiler_params=_cparams(("arbitrary",)),
        name="inproj_s",
    )(x, g, w_all)


def _spre_kernel(z_ref, gbuf_ref, sbuf_ref, bcol_ref, acol_ref, gcw_ref, scw_ref, wg_ref, bg_ref,
                 zt_ref, qkvt_ref, glogt_ref, actt_ref, yc_ref, gbuf_out, sbuf_out):
    small = z_ref[:, SM:SM + LANES]
    lane = _iota2((1, LANES), 1)
    actt_ref[...] = _gate_act(small + bcol_ref[...], lane, -jnp.exp(acol_ref[...])).T
    glin = _mm3(small, wg_ref[...]) + bg_ref[...]
    glogt_ref[...] = (_logsigmoid(glin) * (1.0 / GLA_NORMALIZER)).T
    zt_ref[...] = z_ref[...].T

    wq = 3 * W_GROUP
    u = z_ref[:, GQ:GQ + wq]
    b0, b1, b2 = gbuf_ref[:, 0:wq], gbuf_ref[:, wq:2 * wq], gbuf_ref[:, 2 * wq:]
    conv = b0 * gcw_ref[0:1, :] + b1 * gcw_ref[1:2, :] + b2 * gcw_ref[2:3, :] + u * gcw_ref[3:4, :]
    conv = _silu(conv)
    gbuf_out[:, 0:wq] = b1
    gbuf_out[:, wq:2 * wq] = b2
    gbuf_out[:, 2 * wq:] = u
    hsum = (_iota2((W_GROUP, W_GROUP), 0) // HEAD_DIM ==
            _iota2((W_GROUP, W_GROUP), 1) // HEAD_DIM).astype(BF16)

    def l2n(x):
        return x * lax.rsqrt(_sel_r(x * x, hsum) + EPS)

    qkvt_ref[0:W_GROUP, :] = (l2n(conv[:, 0:W_GROUP]) * Q_SCALE).T
    qkvt_ref[W_GROUP:2 * W_GROUP, :] = l2n(conv[:, W_GROUP:2 * W_GROUP]).T
    qkvt_ref[2 * W_GROUP:, :] = conv[:, 2 * W_GROUP:].T

    u2 = z_ref[:, SC:SC + W_GROUP] * z_ref[:, SH:SH + W_GROUP]
    s0, s1 = sbuf_ref[:, 0:W_GROUP], sbuf_ref[:, W_GROUP:]
    cu = s0 * scw_ref[0:1, :] + s1 * scw_ref[1:2, :] + u2 * scw_ref[2:3, :]
    yc_ref[...] = z_ref[:, SB:SB + W_GROUP] * cu
    sbuf_out[:, 0:W_GROUP] = s1
    sbuf_out[:, W_GROUP:] = u2


def _sample_pre(z, gbuf, sbuf, p):
    bn = z.shape[0]
    sd = lambda s: jax.ShapeDtypeStruct(s, F32)
    return pl.pallas_call(
        _spre_kernel,
        out_shape=[sd((NP, bn)), sd((3 * W_GROUP, bn)), sd((W_GROUP, bn)), sd((LANES, bn)),
                   sd((bn, W_GROUP)), sd(gbuf.shape), sd(sbuf.shape)],
        compiler_params=pltpu.CompilerParams(vmem_limit_bytes=VMEM_LIMIT),
        name="sample_pre",
    )(z, gbuf, sbuf, p["bcol"], p["acol"], p["gcw"], p["scw"], p["wg32"], p["bg"])


def _srec_kernel(zt_ref, qkvt_ref, glogt_ref, actt_ref, c_ref, n_ref, m_ref, sg_ref, sl_ref, norm_ref,
                 y_ref, c_out, n_out, m_out, sg_out, sl_out):
    h = pl.program_id(0)
    feat = lambda ref, col: ref[pl.ds(pl.multiple_of(col + h * HEAD_DIM, HEAD_DIM), HEAD_DIM), :]
    gate = lambda kind: actt_ref[pl.ds(4 * kind + h, 1), :]
    colsum = lambda x: jnp.sum(x, axis=0, keepdims=True)
    rows = range(HEAD_DIM)

    def headnorm(o, w):
        return o * lax.rsqrt(jnp.mean(o * o, axis=0, keepdims=True) + EPS) * w

    def contract(vecs, s_ref):
        accs = [v[0:1, :] * s_ref[0] for v in vecs]
        for d in rows[1:]:
            sd_ = s_ref[d]
            accs = [a + v[d:d + 1, :] * sd_ for a, v in zip(accs, vecs)]
        return accs

    mq, mk, mv, mo = feat(zt_ref, MQ), feat(zt_ref, MK) * Q_SCALE, feat(zt_ref, MV), feat(zt_ref, MO)
    ic, lf = gate(0), gate(1)
    nvec = n_ref[...]
    m_old = m_ref[pl.ds(h, 1), :]
    a = lf + m_old
    mt = jnp.maximum(a, ic)
    w_int = jnp.exp(a - mt)
    e_i = jnp.exp(ic - mt)
    s = colsum(mq * mk) * e_i
    (qc,) = contract([mq], c_ref)
    num = w_int * qc + s * mv
    den = w_int * colsum(mq * nvec) + s
    hh = num / jnp.maximum(jnp.abs(den), jnp.exp(-mt))
    kw = e_i * mk
    for d in rows:
        c_out[d] = w_int * c_ref[d] + kw[d:d + 1, :] * mv
    n_out[...] = w_int * nvec + kw
    m_out[...] = mt
    y_ref[0:HEAD_DIM, :] = _sigmoid(mo) * headnorm(hh, norm_ref[0])

    gq, gk, gv = feat(qkvt_ref, 0), feat(qkvt_ref, W_GROUP), feat(qkvt_ref, 2 * W_GROUP)
    beta = gate(3)
    eg = jnp.exp(gate(2))
    ks, qs = contract([gk, gq], sg_ref)
    vn = beta * gv - (beta * eg) * ks
    o = eg * qs + colsum(gq * gk) * vn
    for d in rows:
        sg_out[d] = eg * sg_ref[d] + gk[d:d + 1, :] * vn
    y_ref[HEAD_DIM:2 * HEAD_DIM, :] = headnorm(o, norm_ref[1]) * _silu(feat(zt_ref, GZ))

    lq, lk, lv = feat(zt_ref, LQ) * Q_SCALE, feat(zt_ref, LK), feat(zt_ref, LV)
    egk = jnp.exp(feat(glogt_ref, 0))
    (ql,) = contract([lq * egk], sl_ref)
    o = ql + colsum(lq * lk) * lv
    for d in rows:
        sl_out[d] = egk[d:d + 1, :] * sl_ref[d] + lk[d:d + 1, :] * lv
    y_ref[2 * HEAD_DIM:, :] = headnorm(o, norm_ref[2]) * _silu(feat(zt_ref, LR))


def _sample_rec(layer, zt, qkvt, glogt, actt, cst, nst, mst, sgst, slst, norms):
    bn = zt.shape[1]
    mat_in = pl.BlockSpec((None, None, HEAD_DIM, HEAD_DIM, bn), lambda h: (layer, h, 0, 0, 0))
    mat_out = pl.BlockSpec((None, HEAD_DIM, HEAD_DIM, bn), lambda h: (h, 0, 0, 0))
    full = lambda a: pl.BlockSpec(a.shape, lambda h: (0,) * a.ndim)
    sd = lambda *s: jax.ShapeDtypeStruct(s, F32)
    return pl.pallas_call(
        _srec_kernel,
        grid=(N_HEADS,),
        in_specs=[full(zt), full(qkvt), full(glogt), full(actt), mat_in,
                  pl.BlockSpec((None, None, HEAD_DIM, bn), lambda h: (layer, h, 0, 0)),
                  pl.BlockSpec((None, N_HEADS, bn), lambda h: (layer, 0, 0)),
                  mat_in, mat_in,
                  pl.BlockSpec((3, HEAD_DIM, bn), lambda h: (0, h, 0))],
        out_specs=[pl.BlockSpec((None, 3 * HEAD_DIM, bn), lambda h: (h, 0, 0)), mat_out,
                   pl.BlockSpec((None, HEAD_DIM, bn), lambda h: (h, 0, 0)),
                   pl.BlockSpec((None, 1, bn), lambda h: (h, 0, 0)), mat_out, mat_out],
        out_shape=[sd(N_HEADS, 3 * HEAD_DIM, bn), sd(N_HEADS, HEAD_DIM, HEAD_DIM, bn),
                   sd(N_HEADS, HEAD_DIM, bn), sd(N_HEADS, 1, bn),
                   sd(N_HEADS, HEAD_DIM, HEAD_DIM, bn), sd(N_HEADS, HEAD_DIM, HEAD_DIM, bn)],
        compiler_params=_cparams(("parallel",)),
        name="sample_rec",
    )(zt, qkvt, glogt, actt, cst, nst, mst, sgst, slst, norms)


def _outproj_s_kernel(y_ref, w_ref, x_ref, o_ref):
    o_ref[...] = x_ref[...] + _mm3(y_ref[...], w_ref[...])


def _outproj_s(y, w_all, layer, x):
    full = lambda a: pl.BlockSpec(a.shape, lambda i: (0, 0))
    return pl.pallas_call(
        _outproj_s_kernel,
        grid=(1,),
        in_specs=[full(y), pl.BlockSpec((None, D_MODEL, D_MODEL), lambda i: (layer, 0, 0)), full(x)],
        out_specs=full(x),
        out_shape=jax.ShapeDtypeStruct(x.shape, F32),
        compiler_params=_cparams(("arbitrary",)),
        name="outproj_s",
    )(y, w_all, x)


def _ffn_s_kernel(x_ref, g_ref, wg_ref, wu_ref, wd_ref, gf_ref, o_ref, h_ref, acc_ref, *, final_norm):
    j = pl.program_id(0)

    @pl.when(j == 0)
    def _():
        h_ref[...] = _rms(x_ref[...], g_ref[...])
        acc_ref[...] = jnp.zeros_like(acc_ref)

    h = h_ref[...]
    acc_ref[...] += _mm3(_silu(_mm3(h, wg_ref[...])) * _mm3(h, wu_ref[...]), wd_ref[...])

    @pl.when(j == pl.num_programs(0) - 1)
    def _():
        out = x_ref[...] + acc_ref[...]
        o_ref[...] = _rms(out, gf_ref[...]) if final_norm else out


def _ffn_s(x, g, wg, wu, wd, gf, tf, final_norm):
    t = x.shape[0]
    dff = wg.shape[1]
    full = lambda j: (0, 0)
    return pl.pallas_call(
        functools.partial(_ffn_s_kernel, final_norm=final_norm),
        grid=(dff // tf,),
        in_specs=[pl.BlockSpec((t, D_MODEL), full),
                  pl.BlockSpec((1, D_MODEL), full),
                  pl.BlockSpec((D_MODEL, tf), lambda j: (0, j)),
                  pl.BlockSpec((D_MODEL, tf), lambda j: (0, j)),
                  pl.BlockSpec((tf, D_MODEL), lambda j: (j, 0)),
                  pl.BlockSpec((1, D_MODEL), full)],
        out_specs=pl.BlockSpec((t, D_MODEL), full),
        out_shape=jax.ShapeDtypeStruct((t, D_MODEL), F32),
        scratch_shapes=[pltpu.VMEM((t, D_MODEL), F32), pltpu.VMEM((t, D_MODEL), F32)],
        compiler_params=_cparams(("arbitrary",)),
        name="ffn_s",
    )(x, g, wg, wu, wd, gf)


def _outproj_kernel(y_ref, w_ref, x_ref, o_ref):
    o_ref[...] = x_ref[...] + _dot(y_ref[...], w_ref[...])


def _outproj(y, w, x, tm):
    t = x.shape[0]
    return pl.pallas_call(
        _outproj_kernel,
        grid=(t // tm,),
        in_specs=[pl.BlockSpec((tm, D_MODEL), lambda i: (i, 0)),
                  pl.BlockSpec((D_MODEL, D_MODEL), lambda i: (0, 0)),
                  pl.BlockSpec((tm, D_MODEL), lambda i: (i, 0))],
        out_specs=pl.BlockSpec((tm, D_MODEL), lambda i: (i, 0)),
        out_shape=jax.ShapeDtypeStruct((t, D_MODEL), F32),
        compiler_params=_cparams(("parallel",)),
        name="outproj",
    )(y, w, x)


def _ffn_kernel(x_ref, y_ref, wo_ref, g_ref, wg_ref, wu_ref, wd_ref, gf_ref, o_ref,
                h_ref, acc_ref, xn_ref, *, final_norm):
    j = pl.program_id(1)

    @pl.when(j == 0)
    def _():
        xn = x_ref[...] + _dot(y_ref[...], wo_ref[...])
        xn_ref[...] = xn
        h_ref[...] = _rms(xn, g_ref[...]).astype(BF16)
        acc_ref[...] = jnp.zeros_like(acc_ref)

    h = h_ref[...]
    acc_ref[...] += _dot((_silu(_dot(h, wg_ref[...])) * _dot(h, wu_ref[...])).astype(BF16), wd_ref[...])

    @pl.when(j == pl.num_programs(1) - 1)
    def _():
        out = xn_ref[...] + acc_ref[...]
        o_ref[...] = _rms(out, gf_ref[...]) if final_norm else out


def _ffn(x, y, wo, g, wg, wu, wd, gf, tm, tf, final_norm):
    t = x.shape[0]
    dff = wg.shape[1]
    return pl.pallas_call(
        functools.partial(_ffn_kernel, final_norm=final_norm),
        grid=(t // tm, dff // tf),
        in_specs=[pl.BlockSpec((tm, D_MODEL), lambda i, j: (i, 0)),
                  pl.BlockSpec((tm, D_MODEL), lambda i, j: (i, 0)),
                  pl.BlockSpec((D_MODEL, D_MODEL), lambda i, j: (0, 0)),
                  pl.BlockSpec((1, D_MODEL), lambda i, j: (0, 0)),
                  pl.BlockSpec((D_MODEL, tf), lambda i, j: (0, j)),
                  pl.BlockSpec((D_MODEL, tf), lambda i, j: (0, j)),
                  pl.BlockSpec((tf, D_MODEL), lambda i, j: (j, 0)),
                  pl.BlockSpec((1, D_MODEL), lambda i, j: (0, 0))],
        out_specs=pl.BlockSpec((tm, D_MODEL), lambda i, j: (i, 0)),
        out_shape=jax.ShapeDtypeStruct((t, D_MODEL), F32),
        scratch_shapes=[pltpu.VMEM((tm, D_MODEL), BF16), pltpu.VMEM((tm, D_MODEL), F32),
                        pltpu.VMEM((tm, D_MODEL), F32)],
        compiler_params=_cparams(("parallel", "arbitrary")),
        name="ffn",
    )(x, y, wo, g, wg, wu, wd, gf)


def _top2_gates(logits):
    return _top2_select(logits)[0]


def _top2_select(logits):
    lane = _iota2((1, LANES), 1)
    valid = lane < N_EXPERTS
    logits = jnp.where(valid, logits, -jnp.inf)
    ex = jnp.exp(logits - jnp.max(logits, axis=-1, keepdims=True))
    probs = ex / jnp.sum(ex, axis=-1, keepdims=True)
    v1 = jnp.max(probs, axis=-1, keepdims=True)
    i1 = jnp.min(jnp.where(probs == v1, lane, LANES), axis=-1, keepdims=True)
    rest = jnp.where((lane == i1) | jnp.logical_not(valid), -1.0, probs)
    v2 = jnp.max(rest, axis=-1, keepdims=True)
    i2 = jnp.min(jnp.where(rest == v2, lane, LANES), axis=-1, keepdims=True)
    tot = v1 + v2
    gates = jnp.where(lane == i1, v1 / tot, 0.0) + jnp.where(lane == i2, v2 / tot, 0.0)
    return gates, ((lane == i1) | (lane == i2)).astype(F32)


def _moe_kernel(x_ref, g_ref, wr_ref, br_ref, wg_ref, wu_ref, wd_ref, gf_ref, o_ref,
                h_ref, acc_ref, gates_ref, *, final_norm, precise_router):
    e = pl.program_id(1)

    @pl.when(e == 0)
    def _():
        h = _rms(x_ref[...], g_ref[...])
        h_ref[...] = h.astype(BF16)
        acc_ref[...] = jnp.zeros_like(acc_ref)
        if precise_router:
            logits = _mm3(h, wr_ref[...])
        else:
            logits = _dot(h.astype(BF16), wr_ref[...].astype(BF16))
        gates_ref[...] = _top2_gates(logits + br_ref[...])

    h = h_ref[...]
    y = _dot((_silu(_dot(h, wg_ref[0])) * _dot(h, wu_ref[0])).astype(BF16), wd_ref[0])
    lane = _iota2((1, LANES), 1)
    ge = jnp.sum(jnp.where(lane == e, gates_ref[...], 0.0), axis=-1, keepdims=True)
    acc_ref[...] += ge * y

    @pl.when(e == pl.num_programs(1) - 1)
    def _():
        out = x_ref[...] + acc_ref[...]
        o_ref[...] = _rms(out, gf_ref[...]) if final_norm else out


def _moe(x, g, wr, br, wg, wu, wd, gf, tm, final_norm, precise_router):
    t = x.shape[0]
    fe = wg.shape[2]
    return pl.pallas_call(
        functools.partial(_moe_kernel, final_norm=final_norm, precise_router=precise_router),
        grid=(t // tm, N_EXPERTS),
        in_specs=[pl.BlockSpec((tm, D_MODEL), lambda i, e: (i, 0)),
                  pl.BlockSpec((1, D_MODEL), lambda i, e: (0, 0)),
                  pl.BlockSpec((D_MODEL, LANES), lambda i, e: (0, 0)),
                  pl.BlockSpec((1, LANES), lambda i, e: (0, 0)),
                  pl.BlockSpec((1, D_MODEL, fe), lambda i, e: (e, 0, 0)),
                  pl.BlockSpec((1, D_MODEL, fe), lambda i, e: (e, 0, 0)),
                  pl.BlockSpec((1, fe, D_MODEL), lambda i, e: (e, 0, 0)),
                  pl.BlockSpec((1, D_MODEL), lambda i, e: (0, 0))],
        out_specs=pl.BlockSpec((tm, D_MODEL), lambda i, e: (i, 0)),
        out_shape=jax.ShapeDtypeStruct((t, D_MODEL), F32),
        scratch_shapes=[pltpu.VMEM((tm, D_MODEL), BF16), pltpu.VMEM((tm, D_MODEL), F32),
                        pltpu.VMEM((tm, LANES), F32)],
        compiler_params=_cparams(("parallel", "arbitrary")),
        name="moe",
    )(x, g, wr, br, wg, wu, wd, gf)


def _moe_win_kernel(x_ref, g_ref, wr_ref, br_ref, tril_ref, wg_ref, wu_ref, wd_ref, gf_ref, o_ref,
                    h_ref, gate_ref, key_ref, keyt_ref, *, final_norm, cap, tc):
    e = pl.program_id(1)
    w_rows = x_ref.shape[0]
    lane = _iota2((1, LANES), 1)

    @pl.when(e == 0)
    def _():
        x = x_ref[...]
        hb = _rms(x, g_ref[...]).astype(BF16)
        h_ref[...] = hb
        o_ref[...] = x
        gates, sel = _top2_select(_dot(hb, wr_ref[...].astype(BF16)) + br_ref[...])
        rank = _dot(tril_ref[...], sel.astype(BF16))
        key = jnp.where(sel > 0.0, rank, -1.0)
        gate_ref[...] = gates
        key_ref[...] = key
        keyt_ref[...] = key.T

    key_r = keyt_ref[pl.ds(e, 1), :]
    count = jnp.sum((key_r >= 0.0).astype(F32), axis=-1, keepdims=True)[0, 0].astype(jnp.int32)
    pick_e = (_iota2((LANES, LANES), 0) == e).astype(BF16)
    key_c = _sel_r(key_ref[...], pick_e)
    key_c = jnp.concatenate([key_c] * pl.cdiv(cap, LANES), axis=1)[:, :cap]
    gate_c = _sel_r(gate_ref[...], pick_e)
    gate_c = jnp.concatenate([gate_c] * (tc // LANES), axis=1)
    row_id = _iota2((cap, w_rows), 0).astype(F32)
    col_id = _iota2((w_rows, cap), 1).astype(F32)

    def trip(j, carry):
        base = (j * cap).astype(F32)
        gather = jnp.where(row_id + base == key_r, 1.0, 0.0).astype(BF16)
        xg = _dot(gather, h_ref[...]).astype(BF16)
        act = (_silu(_dot(xg, wg_ref[0])) * _dot(xg, wu_ref[0])).astype(BF16)
        y1, y2 = _split2(_dot(act, wd_ref[0]))
        scatter = jnp.where(col_id + base == key_c, 1.0, 0.0).astype(BF16)
        for c0 in range(0, D_MODEL, tc):
            cs = slice(c0, c0 + tc)
            o_ref[:, cs] += gate_c * (_dot(scatter, y1[:, cs]) + _dot(scatter, y2[:, cs]))
        return carry

    lax.fori_loop(0, (count + cap - 1) // cap, trip, 0)

    if final_norm:
        @pl.when(e == pl.num_programs(1) - 1)
        def _():
            o_ref[...] = _rms(o_ref[...], gf_ref[...])


def _moe_win(x, g, wr, br, wg, wu, wd, gf, w_rows, final_norm):
    t = x.shape[0]
    fe = wg.shape[2]
    cap = max(16, int(w_rows / 4 + 2.3 * (w_rows * 3 / 16) ** 0.5) // 16 * 16)
    tc = min(w_rows, 256)
    tril = jnp.asarray(np.tril(np.ones((w_rows, w_rows), np.float32), -1), dtype=BF16)
    const = lambda i, e: (0, 0)
    return pl.pallas_call(
        functools.partial(_moe_win_kernel, final_norm=final_norm, cap=cap, tc=tc),
        grid=(t // w_rows, N_EXPERTS),
        in_specs=[pl.BlockSpec((w_rows, D_MODEL), lambda i, e: (i, 0)),
                  pl.BlockSpec((1, D_MODEL), const),
                  pl.BlockSpec((D_MODEL, LANES), const),
                  pl.BlockSpec((1, LANES), const),
                  pl.BlockSpec((w_rows, w_rows), const),
                  pl.BlockSpec((1, D_MODEL, fe), lambda i, e: (e, 0, 0)),
                  pl.BlockSpec((1, D_MODEL, fe), lambda i, e: (e, 0, 0)),
                  pl.BlockSpec((1, fe, D_MODEL), lambda i, e: (e, 0, 0)),
                  pl.BlockSpec((1, D_MODEL), const)],
        out_specs=pl.BlockSpec((w_rows, D_MODEL), lambda i, e: (i, 0)),
        out_shape=jax.ShapeDtypeStruct((t, D_MODEL), F32),
        scratch_shapes=[pltpu.VMEM((w_rows, D_MODEL), BF16),
                        pltpu.VMEM((w_rows, LANES), F32),
                        pltpu.VMEM((w_rows, LANES), F32), pltpu.VMEM((LANES, w_rows), F32)],
        compiler_params=_cparams(("parallel", "arbitrary")),
        name="moe_win",
    )(x, g, wr, br, tril, wg, wu, wd, gf)


def _prep_w_in(w):
    seg = lambda i: w[:, _OFF[i]:_OFF[i + 1]]
    small = jnp.concatenate([seg(i) for i in _SMALL], axis=1)
    pad = jnp.zeros((w.shape[0], LANES - N_SMALL), w.dtype)
    wfull = jnp.concatenate([seg(i) for i in _BIG] + [small, pad], axis=1).astype(BF16)
    gates = jnp.concatenate([seg(i) for i in _GATES], axis=1)
    return wfull, gates.T.astype(BF16)


def _regroup_z(z):
    seg = lambda i: z[:, _OFF[i]:_OFF[i + 1]]
    pad = jnp.zeros((z.shape[0], LANES - N_SMALL), z.dtype)
    return jnp.concatenate([seg(i) for i in _BIG] + [seg(i) for i in _SMALL] + [pad], axis=1)


def _pad_lanes(v, start):
    out = jnp.zeros((1, LANES), F32)
    return lax.dynamic_update_slice(out, v.reshape(1, -1).astype(F32), (0, start))


def _layer_params(l, mlstm_b_i, mlstm_b_f, mlstm_norm, gdn_conv_w, gdn_a_log, gdn_dt_bias, gdn_norm,
                  sc_conv_w, gla_w_gate, gla_b_gate, gla_norm):
    bias = jnp.concatenate([mlstm_b_i[l], mlstm_b_f[l], gdn_dt_bias[l]]).astype(F32)
    alog = gdn_a_log[l].astype(F32)
    per_lane = lambda v: jnp.repeat(v, HEAD_DIM)
    zero = jnp.zeros((W_GROUP,), F32)
    brow = jnp.stack([per_lane(mlstm_b_i[l]), per_lane(mlstm_b_f[l]), per_lane(gdn_dt_bias[l]), zero])
    arow = jnp.stack([zero, zero, per_lane(alog), zero])
    wg32 = jnp.zeros((LANES, W_GROUP), F32).at[16:16 + GLA_RANK].set(gla_w_gate[l])
    return dict(bcol=_pad_lanes(bias, 0), acol=_pad_lanes(alog, 8), brow=brow, arow=arow,
                mnorm=mlstm_norm[l].reshape(1, -1), gnorm=gdn_norm[l].reshape(1, -1),
                lnorm=gla_norm[l].reshape(1, -1),
                gcw=gdn_conv_w[l].T, scw=sc_conv_w[l].T, wg=wg32.astype(BF16), wg32=wg32,
                bg=gla_b_gate[l].reshape(1, -1))


def _pick_tile(n, pref):
    for c in pref:
        if n % c == 0:
            return c
    return n


def _heads_major(a):
    return a.reshape(a.shape[0], N_HEADS, HEAD_DIM).transpose(1, 0, 2)


def kernel(x_prompt, x_sample, state_mlstm_C, state_mlstm_n, state_mlstm_m, state_gdn_S, state_gdn_conv,
           state_sc_conv, state_gla_S, w_in, g_mix, mlstm_b_i, mlstm_b_f, mlstm_norm, gdn_conv_w, gdn_a_log,
           gdn_dt_bias, gdn_norm, sc_conv_w, gla_w_gate, gla_b_gate, gla_norm, w_out, g_ffn, ffn_w_gate,
           ffn_w_up, ffn_w_down, moe_w_router, moe_b_router, moe_w_gate, moe_w_up, moe_w_down, g_final):
    depth = w_in.shape[0]
    bsz, seq, _ = x_prompt.shape
    bn = x_sample.shape[0]
    tp = bsz * seq
    assert x_sample.shape[1] == 1 and seq % CHUNK == 0

    xp = x_prompt.reshape(tp, D_MODEL)
    xs = x_sample.reshape(bn, D_MODEL)
    lb_rows = _pick_tile(seq, (128, 64))
    tm_p = _pick_tile(tp, (256, 128, 64))
    tm_f = _pick_tile(tp, (1024, 512, 256, 128, 64))
    tm_e = _pick_tile(tp, (1024, 512, 256, 128))
    nseq = _pick_tile(bsz, (4, 2, 1))

    s_minor = (state_mlstm_C.transpose(0, 2, 3, 4, 1), state_mlstm_n.transpose(0, 2, 3, 1),
               state_mlstm_m.transpose(0, 2, 1), state_gdn_S.transpose(0, 2, 3, 4, 1),
               state_gla_S.transpose(0, 2, 3, 4, 1))
    gfin = g_final.reshape(1, -1)

    p_states = [[] for _ in range(7)]
    s_states = [[] for _ in range(7)]
    for l in range(depth):
        p = _layer_params(l, mlstm_b_i, mlstm_b_f, mlstm_norm, gdn_conv_w, gdn_a_log, gdn_dt_bias,
                          gdn_norm, sc_conv_w, gla_w_gate, gla_b_gate, gla_norm)
        wfull, wst = _prep_w_in(w_in[l])
        gm = g_mix[l].reshape(1, -1)
        gf = g_ffn[l].reshape(1, -1)
        last = l == depth - 1

        z, zt = _inproj(xp, gm, wfull, wst, tm_p, lb_rows)
        y, c1, n1, m1, sg1, gb1, sb1, sl1 = _mix_prompt(
            z.reshape(bsz, seq, NP), zt.reshape((bsz, seq // lb_rows) + zt.shape[1:]), p,
            bsz, seq, lb_rows, nseq)
        y = y.reshape(tp, D_MODEL)
        wo = w_out[l].astype(BF16)
        if l % 2:
            xp = _outproj(y, wo, xp, tm_p)
        for lst, v in zip(p_states, (c1, n1, m1.reshape(bsz, N_HEADS), sg1, gb1, sb1, sl1)):
            lst.append(v)

        zs = _regroup_z(_inproj_s(xs, gm, w_in, l))
        gbuf = state_gdn_conv[l].reshape(bn, -1)
        sbuf = state_sc_conv[l].reshape(bn, -1)
        zt_s, qkvt, glogt, actt, yc, gbuf1, sbuf1 = _sample_pre(zs, gbuf, sbuf, p)
        norms = jnp.stack([jnp.broadcast_to(v[:, None], (W_GROUP, bn))
                           for v in (mlstm_norm[l], gdn_norm[l], gla_norm[l])])
        ysr, c2, n2, m2, sg2, sl2 = _sample_rec(l, zt_s, qkvt, glogt, actt, *s_minor, norms)
        yt = ysr.reshape(N_HEADS, 3, HEAD_DIM, bn).transpose(3, 1, 0, 2).reshape(bn, 3, W_GROUP)
        ymix = jnp.concatenate([yt[:, 0], yt[:, 1], yc, yt[:, 2]], axis=-1)
        xs = _outproj_s(ymix, w_out, l, xs)
        for lst, v in zip(s_states, (c2, n2, m2[:, 0], sg2, gbuf1.reshape(bn, 3, -1),
                                     sbuf1.reshape(bn, 2, -1), sl2)):
            lst.append(v)

        j = l // 2
        if l % 2 == 0:
            tf = _pick_tile(ffn_w_gate.shape[2], (256, 128))
            xp = _ffn(xp, y, wo, gf, ffn_w_gate[j].astype(BF16), ffn_w_up[j].astype(BF16),
                      ffn_w_down[j].astype(BF16), gfin, tm_f, tf, last)
            xs = _ffn_s(xs, gf, ffn_w_gate[j], ffn_w_up[j], ffn_w_down[j], gfin, tf, last)
        else:
            wr = jnp.zeros((D_MODEL, LANES), F32).at[:, :N_EXPERTS].set(moe_w_router[j])
            br = _pad_lanes(moe_b_router[j], 0)
            wgt, wup, wdn = (moe_w_gate[j].astype(BF16), moe_w_up[j].astype(BF16),
                             moe_w_down[j].astype(BF16))
            xp = _moe_win(xp, gf, wr, br, wgt, wup, wdn, gfin, tm_e, last)
            xs = _moe(xs, gf, wr, br, wgt, wup, wdn, gfin, bn, last, True)

    y_prompt = xp.reshape(bsz, seq, D_MODEL)
    y_sample = xs.reshape(bn, 1, D_MODEL)
    sp = [jnp.stack(v) for v in p_states]
    ss = [jnp.stack(v) for v in s_states]
    for i, perm in ((0, (0, 4, 1, 2, 3)), (1, (0, 3, 1, 2)), (2, (0, 2, 1)), (3, (0, 4, 1, 2, 3)),
                    (6, (0, 4, 1, 2, 3))):
        ss[i] = ss[i].transpose(perm)
    return (y_prompt, y_sample, sp[0], ss[0], sp[1], ss[1], sp[2], ss[2], sp[3], ss[3],
            sp[4], ss[4], sp[5], ss[5], sp[6], ss[6])
```

```python
import functools

import numpy as np
import jax
import jax.numpy as jnp
from jax import lax
from jax.experimental import pallas as pl
from jax.experimental.pallas import tpu as pltpu

F32 = jnp.float32
BF16 = jnp.bfloat16

D_MODEL = 1024
W_GROUP = 256
N_HEADS = 4
HEAD_DIM = 64
CHUNK = 64
GLA_RANK = 16
GLA_NORMALIZER = 16.0
N_EXPERTS = 8
EPS = 1e-6
Q_SCALE = HEAD_DIM ** -0.5

VMEM_LIMIT = 56 * 1024 * 1024
LANES = 128

SPLIT_SIZES = ([W_GROUP] * 4 + [N_HEADS] * 2 + [W_GROUP] * 4 + [N_HEADS] * 2 +
               [W_GROUP] * 3 + [W_GROUP] * 4 + [GLA_RANK])
_OFF = np.concatenate([[0], np.cumsum(SPLIT_SIZES)])
_BIG = [0, 1, 2, 3, 6, 7, 8, 9, 12, 13, 14, 15, 16, 17, 18]
_SMALL = [4, 5, 10, 11, 19]
_GATES = [4, 5, 10, 11]
MQ, MK, MV, MO = 0, 256, 512, 768
GQ, GK, GV, GZ = 1024, 1280, 1536, 1792
SB, SC, SH = 2048, 2304, 2560
LQ, LK, LV, LR = 2816, 3072, 3328, 3584
SM = 3840
NP = SM + LANES
N_SMALL = 32
N_GATE_ROWS = 16
N_ROW_KINDS = 3
GX_I, GX_B, GX_G, GX_BETA, GX_PM = 0, 256, 512, 768, 1024
GLA_LEVELS = (32, 16, 8, 4, 2)


def _cparams(sem):
    return pltpu.CompilerParams(dimension_semantics=sem, vmem_limit_bytes=VMEM_LIMIT)


def _log1pexp_negabs(x):
    return jnp.log(1.0 + jnp.exp(-jnp.abs(x)))


def _softplus(x):
    return jnp.maximum(x, 0.0) + _log1pexp_negabs(x)


def _logsigmoid(x):
    return -(jnp.maximum(-x, 0.0) + _log1pexp_negabs(x))


def _sigmoid(x):
    return 1.0 / (1.0 + jnp.exp(-x))


def _silu(x):
    return x * _sigmoid(x)


def _rms(x, g):
    return x * lax.rsqrt(jnp.mean(x * x, axis=-1, keepdims=True) + EPS) * g


def _dot(a, b):
    return jnp.dot(a, b, preferred_element_type=F32)


def _dot_nt(a, b):
    return lax.dot_general(a, b, (((1,), (1,)), ((), ())), preferred_element_type=F32)


def _dot_tn(a, b):
    return lax.dot_general(a, b, (((0,), (0,)), ((), ())), preferred_element_type=F32)


def _split2(x):
    x1 = x.astype(BF16)
    return x1, (x - x1.astype(F32)).astype(BF16)


def _split3(x):
    x1 = x.astype(BF16)
    r1 = x - x1.astype(F32)
    x2 = r1.astype(BF16)
    x3 = (r1 - x2.astype(F32)).astype(BF16)
    return x1, x2, x3


def _sel(m, x):
    x1, x2, x3 = _split3(x)
    return _dot(m, x1) + _dot(m, x2) + _dot(m, x3)


def _sel_r(x, m):
    n = x.shape[0]
    if n % 8:
        x1, x2, x3 = _split3(x)
        return _dot(x1, m) + _dot(x2, m) + _dot(x3, m)
    r = _dot(jnp.concatenate(_split3(x), axis=0), m)
    return r[0:n] + r[n:2 * n] + r[2 * n:3 * n]


def _sel_nt(m, x):
    x1, x2, x3 = _split3(x)
    return _dot_nt(m, x1) + _dot_nt(m, x2) + _dot_nt(m, x3)


def _mm3(x, w):
    n = x.shape[0]
    x1, x2 = _split2(x)
    w1, w2 = _split2(w)
    r = _dot(jnp.concatenate([x1, x2], axis=0), w1)
    return r[0:n] + r[n:] + _dot(x1, w2)


def _iota2(shape, dim):
    return lax.broadcasted_iota(jnp.int32, shape, dim)


def _inproj_kernel(x_ref, g_ref, w_ref, wst_ref, z_ref, zt_ref):
    hb = _rms(x_ref[...], g_ref[...]).astype(BF16)
    z_ref[...] = _dot(hb, w_ref[...])
    zt = _dot_nt(wst_ref[...], hb)
    n_blk, n_kind, n_chunks, _ = zt_ref.shape
    for b in range(n_blk):
        for k in range(n_kind):
            for c in range(n_chunks):
                t0 = (b * n_chunks + c) * CHUNK
                for h in range(N_HEADS):
                    zt_ref[b, k, c:c + 1, h * HEAD_DIM:(h + 1) * HEAD_DIM] = (
                        zt[k * 4 + h:k * 4 + h + 1, t0:t0 + CHUNK])


def _inproj(x, g, w, wst, tm, lb_rows):
    t = x.shape[0]
    n_chunks = lb_rows // CHUNK
    return pl.pallas_call(
        _inproj_kernel,
        grid=(t // tm,),
        in_specs=[pl.BlockSpec((tm, D_MODEL), lambda i: (i, 0)),
                  pl.BlockSpec((1, D_MODEL), lambda i: (0, 0)),
                  pl.BlockSpec((D_MODEL, NP), lambda i: (0, 0)),
                  pl.BlockSpec((N_GATE_ROWS, D_MODEL), lambda i: (0, 0))],
        out_specs=[pl.BlockSpec((tm, NP), lambda i: (i, 0)),
                   pl.BlockSpec((tm // lb_rows, N_ROW_KINDS, n_chunks, W_GROUP), lambda i: (i, 0, 0, 0))],
        out_shape=[jax.ShapeDtypeStruct((t, NP), F32),
                   jax.ShapeDtypeStruct((t // lb_rows, N_ROW_KINDS, n_chunks, W_GROUP), F32)],
        compiler_params=_cparams(("parallel",)),
        name="inproj",
    )(x, g, w, wst)


def _gate_act(pre, idx, neg_a):
    tail = _log1pexp_negabs(pre)
    lf = -(jnp.maximum(-pre, 0.0) + tail)
    dec = neg_a * (jnp.maximum(pre, 0.0) + tail)
    beta = _sigmoid(pre)
    return jnp.where(idx < 4, pre,
                     jnp.where(idx < 8, lf,
                               jnp.where(idx < 12, dec,
                                         jnp.where(idx < 16, beta, 0.0))))


def _mix_kernel(z_ref, zt_ref, bcol_ref, acol_ref, brow_ref, arow_ref, mnorm_ref, gnorm_ref,
                lnorm_ref, gcw_ref, scw_ref, wg_ref, bg_ref,
                bdmask_ref, bdtriu_ref, tril64_ref, selexp_ref, eye64_ref,
                y_ref, c_out, n_out, m_out, sg_out, gconv_out, sconv_out, sl_out,
                c_ref, n_ref, m_ref, sg_ref, slt_ref, xg_ref, xs_ref, qkv_ref, gcum_ref, glog_ref,
                gx_ref, rw_ref, *, lb_rows, nseq):
    lb = pl.program_id(1)
    nlb = pl.num_programs(1)
    n_chunks = lb_rows // CHUNK
    wd = W_GROUP
    seqs = range(nseq)

    @pl.when(lb == 0)
    def _():
        c_ref[...] = jnp.zeros_like(c_ref)
        n_ref[...] = jnp.zeros_like(n_ref)
        m_ref[...] = jnp.zeros_like(m_ref)
        sg_ref[...] = jnp.zeros_like(sg_ref)
        slt_ref[...] = jnp.zeros_like(slt_ref)
        for i in seqs:
            xg_ref[i, 0:8, :] = jnp.zeros((8, 3 * wd), F32)
            xs_ref[i, 0:8, :] = jnp.zeros((8, wd), F32)

    ti = _iota2((CHUNK, wd), 0)
    si = _iota2((CHUNK, wd), 1) % HEAD_DIM
    incl = ti >= si
    strict = ti > si
    eye = ti == si
    blk16 = (ti // 16) == (si // 16)
    blk32 = (ti // 32) == (si // 32)
    eye_f = eye.astype(F32)
    lvl_mask = {n: ((ti // (2 * n)) == (si // (2 * n))) & ((ti // n) > (si // n))
                for n in GLA_LEVELS + (1,)}
    bdmask = bdmask_ref[...]
    bdtriu = bdtriu_ref[...]
    tril64 = tril64_ref[...]
    head_of_lane = _iota2((1, wd), 1) // HEAD_DIM
    pos_of_lane = _iota2((8, wd), 1) % HEAD_DIM

    def bd(x):
        xb = x.astype(BF16)
        return jnp.concatenate([xb, xb, xb, xb], axis=0) * bdmask

    def unbd(m):
        out = m[3 * HEAD_DIM:, :]
        for h in (2, 1, 0):
            out = jnp.where(head_of_lane == h, m[h * HEAD_DIM:(h + 1) * HEAD_DIM, :], out)
        return out

    def mm(a, bmat):
        return _dot(a.astype(BF16), bmat)

    def mm_nt(a, bmat):
        return _dot_nt(a.astype(BF16), bmat)

    def mm3_bd(a, b):
        a1, a2 = _split2(a)
        b1, b2 = _split2(b)
        r = _dot(jnp.concatenate([a1, a2], axis=0), bd(b1))
        return r[0:CHUNK] + r[CHUNK:] + _dot(a1, bd(b2))

    def hsum(x):
        x1, x2 = _split2(x)
        r = _dot(jnp.concatenate([x1, x2], axis=0), bdmask)
        return r[0:x.shape[0]] + r[x.shape[0]:]

    def headnorm(o, w):
        return o * lax.rsqrt(hsum(o * o) * (1.0 / HEAD_DIM) + EPS) * w

    lane = _iota2((1, LANES), 1)
    neg_a_col = -jnp.exp(acol_ref[...])
    neg_a_row = -jnp.exp(arow_ref[2:3, :])
    selexp = selexp_ref[...]
    eye64 = eye64_ref[...]

    def l2n(x):
        return x * lax.rsqrt(hsum(x * x) + EPS)

    rows_all = nseq * lb_rows
    merge = lambda x: x.reshape(rows_all, x.shape[-1])
    split = lambda x: x.reshape(nseq, lb_rows, x.shape[-1])
    small = merge(z_ref[:, :, SM:SM + LANES])
    act = _gate_act(small + bcol_ref[...], lane, neg_a_col)
    def chunk_cumsum(x):
        n_c, width = rows_all // CHUNK, x.shape[1]
        wide = jnp.concatenate([x[c * CHUNK:(c + 1) * CHUNK, :] for c in range(n_c)], axis=1)
        cs = _sel(tril64, wide)
        return jnp.concatenate([cs[:, c * width:(c + 1) * width] for c in range(n_c)], axis=0)

    csum = chunk_cumsum(act)
    comp = jnp.where((lane >= 4) & (lane < 12), csum, act)
    gx_ref[:, :, GX_I:GX_I + 4 * wd] = split(_sel_r(comp, selexp))
    glin = _dot(small.astype(BF16), wg_ref[...]) + bg_ref[...]
    glog = _logsigmoid(glin) * (1.0 / GLA_NORMALIZER)
    glog_ref[...] = split(glog)
    gcum_ref[...] = split(chunk_cumsum(glog))

    n_r = nseq * n_chunks
    trow = lambda k: jnp.concatenate([zt_ref[i, 0, k] for i in seqs], axis=0) + brow_ref[k:k + 1, :]
    ic_r = trow(0)
    lf_r = _logsigmoid(trow(1))
    dec_r = neg_a_row * _softplus(trow(2))
    cs_r = _sel_r(jnp.concatenate([lf_r, dec_r], axis=0), bdtriu)
    c_rows = ic_r - cs_r[0:n_r, :]
    g_rows = cs_r[n_r:, :]
    pm = jnp.concatenate([c_rows] * max(1, 8 // n_r), axis=0)
    for sh in (1, 2, 4, 8, 16, 32):
        pm = jnp.where(pos_of_lane[0:1, :] >= sh, jnp.maximum(pm, pltpu.roll(pm, sh, axis=1)), pm)
    for r in range(n_r):
        rw_ref[r // n_chunks, r % n_chunks, 0:1, :] = c_rows[r:r + 1, :]
        rw_ref[r // n_chunks, r % n_chunks, 1:2, :] = g_rows[r:r + 1, :]
    diag = jnp.concatenate(
        [jnp.where(eye, jnp.broadcast_to(pm[r:r + 1, :], (CHUNK, wd)), 0.0) for r in range(n_r)], axis=0)
    gx_ref[:, :, GX_PM:GX_PM + wd] = split(_sel_r(diag, bdmask))

    xg_ref[:, 8:, :] = z_ref[:, :, GQ:GQ + 3 * wd]
    conv = xg_ref[:, 5:5 + lb_rows, :] * gcw_ref[0:1, :]
    for j in range(1, 4):
        conv = conv + xg_ref[:, 5 + j:5 + j + lb_rows, :] * gcw_ref[j:j + 1, :]
    conv = merge(_silu(conv))
    qkv_ref[:, :, 0:wd] = split(l2n(conv[:, 0:wd]) * Q_SCALE)
    qkv_ref[:, :, wd:2 * wd] = split(l2n(conv[:, wd:2 * wd]))
    qkv_ref[:, :, 2 * wd:] = split(conv[:, 2 * wd:])

    xs_ref[:, 8:, :] = z_ref[:, :, SC:SC + wd] * z_ref[:, :, SH:SH + wd]
    cu = xs_ref[:, 6:6 + lb_rows, :] * scw_ref[0:1, :]
    for j in range(1, 3):
        cu = cu + xs_ref[:, 6 + j:6 + j + lb_rows, :] * scw_ref[j:j + 1, :]
    y_ref[:, :, 2 * wd:3 * wd] = (z_ref[:, :, SB:SB + wd] * cu).astype(y_ref.dtype)


    def chunk(c, carry):
        r0 = pl.multiple_of(c * CHUNK, CHUNK)
        rs = pl.ds(r0, CHUNK)
        each = lambda f, *xs: [f(*a) for a in zip(*xs)]
        zcol = lambda o: [z_ref[i, rs, o:o + wd] for i in seqs]
        gxcol = lambda o: [gx_ref[i, rs, o:o + wd] for i in seqs]
        last = lambda xs: [x[CHUNK - 1:CHUNK, :] for x in xs]
        tn_bd = lambda a, b: each(lambda x, y: unbd(_dot_tn(x.astype(BF16), y.astype(BF16))), a, b)
        mm3s = lambda a, b: each(mm3_bd, a, b)
        rw = [rw_ref[i, c, 0:2, :] for i in seqs]

        q = zcol(MQ)
        k = [x * Q_SCALE for x in zcol(MK)]
        v = zcol(MV)
        ic_e, b_e, pm_e = gxcol(GX_I), gxcol(GX_B), gxcol(GX_PM)
        m_old = [m_ref[i, 0:1, :] for i in seqs]
        cmat = [c_ref[i] for i in seqs]
        nrow = [n_ref[i, 0:1, :] for i in seqs]
        mx = each(jnp.maximum, m_old, pm_e)
        w_int = each(lambda a, b: jnp.exp(a - b), m_old, mx)
        p = each(lambda r, m: jnp.where(incl, jnp.exp(r[0:1, :] - m), 0.0), rw, mx)
        s = each(lambda a, b, c_: mm_nt(a, bd(b)) * c_, q, k, p)
        num = each(lambda w_, a, cm, s_, v_: w_ * mm(a, bd(cm)) + mm(s_, bd(v_)), w_int, q, cmat, s, v)
        den = each(lambda w_, a, n_, s_: hsum(w_ * (a * n_) + s_), w_int, q, nrow, s)
        hh = each(lambda n_, d_, b_, m: n_ / jnp.maximum(jnp.abs(d_), jnp.exp(-(b_ + m))), num, den, b_e, mx)
        mx_last = last(mx)
        wgk = each(lambda i_, b_, ml, k_: jnp.exp((i_ - b_) - ml) * k_, ic_e, b_e, mx_last, k)
        dec = each(lambda a, b: jnp.exp(a - b), m_old, mx_last)
        c_new = each(lambda d_, cm, u_: d_ * cm + u_, dec, cmat, tn_bd(wgk, v))
        for i in seqs:
            c_ref[i] = c_new[i]
            n_ref[i] = jnp.broadcast_to(dec[i] * nrow[i] + jnp.sum(wgk[i], axis=0, keepdims=True), (8, wd))
            m_ref[i] = jnp.broadcast_to(b_e[i][CHUNK - 1:CHUNK, :] + mx_last[i], (8, wd))
        y_a = each(lambda o_, h_: _sigmoid(o_) * headnorm(h_, mnorm_ref[...]), zcol(MO), hh)
        for i in seqs:
            y_ref[i, rs, 0:wd] = y_a[i].astype(y_ref.dtype)

        q = [qkv_ref[i, rs, 0:wd] for i in seqs]
        k = [qkv_ref[i, rs, wd:2 * wd] for i in seqs]
        v = [qkv_ref[i, rs, 2 * wd:] for i in seqs]
        g_e, beta = gxcol(GX_G), gxcol(GX_BETA)
        s_old = [sg_ref[i] for i in seqs]
        bdk = each(bd, k)
        lw = each(lambda g_, r: jnp.where(incl, jnp.exp(g_ - r[1:2, :]), 0.0), g_e, rw)
        amat = each(lambda b_, k_, bk, l_: jnp.where(strict, b_ * mm_nt(k_, bk) * l_, 0.0), beta, k, bdk, lw)
        egc = each(jnp.exp, g_e)
        ad = [jnp.where(blk16, a, 0.0) for a in amat]
        a2 = mm3s(ad, ad)
        a4 = mm3s(a2, a2)
        a8 = mm3s(a4, a4)
        tinv = [eye_f - a for a in ad]
        for apow in (a2, a4, a8):
            tinv = each(lambda t_, u_: t_ + u_, tinv, mm3s(tinv, apow))
        n1 = [jnp.where(blk32 & jnp.logical_not(blk16), a, 0.0) for a in amat]
        tinv = each(lambda t_, u_: t_ - u_, tinv, mm3s(mm3s(tinv, n1), tinv))
        n2 = [jnp.where(blk32, 0.0, a) for a in amat]
        tinv = each(lambda t_, u_: t_ - u_, tinv, mm3s(mm3s(tinv, n2), tinv))
        u = mm3s(tinv, each(lambda b_, v_: b_ * v_, beta, v))
        w = mm3s(tinv, each(lambda b_, e_, k_: (b_ * e_) * k_, beta, egc, k))
        bds = each(bd, s_old)
        vn = each(lambda u_, w_, bs: u_ - mm(w_, bs), u, w, bds)
        qk = each(lambda q_, bk, l_: mm_nt(q_, bk) * l_, q, bdk, lw)
        o = each(lambda q_, e_, bs, qk_, vn_: mm(q_ * e_, bs) + mm(qk_, bd(vn_)), q, egc, bds, qk, vn)
        g_last = last(g_e)
        kdec = each(lambda k_, gl, g_: k_ * jnp.exp(gl - g_), k, g_last, g_e)
        s_new = each(lambda gl, so, u_: jnp.exp(gl) * so + u_, g_last, s_old, tn_bd(kdec, vn))
        y_b = each(lambda o_, z_: headnorm(o_, gnorm_ref[...]) * _silu(z_), o, zcol(GZ))
        for i in seqs:
            sg_ref[i] = s_new[i]
            y_ref[i, rs, wd:2 * wd] = y_b[i].astype(y_ref.dtype)

        q = [x * Q_SCALE for x in zcol(LQ)]
        k = zcol(LK)
        v = zcol(LV)
        gk = [glog_ref[i, rs, :] for i in seqs]
        gcum = [gcum_ref[i, rs, :] for i in seqs]
        st_old = [slt_ref[i] for i in seqs]
        bdk = each(bd, k)
        amat = each(lambda q_, bk: jnp.where(eye, mm_nt(q_, bk), 0.0), q, bdk)
        amat = each(lambda a_, q_, g_, bk: a_ + jnp.where(lvl_mask[1], mm_nt(q_ * jnp.exp(g_), bk), 0.0),
                    amat, q, gk, bdk)
        g_end = gcum
        g_prev = [jnp.where(ti == 0, 0.0, pltpu.roll(g_, 1, axis=0)) for g_ in gcum]
        for n in GLA_LEVELS[::-1]:
            half = n // 2
            upper = (ti % n) >= half
            g_end = [jnp.where(upper, e_, pltpu.roll(e_, CHUNK - half, axis=0)) for e_ in g_end]
            g_prev = [jnp.where(upper, pltpu.roll(p_, half, axis=0), p_) for p_ in g_prev]
            qn = each(lambda q_, gc, gp: q_ * jnp.exp(gc - gp), q, gcum, g_prev)
            kn = each(lambda k_, ge, gc: k_ * jnp.exp(ge - gc), k, g_end, gcum)
            amat = each(lambda a_, q_, k_, m=lvl_mask[n]: a_ + jnp.where(m, mm_nt(q_, bd(k_)), 0.0),
                        amat, qn, kn)
        g_last = last(gcum)
        o = each(lambda q_, gc, st, a_, v_: mm_nt(q_ * jnp.exp(gc), bd(st)) + mm(a_, bd(v_)),
                 q, gcum, st_old, amat, v)
        kdec = each(lambda k_, gl, gc: k_ * jnp.exp(gl - gc), k, g_last, gcum)
        st_new = each(lambda st, gl, u_: st * jnp.exp(gl) + u_, st_old, g_last, tn_bd(v, kdec))
        y_d = each(lambda o_, z_: headnorm(o_, lnorm_ref[...]) * _silu(z_), o, zcol(LR))
        for i in seqs:
            slt_ref[i] = st_new[i]
            y_ref[i, rs, 3 * wd:] = y_d[i].astype(y_ref.dtype)
        return carry

    lax.fori_loop(0, n_chunks, chunk, 0)

    gtail = [xg_ref[i, lb_rows:lb_rows + 8, :] for i in seqs]
    stail = [xs_ref[i, lb_rows:lb_rows + 8, :] for i in seqs]
    for i in seqs:
        xg_ref[i, 0:8, :] = gtail[i]
        xs_ref[i, 0:8, :] = stail[i]

    @pl.when(lb == nlb - 1)
    def _():
        lane4 = _iota2((1, N_HEADS), 1)
        for i in seqs:
            gconv_out[i] = gtail[i][5:8, :]
            sconv_out[i] = stail[i][6:8, :]
            m_row = jnp.zeros((1, N_HEADS), F32)
            for h in range(N_HEADS):
                hs = slice(h * HEAD_DIM, (h + 1) * HEAD_DIM)
                c_out[i, h] = c_ref[i, :, hs]
                n_out[i, h:h + 1, :] = n_ref[i, 0:1, hs]
                m_row = jnp.where(lane4 == h, m_ref[i, 0:1, h * HEAD_DIM:h * HEAD_DIM + 1], m_row)
                sg_out[i, h] = sg_ref[i, :, hs]
                sl_out[i, h] = _sel_nt(eye64, slt_ref[i, :, hs])
            m_out[i] = m_row


def _mix_consts():
    wd = W_GROUP
    r, c = np.indices((wd, wd))
    same_head = (r // HEAD_DIM) == (c // HEAD_DIM)
    bdmask = same_head
    bdtriu = same_head & ((r % HEAD_DIM) <= (c % HEAD_DIM))
    tril64 = np.tril(np.ones((CHUNK, CHUNK), bool))
    rr, cc = np.indices((LANES, 4 * wd))
    selexp = rr == 4 * (cc // wd) + (cc % wd) // HEAD_DIM
    eye64 = np.eye(CHUNK, dtype=bool)
    return [jnp.asarray(m, dtype=BF16) for m in (bdmask, bdtriu, tril64, selexp, eye64)]


def _mix_prompt(z, zt, p, bsz, seq, lb_rows, nseq):
    nlb = seq // lb_rows
    n_chunks = lb_rows // CHUNK
    row = lambda b, l: (b, l, 0)
    const2 = lambda b, l: (0, 0)
    st4 = lambda b, l: (b, 0, 0, 0)
    st3 = lambda b, l: (b, 0, 0)
    params = [p["bcol"], p["acol"], p["brow"], p["arow"], p["mnorm"], p["gnorm"], p["lnorm"],
              p["gcw"], p["scw"], p["wg"], p["bg"]] + _mix_consts()
    in_specs = [pl.BlockSpec((nseq, lb_rows, NP), row),
                pl.BlockSpec((nseq, 1, N_ROW_KINDS, n_chunks, W_GROUP), lambda b, l: (b, l, 0, 0, 0))]
    in_specs += [pl.BlockSpec(a.shape, const2) for a in params]
    hd = (N_HEADS, HEAD_DIM, HEAD_DIM)
    out_shape = [jax.ShapeDtypeStruct((bsz, seq, D_MODEL), BF16),
                 jax.ShapeDtypeStruct((bsz,) + hd, F32),
                 jax.ShapeDtypeStruct((bsz, N_HEADS, HEAD_DIM), F32),
                 jax.ShapeDtypeStruct((bsz, 1, N_HEADS), F32),
                 jax.ShapeDtypeStruct((bsz,) + hd, F32),
                 jax.ShapeDtypeStruct((bsz, 3, 3 * W_GROUP), F32),
                 jax.ShapeDtypeStruct((bsz, 2, W_GROUP), F32),
                 jax.ShapeDtypeStruct((bsz,) + hd, F32)]
    out_specs = [pl.BlockSpec((nseq, lb_rows, D_MODEL), row),
                 pl.BlockSpec((nseq,) + hd, st4),
                 pl.BlockSpec((nseq, N_HEADS, HEAD_DIM), st3),
                 pl.BlockSpec((nseq, 1, N_HEADS), st3),
                 pl.BlockSpec((nseq,) + hd, st4),
                 pl.BlockSpec((nseq, 3, 3 * W_GROUP), st3),
                 pl.BlockSpec((nseq, 2, W_GROUP), st3),
                 pl.BlockSpec((nseq,) + hd, st4)]
    vm = lambda *shape: pltpu.VMEM((nseq,) + shape, F32)
    scratch = [vm(HEAD_DIM, W_GROUP),
               vm(8, W_GROUP),
               vm(8, W_GROUP),
               vm(HEAD_DIM, W_GROUP),
               vm(HEAD_DIM, W_GROUP),
               vm(lb_rows + 8, 3 * W_GROUP),
               vm(lb_rows + 8, W_GROUP),
               vm(lb_rows, 3 * W_GROUP),
               vm(lb_rows, W_GROUP),
               vm(lb_rows, W_GROUP),
               vm(lb_rows, 5 * W_GROUP),
               vm(n_chunks, 8, W_GROUP)]
    return pl.pallas_call(
        functools.partial(_mix_kernel, lb_rows=lb_rows, nseq=nseq),
        grid=(bsz // nseq, nlb),
        in_specs=in_specs,
        out_specs=out_specs,
        out_shape=out_shape,
        scratch_shapes=scratch,
        compiler_params=_cparams(("parallel", "arbitrary")),
        name="mix_prompt",
    )(z, zt, *params)


def _inproj_s_kernel(x_ref, g_ref, w_ref, z_ref, h_ref):
    k = pl.program_id(0)
    n_k, _, tk = h_ref.shape

    @pl.when(k == 0)
    def _():
        h = _rms(x_ref[...], g_ref[...])
        for j in range(n_k):
            h_ref[j] = h[:, j * tk:(j + 1) * tk]
        z_ref[...] = jnp.zeros_like(z_ref)

    hk = h_ref[k]
    n = w_ref.shape[1]
    step = 4 * LANES
    for j in range(0, n, step):
        cs = slice(j, min(j + step, n))
        z_ref[:, cs] += _mm3(hk, w_ref[:, cs])


def _inproj_s(x, g, w_all, layer, tk=256):
    bn = x.shape[0]
    d_in = w_all.shape[2]
    return pl.pallas_call(
        _inproj_s_kernel,
        grid=(D_MODEL // tk,),
        in_specs=[pl.BlockSpec((bn, D_MODEL), lambda k: (0, 0)),
                  pl.BlockSpec((1, D_MODEL), lambda k: (0, 0)),
                  pl.BlockSpec((None, tk, d_in), lambda k: (layer, k, 0))],
        out_specs=pl.BlockSpec((bn, d_in), lambda k: (0, 0)),
        out_shape=jax.ShapeDtypeStruct((bn, d_in), F32),
        scratch_shapes=[pltpu.VMEM((D_MODEL // tk, bn, tk), F32)],
        compiler_params=_cparams(("arbitrary",)),
        name="inproj_s",
    )(x, g, w_all)


def _spre_kernel(z_ref, gbuf_ref, sbuf_ref, bcol_ref, acol_ref, gcw_ref, scw_ref, wg_ref, bg_ref,
                 zt_ref, qkvt_ref, glogt_ref, actt_ref, yc_ref, gbuf_out, sbuf_out):
    small = z_ref[:, SM:SM + LANES]
    lane = _iota2((1, LANES), 1)
    actt_ref[...] = _gate_act(small + bcol_ref[...], lane, -jnp.exp(acol_ref[...])).T
    glin = _mm3(small, wg_ref[...]) + bg_ref[...]
    glogt_ref[...] = (_logsigmoid(glin) * (1.0 / GLA_NORMALIZER)).T
    zt_ref[...] = z_ref[...].T

    wq = 3 * W_GROUP
    u = z_ref[:, GQ:GQ + wq]
    b0, b1, b2 = gbuf_ref[:, 0:wq], gbuf_ref[:, wq:2 * wq], gbuf_ref[:, 2 * wq:]
    conv = b0 * gcw_ref[0:1, :] + b1 * gcw_ref[1:2, :] + b2 * gcw_ref[2:3, :] + u * gcw_ref[3:4, :]
    conv = _silu(conv)
    gbuf_out[:, 0:wq] = b1
    gbuf_out[:, wq:2 * wq] = b2
    gbuf_out[:, 2 * wq:] = u
    hsum = (_iota2((W_GROUP, W_GROUP), 0) // HEAD_DIM ==
            _iota2((W_GROUP, W_GROUP), 1) // HEAD_DIM).astype(BF16)

    def l2n(x):
        return x * lax.rsqrt(_sel_r(x * x, hsum) + EPS)

    qkvt_ref[0:W_GROUP, :] = (l2n(conv[:, 0:W_GROUP]) * Q_SCALE).T
    qkvt_ref[W_GROUP:2 * W_GROUP, :] = l2n(conv[:, W_GROUP:2 * W_GROUP]).T
    qkvt_ref[2 * W_GROUP:, :] = conv[:, 2 * W_GROUP:].T

    u2 = z_ref[:, SC:SC + W_GROUP] * z_ref[:, SH:SH + W_GROUP]
    s0, s1 = sbuf_ref[:, 0:W_GROUP], sbuf_ref[:, W_GROUP:]
    cu = s0 * scw_ref[0:1, :] + s1 * scw_ref[1:2, :] + u2 * scw_ref[2:3, :]
    yc_ref[...] = z_ref[:, SB:SB + W_GROUP] * cu
    sbuf_out[:, 0:W_GROUP] = s1
    sbuf_out[:, W_GROUP:] = u2


def _sample_pre(z, gbuf, sbuf, p):
    bn = z.shape[0]
    sd = lambda s: jax.ShapeDtypeStruct(s, F32)
    return pl.pallas_call(
        _spre_kernel,
        out_shape=[sd((NP, bn)), sd((3 * W_GROUP, bn)), sd((W_GROUP, bn)), sd((LANES, bn)),
                   sd((bn, W_GROUP)), sd(gbuf.shape), sd(sbuf.shape)],
        compiler_params=pltpu.CompilerParams(vmem_limit_bytes=VMEM_LIMIT),
        name="sample_pre",
    )(z, gbuf, sbuf, p["bcol"], p["acol"], p["gcw"], p["scw"], p["wg32"], p["bg"])


def _srec_kernel(zt_ref, qkvt_ref, glogt_ref, actt_ref, c_ref, n_ref, m_ref, sg_ref, sl_ref, norm_ref,
                 y_ref, c_out, n_out, m_out, sg_out, sl_out):
    h = pl.program_id(0)
    feat = lambda ref, col: ref[pl.ds(pl.multiple_of(col + h * HEAD_DIM, HEAD_DIM), HEAD_DIM), :]
    gate = lambda kind: actt_ref[pl.ds(4 * kind + h, 1), :]
    colsum = lambda x: jnp.sum(x, axis=0, keepdims=True)
    rows = range(HEAD_DIM)

    def headnorm(o, w):
        return o * lax.rsqrt(jnp.mean(o * o, axis=0, keepdims=True) + EPS) * w

    def contract(vecs, s_ref):
        accs = [v[0:1, :] * s_ref[0] for v in vecs]
        for d in rows[1:]:
            sd_ = s_ref[d]
            accs = [a + v[d:d + 1, :] * sd_ for a, v in zip(accs, vecs)]
        return accs

    mq, mk, mv, mo = feat(zt_ref, MQ), feat(zt_ref, MK) * Q_SCALE, feat(zt_ref, MV), feat(zt_ref, MO)
    ic, lf = gate(0), gate(1)
    nvec = n_ref[...]
    m_old = m_ref[pl.ds(h, 1), :]
    a = lf + m_old
    mt = jnp.maximum(a, ic)
    w_int = jnp.exp(a - mt)
    e_i = jnp.exp(ic - mt)
    s = colsum(mq * mk) * e_i
    (qc,) = contract([mq], c_ref)
    num = w_int * qc + s * mv
    den = w_int * colsum(mq * nvec) + s
    hh = num / jnp.maximum(jnp.abs(den), jnp.exp(-mt))
    kw = e_i * mk
    for d in rows:
        c_out[d] = w_int * c_ref[d] + kw[d:d + 1, :] * mv
    n_out[...] = w_int * nvec + kw
    m_out[...] = mt
    y_ref[0:HEAD_DIM, :] = _sigmoid(mo) * headnorm(hh, norm_ref[0])

    gq, gk, gv = feat(qkvt_ref, 0), feat(qkvt_ref, W_GROUP), feat(qkvt_ref, 2 * W_GROUP)
    beta = gate(3)
    eg = jnp.exp(gate(2))
    ks, qs = contract([gk, gq], sg_ref)
    vn = beta * gv - (beta * eg) * ks
    o = eg * qs + colsum(gq * gk) * vn
    for d in rows:
        sg_out[d] = eg * sg_ref[d] + gk[d:d + 1, :] * vn
    y_ref[HEAD_DIM:2 * HEAD_DIM, :] = headnorm(o, norm_ref[1]) * _silu(feat(zt_ref, GZ))

    lq, lk, lv = feat(zt_ref, LQ) * Q_SCALE, feat(zt_ref, LK), feat(zt_ref, LV)
    egk = jnp.exp(feat(glogt_ref, 0))
    (ql,) = contract([lq * egk], sl_ref)
    o = ql + colsum(lq * lk) * lv
    for d in rows:
        sl_out[d] = egk[d:d + 1, :] * sl_ref[d] + lk[d:d + 1, :] * lv
    y_ref[2 * HEAD_DIM:, :] = headnorm(o, norm_ref[2]) * _silu(feat(zt_ref, LR))


def _sample_rec(layer, zt, qkvt, glogt, actt, cst, nst, mst, sgst, slst, norms):
    bn = zt.shape[1]
    mat_in = pl.BlockSpec((None, None, HEAD_DIM, HEAD_DIM, bn), lambda h: (layer, h, 0, 0, 0))
    mat_out = pl.BlockSpec((None, HEAD_DIM, HEAD_DIM, bn), lambda h: (h, 0, 0, 0))
    full = lambda a: pl.BlockSpec(a.shape, lambda h: (0,) * a.ndim)
    sd = lambda *s: jax.ShapeDtypeStruct(s, F32)
    return pl.pallas_call(
        _srec_kernel,
        grid=(N_HEADS,),
        in_specs=[full(zt), full(qkvt), full(glogt), full(actt), mat_in,
                  pl.BlockSpec((None, None, HEAD_DIM, bn), lambda h: (layer, h, 0, 0)),
                  pl.BlockSpec((None, N_HEADS, bn), lambda h: (layer, 0, 0)),
                  mat_in, mat_in,
                  pl.BlockSpec((3, HEAD_DIM, bn), lambda h: (0, h, 0))],
        out_specs=[pl.BlockSpec((None, 3 * HEAD_DIM, bn), lambda h: (h, 0, 0)), mat_out,
                   pl.BlockSpec((None, HEAD_DIM, bn), lambda h: (h, 0, 0)),
                   pl.BlockSpec((None, 1, bn), lambda h: (h, 0, 0)), mat_out, mat_out],
        out_shape=[sd(N_HEADS, 3 * HEAD_DIM, bn), sd(N_HEADS, HEAD_DIM, HEAD_DIM, bn),
                   sd(N_HEADS, HEAD_DIM, bn), sd(N_HEADS, 1, bn),
                   sd(N_HEADS, HEAD_DIM, HEAD_DIM, bn), sd(N_HEADS, HEAD_DIM, HEAD_DIM, bn)],
        compiler_params=_cparams(("parallel",)),
        name="sample_rec",
    )(zt, qkvt, glogt, actt, cst, nst, mst, sgst, slst, norms)


def _outproj_s_kernel(y_ref, w_ref, x_ref, o_ref):
    o_ref[...] = x_ref[...] + _mm3(y_ref[...], w_ref[...])


def _outproj_s(y, w_all, layer, x):
    full = lambda a: pl.BlockSpec(a.shape, lambda i: (0, 0))
    return pl.pallas_call(
        _outproj_s_kernel,
        grid=(1,),
        in_specs=[full(y), pl.BlockSpec((None, D_MODEL, D_MODEL), lambda i: (layer, 0, 0)), full(x)],
        out_specs=full(x),
        out_shape=jax.ShapeDtypeStruct(x.shape, F32),
        compiler_params=_cparams(("arbitrary",)),
        name="outproj_s",
    )(y, w_all, x)


def _ffn_s_kernel(x_ref, g_ref, wg_ref, wu_ref, wd_ref, gf_ref, o_ref, h_ref, acc_ref, *, final_norm):
    j = pl.program_id(0)

    @pl.when(j == 0)
    def _():
        h_ref[...] = _rms(x_ref[...], g_ref[...])
        acc_ref[...] = jnp.zeros_like(acc_ref)

    h = h_ref[...]
    acc_ref[...] += _mm3(_silu(_mm3(h, wg_ref[...])) * _mm3(h, wu_ref[...]), wd_ref[...])

    @pl.when(j == pl.num_programs(0) - 1)
    def _():
        out = x_ref[...] + acc_ref[...]
        o_ref[...] = _rms(out, gf_ref[...]) if final_norm else out


def _ffn_s(x, g, wg, wu, wd, gf, tf, final_norm):
    t = x.shape[0]
    dff = wg.shape[1]
    full = lambda j: (0, 0)
    return pl.pallas_call(
        functools.partial(_ffn_s_kernel, final_norm=final_norm),
        grid=(dff // tf,),
        in_specs=[pl.BlockSpec((t, D_MODEL), full),
                  pl.BlockSpec((1, D_MODEL), full),
                  pl.BlockSpec((D_MODEL, tf), lambda j: (0, j)),
                  pl.BlockSpec((D_MODEL, tf), lambda j: (0, j)),
                  pl.BlockSpec((tf, D_MODEL), lambda j: (j, 0)),
                  pl.BlockSpec((1, D_MODEL), full)],
        out_specs=pl.BlockSpec((t, D_MODEL), full),
        out_shape=jax.ShapeDtypeStruct((t, D_MODEL), F32),
        scratch_shapes=[pltpu.VMEM((t, D_MODEL), F32), pltpu.VMEM((t, D_MODEL), F32)],
        compiler_params=_cparams(("arbitrary",)),
        name="ffn_s",
    )(x, g, wg, wu, wd, gf)


def _outproj_kernel(y_ref, w_ref, x_ref, o_ref):
    o_ref[...] = x_ref[...] + _dot(y_ref[...], w_ref[...].astype(BF16))


def _outproj(y, w_all, layer, x, tm):
    t = x.shape[0]
    return pl.pallas_call(
        _outproj_kernel,
        grid=(t // tm,),
        in_specs=[pl.BlockSpec((tm, D_MODEL), lambda i: (i, 0)),
                  pl.BlockSpec((None, D_MODEL, D_MODEL), lambda i: (layer, 0, 0)),
                  pl.BlockSpec((tm, D_MODEL), lambda i: (i, 0))],
        out_specs=pl.BlockSpec((tm, D_MODEL), lambda i: (i, 0)),
        out_shape=jax.ShapeDtypeStruct((t, D_MODEL), F32),
        compiler_params=_cparams(("parallel",)),
        name="outproj",
    )(y, w_all, x)


def _ffn_kernel(x_ref, y_ref, wo_ref, g_ref, wg_ref, wu_ref, wd_ref, gf_ref, o_ref,
                h_ref, acc_ref, xn_ref, *, final_norm):
    j = pl.program_id(1)

    @pl.when(j == 0)
    def _():
        xn = x_ref[...] + _dot(y_ref[...], wo_ref[...].astype(BF16))
        xn_ref[...] = xn
        h_ref[...] = _rms(xn, g_ref[...]).astype(BF16)
        acc_ref[...] = jnp.zeros_like(acc_ref)

    h = h_ref[...]
    act = _silu(_dot(h, wg_ref[...].astype(BF16))) * _dot(h, wu_ref[...].astype(BF16))
    acc_ref[...] += _dot(act.astype(BF16), wd_ref[...].astype(BF16))

    @pl.when(j == pl.num_programs(1) - 1)
    def _():
        out = xn_ref[...] + acc_ref[...]
        o_ref[...] = _rms(out, gf_ref[...]) if final_norm else out


def _ffn(x, y, wo_all, layer, g, wg_all, wu_all, wd_all, dense_idx, gf, tm, tf, final_norm):
    t = x.shape[0]
    dff = wg_all.shape[2]
    return pl.pallas_call(
        functools.partial(_ffn_kernel, final_norm=final_norm),
        grid=(t // tm, dff // tf),
        in_specs=[pl.BlockSpec((tm, D_MODEL), lambda i, j: (i, 0)),
                  pl.BlockSpec((tm, D_MODEL), lambda i, j: (i, 0)),
                  pl.BlockSpec((None, D_MODEL, D_MODEL), lambda i, j: (layer, 0, 0)),
                  pl.BlockSpec((1, D_MODEL), lambda i, j: (0, 0)),
                  pl.BlockSpec((None, D_MODEL, tf), lambda i, j: (dense_idx, 0, j)),
                  pl.BlockSpec((None, D_MODEL, tf), lambda i, j: (dense_idx, 0, j)),
                  pl.BlockSpec((None, tf, D_MODEL), lambda i, j: (dense_idx, j, 0)),
                  pl.BlockSpec((1, D_MODEL), lambda i, j: (0, 0))],
        out_specs=pl.BlockSpec((tm, D_MODEL), lambda i, j: (i, 0)),
        out_shape=jax.ShapeDtypeStruct((t, D_MODEL), F32),
        scratch_shapes=[pltpu.VMEM((tm, D_MODEL), BF16), pltpu.VMEM((tm, D_MODEL), F32),
                        pltpu.VMEM((tm, D_MODEL), F32)],
        compiler_params=_cparams(("parallel", "arbitrary")),
        name="ffn",
    )(x, y, wo_all, g, wg_all, wu_all, wd_all, gf)


def _top2_gates(logits):
    return _top2_select(logits)[0]


def _top2_select(logits):
    lane = _iota2((1, LANES), 1)
    valid = lane < N_EXPERTS
    logits = jnp.where(valid, logits, -jnp.inf)
    ex = jnp.exp(logits - jnp.max(logits, axis=-1, keepdims=True))
    probs = ex / jnp.sum(ex, axis=-1, keepdims=True)
    v1 = jnp.max(probs, axis=-1, keepdims=True)
    i1 = jnp.min(jnp.where(probs == v1, lane, LANES), axis=-1, keepdims=True)
    rest = jnp.where((lane == i1) | jnp.logical_not(valid), -1.0, probs)
    v2 = jnp.max(rest, axis=-1, keepdims=True)
    i2 = jnp.min(jnp.where(rest == v2, lane, LANES), axis=-1, keepdims=True)
    tot = v1 + v2
    gates = jnp.where(lane == i1, v1 / tot, 0.0) + jnp.where(lane == i2, v2 / tot, 0.0)
    return gates, ((lane == i1) | (lane == i2)).astype(F32)


def _moe_kernel(x_ref, g_ref, wr_ref, br_ref, wg_ref, wu_ref, wd_ref, gf_ref, o_ref,
                h_ref, acc_ref, gates_ref, *, final_norm, precise_router):
    e = pl.program_id(1)

    @pl.when(e == 0)
    def _():
        h = _rms(x_ref[...], g_ref[...])
        h_ref[...] = h.astype(BF16)
        acc_ref[...] = jnp.zeros_like(acc_ref)
        if precise_router:
            logits = _mm3(h, wr_ref[...])
        else:
            logits = _dot(h.astype(BF16), wr_ref[...].astype(BF16))
        gates_ref[...] = _top2_gates(logits + br_ref[...])

    h = h_ref[...]
    y = _dot((_silu(_dot(h, wg_ref[0])) * _dot(h, wu_ref[0])).astype(BF16), wd_ref[0])
    lane = _iota2((1, LANES), 1)
    ge = jnp.sum(jnp.where(lane == e, gates_ref[...], 0.0), axis=-1, keepdims=True)
    acc_ref[...] += ge * y

    @pl.when(e == pl.num_programs(1) - 1)
    def _():
        out = x_ref[...] + acc_ref[...]
        o_ref[...] = _rms(out, gf_ref[...]) if final_norm else out


def _moe(x, g, wr, br, wg, wu, wd, gf, tm, final_norm, precise_router):
    t = x.shape[0]
    fe = wg.shape[2]
    return pl.pallas_call(
        functools.partial(_moe_kernel, final_norm=final_norm, precise_router=precise_router),
        grid=(t // tm, N_EXPERTS),
        in_specs=[pl.BlockSpec((tm, D_MODEL), lambda i, e: (i, 0)),
                  pl.BlockSpec((1, D_MODEL), lambda i, e: (0, 0)),
                  pl.BlockSpec((D_MODEL, LANES), lambda i, e: (0, 0)),
                  pl.BlockSpec((1, LANES), lambda i, e: (0, 0)),
                  pl.BlockSpec((1, D_MODEL, fe), lambda i, e: (e, 0, 0)),
                  pl.BlockSpec((1, D_MODEL, fe), lambda i, e: (e, 0, 0)),
                  pl.BlockSpec((1, fe, D_MODEL), lambda i, e: (e, 0, 0)),
                  pl.BlockSpec((1, D_MODEL), lambda i, e: (0, 0))],
        out_specs=pl.BlockSpec((tm, D_MODEL), lambda i, e: (i, 0)),
        out_shape=jax.ShapeDtypeStruct((t, D_MODEL), F32),
        scratch_shapes=[pltpu.VMEM((tm, D_MODEL), BF16), pltpu.VMEM((tm, D_MODEL), F32),
                        pltpu.VMEM((tm, LANES), F32)],
        compiler_params=_cparams(("parallel", "arbitrary")),
        name="moe",
    )(x, g, wr, br, wg, wu, wd, gf)


def _moe_win_kernel(x_ref, g_ref, wr_ref, br_ref, tril_ref, wg_ref, wu_ref, wd_ref, gf_ref, o_ref,
                    h_ref, gate_ref, key_ref, keyt_ref, *, final_norm, cap, tc):
    e = pl.program_id(1)
    w_rows = x_ref.shape[0]
    lane = _iota2((1, LANES), 1)

    @pl.when(e == 0)
    def _():
        x = x_ref[...]
        hb = _rms(x, g_ref[...]).astype(BF16)
        h_ref[...] = hb
        o_ref[...] = x
        gates, sel = _top2_select(_dot(hb, wr_ref[...].astype(BF16)) + br_ref[...])
        rank = _dot(tril_ref[...], sel.astype(BF16))
        key = jnp.where(sel > 0.0, rank, -1.0)
        gate_ref[...] = gates
        key_ref[...] = key
        keyt_ref[...] = key.T

    key_r = keyt_ref[pl.ds(e, 1), :]
    count = jnp.sum((key_r >= 0.0).astype(F32), axis=-1, keepdims=True)[0, 0].astype(jnp.int32)
    pick_e = (_iota2((LANES, LANES), 0) == e).astype(BF16)
    key_c = _sel_r(key_ref[...], pick_e)
    key_c = jnp.concatenate([key_c] * pl.cdiv(cap, LANES), axis=1)[:, :cap]
    gate_c = _sel_r(gate_ref[...], pick_e)
    gate_c = jnp.concatenate([gate_c] * (tc // LANES), axis=1)
    row_id = _iota2((cap, w_rows), 0).astype(F32)
    col_id = _iota2((w_rows, cap), 1).astype(F32)

    def trip(j, carry):
        base = (j * cap).astype(F32)
        gather = jnp.where(row_id + base == key_r, 1.0, 0.0).astype(BF16)
        xg = _dot(gather, h_ref[...]).astype(BF16)
        act = (_silu(_dot(xg, wg_ref[0])) * _dot(xg, wu_ref[0])).astype(BF16)
        y1, y2 = _split2(_dot(act, wd_ref[0]))
        scatter = jnp.where(col_id + base == key_c, 1.0, 0.0).astype(BF16)
        for c0 in range(0, D_MODEL, tc):
            cs = slice(c0, c0 + tc)
            o_ref[:, cs] += gate_c * (_dot(scatter, y1[:, cs]) + _dot(scatter, y2[:, cs]))
        return carry

    lax.fori_loop(0, (count + cap - 1) // cap, trip, 0)

    if final_norm:
        @pl.when(e == pl.num_programs(1) - 1)
        def _():
            o_ref[...] = _rms(o_ref[...], gf_ref[...])


def _moe_win(x, g, wr, br, wg, wu, wd, gf, w_rows, final_norm):
    t = x.shape[0]
    fe = wg.shape[2]
    cap = max(16, int(w_rows / 4 + 2.3 * (w_rows * 3 / 16) ** 0.5) // 16 * 16)
    tc = min(w_rows, 256)
    tril = jnp.asarray(np.tril(np.ones((w_rows, w_rows), np.float32), -1), dtype=BF16)
    const = lambda i, e: (0, 0)
    return pl.pallas_call(
        functools.partial(_moe_win_kernel, final_norm=final_norm, cap=cap, tc=tc),
        grid=(t // w_rows, N_EXPERTS),
        in_specs=[pl.BlockSpec((w_rows, D_MODEL), lambda i, e: (i, 0)),
                  pl.BlockSpec((1, D_MODEL), const),
                  pl.BlockSpec((D_MODEL, LANES), const),
                  pl.BlockSpec((1, LANES), const),
                  pl.BlockSpec((w_rows, w_rows), const),
                  pl.BlockSpec((1, D_MODEL, fe), lambda i, e: (e, 0, 0)),
                  pl.BlockSpec((1, D_MODEL, fe), lambda i, e: (e, 0, 0)),
                  pl.BlockSpec((1, fe, D_MODEL), lambda i, e: (e, 0, 0)),
                  pl.BlockSpec((1, D_MODEL), const)],
        out_specs=pl.BlockSpec((w_rows, D_MODEL), lambda i, e: (i, 0)),
        out_shape=jax.ShapeDtypeStruct((t, D_MODEL), F32),
        scratch_shapes=[pltpu.VMEM((w_rows, D_MODEL), BF16),
                        pltpu.VMEM((w_rows, LANES), F32),
                        pltpu.VMEM((w_rows, LANES), F32), pltpu.VMEM((LANES, w_rows), F32)],
        compiler_params=_cparams(("parallel", "arbitrary")),
        name="moe_win",
    )(x, g, wr, br, tril, wg, wu, wd, gf)


def _prep_w_in(w):
    seg = lambda i: w[:, _OFF[i]:_OFF[i + 1]]
    small = jnp.concatenate([seg(i) for i in _SMALL], axis=1)
    pad = jnp.zeros((w.shape[0], LANES - N_SMALL), w.dtype)
    wfull = jnp.concatenate([seg(i) for i in _BIG] + [small, pad], axis=1).astype(BF16)
    gates = jnp.concatenate([seg(i) for i in _GATES], axis=1)
    return wfull, gates.T.astype(BF16)


def _regroup_z(z):
    seg = lambda i: z[:, _OFF[i]:_OFF[i + 1]]
    pad = jnp.zeros((z.shape[0], LANES - N_SMALL), z.dtype)
    return jnp.concatenate([seg(i) for i in _BIG] + [seg(i) for i in _SMALL] + [pad], axis=1)


def _pad_lanes(v, start):
    out = jnp.zeros((1, LANES), F32)
    return lax.dynamic_update_slice(out, v.reshape(1, -1).astype(F32), (0, start))


def _layer_params(l, mlstm_b_i, mlstm_b_f, mlstm_norm, gdn_conv_w, gdn_a_log, gdn_dt_bias, gdn_norm,
                  sc_conv_w, gla_w_gate, gla_b_gate, gla_norm):
    bias = jnp.concatenate([mlstm_b_i[l], mlstm_b_f[l], gdn_dt_bias[l]]).astype(F32)
    alog = gdn_a_log[l].astype(F32)
    per_lane = lambda v: jnp.repeat(v, HEAD_DIM)
    zero = jnp.zeros((W_GROUP,), F32)
    brow = jnp.stack([per_lane(mlstm_b_i[l]), per_lane(mlstm_b_f[l]), per_lane(gdn_dt_bias[l]), zero])
    arow = jnp.stack([zero, zero, per_lane(alog), zero])
    wg32 = jnp.zeros((LANES, W_GROUP), F32).at[16:16 + GLA_RANK].set(gla_w_gate[l])
    return dict(bcol=_pad_lanes(bias, 0), acol=_pad_lanes(alog, 8), brow=brow, arow=arow,
                mnorm=mlstm_norm[l].reshape(1, -1), gnorm=gdn_norm[l].reshape(1, -1),
                lnorm=gla_norm[l].reshape(1, -1),
                gcw=gdn_conv_w[l].T, scw=sc_conv_w[l].T, wg=wg32.astype(BF16), wg32=wg32,
                bg=gla_b_gate[l].reshape(1, -1))


def _pick_tile(n, pref):
    for c in pref:
        if n % c == 0:
            return c
    return n


def kernel(x_prompt, x_sample, state_mlstm_C, state_mlstm_n, state_mlstm_m, state_gdn_S, state_gdn_conv,
           state_sc_conv, state_gla_S, w_in, g_mix, mlstm_b_i, mlstm_b_f, mlstm_norm, gdn_conv_w, gdn_a_log,
           gdn_dt_bias, gdn_norm, sc_conv_w, gla_w_gate, gla_b_gate, gla_norm, w_out, g_ffn, ffn_w_gate,
           ffn_w_up, ffn_w_down, moe_w_router, moe_b_router, moe_w_gate, moe_w_up, moe_w_down, g_final):
    depth = w_in.shape[0]
    bsz, seq, _ = x_prompt.shape
    bn = x_sample.shape[0]
    tp = bsz * seq
    assert x_sample.shape[1] == 1 and seq % CHUNK == 0

    xp = x_prompt.reshape(tp, D_MODEL)
    xs = x_sample.reshape(bn, D_MODEL)
    lb_rows = _pick_tile(seq, (128, 64))
    tm_p = _pick_tile(tp, (512, 256, 128, 64))
    tm_f = _pick_tile(tp, (1024, 512, 256, 128, 64))
    tm_e = _pick_tile(tp, (1024, 512, 256, 128))
    nseq = _pick_tile(bsz, (4, 2, 1))

    s_minor = (state_mlstm_C.transpose(0, 2, 3, 4, 1), state_mlstm_n.transpose(0, 2, 3, 1),
               state_mlstm_m.transpose(0, 2, 1), state_gdn_S.transpose(0, 2, 3, 4, 1),
               state_gla_S.transpose(0, 2, 3, 4, 1))
    gfin = g_final.reshape(1, -1)

    p_states = [[] for _ in range(7)]
    s_states = [[] for _ in range(7)]
    for l in range(depth):
        p = _layer_params(l, mlstm_b_i, mlstm_b_f, mlstm_norm, gdn_conv_w, gdn_a_log, gdn_dt_bias,
                          gdn_norm, sc_conv_w, gla_w_gate, gla_b_gate, gla_norm)
        wfull, wst = _prep_w_in(w_in[l])
        gm = g_mix[l].reshape(1, -1)
        gf = g_ffn[l].reshape(1, -1)
        last = l == depth - 1

        z, zt = _inproj(xp, gm, wfull, wst, tm_p, lb_rows)
        y, c1, n1, m1, sg1, gb1, sb1, sl1 = _mix_prompt(
            z.reshape(bsz, seq, NP), zt.reshape((bsz, seq // lb_rows) + zt.shape[1:]), p,
            bsz, seq, lb_rows, nseq)
        y = y.reshape(tp, D_MODEL)
        if l % 2:
            xp = _outproj(y, w_out, l, xp, tm_f)
        for lst, v in zip(p_states, (c1, n1, m1.reshape(bsz, N_HEADS), sg1, gb1, sb1, sl1)):
            lst.append(v)

        zs = _regroup_z(_inproj_s(xs, gm, w_in, l))
        gbuf = state_gdn_conv[l].reshape(bn, -1)
        sbuf = state_sc_conv[l].reshape(bn, -1)
        zt_s, qkvt, glogt, actt, yc, gbuf1, sbuf1 = _sample_pre(zs, gbuf, sbuf, p)
        norms = jnp.stack([jnp.broadcast_to(v[:, None], (W_GROUP, bn))
                           for v in (mlstm_norm[l], gdn_norm[l], gla_norm[l])])
        ysr, c2, n2, m2, sg2, sl2 = _sample_rec(l, zt_s, qkvt, glogt, actt, *s_minor, norms)
        yt = ysr.reshape(N_HEADS, 3, HEAD_DIM, bn).transpose(3, 1, 0, 2).reshape(bn, 3, W_GROUP)
        ymix = jnp.concatenate([yt[:, 0], yt[:, 1], yc, yt[:, 2]], axis=-1)
        xs = _outproj_s(ymix, w_out, l, xs)
        for lst, v in zip(s_states, (c2, n2, m2[:, 0], sg2, gbuf1.reshape(bn, 3, -1),
                                     sbuf1.reshape(bn, 2, -1), sl2)):
            lst.append(v)

        j = l // 2
        if l % 2 == 0:
            tf = _pick_tile(ffn_w_gate.shape[2], (256, 128))
            xp = _ffn(xp, y, w_out, l, gf, ffn_w_gate, ffn_w_up, ffn_w_down, j, gfin, tm_f, tf, last)
            xs = _ffn_s(xs, gf, ffn_w_gate[j], ffn_w_up[j], ffn_w_down[j], gfin, tf, last)
        else:
            wr = jnp.zeros((D_MODEL, LANES), F32).at[:, :N_EXPERTS].set(moe_w_router[j])
            br = _pad_lanes(moe_b_router[j], 0)
            wgt, wup, wdn = (moe_w_gate[j].astype(BF16), moe_w_up[j].astype(BF16),
                             moe_w_down[j].astype(BF16))
            xp = _moe_win(xp, gf, wr, br, wgt, wup, wdn, gfin, tm_e, last)
            xs = _moe(xs, gf, wr, br, wgt, wup, wdn, gfin, bn, last, True)

    y_prompt = xp.reshape(bsz, seq, D_MODEL)
    y_sample = xs.reshape(bn, 1, D_MODEL)
    sp = [jnp.stack(v) for v in p_states]
    ss = [jnp.stack(v) for v in s_states]
    for i, perm in ((0, (0, 4, 1, 2, 3)), (1, (0, 3, 1, 2)), (2, (0, 2, 1)), (3, (0, 4, 1, 2, 3)),
                    (6, (0, 4, 1, 2, 3))):
        ss[i] = ss[i].transpose(perm)
    return (y_prompt, y_sample, sp[0], ss[0], sp[1], ss[1], sp[2], ss[2], sp[3], ss[3],
            sp[4], ss[4], sp[5], ss[5], sp[6], ss[6])
```

```python
import functools

import numpy as np
import jax
import jax.numpy as jnp
from jax import lax
from jax.experimental import pallas as pl
from jax.experimental.pallas import tpu as pltpu

F32 = jnp.float32
BF16 = jnp.bfloat16

D_MODEL = 1024
W_GROUP = 256
N_HEADS = 4
HEAD_DIM = 64
CHUNK = 64
GLA_RANK = 16
GLA_NORMALIZER = 16.0
N_EXPERTS = 8
EPS = 1e-6
Q_SCALE = HEAD_DIM ** -0.5

VMEM_LIMIT = 56 * 1024 * 1024
LANES = 128

SPLIT_SIZES = ([W_GROUP] * 4 + [N_HEADS] * 2 + [W_GROUP] * 4 + [N_HEADS] * 2 +
               [W_GROUP] * 3 + [W_GROUP] * 4 + [GLA_RANK])
_OFF = np.concatenate([[0], np.cumsum(SPLIT_SIZES)])
_BIG = [0, 1, 2, 3, 6, 7, 8, 9, 12, 13, 14, 15, 16, 17, 18]
_SMALL = [4, 5, 10, 11, 19]
_GATES = [4, 5, 10, 11]
MQ, MK, MV, MO = 0, 256, 512, 768
GQ, GK, GV, GZ = 1024, 1280, 1536, 1792
SB, SC, SH = 2048, 2304, 2560
LQ, LK, LV, LR = 2816, 3072, 3328, 3584
SM = 3840
NP = SM + LANES
N_SMALL = 32
N_GATE_ROWS = 16
N_ROW_KINDS = 3
GX_I, GX_B, GX_G, GX_BETA, GX_PM = 0, 256, 512, 768, 1024
GLA_LEVELS = (32, 16, 8, 4, 2)


def _cparams(sem):
    return pltpu.CompilerParams(dimension_semantics=sem, vmem_limit_bytes=VMEM_LIMIT)


def _log1pexp_negabs(x):
    return jnp.log(1.0 + jnp.exp(-jnp.abs(x)))


def _softplus(x):
    return jnp.maximum(x, 0.0) + _log1pexp_negabs(x)


def _logsigmoid(x):
    return -(jnp.maximum(-x, 0.0) + _log1pexp_negabs(x))


def _sigmoid(x):
    return 1.0 / (1.0 + jnp.exp(-x))


def _silu(x):
    return x * _sigmoid(x)


def _rms(x, g):
    return x * lax.rsqrt(jnp.mean(x * x, axis=-1, keepdims=True) + EPS) * g


def _dot(a, b):
    return jnp.dot(a, b, preferred_element_type=F32)


def _dot_nt(a, b):
    return lax.dot_general(a, b, (((1,), (1,)), ((), ())), preferred_element_type=F32)


def _dot_tn(a, b):
    return lax.dot_general(a, b, (((0,), (0,)), ((), ())), preferred_element_type=F32)


def _split2(x):
    x1 = x.astype(BF16)
    return x1, (x - x1.astype(F32)).astype(BF16)


def _split3(x):
    x1 = x.astype(BF16)
    r1 = x - x1.astype(F32)
    x2 = r1.astype(BF16)
    x3 = (r1 - x2.astype(F32)).astype(BF16)
    return x1, x2, x3


def _sel(m, x):
    x1, x2, x3 = _split3(x)
    return _dot(m, x1) + _dot(m, x2) + _dot(m, x3)


def _sel_r(x, m):
    n = x.shape[0]
    if n % 8:
        x1, x2, x3 = _split3(x)
        return _dot(x1, m) + _dot(x2, m) + _dot(x3, m)
    r = _dot(jnp.concatenate(_split3(x), axis=0), m)
    return r[0:n] + r[n:2 * n] + r[2 * n:3 * n]


def _sel_nt(m, x):
    x1, x2, x3 = _split3(x)
    return _dot_nt(m, x1) + _dot_nt(m, x2) + _dot_nt(m, x3)


def _mm3(x, w):
    n = x.shape[0]
    x1, x2 = _split2(x)
    w1, w2 = _split2(w)
    r = _dot(jnp.concatenate([x1, x2], axis=0), w1)
    return r[0:n] + r[n:] + _dot(x1, w2)


def _iota2(shape, dim):
    return lax.broadcasted_iota(jnp.int32, shape, dim)


def _inproj_kernel(x_ref, g_ref, w_ref, wst_ref, z_ref, zt_ref):
    hb = _rms(x_ref[...], g_ref[...]).astype(BF16)
    z_ref[...] = _dot(hb, w_ref[...])
    zt = _dot_nt(wst_ref[...], hb)
    n_blk, n_kind, n_chunks, _ = zt_ref.shape
    for b in range(n_blk):
        for k in range(n_kind):
            for c in range(n_chunks):
                t0 = (b * n_chunks + c) * CHUNK
                for h in range(N_HEADS):
                    zt_ref[b, k, c:c + 1, h * HEAD_DIM:(h + 1) * HEAD_DIM] = (
                        zt[k * 4 + h:k * 4 + h + 1, t0:t0 + CHUNK])


def _inproj(x, g, w, wst, tm, lb_rows):
    t = x.shape[0]
    n_chunks = lb_rows // CHUNK
    return pl.pallas_call(
        _inproj_kernel,
        grid=(t // tm,),
        in_specs=[pl.BlockSpec((tm, D_MODEL), lambda i: (i, 0)),
                  pl.BlockSpec((1, D_MODEL), lambda i: (0, 0)),
                  pl.BlockSpec((D_MODEL, NP), lambda i: (0, 0)),
                  pl.BlockSpec((N_GATE_ROWS, D_MODEL), lambda i: (0, 0))],
        out_specs=[pl.BlockSpec((tm, NP), lambda i: (i, 0)),
                   pl.BlockSpec((tm // lb_rows, N_ROW_KINDS, n_chunks, W_GROUP), lambda i: (i, 0, 0, 0))],
        out_shape=[jax.ShapeDtypeStruct((t, NP), F32),
                   jax.ShapeDtypeStruct((t // lb_rows, N_ROW_KINDS, n_chunks, W_GROUP), F32)],
        compiler_params=_cparams(("parallel",)),
        name="inproj",
    )(x, g, w, wst)


def _gate_act(pre, idx, neg_a):
    tail = _log1pexp_negabs(pre)
    lf = -(jnp.maximum(-pre, 0.0) + tail)
    dec = neg_a * (jnp.maximum(pre, 0.0) + tail)
    beta = _sigmoid(pre)
    return jnp.where(idx < 4, pre,
                     jnp.where(idx < 8, lf,
                               jnp.where(idx < 12, dec,
                                         jnp.where(idx < 16, beta, 0.0))))


def _mix_kernel(z_ref, zt_ref, bcol_ref, acol_ref, brow_ref, arow_ref, mnorm_ref, gnorm_ref,
                lnorm_ref, gcw_ref, scw_ref, wg_ref, bg_ref,
                bdmask_ref, bdtriu_ref, tril64_ref, selexp_ref, eye64_ref,
                y_ref, c_out, n_out, m_out, sg_out, gconv_out, sconv_out, sl_out,
                c_ref, n_ref, m_ref, sg_ref, slt_ref, xg_ref, xs_ref, qkv_ref, gcum_ref, glog_ref,
                gx_ref, rw_ref, *, lb_rows, nseq):
    lb = pl.program_id(1)
    nlb = pl.num_programs(1)
    n_chunks = lb_rows // CHUNK
    wd = W_GROUP
    seqs = range(nseq)

    @pl.when(lb == 0)
    def _():
        c_ref[...] = jnp.zeros_like(c_ref)
        n_ref[...] = jnp.zeros_like(n_ref)
        m_ref[...] = jnp.zeros_like(m_ref)
        sg_ref[...] = jnp.zeros_like(sg_ref)
        slt_ref[...] = jnp.zeros_like(slt_ref)
        for i in seqs:
            xg_ref[i, 0:8, :] = jnp.zeros((8, 3 * wd), F32)
            xs_ref[i, 0:8, :] = jnp.zeros((8, wd), F32)

    ti = _iota2((CHUNK, wd), 0)
    si = _iota2((CHUNK, wd), 1) % HEAD_DIM
    incl = ti >= si
    strict = ti > si
    eye = ti == si
    blk16 = (ti // 16) == (si // 16)
    blk32 = (ti // 32) == (si // 32)
    eye_f = eye.astype(F32)
    lvl_mask = {n: ((ti // (2 * n)) == (si // (2 * n))) & ((ti // n) > (si // n))
                for n in GLA_LEVELS + (1,)}
    bdmask = bdmask_ref[...]
    bdtriu = bdtriu_ref[...]
    tril64 = tril64_ref[...]
    head_of_lane = _iota2((1, wd), 1) // HEAD_DIM
    pos_of_lane = _iota2((8, wd), 1) % HEAD_DIM

    lane_half = [(_iota2((1, LANES), 1) // HEAD_DIM == j).astype(BF16) for j in (0, 1)]
    zero_tile = jnp.zeros((HEAD_DIM, LANES), BF16)

    def bd(x):
        xb = x.astype(BF16)
        blocks = []
        for h in range(N_HEADS):
            t = h // 2
            kept = xb[:, t * LANES:(t + 1) * LANES] * lane_half[h % 2]
            blocks.append(jnp.concatenate([kept, zero_tile] if t == 0 else [zero_tile, kept], axis=1))
        return jnp.concatenate(blocks, axis=0)

    def unbd(m):
        out = m[3 * HEAD_DIM:, :]
        for h in (2, 1, 0):
            out = jnp.where(head_of_lane == h, m[h * HEAD_DIM:(h + 1) * HEAD_DIM, :], out)
        return out

    def mm(a, bmat):
        return _dot(a.astype(BF16), bmat)

    def mm_nt(a, bmat):
        return _dot_nt(a.astype(BF16), bmat)

    def mm3_bd(a, b):
        a1, a2 = _split2(a)
        b1, b2 = _split2(b)
        r = _dot(jnp.concatenate([a1, a2], axis=0), bd(b1))
        return r[0:CHUNK] + r[CHUNK:] + _dot(a1, bd(b2))

    def hsum(x):
        x1, x2 = _split2(x)
        r = _dot(jnp.concatenate([x1, x2], axis=0), bdmask)
        return r[0:x.shape[0]] + r[x.shape[0]:]

    def headnorm(o, w):
        return o * lax.rsqrt(hsum(o * o) * (1.0 / HEAD_DIM) + EPS) * w

    lane = _iota2((1, LANES), 1)
    neg_a_col = -jnp.exp(acol_ref[...])
    neg_a_row = -jnp.exp(arow_ref[2:3, :])
    selexp = selexp_ref[...]
    eye64 = eye64_ref[...]

    def l2n(x):
        return x * lax.rsqrt(hsum(x * x) + EPS)

    rows_all = nseq * lb_rows
    merge = lambda x: x.reshape(rows_all, x.shape[-1])
    split = lambda x: x.reshape(nseq, lb_rows, x.shape[-1])
    small = merge(z_ref[:, :, SM:SM + LANES])
    act = _gate_act(small + bcol_ref[...], lane, neg_a_col)
    def chunk_cumsum(x):
        n_c, width = rows_all // CHUNK, x.shape[1]
        wide = jnp.concatenate([x[c * CHUNK:(c + 1) * CHUNK, :] for c in range(n_c)], axis=1)
        cs = _sel(tril64, wide)
        return jnp.concatenate([cs[:, c * width:(c + 1) * width] for c in range(n_c)], axis=0)

    csum = chunk_cumsum(act)
    comp = jnp.where((lane >= 4) & (lane < 12), csum, act)
    gx_ref[:, :, GX_I:GX_I + 4 * wd] = split(_sel_r(comp, selexp))
    glin = _dot(small.astype(BF16), wg_ref[...]) + bg_ref[...]
    glog = _logsigmoid(glin) * (1.0 / GLA_NORMALIZER)
    glog_ref[...] = split(glog)
    gcum_ref[...] = split(chunk_cumsum(glog))

    n_r = nseq * n_chunks
    trow = lambda k: jnp.concatenate([zt_ref[i, 0, k] for i in seqs], axis=0) + brow_ref[k:k + 1, :]
    ic_r = trow(0)
    lf_r = _logsigmoid(trow(1))
    dec_r = neg_a_row * _softplus(trow(2))
    cs_r = _sel_r(jnp.concatenate([lf_r, dec_r], axis=0), bdtriu)
    c_rows = ic_r - cs_r[0:n_r, :]
    g_rows = cs_r[n_r:, :]
    pm = jnp.concatenate([c_rows] * max(1, 8 // n_r), axis=0)
    for sh in (1, 2, 4, 8, 16, 32):
        pm = jnp.where(pos_of_lane[0:1, :] >= sh, jnp.maximum(pm, pltpu.roll(pm, sh, axis=1)), pm)
    for r in range(n_r):
        rw_ref[r // n_chunks, r % n_chunks, 0:1, :] = c_rows[r:r + 1, :]
        rw_ref[r // n_chunks, r % n_chunks, 1:2, :] = g_rows[r:r + 1, :]
    diag = jnp.concatenate(
        [jnp.where(eye, jnp.broadcast_to(pm[r:r + 1, :], (CHUNK, wd)), 0.0) for r in range(n_r)], axis=0)
    gx_ref[:, :, GX_PM:GX_PM + wd] = split(_sel_r(diag, bdmask))

    xg_ref[:, 8:, :] = z_ref[:, :, GQ:GQ + 3 * wd]
    conv = xg_ref[:, 5:5 + lb_rows, :] * gcw_ref[0:1, :]
    for j in range(1, 4):
        conv = conv + xg_ref[:, 5 + j:5 + j + lb_rows, :] * gcw_ref[j:j + 1, :]
    conv = merge(_silu(conv))
    qkv_ref[:, :, 0:wd] = split(l2n(conv[:, 0:wd]) * Q_SCALE)
    qkv_ref[:, :, wd:2 * wd] = split(l2n(conv[:, wd:2 * wd]))
    qkv_ref[:, :, 2 * wd:] = split(conv[:, 2 * wd:])

    xs_ref[:, 8:, :] = z_ref[:, :, SC:SC + wd] * z_ref[:, :, SH:SH + wd]
    cu = xs_ref[:, 6:6 + lb_rows, :] * scw_ref[0:1, :]
    for j in range(1, 3):
        cu = cu + xs_ref[:, 6 + j:6 + j + lb_rows, :] * scw_ref[j:j + 1, :]
    y_ref[:, :, 2 * wd:3 * wd] = (z_ref[:, :, SB:SB + wd] * cu).astype(y_ref.dtype)


    def chunk(c, carry):
        r0 = pl.multiple_of(c * CHUNK, CHUNK)
        rs = pl.ds(r0, CHUNK)
        each = lambda f, *xs: [f(*a) for a in zip(*xs)]
        zcol = lambda o: [z_ref[i, rs, o:o + wd] for i in seqs]
        gxcol = lambda o: [gx_ref[i, rs, o:o + wd] for i in seqs]
        last = lambda xs: [x[CHUNK - 1:CHUNK, :] for x in xs]
        tn_bd = lambda a, b: each(lambda x, y: unbd(_dot_tn(x.astype(BF16), y.astype(BF16))), a, b)
        mm3s = lambda a, b: each(mm3_bd, a, b)
        rw = [rw_ref[i, c, 0:2, :] for i in seqs]

        q = zcol(MQ)
        k = [x * Q_SCALE for x in zcol(MK)]
        v = zcol(MV)
        ic_e, b_e, pm_e = gxcol(GX_I), gxcol(GX_B), gxcol(GX_PM)
        m_old = [m_ref[i, 0:1, :] for i in seqs]
        cmat = [c_ref[i] for i in seqs]
        nrow = [n_ref[i, 0:1, :] for i in seqs]
        mx = each(jnp.maximum, m_old, pm_e)
        w_int = each(lambda a, b: jnp.exp(a - b), m_old, mx)
        p = each(lambda r, m: jnp.where(incl, jnp.exp(r[0:1, :] - m), 0.0), rw, mx)
        s = each(lambda a, b, c_: mm_nt(a, bd(b)) * c_, q, k, p)
        num = each(lambda w_, a, cm, s_, v_: w_ * mm(a, bd(cm)) + mm(s_, bd(v_)), w_int, q, cmat, s, v)
        den = each(lambda w_, a, n_, s_: hsum(w_ * (a * n_) + s_), w_int, q, nrow, s)
        hh = each(lambda n_, d_, b_, m: n_ / jnp.maximum(jnp.abs(d_), jnp.exp(-(b_ + m))), num, den, b_e, mx)
        mx_last = last(mx)
        wgk = each(lambda i_, b_, ml, k_: jnp.exp((i_ - b_) - ml) * k_, ic_e, b_e, mx_last, k)
        dec = each(lambda a, b: jnp.exp(a - b), m_old, mx_last)
        c_new = each(lambda d_, cm, u_: d_ * cm + u_, dec, cmat, tn_bd(wgk, v))
        for i in seqs:
            c_ref[i] = c_new[i]
            n_ref[i] = jnp.broadcast_to(dec[i] * nrow[i] + jnp.sum(wgk[i], axis=0, keepdims=True), (8, wd))
            m_ref[i] = jnp.broadcast_to(b_e[i][CHUNK - 1:CHUNK, :] + mx_last[i], (8, wd))
        y_a = each(lambda o_, h_: _sigmoid(o_) * headnorm(h_, mnorm_ref[...]), zcol(MO), hh)
        for i in seqs:
            y_ref[i, rs, 0:wd] = y_a[i].astype(y_ref.dtype)

        q = [qkv_ref[i, rs, 0:wd] for i in seqs]
        k = [qkv_ref[i, rs, wd:2 * wd] for i in seqs]
        v = [qkv_ref[i, rs, 2 * wd:] for i in seqs]
        g_e, beta = gxcol(GX_G), gxcol(GX_BETA)
        s_old = [sg_ref[i] for i in seqs]
        bdk = each(bd, k)
        lw = each(lambda g_, r: jnp.where(incl, jnp.exp(g_ - r[1:2, :]), 0.0), g_e, rw)
        amat = each(lambda b_, k_, bk, l_: jnp.where(strict, b_ * mm_nt(k_, bk) * l_, 0.0), beta, k, bdk, lw)
        egc = each(jnp.exp, g_e)
        ad = [jnp.where(blk16, a, 0.0) for a in amat]
        a2 = mm3s(ad, ad)
        a4 = mm3s(a2, a2)
        a8 = mm3s(a4, a4)
        tinv = [eye_f - a for a in ad]
        for apow in (a2, a4, a8):
            tinv = each(lambda t_, u_: t_ + u_, tinv, mm3s(tinv, apow))
        n1 = [jnp.where(blk32 & jnp.logical_not(blk16), a, 0.0) for a in amat]
        tinv = each(lambda t_, u_: t_ - u_, tinv, mm3s(mm3s(tinv, n1), tinv))
        n2 = [jnp.where(blk32, 0.0, a) for a in amat]
        tinv = each(lambda t_, u_: t_ - u_, tinv, mm3s(mm3s(tinv, n2), tinv))
        u = mm3s(tinv, each(lambda b_, v_: b_ * v_, beta, v))
        w = mm3s(tinv, each(lambda b_, e_, k_: (b_ * e_) * k_, beta, egc, k))
        bds = each(bd, s_old)
        vn = each(lambda u_, w_, bs: u_ - mm(w_, bs), u, w, bds)
        qk = each(lambda q_, bk, l_: mm_nt(q_, bk) * l_, q, bdk, lw)
        o = each(lambda q_, e_, bs, qk_, vn_: mm(q_ * e_, bs) + mm(qk_, bd(vn_)), q, egc, bds, qk, vn)
        g_last = last(g_e)
        kdec = each(lambda k_, gl, g_: k_ * jnp.exp(gl - g_), k, g_last, g_e)
        s_new = each(lambda gl, so, u_: jnp.exp(gl) * so + u_, g_last, s_old, tn_bd(kdec, vn))
        y_b = each(lambda o_, z_: headnorm(o_, gnorm_ref[...]) * _silu(z_), o, zcol(GZ))
        for i in seqs:
            sg_ref[i] = s_new[i]
            y_ref[i, rs, wd:2 * wd] = y_b[i].astype(y_ref.dtype)

        q = [x * Q_SCALE for x in zcol(LQ)]
        k = zcol(LK)
        v = zcol(LV)
        gk = [glog_ref[i, rs, :] for i in seqs]
        gcum = [gcum_ref[i, rs, :] for i in seqs]
        st_old = [slt_ref[i] for i in seqs]
        bdk = each(bd, k)
        amat = each(lambda q_, bk: jnp.where(eye, mm_nt(q_, bk), 0.0), q, bdk)
        amat = each(lambda a_, q_, g_, bk: a_ + jnp.where(lvl_mask[1], mm_nt(q_ * jnp.exp(g_), bk), 0.0),
                    amat, q, gk, bdk)
        g_end = gcum
        g_prev = [jnp.where(ti == 0, 0.0, pltpu.roll(g_, 1, axis=0)) for g_ in gcum]
        for n in GLA_LEVELS[::-1]:
            half = n // 2
            upper = (ti % n) >= half
            g_end = [jnp.where(upper, e_, pltpu.roll(e_, CHUNK - half, axis=0)) for e_ in g_end]
            g_prev = [jnp.where(upper, pltpu.roll(p_, half, axis=0), p_) for p_ in g_prev]
            qn = each(lambda q_, gc, gp: q_ * jnp.exp(gc - gp), q, gcum, g_prev)
            kn = each(lambda k_, ge, gc: k_ * jnp.exp(ge - gc), k, g_end, gcum)
            amat = each(lambda a_, q_, k_, m=lvl_mask[n]: a_ + jnp.where(m, mm_nt(q_, bd(k_)), 0.0),
                        amat, qn, kn)
        g_last = last(gcum)
        o = each(lambda q_, gc, st, a_, v_: mm_nt(q_ * jnp.exp(gc), bd(st)) + mm(a_, bd(v_)),
                 q, gcum, st_old, amat, v)
        kdec = each(lambda k_, gl, gc: k_ * jnp.exp(gl - gc), k, g_last, gcum)
        st_new = each(lambda st, gl, u_: st * jnp.exp(gl) + u_, st_old, g_last, tn_bd(v, kdec))
        y_d = each(lambda o_, z_: headnorm(o_, lnorm_ref[...]) * _silu(z_), o, zcol(LR))
        for i in seqs:
            slt_ref[i] = st_new[i]
            y_ref[i, rs, 3 * wd:] = y_d[i].astype(y_ref.dtype)
        return carry

    lax.fori_loop(0, n_chunks, chunk, 0)

    gtail = [xg_ref[i, lb_rows:lb_rows + 8, :] for i in seqs]
    stail = [xs_ref[i, lb_rows:lb_rows + 8, :] for i in seqs]
    for i in seqs:
        xg_ref[i, 0:8, :] = gtail[i]
        xs_ref[i, 0:8, :] = stail[i]

    @pl.when(lb == nlb - 1)
    def _():
        lane4 = _iota2((1, N_HEADS), 1)
        for i in seqs:
            gconv_out[i] = gtail[i][5:8, :]
            sconv_out[i] = stail[i][6:8, :]
            m_row = jnp.zeros((1, N_HEADS), F32)
            for h in range(N_HEADS):
                hs = slice(h * HEAD_DIM, (h + 1) * HEAD_DIM)
                c_out[i, h] = c_ref[i, :, hs]
                n_out[i, h:h + 1, :] = n_ref[i, 0:1, hs]
                m_row = jnp.where(lane4 == h, m_ref[i, 0:1, h * HEAD_DIM:h * HEAD_DIM + 1], m_row)
                sg_out[i, h] = sg_ref[i, :, hs]
                sl_out[i, h] = _sel_nt(eye64, slt_ref[i, :, hs])
            m_out[i] = m_row


def _mix_consts():
    wd = W_GROUP
    r, c = np.indices((wd, wd))
    same_head = (r // HEAD_DIM) == (c // HEAD_DIM)
    bdmask = same_head
    bdtriu = same_head & ((r % HEAD_DIM) <= (c % HEAD_DIM))
    tril64 = np.tril(np.ones((CHUNK, CHUNK), bool))
    rr, cc = np.indices((LANES, 4 * wd))
    selexp = rr == 4 * (cc // wd) + (cc % wd) // HEAD_DIM
    eye64 = np.eye(CHUNK, dtype=bool)
    return [jnp.asarray(m, dtype=BF16) for m in (bdmask, bdtriu, tril64, selexp, eye64)]


def _mix_prompt(z, zt, p, bsz, seq, lb_rows, nseq):
    nlb = seq // lb_rows
    n_chunks = lb_rows // CHUNK
    row = lambda b, l: (b, l, 0)
    const2 = lambda b, l: (0, 0)
    st4 = lambda b, l: (b, 0, 0, 0)
    st3 = lambda b, l: (b, 0, 0)
    params = [p["bcol"], p["acol"], p["brow"], p["arow"], p["mnorm"], p["gnorm"], p["lnorm"],
              p["gcw"], p["scw"], p["wg"], p["bg"]] + _mix_consts()
    in_specs = [pl.BlockSpec((nseq, lb_rows, NP), row),
                pl.BlockSpec((nseq, 1, N_ROW_KINDS, n_chunks, W_GROUP), lambda b, l: (b, l, 0, 0, 0))]
    in_specs += [pl.BlockSpec(a.shape, const2) for a in params]
    hd = (N_HEADS, HEAD_DIM, HEAD_DIM)
    out_shape = [jax.ShapeDtypeStruct((bsz, seq, D_MODEL), BF16),
                 jax.ShapeDtypeStruct((bsz,) + hd, F32),
                 jax.ShapeDtypeStruct((bsz, N_HEADS, HEAD_DIM), F32),
                 jax.ShapeDtypeStruct((bsz, 1, N_HEADS), F32),
                 jax.ShapeDtypeStruct((bsz,) + hd, F32),
                 jax.ShapeDtypeStruct((bsz, 3, 3 * W_GROUP), F32),
                 jax.ShapeDtypeStruct((bsz, 2, W_GROUP), F32),
                 jax.ShapeDtypeStruct((bsz,) + hd, F32)]
    out_specs = [pl.BlockSpec((nseq, lb_rows, D_MODEL), row),
                 pl.BlockSpec((nseq,) + hd, st4),
                 pl.BlockSpec((nseq, N_HEADS, HEAD_DIM), st3),
                 pl.BlockSpec((nseq, 1, N_HEADS), st3),
                 pl.BlockSpec((nseq,) + hd, st4),
                 pl.BlockSpec((nseq, 3, 3 * W_GROUP), st3),
                 pl.BlockSpec((nseq, 2, W_GROUP), st3),
                 pl.BlockSpec((nseq,) + hd, st4)]
    vm = lambda *shape: pltpu.VMEM((nseq,) + shape, F32)
    scratch = [vm(HEAD_DIM, W_GROUP),
               vm(8, W_GROUP),
               vm(8, W_GROUP),
               vm(HEAD_DIM, W_GROUP),
               vm(HEAD_DIM, W_GROUP),
               vm(lb_rows + 8, 3 * W_GROUP),
               vm(lb_rows + 8, W_GROUP),
               vm(lb_rows, 3 * W_GROUP),
               vm(lb_rows, W_GROUP),
               vm(lb_rows, W_GROUP),
               vm(lb_rows, 5 * W_GROUP),
               vm(n_chunks, 8, W_GROUP)]
    return pl.pallas_call(
        functools.partial(_mix_kernel, lb_rows=lb_rows, nseq=nseq),
        grid=(bsz // nseq, nlb),
        in_specs=in_specs,
        out_specs=out_specs,
        out_shape=out_shape,
        scratch_shapes=scratch,
        compiler_params=_cparams(("parallel", "arbitrary")),
        name="mix_prompt",
    )(z, zt, *params)


def _inproj_s_kernel(x_ref, g_ref, wt_ref, zt_ref):
    w = wt_ref[...]
    n = w.shape[0]
    w1, w2 = _split2(w)
    h1, h2 = _split2(_rms(x_ref[...], g_ref[...]))
    r = _dot_nt(jnp.concatenate([w1, w2], axis=0), h1)
    zt_ref[...] = r[0:n] + r[n:] + _dot_nt(w1, h2)


def _inproj_s(x, g, wt_all, layer, n_blk=4):
    bn = x.shape[0]
    d_in = wt_all.shape[1]
    tn = d_in // n_blk
    return pl.pallas_call(
        _inproj_s_kernel,
        grid=(n_blk,),
        in_specs=[pl.BlockSpec((bn, D_MODEL), lambda j: (0, 0)),
                  pl.BlockSpec((1, D_MODEL), lambda j: (0, 0)),
                  pl.BlockSpec((None, tn, D_MODEL), lambda j: (layer, j, 0))],
        out_specs=pl.BlockSpec((tn, bn), lambda j: (j, 0)),
        out_shape=jax.ShapeDtypeStruct((d_in, bn), F32),
        compiler_params=_cparams(("parallel",)),
        name="inproj_s",
    )(x, g, wt_all)


def _spre_kernel(zt_ref, gbuf_ref, sbuf_ref, bcol_ref, acol_ref, gcw_ref, scw_ref, wg_ref, bg_ref,
                 qkvt_ref, glogt_ref, actt_ref, yc_ref, gbuf_out, sbuf_out):
    rows_t = lambda col, n: zt_ref[col:col + n, :].T
    small = rows_t(SM, LANES)
    lane = _iota2((1, LANES), 1)
    actt_ref[...] = _gate_act(small + bcol_ref[...], lane, -jnp.exp(acol_ref[...])).T
    glin = _mm3(small, wg_ref[...]) + bg_ref[...]
    glogt_ref[...] = (_logsigmoid(glin) * (1.0 / GLA_NORMALIZER)).T

    wq = 3 * W_GROUP
    u = rows_t(GQ, wq)
    b0, b1, b2 = gbuf_ref[:, 0:wq], gbuf_ref[:, wq:2 * wq], gbuf_ref[:, 2 * wq:]
    conv = b0 * gcw_ref[0:1, :] + b1 * gcw_ref[1:2, :] + b2 * gcw_ref[2:3, :] + u * gcw_ref[3:4, :]
    conv = _silu(conv)
    gbuf_out[:, 0:wq] = b1
    gbuf_out[:, wq:2 * wq] = b2
    gbuf_out[:, 2 * wq:] = u
    hsum = (_iota2((W_GROUP, W_GROUP), 0) // HEAD_DIM ==
            _iota2((W_GROUP, W_GROUP), 1) // HEAD_DIM).astype(BF16)

    def l2n(x):
        return x * lax.rsqrt(_sel_r(x * x, hsum) + EPS)

    qkvt_ref[0:W_GROUP, :] = (l2n(conv[:, 0:W_GROUP]) * Q_SCALE).T
    qkvt_ref[W_GROUP:2 * W_GROUP, :] = l2n(conv[:, W_GROUP:2 * W_GROUP]).T
    qkvt_ref[2 * W_GROUP:, :] = conv[:, 2 * W_GROUP:].T

    u2 = rows_t(SC, W_GROUP) * rows_t(SH, W_GROUP)
    s0, s1 = sbuf_ref[:, 0:W_GROUP], sbuf_ref[:, W_GROUP:]
    cu = s0 * scw_ref[0:1, :] + s1 * scw_ref[1:2, :] + u2 * scw_ref[2:3, :]
    yc_ref[...] = rows_t(SB, W_GROUP) * cu
    sbuf_out[:, 0:W_GROUP] = s1
    sbuf_out[:, W_GROUP:] = u2


def _sample_pre(zt, gbuf, sbuf, p):
    bn = zt.shape[1]
    sd = lambda s: jax.ShapeDtypeStruct(s, F32)
    return pl.pallas_call(
        _spre_kernel,
        out_shape=[sd((3 * W_GROUP, bn)), sd((W_GROUP, bn)), sd((LANES, bn)),
                   sd((bn, W_GROUP)), sd(gbuf.shape), sd(sbuf.shape)],
        compiler_params=pltpu.CompilerParams(vmem_limit_bytes=VMEM_LIMIT),
        name="sample_pre",
    )(zt, gbuf, sbuf, p["bcol"], p["acol"], p["gcw"], p["scw"], p["wg32"], p["bg"])


def _srec_kernel(zt_ref, qkvt_ref, glogt_ref, actt_ref, c_ref, n_ref, m_ref, sg_ref, sl_ref, norm_ref,
                 y_ref, c_out, n_out, m_out, sg_out, sl_out):
    h = pl.program_id(0)
    feat = lambda ref, col: ref[pl.ds(pl.multiple_of(col + h * HEAD_DIM, HEAD_DIM), HEAD_DIM), :]
    gate = lambda kind: actt_ref[pl.ds(4 * kind + h, 1), :]
    colsum = lambda x: jnp.sum(x, axis=0, keepdims=True)
    rows = range(HEAD_DIM)

    def headnorm(o, w):
        return o * lax.rsqrt(jnp.mean(o * o, axis=0, keepdims=True) + EPS) * w

    def contract(vecs, s_ref):
        accs = [v[0:1, :] * s_ref[0] for v in vecs]
        for d in rows[1:]:
            sd_ = s_ref[d]
            accs = [a + v[d:d + 1, :] * sd_ for a, v in zip(accs, vecs)]
        return accs

    mq, mk, mv, mo = feat(zt_ref, MQ), feat(zt_ref, MK) * Q_SCALE, feat(zt_ref, MV), feat(zt_ref, MO)
    ic, lf = gate(0), gate(1)
    nvec = n_ref[...]
    m_old = m_ref[pl.ds(h, 1), :]
    a = lf + m_old
    mt = jnp.maximum(a, ic)
    w_int = jnp.exp(a - mt)
    e_i = jnp.exp(ic - mt)
    s = colsum(mq * mk) * e_i
    (qc,) = contract([mq], c_ref)
    num = w_int * qc + s * mv
    den = w_int * colsum(mq * nvec) + s
    hh = num / jnp.maximum(jnp.abs(den), jnp.exp(-mt))
    kw = e_i * mk
    for d in rows:
        c_out[d] = w_int * c_ref[d] + kw[d:d + 1, :] * mv
    n_out[...] = w_int * nvec + kw
    m_out[...] = mt
    y_ref[0:HEAD_DIM, :] = _sigmoid(mo) * headnorm(hh, norm_ref[0])

    gq, gk, gv = feat(qkvt_ref, 0), feat(qkvt_ref, W_GROUP), feat(qkvt_ref, 2 * W_GROUP)
    beta = gate(3)
    eg = jnp.exp(gate(2))
    ks, qs = contract([gk, gq], sg_ref)
    vn = beta * gv - (beta * eg) * ks
    o = eg * qs + colsum(gq * gk) * vn
    for d in rows:
        sg_out[d] = eg * sg_ref[d] + gk[d:d + 1, :] * vn
    y_ref[HEAD_DIM:2 * HEAD_DIM, :] = headnorm(o, norm_ref[1]) * _silu(feat(zt_ref, GZ))

    lq, lk, lv = feat(zt_ref, LQ) * Q_SCALE, feat(zt_ref, LK), feat(zt_ref, LV)
    egk = jnp.exp(feat(glogt_ref, 0))
    (ql,) = contract([lq * egk], sl_ref)
    o = ql + colsum(lq * lk) * lv
    for d in rows:
        sl_out[d] = egk[d:d + 1, :] * sl_ref[d] + lk[d:d + 1, :] * lv
    y_ref[2 * HEAD_DIM:, :] = headnorm(o, norm_ref[2]) * _silu(feat(zt_ref, LR))


def _sample_rec(layer, zt, qkvt, glogt, actt, cst, nst, mst, sgst, slst, norms):
    bn = zt.shape[1]
    mat_in = pl.BlockSpec((None, None, HEAD_DIM, HEAD_DIM, bn), lambda h: (layer, h, 0, 0, 0))
    mat_out = pl.BlockSpec((None, HEAD_DIM, HEAD_DIM, bn), lambda h: (h, 0, 0, 0))
    full = lambda a: pl.BlockSpec(a.shape, lambda h: (0,) * a.ndim)
    sd = lambda *s: jax.ShapeDtypeStruct(s, F32)
    return pl.pallas_call(
        _srec_kernel,
        grid=(N_HEADS,),
        in_specs=[full(zt), full(qkvt), full(glogt), full(actt), mat_in,
                  pl.BlockSpec((None, None, HEAD_DIM, bn), lambda h: (layer, h, 0, 0)),
                  pl.BlockSpec((None, N_HEADS, bn), lambda h: (layer, 0, 0)),
                  mat_in, mat_in,
                  pl.BlockSpec((3, HEAD_DIM, bn), lambda h: (0, h, 0))],
        out_specs=[pl.BlockSpec((None, 3 * HEAD_DIM, bn), lambda h: (h, 0, 0)), mat_out,
                   pl.BlockSpec((None, HEAD_DIM, bn), lambda h: (h, 0, 0)),
                   pl.BlockSpec((None, 1, bn), lambda h: (h, 0, 0)), mat_out, mat_out],
        out_shape=[sd(N_HEADS, 3 * HEAD_DIM, bn), sd(N_HEADS, HEAD_DIM, HEAD_DIM, bn),
                   sd(N_HEADS, HEAD_DIM, bn), sd(N_HEADS, 1, bn),
                   sd(N_HEADS, HEAD_DIM, HEAD_DIM, bn), sd(N_HEADS, HEAD_DIM, HEAD_DIM, bn)],
        compiler_params=_cparams(("parallel",)),
        name="sample_rec",
    )(zt, qkvt, glogt, actt, cst, nst, mst, sgst, slst, norms)


def _outproj_s_kernel(y_ref, w_ref, x_ref, o_ref):
    o_ref[...] = x_ref[...] + _mm3(y_ref[...], w_ref[...])


def _outproj_s(y, w_all, layer, x):
    full = lambda a: pl.BlockSpec(a.shape, lambda i: (0, 0))
    return pl.pallas_call(
        _outproj_s_kernel,
        grid=(1,),
        in_specs=[full(y), pl.BlockSpec((None, D_MODEL, D_MODEL), lambda i: (layer, 0, 0)), full(x)],
        out_specs=full(x),
        out_shape=jax.ShapeDtypeStruct(x.shape, F32),
        compiler_params=_cparams(("arbitrary",)),
        name="outproj_s",
    )(y, w_all, x)


def _ffn_s_kernel(x_ref, g_ref, wg_ref, wu_ref, wd_ref, gf_ref, o_ref, h_ref, acc_ref, *, final_norm):
    j = pl.program_id(0)

    @pl.when(j == 0)
    def _():
        h_ref[...] = _rms(x_ref[...], g_ref[...])
        acc_ref[...] = jnp.zeros_like(acc_ref)

    h = h_ref[...]
    acc_ref[...] += _mm3(_silu(_mm3(h, wg_ref[...])) * _mm3(h, wu_ref[...]), wd_ref[...])

    @pl.when(j == pl.num_programs(0) - 1)
    def _():
        out = x_ref[...] + acc_ref[...]
        o_ref[...] = _rms(out, gf_ref[...]) if final_norm else out


def _ffn_s(x, g, wg, wu, wd, gf, tf, final_norm):
    t = x.shape[0]
    dff = wg.shape[1]
    full = lambda j: (0, 0)
    return pl.pallas_call(
        functools.partial(_ffn_s_kernel, final_norm=final_norm),
        grid=(dff // tf,),
        in_specs=[pl.BlockSpec((t, D_MODEL), full),
                  pl.BlockSpec((1, D_MODEL), full),
                  pl.BlockSpec((D_MODEL, tf), lambda j: (0, j)),
                  pl.BlockSpec((D_MODEL, tf), lambda j: (0, j)),
                  pl.BlockSpec((tf, D_MODEL), lambda j: (j, 0)),
                  pl.BlockSpec((1, D_MODEL), full)],
        out_specs=pl.BlockSpec((t, D_MODEL), full),
        out_shape=jax.ShapeDtypeStruct((t, D_MODEL), F32),
        scratch_shapes=[pltpu.VMEM((t, D_MODEL), F32), pltpu.VMEM((t, D_MODEL), F32)],
        compiler_params=_cparams(("arbitrary",)),
        name="ffn_s",
    )(x, g, wg, wu, wd, gf)


def _outproj_kernel(y_ref, w_ref, x_ref, o_ref):
    o_ref[...] = x_ref[...] + _dot(y_ref[...], w_ref[...].astype(BF16))


def _outproj(y, w_all, layer, x, tm):
    t = x.shape[0]
    return pl.pallas_call(
        _outproj_kernel,
        grid=(t // tm,),
        in_specs=[pl.BlockSpec((tm, D_MODEL), lambda i: (i, 0)),
                  pl.BlockSpec((None, D_MODEL, D_MODEL), lambda i: (layer, 0, 0)),
                  pl.BlockSpec((tm, D_MODEL), lambda i: (i, 0))],
        out_specs=pl.BlockSpec((tm, D_MODEL), lambda i: (i, 0)),
        out_shape=jax.ShapeDtypeStruct((t, D_MODEL), F32),
        compiler_params=_cparams(("parallel",)),
        name="outproj",
    )(y, w_all, x)


def _ffn_kernel(x_ref, y_ref, wo_ref, g_ref, wg_ref, wu_ref, wd_ref, gf_ref, o_ref,
                h_ref, acc_ref, xn_ref, *, final_norm):
    j = pl.program_id(1)

    @pl.when(j == 0)
    def _():
        xn = x_ref[...] + _dot(y_ref[...], wo_ref[...].astype(BF16))
        xn_ref[...] = xn
        h_ref[...] = _rms(xn, g_ref[...]).astype(BF16)
        acc_ref[...] = jnp.zeros_like(acc_ref)

    h = h_ref[...]
    act = _silu(_dot(h, wg_ref[...].astype(BF16))) * _dot(h, wu_ref[...].astype(BF16))
    acc_ref[...] += _dot(act.astype(BF16), wd_ref[...].astype(BF16))

    @pl.when(j == pl.num_programs(1) - 1)
    def _():
        out = xn_ref[...] + acc_ref[...]
        o_ref[...] = _rms(out, gf_ref[...]) if final_norm else out


def _ffn(x, y, wo_all, layer, g, wg_all, wu_all, wd_all, dense_idx, gf, tm, tf, final_norm):
    t = x.shape[0]
    dff = wg_all.shape[2]
    return pl.pallas_call(
        functools.partial(_ffn_kernel, final_norm=final_norm),
        grid=(t // tm, dff // tf),
        in_specs=[pl.BlockSpec((tm, D_MODEL), lambda i, j: (i, 0)),
                  pl.BlockSpec((tm, D_MODEL), lambda i, j: (i, 0)),
                  pl.BlockSpec((None, D_MODEL, D_MODEL), lambda i, j: (layer, 0, 0)),
                  pl.BlockSpec((1, D_MODEL), lambda i, j: (0, 0)),
                  pl.BlockSpec((None, D_MODEL, tf), lambda i, j: (dense_idx, 0, j)),
                  pl.BlockSpec((None, D_MODEL, tf), lambda i, j: (dense_idx, 0, j)),
                  pl.BlockSpec((None, tf, D_MODEL), lambda i, j: (dense_idx, j, 0)),
                  pl.BlockSpec((1, D_MODEL), lambda i, j: (0, 0))],
        out_specs=pl.BlockSpec((tm, D_MODEL), lambda i, j: (i, 0)),
        out_shape=jax.ShapeDtypeStruct((t, D_MODEL), F32),
        scratch_shapes=[pltpu.VMEM((tm, D_MODEL), BF16), pltpu.VMEM((tm, D_MODEL), F32),
                        pltpu.VMEM((tm, D_MODEL), F32)],
        compiler_params=_cparams(("parallel", "arbitrary")),
        name="ffn",
    )(x, y, wo_all, g, wg_all, wu_all, wd_all, gf)


def _top2_gates(logits):
    return _top2_select(logits)[0]


def _top2_select(logits):
    lane = _iota2((1, LANES), 1)
    valid = lane < N_EXPERTS
    logits = jnp.where(valid, logits, -jnp.inf)
    ex = jnp.exp(logits - jnp.max(logits, axis=-1, keepdims=True))
    probs = ex / jnp.sum(ex, axis=-1, keepdims=True)
    v1 = jnp.max(probs, axis=-1, keepdims=True)
    i1 = jnp.min(jnp.where(probs == v1, lane, LANES), axis=-1, keepdims=True)
    rest = jnp.where((lane == i1) | jnp.logical_not(valid), -1.0, probs)
    v2 = jnp.max(rest, axis=-1, keepdims=True)
    i2 = jnp.min(jnp.where(rest == v2, lane, LANES), axis=-1, keepdims=True)
    tot = v1 + v2
    gates = jnp.where(lane == i1, v1 / tot, 0.0) + jnp.where(lane == i2, v2 / tot, 0.0)
    return gates, ((lane == i1) | (lane == i2)).astype(F32)


def _moe_kernel(x_ref, g_ref, wr_ref, br_ref, wg_ref, wu_ref, wd_ref, gf_ref, o_ref,
                h_ref, acc_ref, gates_ref, *, final_norm, precise_router):
    e = pl.program_id(1)

    @pl.when(e == 0)
    def _():
        h = _rms(x_ref[...], g_ref[...])
        h_ref[...] = h.astype(BF16)
        acc_ref[...] = jnp.zeros_like(acc_ref)
        if precise_router:
            logits = _mm3(h, wr_ref[...])
        else:
            logits = _dot(h.astype(BF16), wr_ref[...].astype(BF16))
        gates_ref[...] = _top2_gates(logits + br_ref[...])

    h = h_ref[...]
    y = _dot((_silu(_dot(h, wg_ref[0])) * _dot(h, wu_ref[0])).astype(BF16), wd_ref[0])
    lane = _iota2((1, LANES), 1)
    ge = jnp.sum(jnp.where(lane == e, gates_ref[...], 0.0), axis=-1, keepdims=True)
    acc_ref[...] += ge * y

    @pl.when(e == pl.num_programs(1) - 1)
    def _():
        out = x_ref[...] + acc_ref[...]
        o_ref[...] = _rms(out, gf_ref[...]) if final_norm else out


def _moe(x, g, wr, br, wg, wu, wd, gf, tm, final_norm, precise_router):
    t = x.shape[0]
    fe = wg.shape[2]
    return pl.pallas_call(
        functools.partial(_moe_kernel, final_norm=final_norm, precise_router=precise_router),
        grid=(t // tm, N_EXPERTS),
        in_specs=[pl.BlockSpec((tm, D_MODEL), lambda i, e: (i, 0)),
                  pl.BlockSpec((1, D_MODEL), lambda i, e: (0, 0)),
                  pl.BlockSpec((D_MODEL, LANES), lambda i, e: (0, 0)),
                  pl.BlockSpec((1, LANES), lambda i, e: (0, 0)),
                  pl.BlockSpec((1, D_MODEL, fe), lambda i, e: (e, 0, 0)),
                  pl.BlockSpec((1, D_MODEL, fe), lambda i, e: (e, 0, 0)),
                  pl.BlockSpec((1, fe, D_MODEL), lambda i, e: (e, 0, 0)),
                  pl.BlockSpec((1, D_MODEL), lambda i, e: (0, 0))],
        out_specs=pl.BlockSpec((tm, D_MODEL), lambda i, e: (i, 0)),
        out_shape=jax.ShapeDtypeStruct((t, D_MODEL), F32),
        scratch_shapes=[pltpu.VMEM((tm, D_MODEL), BF16), pltpu.VMEM((tm, D_MODEL), F32),
                        pltpu.VMEM((tm, LANES), F32)],
        compiler_params=_cparams(("parallel", "arbitrary")),
        name="moe",
    )(x, g, wr, br, wg, wu, wd, gf)


def _moe_win_kernel(x_ref, g_ref, wr_ref, br_ref, tril_ref, wg_ref, wu_ref, wd_ref, gf_ref, o_ref,
                    h_ref, gate_ref, key_ref, keyt_ref, *, final_norm, cap, tc):
    e = pl.program_id(1)
    w_rows = x_ref.shape[0]
    lane = _iota2((1, LANES), 1)

    @pl.when(e == 0)
    def _():
        x = x_ref[...]
        hb = _rms(x, g_ref[...]).astype(BF16)
        h_ref[...] = hb
        o_ref[...] = x
        gates, sel = _top2_select(_dot(hb, wr_ref[...].astype(BF16)) + br_ref[...])
        rank = _dot(tril_ref[...], sel.astype(BF16))
        key = jnp.where(sel > 0.0, rank, -1.0)
        gate_ref[...] = gates
        key_ref[...] = key
        keyt_ref[...] = key.T

    key_r = keyt_ref[pl.ds(e, 1), :]
    count = jnp.sum((key_r >= 0.0).astype(F32), axis=-1, keepdims=True)[0, 0].astype(jnp.int32)
    pick_e = (_iota2((LANES, LANES), 0) == e).astype(BF16)
    key_c = _sel_r(key_ref[...], pick_e)
    key_c = jnp.concatenate([key_c] * pl.cdiv(cap, LANES), axis=1)[:, :cap]
    gate_c = _sel_r(gate_ref[...], pick_e)
    gate_c = jnp.concatenate([gate_c] * (tc // LANES), axis=1)
    row_id = _iota2((cap, w_rows), 0).astype(F32)
    col_id = _iota2((w_rows, cap), 1).astype(F32)

    def trip(j, carry):
        base = (j * cap).astype(F32)
        gather = jnp.where(row_id + base == key_r, 1.0, 0.0).astype(BF16)
        xg = _dot(gather, h_ref[...]).astype(BF16)
        act = (_silu(_dot(xg, wg_ref[0])) * _dot(xg, wu_ref[0])).astype(BF16)
        y1, y2 = _split2(_dot(act, wd_ref[0]))
        scatter = jnp.where(col_id + base == key_c, 1.0, 0.0).astype(BF16)
        for c0 in range(0, D_MODEL, tc):
            cs = slice(c0, c0 + tc)
            o_ref[:, cs] += gate_c * (_dot(scatter, y1[:, cs]) + _dot(scatter, y2[:, cs]))
        return carry

    lax.fori_loop(0, (count + cap - 1) // cap, trip, 0)

    if final_norm:
        @pl.when(e == pl.num_programs(1) - 1)
        def _():
            o_ref[...] = _rms(o_ref[...], gf_ref[...])


def _moe_win(x, g, wr, br, wg, wu, wd, gf, w_rows, final_norm):
    t = x.shape[0]
    fe = wg.shape[2]
    cap = max(16, int(w_rows / 4 + 3.5 * (w_rows * 3 / 16) ** 0.5) // 16 * 16)
    tc = min(w_rows, 256)
    tril = jnp.asarray(np.tril(np.ones((w_rows, w_rows), np.float32), -1), dtype=BF16)
    const = lambda i, e: (0, 0)
    return pl.pallas_call(
        functools.partial(_moe_win_kernel, final_norm=final_norm, cap=cap, tc=tc),
        grid=(t // w_rows, N_EXPERTS),
        in_specs=[pl.BlockSpec((w_rows, D_MODEL), lambda i, e: (i, 0)),
                  pl.BlockSpec((1, D_MODEL), const),
                  pl.BlockSpec((D_MODEL, LANES), const),
                  pl.BlockSpec((1, LANES), const),
                  pl.BlockSpec((w_rows, w_rows), const),
                  pl.BlockSpec((1, D_MODEL, fe), lambda i, e: (e, 0, 0)),
                  pl.BlockSpec((1, D_MODEL, fe), lambda i, e: (e, 0, 0)),
                  pl.BlockSpec((1, fe, D_MODEL), lambda i, e: (e, 0, 0)),
                  pl.BlockSpec((1, D_MODEL), const)],
        out_specs=pl.BlockSpec((w_rows, D_MODEL), lambda i, e: (i, 0)),
        out_shape=jax.ShapeDtypeStruct((t, D_MODEL), F32),
        scratch_shapes=[pltpu.VMEM((w_rows, D_MODEL), BF16),
                        pltpu.VMEM((w_rows, LANES), F32),
                        pltpu.VMEM((w_rows, LANES), F32), pltpu.VMEM((LANES, w_rows), F32)],
        compiler_params=_cparams(("parallel", "arbitrary")),
        name="moe_win",
    )(x, g, wr, br, tril, wg, wu, wd, gf)


def _prep_w_in(w):
    seg = lambda i: w[:, _OFF[i]:_OFF[i + 1]]
    small = jnp.concatenate([seg(i) for i in _SMALL], axis=1)
    pad = jnp.zeros((w.shape[0], LANES - N_SMALL), w.dtype)
    wfull = jnp.concatenate([seg(i) for i in _BIG] + [small, pad], axis=1).astype(BF16)
    gates = jnp.concatenate([seg(i) for i in _GATES], axis=1)
    return wfull, gates.T.astype(BF16)


def _regroup_zt(zt):
    seg = lambda i: zt[_OFF[i]:_OFF[i + 1], :]
    pad = jnp.zeros((LANES - N_SMALL, zt.shape[1]), zt.dtype)
    return jnp.concatenate([seg(i) for i in _BIG] + [seg(i) for i in _SMALL] + [pad], axis=0)


def _pad_lanes(v, start):
    out = jnp.zeros((1, LANES), F32)
    return lax.dynamic_update_slice(out, v.reshape(1, -1).astype(F32), (0, start))


def _layer_params(l, mlstm_b_i, mlstm_b_f, mlstm_norm, gdn_conv_w, gdn_a_log, gdn_dt_bias, gdn_norm,
                  sc_conv_w, gla_w_gate, gla_b_gate, gla_norm):
    bias = jnp.concatenate([mlstm_b_i[l], mlstm_b_f[l], gdn_dt_bias[l]]).astype(F32)
    alog = gdn_a_log[l].astype(F32)
    per_lane = lambda v: jnp.repeat(v, HEAD_DIM)
    zero = jnp.zeros((W_GROUP,), F32)
    brow = jnp.stack([per_lane(mlstm_b_i[l]), per_lane(mlstm_b_f[l]), per_lane(gdn_dt_bias[l]), zero])
    arow = jnp.stack([zero, zero, per_lane(alog), zero])
    wg32 = jnp.zeros((LANES, W_GROUP), F32).at[16:16 + GLA_RANK].set(gla_w_gate[l])
    return dict(bcol=_pad_lanes(bias, 0), acol=_pad_lanes(alog, 8), brow=brow, arow=arow,
                mnorm=mlstm_norm[l].reshape(1, -1), gnorm=gdn_norm[l].reshape(1, -1),
                lnorm=gla_norm[l].reshape(1, -1),
                gcw=gdn_conv_w[l].T, scw=sc_conv_w[l].T, wg=wg32.astype(BF16), wg32=wg32,
                bg=gla_b_gate[l].reshape(1, -1))


def _pick_tile(n, pref):
    for c in pref:
        if n % c == 0:
            return c
    return n


def kernel(x_prompt, x_sample, state_mlstm_C, state_mlstm_n, state_mlstm_m, state_gdn_S, state_gdn_conv,
           state_sc_conv, state_gla_S, w_in, g_mix, mlstm_b_i, mlstm_b_f, mlstm_norm, gdn_conv_w, gdn_a_log,
           gdn_dt_bias, gdn_norm, sc_conv_w, gla_w_gate, gla_b_gate, gla_norm, w_out, g_ffn, ffn_w_gate,
           ffn_w_up, ffn_w_down, moe_w_router, moe_b_router, moe_w_gate, moe_w_up, moe_w_down, g_final):
    depth = w_in.shape[0]
    bsz, seq, _ = x_prompt.shape
    bn = x_sample.shape[0]
    tp = bsz * seq
    assert x_sample.shape[1] == 1 and seq % CHUNK == 0

    xp = x_prompt.reshape(tp, D_MODEL)
    xs = x_sample.reshape(bn, D_MODEL)
    lb_rows = _pick_tile(seq, (128, 64))
    tm_p = _pick_tile(tp, (512, 256, 128, 64))
    tm_f = _pick_tile(tp, (1024, 512, 256, 128, 64))
    tm_e = _pick_tile(tp, (1024, 512, 256, 128))
    nseq = _pick_tile(bsz, (4, 2, 1))

    s_minor = (state_mlstm_C.transpose(0, 2, 3, 4, 1), state_mlstm_n.transpose(0, 2, 3, 1),
               state_mlstm_m.transpose(0, 2, 1), state_gdn_S.transpose(0, 2, 3, 4, 1),
               state_gla_S.transpose(0, 2, 3, 4, 1))
    w_in_t = w_in.transpose(0, 2, 1)
    gfin = g_final.reshape(1, -1)

    p_states = [[] for _ in range(7)]
    s_states = [[] for _ in range(7)]
    for l in range(depth):
        p = _layer_params(l, mlstm_b_i, mlstm_b_f, mlstm_norm, gdn_conv_w, gdn_a_log, gdn_dt_bias,
                          gdn_norm, sc_conv_w, gla_w_gate, gla_b_gate, gla_norm)
        wfull, wst = _prep_w_in(w_in[l])
        gm = g_mix[l].reshape(1, -1)
        gf = g_ffn[l].reshape(1, -1)
        last = l == depth - 1

        z, zt = _inproj(xp, gm, wfull, wst, tm_p, lb_rows)
        y, c1, n1, m1, sg1, gb1, sb1, sl1 = _mix_prompt(
            z.reshape(bsz, seq, NP), zt.reshape((bsz, seq // lb_rows) + zt.shape[1:]), p,
            bsz, seq, lb_rows, nseq)
        y = y.reshape(tp, D_MODEL)
        if l % 2:
            xp = _outproj(y, w_out, l, xp, tm_f)
        for lst, v in zip(p_states, (c1, n1, m1.reshape(bsz, N_HEADS), sg1, gb1, sb1, sl1)):
            lst.append(v)

        zt_s = _regroup_zt(_inproj_s(xs, gm, w_in_t, l))
        gbuf = state_gdn_conv[l].reshape(bn, -1)
        sbuf = state_sc_conv[l].reshape(bn, -1)
        qkvt, glogt, actt, yc, gbuf1, sbuf1 = _sample_pre(zt_s, gbuf, sbuf, p)
        norms = jnp.stack([jnp.broadcast_to(v[:, None], (W_GROUP, bn))
                           for v in (mlstm_norm[l], gdn_norm[l], gla_norm[l])])
        ysr, c2, n2, m2, sg2, sl2 = _sample_rec(l, zt_s, qkvt, glogt, actt, *s_minor, norms)
        yt = ysr.reshape(N_HEADS, 3, HEAD_DIM, bn).transpose(3, 1, 0, 2).reshape(bn, 3, W_GROUP)
        ymix = jnp.concatenate([yt[:, 0], yt[:, 1], yc, yt[:, 2]], axis=-1)
        xs = _outproj_s(ymix, w_out, l, xs)
        for lst, v in zip(s_states, (c2, n2, m2[:, 0], sg2, gbuf1.reshape(bn, 3, -1),
                                     sbuf1.reshape(bn, 2, -1), sl2)):
            lst.append(v)

        j = l // 2
        if l % 2 == 0:
            tf = _pick_tile(ffn_w_gate.shape[2], (256, 128))
            xp = _ffn(xp, y, w_out, l, gf, ffn_w_gate, ffn_w_up, ffn_w_down, j, gfin, tm_f, tf, last)
            xs = _ffn_s(xs, gf, ffn_w_gate[j], ffn_w_up[j], ffn_w_down[j], gfin, tf, last)
        else:
            wr = jnp.zeros((D_MODEL, LANES), F32).at[:, :N_EXPERTS].set(moe_w_router[j])
            br = _pad_lanes(moe_b_router[j], 0)
            wgt, wup, wdn = (moe_w_gate[j].astype(BF16), moe_w_up[j].astype(BF16),
                             moe_w_down[j].astype(BF16))
            xp = _moe_win(xp, gf, wr, br, wgt, wup, wdn, gfin, tm_e, last)
            xs = _moe(xs, gf, wr, br, wgt, wup, wdn, gfin, bn, last, True)

    y_prompt = xp.reshape(bsz, seq, D_MODEL)
    y_sample = xs.reshape(bn, 1, D_MODEL)
    sp = [jnp.stack(v) for v in p_states]
    ss = [jnp.stack(v) for v in s_states]
    for i, perm in ((0, (0, 4, 1, 2, 3)), (1, (0, 3, 1, 2)), (2, (0, 2, 1)), (3, (0, 4, 1, 2, 3)),
                    (6, (0, 4, 1, 2, 3))):
        ss[i] = ss[i].transpose(perm)
    return (y_prompt, y_sample, sp[0], ss[0], sp[1], ss[1], sp[2], ss[2], sp[3], ss[3],
            sp[4], ss[4], sp[5], ss[5], sp[6], ss[6])
```

```python
import functools

import numpy as np
import jax
import jax.numpy as jnp
from jax import lax
from jax.experimental import pallas as pl
from jax.experimental.pallas import tpu as pltpu

F32 = jnp.float32
BF16 = jnp.bfloat16

D_MODEL = 1024
W_GROUP = 256
N_HEADS = 4
HEAD_DIM = 64
CHUNK = 64
GLA_RANK = 16
GLA_NORMALIZER = 16.0
N_EXPERTS = 8
EPS = 1e-6
Q_SCALE = HEAD_DIM ** -0.5

VMEM_LIMIT = 56 * 1024 * 1024
LANES = 128

SPLIT_SIZES = ([W_GROUP] * 4 + [N_HEADS] * 2 + [W_GROUP] * 4 + [N_HEADS] * 2 +
               [W_GROUP] * 3 + [W_GROUP] * 4 + [GLA_RANK])
_OFF = np.concatenate([[0], np.cumsum(SPLIT_SIZES)])
_BIG = [0, 1, 2, 3, 6, 7, 8, 9, 12, 13, 14, 15, 16, 17, 18]
_SMALL = [4, 5, 10, 11, 19]
_GATES = [4, 5, 10, 11]
MQ, MK, MV, MO = 0, 256, 512, 768
GQ, GK, GV, GZ = 1024, 1280, 1536, 1792
SB, SC, SH = 2048, 2304, 2560
LQ, LK, LV, LR = 2816, 3072, 3328, 3584
SM = 3840
NP = SM + LANES
N_SMALL = 32
N_GATE_ROWS = 16
N_ROW_KINDS = 3
GX_I, GX_B, GX_G, GX_BETA, GX_PM = 0, 256, 512, 768, 1024
GLA_LEVELS = (32, 16, 8, 4, 2)


def _cparams(sem):
    return pltpu.CompilerParams(dimension_semantics=sem, vmem_limit_bytes=VMEM_LIMIT)


def _log1pexp_negabs(x):
    return jnp.log(1.0 + jnp.exp(-jnp.abs(x)))


def _softplus(x):
    return jnp.maximum(x, 0.0) + _log1pexp_negabs(x)


def _logsigmoid(x):
    return -(jnp.maximum(-x, 0.0) + _log1pexp_negabs(x))


def _sigmoid(x):
    return 1.0 / (1.0 + jnp.exp(-x))


def _silu(x):
    return x * _sigmoid(x)


def _rms(x, g):
    return x * lax.rsqrt(jnp.mean(x * x, axis=-1, keepdims=True) + EPS) * g


def _dot(a, b):
    return jnp.dot(a, b, preferred_element_type=F32)


def _dot_nt(a, b):
    return lax.dot_general(a, b, (((1,), (1,)), ((), ())), preferred_element_type=F32)


def _dot_tn(a, b):
    return lax.dot_general(a, b, (((0,), (0,)), ((), ())), preferred_element_type=F32)


def _split2(x):
    x1 = x.astype(BF16)
    return x1, (x - x1.astype(F32)).astype(BF16)


def _split3(x):
    x1 = x.astype(BF16)
    r1 = x - x1.astype(F32)
    x2 = r1.astype(BF16)
    x3 = (r1 - x2.astype(F32)).astype(BF16)
    return x1, x2, x3


def _sel(m, x):
    x1, x2, x3 = _split3(x)
    return _dot(m, x1) + _dot(m, x2) + _dot(m, x3)


def _sel_r(x, m):
    n = x.shape[0]
    if n % 8:
        x1, x2, x3 = _split3(x)
        return _dot(x1, m) + _dot(x2, m) + _dot(x3, m)
    r = _dot(jnp.concatenate(_split3(x), axis=0), m)
    return r[0:n] + r[n:2 * n] + r[2 * n:3 * n]


def _sel_nt(m, x):
    x1, x2, x3 = _split3(x)
    return _dot_nt(m, x1) + _dot_nt(m, x2) + _dot_nt(m, x3)


def _mm3(x, w):
    n = x.shape[0]
    x1, x2 = _split2(x)
    w1, w2 = _split2(w)
    r = _dot(jnp.concatenate([x1, x2], axis=0), w1)
    return r[0:n] + r[n:] + _dot(x1, w2)


def _iota2(shape, dim):
    return lax.broadcasted_iota(jnp.int32, shape, dim)


def _inproj_kernel(x_ref, g_ref, w_ref, wst_ref, z_ref, zt_ref):
    hb = _rms(x_ref[...], g_ref[...]).astype(BF16)
    z_ref[...] = _dot(hb, w_ref[...])
    zt = _dot_nt(wst_ref[...], hb)
    n_blk, n_kind, n_chunks, _ = zt_ref.shape
    for b in range(n_blk):
        for k in range(n_kind):
            for c in range(n_chunks):
                t0 = (b * n_chunks + c) * CHUNK
                for h in range(N_HEADS):
                    zt_ref[b, k, c:c + 1, h * HEAD_DIM:(h + 1) * HEAD_DIM] = (
                        zt[k * 4 + h:k * 4 + h + 1, t0:t0 + CHUNK])


def _inproj(x, g, w, wst, tm, lb_rows):
    t = x.shape[0]
    n_chunks = lb_rows // CHUNK
    return pl.pallas_call(
        _inproj_kernel,
        grid=(t // tm,),
        in_specs=[pl.BlockSpec((tm, D_MODEL), lambda i: (i, 0)),
                  pl.BlockSpec((1, D_MODEL), lambda i: (0, 0)),
                  pl.BlockSpec((D_MODEL, NP), lambda i: (0, 0)),
                  pl.BlockSpec((N_GATE_ROWS, D_MODEL), lambda i: (0, 0))],
        out_specs=[pl.BlockSpec((tm, NP), lambda i: (i, 0)),
                   pl.BlockSpec((tm // lb_rows, N_ROW_KINDS, n_chunks, W_GROUP), lambda i: (i, 0, 0, 0))],
        out_shape=[jax.ShapeDtypeStruct((t, NP), F32),
                   jax.ShapeDtypeStruct((t // lb_rows, N_ROW_KINDS, n_chunks, W_GROUP), F32)],
        compiler_params=_cparams(("parallel",)),
        name="inproj",
    )(x, g, w, wst)


def _gate_act(pre, idx, neg_a):
    tail = _log1pexp_negabs(pre)
    lf = -(jnp.maximum(-pre, 0.0) + tail)
    dec = neg_a * (jnp.maximum(pre, 0.0) + tail)
    beta = _sigmoid(pre)
    return jnp.where(idx < 4, pre,
                     jnp.where(idx < 8, lf,
                               jnp.where(idx < 12, dec,
                                         jnp.where(idx < 16, beta, 0.0))))


def _mix_kernel(z_ref, zt_ref, bcol_ref, acol_ref, brow_ref, arow_ref, mnorm_ref, gnorm_ref,
                lnorm_ref, gcw_ref, scw_ref, wg_ref, bg_ref,
                bdmask_ref, bdtriu_ref, tril64_ref, selexp_ref, eye64_ref,
                y_ref, c_out, n_out, m_out, sg_out, gconv_out, sconv_out, sl_out,
                c_ref, n_ref, m_ref, sg_ref, slt_ref, xg_ref, xs_ref, qkv_ref, gcum_ref, glog_ref,
                gx_ref, rw_ref, *, lb_rows, nseq):
    lb = pl.program_id(1)
    nlb = pl.num_programs(1)
    n_chunks = lb_rows // CHUNK
    wd = W_GROUP
    seqs = range(nseq)

    @pl.when(lb == 0)
    def _():
        c_ref[...] = jnp.zeros_like(c_ref)
        n_ref[...] = jnp.zeros_like(n_ref)
        m_ref[...] = jnp.zeros_like(m_ref)
        sg_ref[...] = jnp.zeros_like(sg_ref)
        slt_ref[...] = jnp.zeros_like(slt_ref)
        for i in seqs:
            xg_ref[i, 0:8, :] = jnp.zeros((8, 3 * wd), F32)
            xs_ref[i, 0:8, :] = jnp.zeros((8, wd), F32)

    ti = _iota2((CHUNK, wd), 0)
    si = _iota2((CHUNK, wd), 1) % HEAD_DIM
    incl = ti >= si
    strict = ti > si
    eye = ti == si
    blk16 = (ti // 16) == (si // 16)
    blk32 = (ti // 32) == (si // 32)
    eye_f = eye.astype(F32)
    lvl_mask = {n: ((ti // (2 * n)) == (si // (2 * n))) & ((ti // n) > (si // n))
                for n in GLA_LEVELS + (1,)}
    bdmask = bdmask_ref[...]
    bdtriu = bdtriu_ref[...]
    tril64 = tril64_ref[...]
    head_of_lane = _iota2((1, wd), 1) // HEAD_DIM
    pos_of_lane = _iota2((8, wd), 1) % HEAD_DIM

    lane_half = [(_iota2((1, LANES), 1) // HEAD_DIM == j).astype(BF16) for j in (0, 1)]
    zero_tile = jnp.zeros((HEAD_DIM, LANES), BF16)

    def bd(x):
        xb = x.astype(BF16)
        blocks = []
        for h in range(N_HEADS):
            t = h // 2
            kept = xb[:, t * LANES:(t + 1) * LANES] * lane_half[h % 2]
            blocks.append(jnp.concatenate([kept, zero_tile] if t == 0 else [zero_tile, kept], axis=1))
        return jnp.concatenate(blocks, axis=0)

    def unbd(m):
        out = m[3 * HEAD_DIM:, :]
        for h in (2, 1, 0):
            out = jnp.where(head_of_lane == h, m[h * HEAD_DIM:(h + 1) * HEAD_DIM, :], out)
        return out

    def mm(a, bmat):
        return _dot(a.astype(BF16), bmat)

    def mm_nt(a, bmat):
        return _dot_nt(a.astype(BF16), bmat)

    def mm3_bd(a, b):
        a1, a2 = _split2(a)
        b1, b2 = _split2(b)
        r = _dot(jnp.concatenate([a1, a2], axis=0), bd(b1))
        return r[0:CHUNK] + r[CHUNK:] + _dot(a1, bd(b2))

    def hsum(x):
        x1, x2 = _split2(x)
        r = _dot(jnp.concatenate([x1, x2], axis=0), bdmask)
        return r[0:x.shape[0]] + r[x.shape[0]:]

    def headnorm(o, w):
        return o * lax.rsqrt(hsum(o * o) * (1.0 / HEAD_DIM) + EPS) * w

    lane = _iota2((1, LANES), 1)
    neg_a_col = -jnp.exp(acol_ref[...])
    neg_a_row = -jnp.exp(arow_ref[2:3, :])
    selexp = selexp_ref[...]
    eye64 = eye64_ref[...]

    def l2n(x):
        return x * lax.rsqrt(hsum(x * x) + EPS)

    rows_all = nseq * lb_rows
    merge = lambda x: x.reshape(rows_all, x.shape[-1])
    split = lambda x: x.reshape(nseq, lb_rows, x.shape[-1])
    small = merge(z_ref[:, :, SM:SM + LANES])
    act = _gate_act(small + bcol_ref[...], lane, neg_a_col)
    def chunk_cumsum(x):
        n_c, width = rows_all // CHUNK, x.shape[1]
        wide = jnp.concatenate([x[c * CHUNK:(c + 1) * CHUNK, :] for c in range(n_c)], axis=1)
        cs = _sel(tril64, wide)
        return jnp.concatenate([cs[:, c * width:(c + 1) * width] for c in range(n_c)], axis=0)

    csum = chunk_cumsum(act)
    comp = jnp.where((lane >= 4) & (lane < 12), csum, act)
    gx_ref[:, :, GX_I:GX_I + 4 * wd] = split(_sel_r(comp, selexp))
    glin = _dot(small.astype(BF16), wg_ref[...]) + bg_ref[...]
    glog = _logsigmoid(glin) * (1.0 / GLA_NORMALIZER)
    glog_ref[...] = split(glog)
    gcum_ref[...] = split(chunk_cumsum(glog))

    n_r = nseq * n_chunks
    trow = lambda k: jnp.concatenate([zt_ref[i, 0, k] for i in seqs], axis=0) + brow_ref[k:k + 1, :]
    ic_r = trow(0)
    lf_r = _logsigmoid(trow(1))
    dec_r = neg_a_row * _softplus(trow(2))
    cs_r = _sel_r(jnp.concatenate([lf_r, dec_r], axis=0), bdtriu)
    c_rows = ic_r - cs_r[0:n_r, :]
    g_rows = cs_r[n_r:, :]
    pm = jnp.concatenate([c_rows] * max(1, 8 // n_r), axis=0)
    for sh in (1, 2, 4, 8, 16, 32):
        pm = jnp.where(pos_of_lane[0:1, :] >= sh, jnp.maximum(pm, pltpu.roll(pm, sh, axis=1)), pm)
    for r in range(n_r):
        rw_ref[r // n_chunks, r % n_chunks, 0:1, :] = c_rows[r:r + 1, :]
        rw_ref[r // n_chunks, r % n_chunks, 1:2, :] = g_rows[r:r + 1, :]
    diag = jnp.concatenate(
        [jnp.where(eye, jnp.broadcast_to(pm[r:r + 1, :], (CHUNK, wd)), 0.0) for r in range(n_r)], axis=0)
    gx_ref[:, :, GX_PM:GX_PM + wd] = split(_sel_r(diag, bdmask))

    xg_ref[:, 8:, :] = z_ref[:, :, GQ:GQ + 3 * wd]
    conv = xg_ref[:, 5:5 + lb_rows, :] * gcw_ref[0:1, :]
    for j in range(1, 4):
        conv = conv + xg_ref[:, 5 + j:5 + j + lb_rows, :] * gcw_ref[j:j + 1, :]
    conv = merge(_silu(conv))
    qkv_ref[:, :, 0:wd] = split(l2n(conv[:, 0:wd]) * Q_SCALE)
    qkv_ref[:, :, wd:2 * wd] = split(l2n(conv[:, wd:2 * wd]))
    qkv_ref[:, :, 2 * wd:] = split(conv[:, 2 * wd:])

    xs_ref[:, 8:, :] = z_ref[:, :, SC:SC + wd] * z_ref[:, :, SH:SH + wd]
    cu = xs_ref[:, 6:6 + lb_rows, :] * scw_ref[0:1, :]
    for j in range(1, 3):
        cu = cu + xs_ref[:, 6 + j:6 + j + lb_rows, :] * scw_ref[j:j + 1, :]
    y_ref[:, :, 2 * wd:3 * wd] = (z_ref[:, :, SB:SB + wd] * cu).astype(y_ref.dtype)


    def chunk(c, carry):
        r0 = pl.multiple_of(c * CHUNK, CHUNK)
        rs = pl.ds(r0, CHUNK)
        each = lambda f, *xs: [f(*a) for a in zip(*xs)]
        zcol = lambda o: [z_ref[i, rs, o:o + wd] for i in seqs]
        gxcol = lambda o: [gx_ref[i, rs, o:o + wd] for i in seqs]
        last = lambda xs: [x[CHUNK - 1:CHUNK, :] for x in xs]
        tn_bd = lambda a, b: each(lambda x, y: unbd(_dot_tn(x.astype(BF16), y.astype(BF16))), a, b)
        mm3s = lambda a, b: each(mm3_bd, a, b)
        rw = [rw_ref[i, c, 0:2, :] for i in seqs]

        q = zcol(MQ)
        k = [x * Q_SCALE for x in zcol(MK)]
        v = zcol(MV)
        ic_e, b_e, pm_e = gxcol(GX_I), gxcol(GX_B), gxcol(GX_PM)
        m_old = [m_ref[i, 0:1, :] for i in seqs]
        cmat = [c_ref[i] for i in seqs]
        nrow = [n_ref[i, 0:1, :] for i in seqs]
        mx = each(jnp.maximum, m_old, pm_e)
        w_int = each(lambda a, b: jnp.exp(a - b), m_old, mx)
        p = each(lambda r, m: jnp.where(incl, jnp.exp(r[0:1, :] - m), 0.0), rw, mx)
        s = each(lambda a, b, c_: mm_nt(a, bd(b)) * c_, q, k, p)
        num = each(lambda w_, a, cm, s_, v_: w_ * mm(a, bd(cm)) + mm(s_, bd(v_)), w_int, q, cmat, s, v)
        den = each(lambda w_, a, n_, s_: hsum(w_ * (a * n_) + s_), w_int, q, nrow, s)
        hh = each(lambda n_, d_, b_, m: n_ / jnp.maximum(jnp.abs(d_), jnp.exp(-(b_ + m))), num, den, b_e, mx)
        mx_last = last(mx)
        wgk = each(lambda i_, b_, ml, k_: jnp.exp((i_ - b_) - ml) * k_, ic_e, b_e, mx_last, k)
        dec = each(lambda a, b: jnp.exp(a - b), m_old, mx_last)
        c_new = each(lambda d_, cm, u_: d_ * cm + u_, dec, cmat, tn_bd(wgk, v))
        for i in seqs:
            c_ref[i] = c_new[i]
            n_ref[i] = jnp.broadcast_to(dec[i] * nrow[i] + jnp.sum(wgk[i], axis=0, keepdims=True), (8, wd))
            m_ref[i] = jnp.broadcast_to(b_e[i][CHUNK - 1:CHUNK, :] + mx_last[i], (8, wd))
        y_a = each(lambda o_, h_: _sigmoid(o_) * headnorm(h_, mnorm_ref[...]), zcol(MO), hh)
        for i in seqs:
            y_ref[i, rs, 0:wd] = y_a[i].astype(y_ref.dtype)

        q = [qkv_ref[i, rs, 0:wd] for i in seqs]
        k = [qkv_ref[i, rs, wd:2 * wd] for i in seqs]
        v = [qkv_ref[i, rs, 2 * wd:] for i in seqs]
        g_e, beta = gxcol(GX_G), gxcol(GX_BETA)
        s_old = [sg_ref[i] for i in seqs]
        bdk = each(bd, k)
        lw = each(lambda g_, r: jnp.where(incl, jnp.exp(g_ - r[1:2, :]), 0.0), g_e, rw)
        amat = each(lambda b_, k_, bk, l_: jnp.where(strict, b_ * mm_nt(k_, bk) * l_, 0.0), beta, k, bdk, lw)
        egc = each(jnp.exp, g_e)
        ad = [jnp.where(blk16, a, 0.0) for a in amat]
        a2 = mm3s(ad, ad)
        a4 = mm3s(a2, a2)
        a8 = mm3s(a4, a4)
        tinv = [eye_f - a for a in ad]
        for apow in (a2, a4, a8):
            tinv = each(lambda t_, u_: t_ + u_, tinv, mm3s(tinv, apow))
        n1 = [jnp.where(blk32 & jnp.logical_not(blk16), a, 0.0) for a in amat]
        tinv = each(lambda t_, u_: t_ - u_, tinv, mm3s(mm3s(tinv, n1), tinv))
        n2 = [jnp.where(blk32, 0.0, a) for a in amat]
        tinv = each(lambda t_, u_: t_ - u_, tinv, mm3s(mm3s(tinv, n2), tinv))
        u = mm3s(tinv, each(lambda b_, v_: b_ * v_, beta, v))
        w = mm3s(tinv, each(lambda b_, e_, k_: (b_ * e_) * k_, beta, egc, k))
        bds = each(bd, s_old)
        vn = each(lambda u_, w_, bs: u_ - mm(w_, bs), u, w, bds)
        qk = each(lambda q_, bk, l_: mm_nt(q_, bk) * l_, q, bdk, lw)
        o = each(lambda q_, e_, bs, qk_, vn_: mm(q_ * e_, bs) + mm(qk_, bd(vn_)), q, egc, bds, qk, vn)
        g_last = last(g_e)
        kdec = each(lambda k_, gl, g_: k_ * jnp.exp(gl - g_), k, g_last, g_e)
        s_new = each(lambda gl, so, u_: jnp.exp(gl) * so + u_, g_last, s_old, tn_bd(kdec, vn))
        y_b = each(lambda o_, z_: headnorm(o_, gnorm_ref[...]) * _silu(z_), o, zcol(GZ))
        for i in seqs:
            sg_ref[i] = s_new[i]
            y_ref[i, rs, wd:2 * wd] = y_b[i].astype(y_ref.dtype)

        q = [x * Q_SCALE for x in zcol(LQ)]
        k = zcol(LK)
        v = zcol(LV)
        gk = [glog_ref[i, rs, :] for i in seqs]
        gcum = [gcum_ref[i, rs, :] for i in seqs]
        st_old = [slt_ref[i] for i in seqs]
        bdk = each(bd, k)
        amat = each(lambda q_, bk: jnp.where(eye, mm_nt(q_, bk), 0.0), q, bdk)
        amat = each(lambda a_, q_, g_, bk: a_ + jnp.where(lvl_mask[1], mm_nt(q_ * jnp.exp(g_), bk), 0.0),
                    amat, q, gk, bdk)
        g_end = gcum
        g_prev = [jnp.where(ti == 0, 0.0, pltpu.roll(g_, 1, axis=0)) for g_ in gcum]
        for n in GLA_LEVELS[::-1]:
            half = n // 2
            upper = (ti % n) >= half
            g_end = [jnp.where(upper, e_, pltpu.roll(e_, CHUNK - half, axis=0)) for e_ in g_end]
            g_prev = [jnp.where(upper, pltpu.roll(p_, half, axis=0), p_) for p_ in g_prev]
            qn = each(lambda q_, gc, gp: q_ * jnp.exp(gc - gp), q, gcum, g_prev)
            kn = each(lambda k_, ge, gc: k_ * jnp.exp(ge - gc), k, g_end, gcum)
            amat = each(lambda a_, q_, k_, m=lvl_mask[n]: a_ + jnp.where(m, mm_nt(q_, bd(k_)), 0.0),
                        amat, qn, kn)
        g_last = last(gcum)
        o = each(lambda q_, gc, st, a_, v_: mm_nt(q_ * jnp.exp(gc), bd(st)) + mm(a_, bd(v_)),
                 q, gcum, st_old, amat, v)
        kdec = each(lambda k_, gl, gc: k_ * jnp.exp(gl - gc), k, g_last, gcum)
        st_new = each(lambda st, gl, u_: st * jnp.exp(gl) + u_, st_old, g_last, tn_bd(v, kdec))
        y_d = each(lambda o_, z_: headnorm(o_, lnorm_ref[...]) * _silu(z_), o, zcol(LR))
        for i in seqs:
            slt_ref[i] = st_new[i]
            y_ref[i, rs, 3 * wd:] = y_d[i].astype(y_ref.dtype)
        return carry

    lax.fori_loop(0, n_chunks, chunk, 0)

    gtail = [xg_ref[i, lb_rows:lb_rows + 8, :] for i in seqs]
    stail = [xs_ref[i, lb_rows:lb_rows + 8, :] for i in seqs]
    for i in seqs:
        xg_ref[i, 0:8, :] = gtail[i]
        xs_ref[i, 0:8, :] = stail[i]

    @pl.when(lb == nlb - 1)
    def _():
        lane4 = _iota2((1, N_HEADS), 1)
        for i in seqs:
            gconv_out[i] = gtail[i][5:8, :]
            sconv_out[i] = stail[i][6:8, :]
            m_row = jnp.zeros((1, N_HEADS), F32)
            for h in range(N_HEADS):
                hs = slice(h * HEAD_DIM, (h + 1) * HEAD_DIM)
                c_out[i, h] = c_ref[i, :, hs]
                n_out[i, h:h + 1, :] = n_ref[i, 0:1, hs]
                m_row = jnp.where(lane4 == h, m_ref[i, 0:1, h * HEAD_DIM:h * HEAD_DIM + 1], m_row)
                sg_out[i, h] = sg_ref[i, :, hs]
                sl_out[i, h] = _sel_nt(eye64, slt_ref[i, :, hs])
            m_out[i] = m_row


def _mix_consts():
    wd = W_GROUP
    r, c = np.indices((wd, wd))
    same_head = (r // HEAD_DIM) == (c // HEAD_DIM)
    bdmask = same_head
    bdtriu = same_head & ((r % HEAD_DIM) <= (c % HEAD_DIM))
    tril64 = np.tril(np.ones((CHUNK, CHUNK), bool))
    rr, cc = np.indices((LANES, 4 * wd))
    selexp = rr == 4 * (cc // wd) + (cc % wd) // HEAD_DIM
    eye64 = np.eye(CHUNK, dtype=bool)
    return [jnp.asarray(m, dtype=BF16) for m in (bdmask, bdtriu, tril64, selexp, eye64)]


def _mix_prompt(z, zt, p, bsz, seq, lb_rows, nseq):
    nlb = seq // lb_rows
    n_chunks = lb_rows // CHUNK
    row = lambda b, l: (b, l, 0)
    const2 = lambda b, l: (0, 0)
    st4 = lambda b, l: (b, 0, 0, 0)
    st3 = lambda b, l: (b, 0, 0)
    params = [p["bcol"], p["acol"], p["brow"], p["arow"], p["mnorm"], p["gnorm"], p["lnorm"],
              p["gcw"], p["scw"], p["wg"], p["bg"]] + _mix_consts()
    in_specs = [pl.BlockSpec((nseq, lb_rows, NP), row),
                pl.BlockSpec((nseq, 1, N_ROW_KINDS, n_chunks, W_GROUP), lambda b, l: (b, l, 0, 0, 0))]
    in_specs += [pl.BlockSpec(a.shape, const2) for a in params]
    hd = (N_HEADS, HEAD_DIM, HEAD_DIM)
    out_shape = [jax.ShapeDtypeStruct((bsz, seq, D_MODEL), BF16),
                 jax.ShapeDtypeStruct((bsz,) + hd, F32),
                 jax.ShapeDtypeStruct((bsz, N_HEADS, HEAD_DIM), F32),
                 jax.ShapeDtypeStruct((bsz, 1, N_HEADS), F32),
                 jax.ShapeDtypeStruct((bsz,) + hd, F32),
                 jax.ShapeDtypeStruct((bsz, 3, 3 * W_GROUP), F32),
                 jax.ShapeDtypeStruct((bsz, 2, W_GROUP), F32),
                 jax.ShapeDtypeStruct((bsz,) + hd, F32)]
    out_specs = [pl.BlockSpec((nseq, lb_rows, D_MODEL), row),
                 pl.BlockSpec((nseq,) + hd, st4),
                 pl.BlockSpec((nseq, N_HEADS, HEAD_DIM), st3),
                 pl.BlockSpec((nseq, 1, N_HEADS), st3),
                 pl.BlockSpec((nseq,) + hd, st4),
                 pl.BlockSpec((nseq, 3, 3 * W_GROUP), st3),
                 pl.BlockSpec((nseq, 2, W_GROUP), st3),
                 pl.BlockSpec((nseq,) + hd, st4)]
    vm = lambda *shape: pltpu.VMEM((nseq,) + shape, F32)
    scratch = [vm(HEAD_DIM, W_GROUP),
               vm(8, W_GROUP),
               vm(8, W_GROUP),
               vm(HEAD_DIM, W_GROUP),
               vm(HEAD_DIM, W_GROUP),
               vm(lb_rows + 8, 3 * W_GROUP),
               vm(lb_rows + 8, W_GROUP),
               vm(lb_rows, 3 * W_GROUP),
               vm(lb_rows, W_GROUP),
               vm(lb_rows, W_GROUP),
               vm(lb_rows, 5 * W_GROUP),
               vm(n_chunks, 8, W_GROUP)]
    return pl.pallas_call(
        functools.partial(_mix_kernel, lb_rows=lb_rows, nseq=nseq),
        grid=(bsz // nseq, nlb),
        in_specs=in_specs,
        out_specs=out_specs,
        out_shape=out_shape,
        scratch_shapes=scratch,
        compiler_params=_cparams(("parallel", "arbitrary")),
        name="mix_prompt",
    )(z, zt, *params)


def _inproj_s_kernel(x_ref, g_ref, wt_ref, zt_ref):
    w = wt_ref[...]
    n = w.shape[0]
    w1, w2 = _split2(w)
    h1, h2 = _split2(_rms(x_ref[...], g_ref[...]))
    r = _dot_nt(jnp.concatenate([w1, w2], axis=0), h1)
    zt_ref[...] = r[0:n] + r[n:] + _dot_nt(w1, h2)


def _inproj_s(x, g, wt_all, layer, n_blk=4):
    bn = x.shape[0]
    d_in = wt_all.shape[1]
    tn = d_in // n_blk
    return pl.pallas_call(
        _inproj_s_kernel,
        grid=(n_blk,),
        in_specs=[pl.BlockSpec((bn, D_MODEL), lambda j: (0, 0)),
                  pl.BlockSpec((1, D_MODEL), lambda j: (0, 0)),
                  pl.BlockSpec((None, tn, D_MODEL), lambda j: (layer, j, 0))],
        out_specs=pl.BlockSpec((tn, bn), lambda j: (j, 0)),
        out_shape=jax.ShapeDtypeStruct((d_in, bn), F32),
        compiler_params=_cparams(("parallel",)),
        name="inproj_s",
    )(x, g, wt_all)


def _spre_kernel(zt_ref, gbuf_ref, sbuf_ref, bcol_ref, acol_ref, gcw_ref, scw_ref, wg_ref, bg_ref,
                 qkvt_ref, glogt_ref, actt_ref, yc_ref, gbuf_out, sbuf_out):
    rows_t = lambda col, n: zt_ref[col:col + n, :].T
    small = rows_t(SM, LANES)
    lane = _iota2((1, LANES), 1)
    actt_ref[...] = _gate_act(small + bcol_ref[...], lane, -jnp.exp(acol_ref[...])).T
    glin = _mm3(small, wg_ref[...]) + bg_ref[...]
    glogt_ref[...] = (_logsigmoid(glin) * (1.0 / GLA_NORMALIZER)).T

    wq = 3 * W_GROUP
    u = rows_t(GQ, wq)
    b0, b1, b2 = gbuf_ref[:, 0:wq], gbuf_ref[:, wq:2 * wq], gbuf_ref[:, 2 * wq:]
    conv = b0 * gcw_ref[0:1, :] + b1 * gcw_ref[1:2, :] + b2 * gcw_ref[2:3, :] + u * gcw_ref[3:4, :]
    conv = _silu(conv)
    gbuf_out[:, 0:wq] = b1
    gbuf_out[:, wq:2 * wq] = b2
    gbuf_out[:, 2 * wq:] = u
    hsum = (_iota2((W_GROUP, W_GROUP), 0) // HEAD_DIM ==
            _iota2((W_GROUP, W_GROUP), 1) // HEAD_DIM).astype(BF16)

    def l2n(x):
        return x * lax.rsqrt(_sel_r(x * x, hsum) + EPS)

    qkvt_ref[0:W_GROUP, :] = (l2n(conv[:, 0:W_GROUP]) * Q_SCALE).T
    qkvt_ref[W_GROUP:2 * W_GROUP, :] = l2n(conv[:, W_GROUP:2 * W_GROUP]).T
    qkvt_ref[2 * W_GROUP:, :] = conv[:, 2 * W_GROUP:].T

    u2 = rows_t(SC, W_GROUP) * rows_t(SH, W_GROUP)
    s0, s1 = sbuf_ref[:, 0:W_GROUP], sbuf_ref[:, W_GROUP:]
    cu = s0 * scw_ref[0:1, :] + s1 * scw_ref[1:2, :] + u2 * scw_ref[2:3, :]
    yc_ref[...] = rows_t(SB, W_GROUP) * cu
    sbuf_out[:, 0:W_GROUP] = s1
    sbuf_out[:, W_GROUP:] = u2


def _sample_pre(zt, gbuf, sbuf, p):
    bn = zt.shape[1]
    sd = lambda s: jax.ShapeDtypeStruct(s, F32)
    return pl.pallas_call(
        _spre_kernel,
        out_shape=[sd((3 * W_GROUP, bn)), sd((W_GROUP, bn)), sd((LANES, bn)),
                   sd((bn, W_GROUP)), sd(gbuf.shape), sd(sbuf.shape)],
        compiler_params=pltpu.CompilerParams(vmem_limit_bytes=VMEM_LIMIT),
        name="sample_pre",
    )(zt, gbuf, sbuf, p["bcol"], p["acol"], p["gcw"], p["scw"], p["wg32"], p["bg"])


def _srec_kernel(zt_ref, qkvt_ref, glogt_ref, actt_ref, c_ref, n_ref, m_ref, sg_ref, sl_ref, norm_ref,
                 y_ref, c_out, n_out, m_out, sg_out, sl_out):
    h = pl.program_id(0)
    feat = lambda ref, col: ref[pl.ds(pl.multiple_of(col + h * HEAD_DIM, HEAD_DIM), HEAD_DIM), :]
    gate = lambda kind: actt_ref[pl.ds(4 * kind + h, 1), :]
    colsum = lambda x: jnp.sum(x, axis=0, keepdims=True)
    rows = range(HEAD_DIM)

    def headnorm(o, w):
        return o * lax.rsqrt(jnp.mean(o * o, axis=0, keepdims=True) + EPS) * w

    def contract(vecs, s_ref):
        accs = [v[0:1, :] * s_ref[0] for v in vecs]
        for d in rows[1:]:
            sd_ = s_ref[d]
            accs = [a + v[d:d + 1, :] * sd_ for a, v in zip(accs, vecs)]
        return accs

    mq, mk, mv, mo = feat(zt_ref, MQ), feat(zt_ref, MK) * Q_SCALE, feat(zt_ref, MV), feat(zt_ref, MO)
    ic, lf = gate(0), gate(1)
    nvec = n_ref[...]
    m_old = m_ref[pl.ds(h, 1), :]
    a = lf + m_old
    mt = jnp.maximum(a, ic)
    w_int = jnp.exp(a - mt)
    e_i = jnp.exp(ic - mt)
    s = colsum(mq * mk) * e_i
    (qc,) = contract([mq], c_ref)
    num = w_int * qc + s * mv
    den = w_int * colsum(mq * nvec) + s
    hh = num / jnp.maximum(jnp.abs(den), jnp.exp(-mt))
    kw = e_i * mk
    for d in rows:
        c_out[d] = w_int * c_ref[d] + kw[d:d + 1, :] * mv
    n_out[...] = w_int * nvec + kw
    m_out[...] = mt
    y_ref[0:HEAD_DIM, :] = _sigmoid(mo) * headnorm(hh, norm_ref[0])

    gq, gk, gv = feat(qkvt_ref, 0), feat(qkvt_ref, W_GROUP), feat(qkvt_ref, 2 * W_GROUP)
    beta = gate(3)
    eg = jnp.exp(gate(2))
    ks, qs = contract([gk, gq], sg_ref)
    vn = beta * gv - (beta * eg) * ks
    o = eg * qs + colsum(gq * gk) * vn
    for d in rows:
        sg_out[d] = eg * sg_ref[d] + gk[d:d + 1, :] * vn
    y_ref[HEAD_DIM:2 * HEAD_DIM, :] = headnorm(o, norm_ref[1]) * _silu(feat(zt_ref, GZ))

    lq, lk, lv = feat(zt_ref, LQ) * Q_SCALE, feat(zt_ref, LK), feat(zt_ref, LV)
    egk = jnp.exp(feat(glogt_ref, 0))
    (ql,) = contract([lq * egk], sl_ref)
    o = ql + colsum(lq * lk) * lv
    for d in rows:
        sl_out[d] = egk[d:d + 1, :] * sl_ref[d] + lk[d:d + 1, :] * lv
    y_ref[2 * HEAD_DIM:, :] = headnorm(o, norm_ref[2]) * _silu(feat(zt_ref, LR))


def _sample_rec(layer, zt, qkvt, glogt, actt, cst, nst, mst, sgst, slst, norms):
    bn = zt.shape[1]
    mat_in = pl.BlockSpec((None, None, HEAD_DIM, HEAD_DIM, bn), lambda h: (layer, h, 0, 0, 0))
    mat_out = pl.BlockSpec((None, HEAD_DIM, HEAD_DIM, bn), lambda h: (h, 0, 0, 0))
    full = lambda a: pl.BlockSpec(a.shape, lambda h: (0,) * a.ndim)
    sd = lambda *s: jax.ShapeDtypeStruct(s, F32)
    return pl.pallas_call(
        _srec_kernel,
        grid=(N_HEADS,),
        in_specs=[full(zt), full(qkvt), full(glogt), full(actt), mat_in,
                  pl.BlockSpec((None, None, HEAD_DIM, bn), lambda h: (layer, h, 0, 0)),
                  pl.BlockSpec((None, N_HEADS, bn), lambda h: (layer, 0, 0)),
                  mat_in, mat_in,
                  pl.BlockSpec((3, HEAD_DIM, bn), lambda h: (0, h, 0))],
        out_specs=[pl.BlockSpec((None, 3 * HEAD_DIM, bn), lambda h: (h, 0, 0)), mat_out,
                   pl.BlockSpec((None, HEAD_DIM, bn), lambda h: (h, 0, 0)),
                   pl.BlockSpec((None, 1, bn), lambda h: (h, 0, 0)), mat_out, mat_out],
        out_shape=[sd(N_HEADS, 3 * HEAD_DIM, bn), sd(N_HEADS, HEAD_DIM, HEAD_DIM, bn),
                   sd(N_HEADS, HEAD_DIM, bn), sd(N_HEADS, 1, bn),
                   sd(N_HEADS, HEAD_DIM, HEAD_DIM, bn), sd(N_HEADS, HEAD_DIM, HEAD_DIM, bn)],
        compiler_params=_cparams(("parallel",)),
        name="sample_rec",
    )(zt, qkvt, glogt, actt, cst, nst, mst, sgst, slst, norms)


def _outproj_s_kernel(y_ref, w_ref, x_ref, o_ref):
    o_ref[...] = x_ref[...] + _mm3(y_ref[...], w_ref[...])


def _outproj_s(y, w_all, layer, x):
    full = lambda a: pl.BlockSpec(a.shape, lambda i: (0, 0))
    return pl.pallas_call(
        _outproj_s_kernel,
        grid=(1,),
        in_specs=[full(y), pl.BlockSpec((None, D_MODEL, D_MODEL), lambda i: (layer, 0, 0)), full(x)],
        out_specs=full(x),
        out_shape=jax.ShapeDtypeStruct(x.shape, F32),
        compiler_params=_cparams(("arbitrary",)),
        name="outproj_s",
    )(y, w_all, x)


def _ffn_s_kernel(x_ref, g_ref, wg_ref, wu_ref, wd_ref, gf_ref, o_ref, h_ref, acc_ref, *, final_norm):
    j = pl.program_id(0)

    @pl.when(j == 0)
    def _():
        h_ref[...] = _rms(x_ref[...], g_ref[...])
        acc_ref[...] = jnp.zeros_like(acc_ref)

    h = h_ref[...]
    acc_ref[...] += _mm3(_silu(_mm3(h, wg_ref[...])) * _mm3(h, wu_ref[...]), wd_ref[...])

    @pl.when(j == pl.num_programs(0) - 1)
    def _():
        out = x_ref[...] + acc_ref[...]
        o_ref[...] = _rms(out, gf_ref[...]) if final_norm else out


def _ffn_s(x, g, wg, wu, wd, gf, tf, final_norm):
    t = x.shape[0]
    dff = wg.shape[1]
    full = lambda j: (0, 0)
    return pl.pallas_call(
        functools.partial(_ffn_s_kernel, final_norm=final_norm),
        grid=(dff // tf,),
        in_specs=[pl.BlockSpec((t, D_MODEL), full),
                  pl.BlockSpec((1, D_MODEL), full),
                  pl.BlockSpec((D_MODEL, tf), lambda j: (0, j)),
                  pl.BlockSpec((D_MODEL, tf), lambda j: (0, j)),
                  pl.BlockSpec((tf, D_MODEL), lambda j: (j, 0)),
                  pl.BlockSpec((1, D_MODEL), full)],
        out_specs=pl.BlockSpec((t, D_MODEL), full),
        out_shape=jax.ShapeDtypeStruct((t, D_MODEL), F32),
        scratch_shapes=[pltpu.VMEM((t, D_MODEL), F32), pltpu.VMEM((t, D_MODEL), F32)],
        compiler_params=_cparams(("arbitrary",)),
        name="ffn_s",
    )(x, g, wg, wu, wd, gf)


def _outproj_kernel(y_ref, w_ref, x_ref, o_ref):
    o_ref[...] = x_ref[...] + _dot(y_ref[...], w_ref[...].astype(BF16))


def _outproj(y, w_all, layer, x, tm):
    t = x.shape[0]
    return pl.pallas_call(
        _outproj_kernel,
        grid=(t // tm,),
        in_specs=[pl.BlockSpec((tm, D_MODEL), lambda i: (i, 0)),
                  pl.BlockSpec((None, D_MODEL, D_MODEL), lambda i: (layer, 0, 0)),
                  pl.BlockSpec((tm, D_MODEL), lambda i: (i, 0))],
        out_specs=pl.BlockSpec((tm, D_MODEL), lambda i: (i, 0)),
        out_shape=jax.ShapeDtypeStruct((t, D_MODEL), F32),
        compiler_params=_cparams(("parallel",)),
        name="outproj",
    )(y, w_all, x)


def _ffn_kernel(x_ref, y_ref, wo_ref, g_ref, wg_ref, wu_ref, wd_ref, gf_ref, o_ref,
                h_ref, acc_ref, xn_ref, *, final_norm):
    j = pl.program_id(1)

    @pl.when(j == 0)
    def _():
        xn = x_ref[...] + _dot(y_ref[...], wo_ref[...].astype(BF16))
        xn_ref[...] = xn
        h_ref[...] = _rms(xn, g_ref[...]).astype(BF16)
        acc_ref[...] = jnp.zeros_like(acc_ref)

    h = h_ref[...]
    act = _silu(_dot(h, wg_ref[...].astype(BF16))) * _dot(h, wu_ref[...].astype(BF16))
    acc_ref[...] += _dot(act.astype(BF16), wd_ref[...].astype(BF16))

    @pl.when(j == pl.num_programs(1) - 1)
    def _():
        out = xn_ref[...] + acc_ref[...]
        o_ref[...] = _rms(out, gf_ref[...]) if final_norm else out


def _ffn(x, y, wo_all, layer, g, wg_all, wu_all, wd_all, dense_idx, gf, tm, tf, final_norm):
    t = x.shape[0]
    dff = wg_all.shape[2]
    return pl.pallas_call(
        functools.partial(_ffn_kernel, final_norm=final_norm),
        grid=(t // tm, dff // tf),
        in_specs=[pl.BlockSpec((tm, D_MODEL), lambda i, j: (i, 0)),
                  pl.BlockSpec((tm, D_MODEL), lambda i, j: (i, 0)),
                  pl.BlockSpec((None, D_MODEL, D_MODEL), lambda i, j: (layer, 0, 0)),
                  pl.BlockSpec((1, D_MODEL), lambda i, j: (0, 0)),
                  pl.BlockSpec((None, D_MODEL, tf), lambda i, j: (dense_idx, 0, j)),
                  pl.BlockSpec((None, D_MODEL, tf), lambda i, j: (dense_idx, 0, j)),
                  pl.BlockSpec((None, tf, D_MODEL), lambda i, j: (dense_idx, j, 0)),
                  pl.BlockSpec((1, D_MODEL), lambda i, j: (0, 0))],
        out_specs=pl.BlockSpec((tm, D_MODEL), lambda i, j: (i, 0)),
        out_shape=jax.ShapeDtypeStruct((t, D_MODEL), F32),
        scratch_shapes=[pltpu.VMEM((tm, D_MODEL), BF16), pltpu.VMEM((tm, D_MODEL), F32),
                        pltpu.VMEM((tm, D_MODEL), F32)],
        compiler_params=_cparams(("parallel", "arbitrary")),
        name="ffn",
    )(x, y, wo_all, g, wg_all, wu_all, wd_all, gf)


def _top2_gates(logits):
    return _top2_select(logits)[0]


def _top2_select(logits):
    lane = _iota2((1, LANES), 1)
    valid = lane < N_EXPERTS
    logits = jnp.where(valid, logits, -jnp.inf)
    ex = jnp.exp(logits - jnp.max(logits, axis=-1, keepdims=True))
    probs = ex / jnp.sum(ex, axis=-1, keepdims=True)
    v1 = jnp.max(probs, axis=-1, keepdims=True)
    i1 = jnp.min(jnp.where(probs == v1, lane, LANES), axis=-1, keepdims=True)
    rest = jnp.where((lane == i1) | jnp.logical_not(valid), -1.0, probs)
    v2 = jnp.max(rest, axis=-1, keepdims=True)
    i2 = jnp.min(jnp.where(rest == v2, lane, LANES), axis=-1, keepdims=True)
    tot = v1 + v2
    gates = jnp.where(lane == i1, v1 / tot, 0.0) + jnp.where(lane == i2, v2 / tot, 0.0)
    return gates, ((lane == i1) | (lane == i2)).astype(F32)


def _moe_kernel(x_ref, g_ref, wr_ref, br_ref, wg_ref, wu_ref, wd_ref, gf_ref, o_ref,
                h_ref, acc_ref, gates_ref, *, final_norm, precise_router):
    e = pl.program_id(1)

    @pl.when(e == 0)
    def _():
        h = _rms(x_ref[...], g_ref[...])
        h_ref[...] = h.astype(BF16)
        acc_ref[...] = jnp.zeros_like(acc_ref)
        if precise_router:
            logits = _mm3(h, wr_ref[...])
        else:
            logits = _dot(h.astype(BF16), wr_ref[...].astype(BF16))
        gates_ref[...] = _top2_gates(logits + br_ref[...])

    h = h_ref[...]
    y = _dot((_silu(_dot(h, wg_ref[0])) * _dot(h, wu_ref[0])).astype(BF16), wd_ref[0])
    lane = _iota2((1, LANES), 1)
    ge = jnp.sum(jnp.where(lane == e, gates_ref[...], 0.0), axis=-1, keepdims=True)
    acc_ref[...] += ge * y

    @pl.when(e == pl.num_programs(1) - 1)
    def _():
        out = x_ref[...] + acc_ref[...]
        o_ref[...] = _rms(out, gf_ref[...]) if final_norm else out


def _moe(x, g, wr, br, wg, wu, wd, gf, tm, final_norm, precise_router):
    t = x.shape[0]
    fe = wg.shape[2]
    return pl.pallas_call(
        functools.partial(_moe_kernel, final_norm=final_norm, precise_router=precise_router),
        grid=(t // tm, N_EXPERTS),
        in_specs=[pl.BlockSpec((tm, D_MODEL), lambda i, e: (i, 0)),
                  pl.BlockSpec((1, D_MODEL), lambda i, e: (0, 0)),
                  pl.BlockSpec((D_MODEL, LANES), lambda i, e: (0, 0)),
                  pl.BlockSpec((1, LANES), lambda i, e: (0, 0)),
                  pl.BlockSpec((1, D_MODEL, fe), lambda i, e: (e, 0, 0)),
                  pl.BlockSpec((1, D_MODEL, fe), lambda i, e: (e, 0, 0)),
                  pl.BlockSpec((1, fe, D_MODEL), lambda i, e: (e, 0, 0)),
                  pl.BlockSpec((1, D_MODEL), lambda i, e: (0, 0))],
        out_specs=pl.BlockSpec((tm, D_MODEL), lambda i, e: (i, 0)),
        out_shape=jax.ShapeDtypeStruct((t, D_MODEL), F32),
        scratch_shapes=[pltpu.VMEM((tm, D_MODEL), BF16), pltpu.VMEM((tm, D_MODEL), F32),
                        pltpu.VMEM((tm, LANES), F32)],
        compiler_params=_cparams(("parallel", "arbitrary")),
        name="moe",
    )(x, g, wr, br, wg, wu, wd, gf)


def _moe_win_kernel(x_ref, g_ref, wr_ref, br_ref, tril_ref, wg_ref, wu_ref, wd_ref, gf_ref, o_ref,
                    h_ref, gate_ref, key_ref, keyt_ref, *, final_norm, cap, tc):
    e = pl.program_id(1)
    w_rows = x_ref.shape[0]
    lane = _iota2((1, LANES), 1)

    @pl.when(e == 0)
    def _():
        x = x_ref[...]
        hb = _rms(x, g_ref[...]).astype(BF16)
        h_ref[...] = hb
        o_ref[...] = x
        gates, sel = _top2_select(_dot(hb, wr_ref[...].astype(BF16)) + br_ref[...])
        rank = _dot(tril_ref[...], sel.astype(BF16))
        key = jnp.where(sel > 0.0, rank, -1.0)
        gate_ref[...] = gates
        key_ref[...] = key
        keyt_ref[...] = key.T

    key_r = keyt_ref[pl.ds(e, 1), :]
    count = jnp.sum((key_r >= 0.0).astype(F32), axis=-1, keepdims=True)[0, 0].astype(jnp.int32)
    pick_e = (_iota2((LANES, LANES), 0) == e).astype(BF16)
    key_c = _sel_r(key_ref[...], pick_e)
    key_c = jnp.concatenate([key_c] * pl.cdiv(2 * cap, LANES), axis=1)[:, :2 * cap]
    gate_c = _sel_r(gate_ref[...], pick_e)
    gate_c = jnp.concatenate([gate_c] * (tc // LANES), axis=1)
    row_id = _iota2((cap, w_rows), 0).astype(F32)
    col_id = (_iota2((w_rows, 2 * cap), 1) % cap).astype(F32)

    def trip(j, carry):
        base = (j * cap).astype(F32)
        gather = jnp.where(row_id + base == key_r, 1.0, 0.0).astype(BF16)
        xg = _dot(gather, h_ref[...]).astype(BF16)
        act = (_silu(_dot(xg, wg_ref[0])) * _dot(xg, wu_ref[0])).astype(BF16)
        y12 = jnp.concatenate(_split2(_dot(act, wd_ref[0])), axis=0)
        scatter = jnp.where(col_id + base == key_c, 1.0, 0.0).astype(BF16)
        for c0 in range(0, D_MODEL, tc):
            cs = slice(c0, c0 + tc)
            o_ref[:, cs] += gate_c * _dot(scatter, y12[:, cs])
        return carry

    lax.fori_loop(0, (count + cap - 1) // cap, trip, 0)

    if final_norm:
        @pl.when(e == pl.num_programs(1) - 1)
        def _():
            o_ref[...] = _rms(o_ref[...], gf_ref[...])


def _moe_win(x, g, wr, br, wg, wu, wd, gf, w_rows, final_norm):
    t = x.shape[0]
    fe = wg.shape[2]
    cap = max(16, int(w_rows / 4 + 3.5 * (w_rows * 3 / 16) ** 0.5) // 16 * 16)
    tc = min(w_rows, 256)
    tril = jnp.asarray(np.tril(np.ones((w_rows, w_rows), np.float32), -1), dtype=BF16)
    const = lambda i, e: (0, 0)
    return pl.pallas_call(
        functools.partial(_moe_win_kernel, final_norm=final_norm, cap=cap, tc=tc),
        grid=(t // w_rows, N_EXPERTS),
        in_specs=[pl.BlockSpec((w_rows, D_MODEL), lambda i, e: (i, 0)),
                  pl.BlockSpec((1, D_MODEL), const),
                  pl.BlockSpec((D_MODEL, LANES), const),
                  pl.BlockSpec((1, LANES), const),
                  pl.BlockSpec((w_rows, w_rows), const),
                  pl.BlockSpec((1, D_MODEL, fe), lambda i, e: (e, 0, 0)),
                  pl.BlockSpec((1, D_MODEL, fe), lambda i, e: (e, 0, 0)),
                  pl.BlockSpec((1, fe, D_MODEL), lambda i, e: (e, 0, 0)),
                  pl.BlockSpec((1, D_MODEL), const)],
        out_specs=pl.BlockSpec((w_rows, D_MODEL), lambda i, e: (i, 0)),
        out_shape=jax.ShapeDtypeStruct((t, D_MODEL), F32),
        scratch_shapes=[pltpu.VMEM((w_rows, D_MODEL), BF16),
                        pltpu.VMEM((w_rows, LANES), F32),
                        pltpu.VMEM((w_rows, LANES), F32), pltpu.VMEM((LANES, w_rows), F32)],
        compiler_params=_cparams(("parallel", "arbitrary")),
        name="moe_win",
    )(x, g, wr, br, tril, wg, wu, wd, gf)


def _prep_w_in(w):
    seg = lambda i: w[:, _OFF[i]:_OFF[i + 1]]
    small = jnp.concatenate([seg(i) for i in _SMALL], axis=1)
    pad = jnp.zeros((w.shape[0], LANES - N_SMALL), w.dtype)
    wfull = jnp.concatenate([seg(i) for i in _BIG] + [small, pad], axis=1).astype(BF16)
    gates = jnp.concatenate([seg(i) for i in _GATES], axis=1)
    return wfull, gates.T.astype(BF16)


def _regroup_zt(zt):
    seg = lambda i: zt[_OFF[i]:_OFF[i + 1], :]
    pad = jnp.zeros((LANES - N_SMALL, zt.shape[1]), zt.dtype)
    return jnp.concatenate([seg(i) for i in _BIG] + [seg(i) for i in _SMALL] + [pad], axis=0)


def _pad_lanes(v, start):
    out = jnp.zeros((1, LANES), F32)
    return lax.dynamic_update_slice(out, v.reshape(1, -1).astype(F32), (0, start))


def _layer_params(l, mlstm_b_i, mlstm_b_f, mlstm_norm, gdn_conv_w, gdn_a_log, gdn_dt_bias, gdn_norm,
                  sc_conv_w, gla_w_gate, gla_b_gate, gla_norm):
    bias = jnp.concatenate([mlstm_b_i[l], mlstm_b_f[l], gdn_dt_bias[l]]).astype(F32)
    alog = gdn_a_log[l].astype(F32)
    per_lane = lambda v: jnp.repeat(v, HEAD_DIM)
    zero = jnp.zeros((W_GROUP,), F32)
    brow = jnp.stack([per_lane(mlstm_b_i[l]), per_lane(mlstm_b_f[l]), per_lane(gdn_dt_bias[l]), zero])
    arow = jnp.stack([zero, zero, per_lane(alog), zero])
    wg32 = jnp.zeros((LANES, W_GROUP), F32).at[16:16 + GLA_RANK].set(gla_w_gate[l])
    return dict(bcol=_pad_lanes(bias, 0), acol=_pad_lanes(alog, 8), brow=brow, arow=arow,
                mnorm=mlstm_norm[l].reshape(1, -1), gnorm=gdn_norm[l].reshape(1, -1),
                lnorm=gla_norm[l].reshape(1, -1),
                gcw=gdn_conv_w[l].T, scw=sc_conv_w[l].T, wg=wg32.astype(BF16), wg32=wg32,
                bg=gla_b_gate[l].reshape(1, -1))


def _pick_tile(n, pref):
    for c in pref:
        if n % c == 0:
            return c
    return n


def kernel(x_prompt, x_sample, state_mlstm_C, state_mlstm_n, state_mlstm_m, state_gdn_S, state_gdn_conv,
           state_sc_conv, state_gla_S, w_in, g_mix, mlstm_b_i, mlstm_b_f, mlstm_norm, gdn_conv_w, gdn_a_log,
           gdn_dt_bias, gdn_norm, sc_conv_w, gla_w_gate, gla_b_gate, gla_norm, w_out, g_ffn, ffn_w_gate,
           ffn_w_up, ffn_w_down, moe_w_router, moe_b_router, moe_w_gate, moe_w_up, moe_w_down, g_final):
    depth = w_in.shape[0]
    bsz, seq, _ = x_prompt.shape
    bn = x_sample.shape[0]
    tp = bsz * seq
    assert x_sample.shape[1] == 1 and seq % CHUNK == 0

    xp = x_prompt.reshape(tp, D_MODEL)
    xs = x_sample.reshape(bn, D_MODEL)
    lb_rows = _pick_tile(seq, (128, 64))
    tm_p = _pick_tile(tp, (512, 256, 128, 64))
    tm_f = _pick_tile(tp, (1024, 512, 256, 128, 64))
    tm_e = _pick_tile(tp, (1024, 512, 256, 128))
    nseq = _pick_tile(bsz, (4, 2, 1))

    s_minor = (state_mlstm_C.transpose(0, 2, 3, 4, 1), state_mlstm_n.transpose(0, 2, 3, 1),
               state_mlstm_m.transpose(0, 2, 1), state_gdn_S.transpose(0, 2, 3, 4, 1),
               state_gla_S.transpose(0, 2, 3, 4, 1))
    w_in_t = w_in.transpose(0, 2, 1)
    gfin = g_final.reshape(1, -1)

    p_states = [[] for _ in range(7)]
    s_states = [[] for _ in range(7)]
    for l in range(depth):
        p = _layer_params(l, mlstm_b_i, mlstm_b_f, mlstm_norm, gdn_conv_w, gdn_a_log, gdn_dt_bias,
                          gdn_norm, sc_conv_w, gla_w_gate, gla_b_gate, gla_norm)
        wfull, wst = _prep_w_in(w_in[l])
        gm = g_mix[l].reshape(1, -1)
        gf = g_ffn[l].reshape(1, -1)
        last = l == depth - 1

        z, zt = _inproj(xp, gm, wfull, wst, tm_p, lb_rows)
        y, c1, n1, m1, sg1, gb1, sb1, sl1 = _mix_prompt(
            z.reshape(bsz, seq, NP), zt.reshape((bsz, seq // lb_rows) + zt.shape[1:]), p,
            bsz, seq, lb_rows, nseq)
        y = y.reshape(tp, D_MODEL)
        if l % 2:
            xp = _outproj(y, w_out, l, xp, tm_f)
        for lst, v in zip(p_states, (c1, n1, m1.reshape(bsz, N_HEADS), sg1, gb1, sb1, sl1)):
            lst.append(v)

        zt_s = _regroup_zt(_inproj_s(xs, gm, w_in_t, l))
        gbuf = state_gdn_conv[l].reshape(bn, -1)
        sbuf = state_sc_conv[l].reshape(bn, -1)
        qkvt, glogt, actt, yc, gbuf1, sbuf1 = _sample_pre(zt_s, gbuf, sbuf, p)
        norms = jnp.stack([jnp.broadcast_to(v[:, None], (W_GROUP, bn))
                           for v in (mlstm_norm[l], gdn_norm[l], gla_norm[l])])
        ysr, c2, n2, m2, sg2, sl2 = _sample_rec(l, zt_s, qkvt, glogt, actt, *s_minor, norms)
        yt = ysr.reshape(N_HEADS, 3, HEAD_DIM, bn).transpose(3, 1, 0, 2).reshape(bn, 3, W_GROUP)
        ymix = jnp.concatenate([yt[:, 0], yt[:, 1], yc, yt[:, 2]], axis=-1)
        xs = _outproj_s(ymix, w_out, l, xs)
        for lst, v in zip(s_states, (c2, n2, m2[:, 0], sg2, gbuf1.reshape(bn, 3, -1),
                                     sbuf1.reshape(bn, 2, -1), sl2)):
            lst.append(v)

        j = l // 2
        if l % 2 == 0:
            tf = _pick_tile(ffn_w_gate.shape[2], (256, 128))
            xp = _ffn(xp, y, w_out, l, gf, ffn_w_gate, ffn_w_up, ffn_w_down, j, gfin, tm_f, tf, last)
            xs = _ffn_s(xs, gf, ffn_w_gate[j], ffn_w_up[j], ffn_w_down[j], gfin, tf, last)
        else:
            wr = jnp.zeros((D_MODEL, LANES), F32).at[:, :N_EXPERTS].set(moe_w_router[j])
            br = _pad_lanes(moe_b_router[j], 0)
            wgt, wup, wdn = (moe_w_gate[j].astype(BF16), moe_w_up[j].astype(BF16),
                             moe_w_down[j].astype(BF16))
            xp = _moe_win(xp, gf, wr, br, wgt, wup, wdn, gfin, tm_e, last)
            xs = _moe(xs, gf, wr, br, wgt, wup, wdn, gfin, bn, last, True)

    y_prompt = xp.reshape(bsz, seq, D_MODEL)
    y_sample = xs.reshape(bn, 1, D_MODEL)
    sp = [jnp.stack(v) for v in p_states]
    ss = [jnp.stack(v) for v in s_states]
    for i, perm in ((0, (0, 4, 1, 2, 3)), (1, (0, 3, 1, 2)), (2, (0, 2, 1)), (3, (0, 4, 1, 2, 3)),
                    (6, (0, 4, 1, 2, 3))):
        ss[i] = ss[i].transpose(perm)
    return (y_prompt, y_sample, sp[0], ss[0], sp[1], ss[1], sp[2], ss[2], sp[3], ss[3],
            sp[4], ss[4], sp[5], ss[5], sp[6], ss[6])
```

```python
import functools

import numpy as np
import jax
import jax.numpy as jnp
from jax import lax
from jax.experimental import pallas as pl
from jax.experimental.pallas import tpu as pltpu

F32 = jnp.float32
BF16 = jnp.bfloat16

D_MODEL = 1024
W_GROUP = 256
N_HEADS = 4
HEAD_DIM = 64
CHUNK = 64
GLA_RANK = 16
GLA_NORMALIZER = 16.0
N_EXPERTS = 8
EPS = 1e-6
Q_SCALE = HEAD_DIM ** -0.5

VMEM_LIMIT = 56 * 1024 * 1024
LANES = 128

SPLIT_SIZES = ([W_GROUP] * 4 + [N_HEADS] * 2 + [W_GROUP] * 4 + [N_HEADS] * 2 +
               [W_GROUP] * 3 + [W_GROUP] * 4 + [GLA_RANK])
_OFF = np.concatenate([[0], np.cumsum(SPLIT_SIZES)])
_BIG = [0, 1, 2, 3, 6, 7, 8, 9, 12, 13, 14, 15, 16, 17, 18]
_SMALL = [4, 5, 10, 11, 19]
_GATES = [4, 5, 10, 11]
MQ, MK, MV, MO = 0, 256, 512, 768
GQ, GK, GV, GZ = 1024, 1280, 1536, 1792
SB, SC, SH = 2048, 2304, 2560
LQ, LK, LV, LR = 2816, 3072, 3328, 3584
SM = 3840
NP = SM + LANES
N_SMALL = 32
N_GATE_ROWS = 16
N_ROW_KINDS = 3
GX_I, GX_B, GX_G, GX_BETA, GX_PM = 0, 256, 512, 768, 1024
GLA_LEVELS = (32, 16, 8, 4, 2)


def _cparams(sem):
    return pltpu.CompilerParams(dimension_semantics=sem, vmem_limit_bytes=VMEM_LIMIT)


def _log1pexp_negabs(x):
    return jnp.log(1.0 + jnp.exp(-jnp.abs(x)))


def _softplus(x):
    return jnp.maximum(x, 0.0) + _log1pexp_negabs(x)


def _logsigmoid(x):
    return -(jnp.maximum(-x, 0.0) + _log1pexp_negabs(x))


def _sigmoid(x):
    return 1.0 / (1.0 + jnp.exp(-x))


def _silu(x):
    return x * _sigmoid(x)


def _rms(x, g):
    return x * lax.rsqrt(jnp.mean(x * x, axis=-1, keepdims=True) + EPS) * g


def _dot(a, b):
    return jnp.dot(a, b, preferred_element_type=F32)


def _dot_nt(a, b):
    return lax.dot_general(a, b, (((1,), (1,)), ((), ())), preferred_element_type=F32)


def _dot_tn(a, b):
    return lax.dot_general(a, b, (((0,), (0,)), ((), ())), preferred_element_type=F32)


def _split2(x):
    x1 = x.astype(BF16)
    return x1, (x - x1.astype(F32)).astype(BF16)


def _split3(x):
    x1 = x.astype(BF16)
    r1 = x - x1.astype(F32)
    x2 = r1.astype(BF16)
    x3 = (r1 - x2.astype(F32)).astype(BF16)
    return x1, x2, x3


def _sel(m, x):
    x1, x2, x3 = _split3(x)
    return _dot(m, x1) + _dot(m, x2) + _dot(m, x3)


def _sel_r(x, m):
    n = x.shape[0]
    if n % 8:
        x1, x2, x3 = _split3(x)
        return _dot(x1, m) + _dot(x2, m) + _dot(x3, m)
    r = _dot(jnp.concatenate(_split3(x), axis=0), m)
    return r[0:n] + r[n:2 * n] + r[2 * n:3 * n]


def _sel_nt(m, x):
    x1, x2, x3 = _split3(x)
    return _dot_nt(m, x1) + _dot_nt(m, x2) + _dot_nt(m, x3)


def _mm3(x, w):
    n = x.shape[0]
    x1, x2 = _split2(x)
    w1, w2 = _split2(w)
    r = _dot(jnp.concatenate([x1, x2], axis=0), w1)
    return r[0:n] + r[n:] + _dot(x1, w2)


def _iota2(shape, dim):
    return lax.broadcasted_iota(jnp.int32, shape, dim)


def _inproj_kernel(x_ref, g_ref, w_ref, wst_ref, z_ref, zt_ref):
    hb = _rms(x_ref[...], g_ref[...]).astype(BF16)
    z_ref[...] = _dot(hb, w_ref[...])
    zt = _dot_nt(wst_ref[...], hb)
    n_blk, n_kind, n_chunks, _ = zt_ref.shape
    for b in range(n_blk):
        for k in range(n_kind):
            for c in range(n_chunks):
                t0 = (b * n_chunks + c) * CHUNK
                for h in range(N_HEADS):
                    zt_ref[b, k, c:c + 1, h * HEAD_DIM:(h + 1) * HEAD_DIM] = (
                        zt[k * 4 + h:k * 4 + h + 1, t0:t0 + CHUNK])


def _inproj(x, g, w, wst, tm, lb_rows):
    t = x.shape[0]
    n_chunks = lb_rows // CHUNK
    return pl.pallas_call(
        _inproj_kernel,
        grid=(t // tm,),
        in_specs=[pl.BlockSpec((tm, D_MODEL), lambda i: (i, 0)),
                  pl.BlockSpec((1, D_MODEL), lambda i: (0, 0)),
                  pl.BlockSpec((D_MODEL, NP), lambda i: (0, 0)),
                  pl.BlockSpec((N_GATE_ROWS, D_MODEL), lambda i: (0, 0))],
        out_specs=[pl.BlockSpec((tm, NP), lambda i: (i, 0)),
                   pl.BlockSpec((tm // lb_rows, N_ROW_KINDS, n_chunks, W_GROUP), lambda i: (i, 0, 0, 0))],
        out_shape=[jax.ShapeDtypeStruct((t, NP), F32),
                   jax.ShapeDtypeStruct((t // lb_rows, N_ROW_KINDS, n_chunks, W_GROUP), F32)],
        compiler_params=_cparams(("parallel",)),
        name="inproj",
    )(x, g, w, wst)


def _gate_act(pre, idx, neg_a):
    tail = _log1pexp_negabs(pre)
    lf = -(jnp.maximum(-pre, 0.0) + tail)
    dec = neg_a * (jnp.maximum(pre, 0.0) + tail)
    beta = _sigmoid(pre)
    return jnp.where(idx < 4, pre,
                     jnp.where(idx < 8, lf,
                               jnp.where(idx < 12, dec,
                                         jnp.where(idx < 16, beta, 0.0))))


def _mix_kernel(z_ref, zt_ref, bcol_ref, acol_ref, brow_ref, arow_ref, mnorm_ref, gnorm_ref,
                lnorm_ref, gcw_ref, scw_ref, wg_ref, bg_ref,
                bdmask_ref, bdtriu_ref, tril64_ref, selexp_ref, eye64_ref,
                y_ref, c_out, n_out, m_out, sg_out, gconv_out, sconv_out, sl_out,
                c_ref, n_ref, m_ref, sg_ref, slt_ref, xg_ref, xs_ref, qkv_ref, gcum_ref, glog_ref,
                gx_ref, rw_ref, *, lb_rows, nseq):
    lb = pl.program_id(1)
    nlb = pl.num_programs(1)
    n_chunks = lb_rows // CHUNK
    wd = W_GROUP
    seqs = range(nseq)

    @pl.when(lb == 0)
    def _():
        c_ref[...] = jnp.zeros_like(c_ref)
        n_ref[...] = jnp.zeros_like(n_ref)
        m_ref[...] = jnp.zeros_like(m_ref)
        sg_ref[...] = jnp.zeros_like(sg_ref)
        slt_ref[...] = jnp.zeros_like(slt_ref)
        for i in seqs:
            xg_ref[i, 0:8, :] = jnp.zeros((8, 3 * wd), F32)
            xs_ref[i, 0:8, :] = jnp.zeros((8, wd), F32)

    ti = _iota2((CHUNK, wd), 0)
    si = _iota2((CHUNK, wd), 1) % HEAD_DIM
    incl = ti >= si
    strict = ti > si
    eye = ti == si
    blk16 = (ti // 16) == (si // 16)
    blk32 = (ti // 32) == (si // 32)
    eye_f = eye.astype(F32)
    lvl_mask = {n: ((ti // (2 * n)) == (si // (2 * n))) & ((ti // n) > (si // n))
                for n in GLA_LEVELS + (1,)}
    bdmask = bdmask_ref[...]
    bdtriu = bdtriu_ref[...]
    tril64 = tril64_ref[...]
    head_of_lane = _iota2((1, wd), 1) // HEAD_DIM
    pos_of_lane = _iota2((8, wd), 1) % HEAD_DIM

    lane_half = [(_iota2((1, LANES), 1) // HEAD_DIM == j).astype(BF16) for j in (0, 1)]
    zero_tile = jnp.zeros((HEAD_DIM, LANES), BF16)

    def bd(x):
        xb = x.astype(BF16)
        blocks = []
        for h in range(N_HEADS):
            t = h // 2
            kept = xb[:, t * LANES:(t + 1) * LANES] * lane_half[h % 2]
            blocks.append(jnp.concatenate([kept, zero_tile] if t == 0 else [zero_tile, kept], axis=1))
        return jnp.concatenate(blocks, axis=0)

    def unbd(m):
        out = m[3 * HEAD_DIM:, :]
        for h in (2, 1, 0):
            out = jnp.where(head_of_lane == h, m[h * HEAD_DIM:(h + 1) * HEAD_DIM, :], out)
        return out

    def mm(a, bmat):
        return _dot(a.astype(BF16), bmat)

    def mm_nt(a, bmat):
        return _dot_nt(a.astype(BF16), bmat)

    def mm3_bd(a, b):
        a1, a2 = _split2(a)
        b1, b2 = _split2(b)
        r = _dot(jnp.concatenate([a1, a2], axis=0), bd(b1))
        return r[0:CHUNK] + r[CHUNK:] + _dot(a1, bd(b2))

    def hsum(x):
        x1, x2 = _split2(x)
        r = _dot(jnp.concatenate([x1, x2], axis=0), bdmask)
        return r[0:x.shape[0]] + r[x.shape[0]:]

    def headnorm(o, w):
        return o * lax.rsqrt(hsum(o * o) * (1.0 / HEAD_DIM) + EPS) * w

    lane = _iota2((1, LANES), 1)
    neg_a_col = -jnp.exp(acol_ref[...])
    neg_a_row = -jnp.exp(arow_ref[2:3, :])
    selexp = selexp_ref[...]
    eye64 = eye64_ref[...]

    def l2n(x):
        return x * lax.rsqrt(hsum(x * x) + EPS)

    rows_all = nseq * lb_rows
    merge = lambda x: x.reshape(rows_all, x.shape[-1])
    split = lambda x: x.reshape(nseq, lb_rows, x.shape[-1])
    small = merge(z_ref[:, :, SM:SM + LANES])
    act = _gate_act(small + bcol_ref[...], lane, neg_a_col)
    def chunk_cumsum(x):
        n_c, width = rows_all // CHUNK, x.shape[1]
        wide = jnp.concatenate([x[c * CHUNK:(c + 1) * CHUNK, :] for c in range(n_c)], axis=1)
        cs = _sel(tril64, wide)
        return jnp.concatenate([cs[:, c * width:(c + 1) * width] for c in range(n_c)], axis=0)

    csum = chunk_cumsum(act)
    comp = jnp.where((lane >= 4) & (lane < 12), csum, act)
    gx_ref[:, :, GX_I:GX_I + 4 * wd] = split(_sel_r(comp, selexp))
    glin = _dot(small.astype(BF16), wg_ref[...]) + bg_ref[...]
    glog = _logsigmoid(glin) * (1.0 / GLA_NORMALIZER)
    glog_ref[...] = split(glog)
    gcum_ref[...] = split(chunk_cumsum(glog))

    n_r = nseq * n_chunks
    trow = lambda k: jnp.concatenate([zt_ref[i, 0, k] for i in seqs], axis=0) + brow_ref[k:k + 1, :]
    ic_r = trow(0)
    lf_r = _logsigmoid(trow(1))
    dec_r = neg_a_row * _softplus(trow(2))
    cs_r = _sel_r(jnp.concatenate([lf_r, dec_r], axis=0), bdtriu)
    c_rows = ic_r - cs_r[0:n_r, :]
    g_rows = cs_r[n_r:, :]
    pm = jnp.concatenate([c_rows] * max(1, 8 // n_r), axis=0)
    for sh in (1, 2, 4, 8, 16, 32):
        pm = jnp.where(pos_of_lane[0:1, :] >= sh, jnp.maximum(pm, pltpu.roll(pm, sh, axis=1)), pm)
    for r in range(n_r):
        rw_ref[r // n_chunks, r % n_chunks, 0:1, :] = c_rows[r:r + 1, :]
        rw_ref[r // n_chunks, r % n_chunks, 1:2, :] = g_rows[r:r + 1, :]
    diag = jnp.concatenate(
        [jnp.where(eye, jnp.broadcast_to(pm[r:r + 1, :], (CHUNK, wd)), 0.0) for r in range(n_r)], axis=0)
    gx_ref[:, :, GX_PM:GX_PM + wd] = split(_sel_r(diag, bdmask))

    xg_ref[:, 8:, :] = z_ref[:, :, GQ:GQ + 3 * wd]
    conv = xg_ref[:, 5:5 + lb_rows, :] * gcw_ref[0:1, :]
    for j in range(1, 4):
        conv = conv + xg_ref[:, 5 + j:5 + j + lb_rows, :] * gcw_ref[j:j + 1, :]
    conv = merge(_silu(conv))
    qkv_ref[:, :, 0:wd] = split(l2n(conv[:, 0:wd]) * Q_SCALE)
    qkv_ref[:, :, wd:2 * wd] = split(l2n(conv[:, wd:2 * wd]))
    qkv_ref[:, :, 2 * wd:] = split(conv[:, 2 * wd:])

    xs_ref[:, 8:, :] = z_ref[:, :, SC:SC + wd] * z_ref[:, :, SH:SH + wd]
    cu = xs_ref[:, 6:6 + lb_rows, :] * scw_ref[0:1, :]
    for j in range(1, 3):
        cu = cu + xs_ref[:, 6 + j:6 + j + lb_rows, :] * scw_ref[j:j + 1, :]
    y_ref[:, :, 2 * wd:3 * wd] = (z_ref[:, :, SB:SB + wd] * cu).astype(y_ref.dtype)


    def chunk(c, carry):
        r0 = pl.multiple_of(c * CHUNK, CHUNK)
        rs = pl.ds(r0, CHUNK)
        each = lambda f, *xs: [f(*a) for a in zip(*xs)]
        zcol = lambda o: [z_ref[i, rs, o:o + wd] for i in seqs]
        gxcol = lambda o: [gx_ref[i, rs, o:o + wd] for i in seqs]
        last = lambda xs: [x[CHUNK - 1:CHUNK, :] for x in xs]
        tn_bd = lambda a, b: each(lambda x, y: unbd(_dot_tn(x.astype(BF16), y.astype(BF16))), a, b)
        mm3s = lambda a, b: each(mm3_bd, a, b)
        rw = [rw_ref[i, c, 0:2, :] for i in seqs]

        q = zcol(MQ)
        k = [x * Q_SCALE for x in zcol(MK)]
        v = zcol(MV)
        ic_e, b_e, pm_e = gxcol(GX_I), gxcol(GX_B), gxcol(GX_PM)
        m_old = [m_ref[i, 0:1, :] for i in seqs]
        cmat = [c_ref[i] for i in seqs]
        nrow = [n_ref[i, 0:1, :] for i in seqs]
        mx = each(jnp.maximum, m_old, pm_e)
        w_int = each(lambda a, b: jnp.exp(a - b), m_old, mx)
        p = each(lambda r, m: jnp.where(incl, jnp.exp(r[0:1, :] - m), 0.0), rw, mx)
        s = each(lambda a, b, c_: mm_nt(a, bd(b)) * c_, q, k, p)
        num = each(lambda w_, a, cm, s_, v_: w_ * mm(a, bd(cm)) + mm(s_, bd(v_)), w_int, q, cmat, s, v)
        den = each(lambda w_, a, n_, s_: hsum(w_ * (a * n_) + s_), w_int, q, nrow, s)
        hh = each(lambda n_, d_, b_, m: n_ / jnp.maximum(jnp.abs(d_), jnp.exp(-(b_ + m))), num, den, b_e, mx)
        mx_last = last(mx)
        wgk = each(lambda i_, b_, ml, k_: jnp.exp((i_ - b_) - ml) * k_, ic_e, b_e, mx_last, k)
        dec = each(lambda a, b: jnp.exp(a - b), m_old, mx_last)
        c_new = each(lambda d_, cm, u_: d_ * cm + u_, dec, cmat, tn_bd(wgk, v))
        for i in seqs:
            c_ref[i] = c_new[i]
            n_ref[i] = jnp.broadcast_to(dec[i] * nrow[i] + jnp.sum(wgk[i], axis=0, keepdims=True), (8, wd))
            m_ref[i] = jnp.broadcast_to(b_e[i][CHUNK - 1:CHUNK, :] + mx_last[i], (8, wd))
        y_a = each(lambda o_, h_: _sigmoid(o_) * headnorm(h_, mnorm_ref[...]), zcol(MO), hh)
        for i in seqs:
            y_ref[i, rs, 0:wd] = y_a[i].astype(y_ref.dtype)

        q = [qkv_ref[i, rs, 0:wd] for i in seqs]
        k = [qkv_ref[i, rs, wd:2 * wd] for i in seqs]
        v = [qkv_ref[i, rs, 2 * wd:] for i in seqs]
        g_e, beta = gxcol(GX_G), gxcol(GX_BETA)
        s_old = [sg_ref[i] for i in seqs]
        bdk = each(bd, k)
        lw = each(lambda g_, r: jnp.where(incl, jnp.exp(g_ - r[1:2, :]), 0.0), g_e, rw)
        amat = each(lambda b_, k_, bk, l_: jnp.where(strict, b_ * mm_nt(k_, bk) * l_, 0.0), beta, k, bdk, lw)
        egc = each(jnp.exp, g_e)
        ad = [jnp.where(blk16, a, 0.0) for a in amat]
        a2 = mm3s(ad, ad)
        a4 = mm3s(a2, a2)
        a8 = mm3s(a4, a4)
        tinv = [eye_f - a for a in ad]
        for apow in (a2, a4, a8):
            tinv = each(lambda t_, u_: t_ + u_, tinv, mm3s(tinv, apow))
        n1 = [jnp.where(blk32 & jnp.logical_not(blk16), a, 0.0) for a in amat]
        tinv = each(lambda t_, u_: t_ - u_, tinv, mm3s(mm3s(tinv, n1), tinv))
        n2 = [jnp.where(blk32, 0.0, a) for a in amat]
        tinv = each(lambda t_, u_: t_ - u_, tinv, mm3s(mm3s(tinv, n2), tinv))
        u = mm3s(tinv, each(lambda b_, v_: b_ * v_, beta, v))
        w = mm3s(tinv, each(lambda b_, e_, k_: (b_ * e_) * k_, beta, egc, k))
        bds = each(bd, s_old)
        vn = each(lambda u_, w_, bs: u_ - mm(w_, bs), u, w, bds)
        qk = each(lambda q_, bk, l_: mm_nt(q_, bk) * l_, q, bdk, lw)
        o = each(lambda q_, e_, bs, qk_, vn_: mm(q_ * e_, bs) + mm(qk_, bd(vn_)), q, egc, bds, qk, vn)
        g_last = last(g_e)
        kdec = each(lambda k_, gl, g_: k_ * jnp.exp(gl - g_), k, g_last, g_e)
        s_new = each(lambda gl, so, u_: jnp.exp(gl) * so + u_, g_last, s_old, tn_bd(kdec, vn))
        y_b = each(lambda o_, z_: headnorm(o_, gnorm_ref[...]) * _silu(z_), o, zcol(GZ))
        for i in seqs:
            sg_ref[i] = s_new[i]
            y_ref[i, rs, wd:2 * wd] = y_b[i].astype(y_ref.dtype)

        q = [x * Q_SCALE for x in zcol(LQ)]
        k = zcol(LK)
        v = zcol(LV)
        gk = [glog_ref[i, rs, :] for i in seqs]
        gcum = [gcum_ref[i, rs, :] for i in seqs]
        st_old = [slt_ref[i] for i in seqs]
        bdk = each(bd, k)
        amat = each(lambda q_, bk: jnp.where(eye, mm_nt(q_, bk), 0.0), q, bdk)
        amat = each(lambda a_, q_, g_, bk: a_ + jnp.where(lvl_mask[1], mm_nt(q_ * jnp.exp(g_), bk), 0.0),
                    amat, q, gk, bdk)
        g_end = gcum
        g_prev = [jnp.where(ti == 0, 0.0, pltpu.roll(g_, 1, axis=0)) for g_ in gcum]
        for n in GLA_LEVELS[::-1]:
            half = n // 2
            upper = (ti % n) >= half
            g_end = [jnp.where(upper, e_, pltpu.roll(e_, CHUNK - half, axis=0)) for e_ in g_end]
            g_prev = [jnp.where(upper, pltpu.roll(p_, half, axis=0), p_) for p_ in g_prev]
            qn = each(lambda q_, gc, gp: q_ * jnp.exp(gc - gp), q, gcum, g_prev)
            kn = each(lambda k_, ge, gc: k_ * jnp.exp(ge - gc), k, g_end, gcum)
            amat = each(lambda a_, q_, k_, m=lvl_mask[n]: a_ + jnp.where(m, mm_nt(q_, bd(k_)), 0.0),
                        amat, qn, kn)
        g_last = last(gcum)
        o = each(lambda q_, gc, st, a_, v_: mm_nt(q_ * jnp.exp(gc), bd(st)) + mm(a_, bd(v_)),
                 q, gcum, st_old, amat, v)
        kdec = each(lambda k_, gl, gc: k_ * jnp.exp(gl - gc), k, g_last, gcum)
        st_new = each(lambda st, gl, u_: st * jnp.exp(gl) + u_, st_old, g_last, tn_bd(v, kdec))
        y_d = each(lambda o_, z_: headnorm(o_, lnorm_ref[...]) * _silu(z_), o, zcol(LR))
        for i in seqs:
            slt_ref[i] = st_new[i]
            y_ref[i, rs, 3 * wd:] = y_d[i].astype(y_ref.dtype)
        return carry

    lax.fori_loop(0, n_chunks, chunk, 0)

    gtail = [xg_ref[i, lb_rows:lb_rows + 8, :] for i in seqs]
    stail = [xs_ref[i, lb_rows:lb_rows + 8, :] for i in seqs]
    for i in seqs:
        xg_ref[i, 0:8, :] = gtail[i]
        xs_ref[i, 0:8, :] = stail[i]

    @pl.when(lb == nlb - 1)
    def _():
        lane4 = _iota2((1, N_HEADS), 1)
        for i in seqs:
            gconv_out[i] = gtail[i][5:8, :]
            sconv_out[i] = stail[i][6:8, :]
            m_row = jnp.zeros((1, N_HEADS), F32)
            for h in range(N_HEADS):
                hs = slice(h * HEAD_DIM, (h + 1) * HEAD_DIM)
                c_out[i, h] = c_ref[i, :, hs]
                n_out[i, h:h + 1, :] = n_ref[i, 0:1, hs]
                m_row = jnp.where(lane4 == h, m_ref[i, 0:1, h * HEAD_DIM:h * HEAD_DIM + 1], m_row)
                sg_out[i, h] = sg_ref[i, :, hs]
                sl_out[i, h] = _sel_nt(eye64, slt_ref[i, :, hs])
            m_out[i] = m_row


def _mix_consts():
    wd = W_GROUP
    r, c = np.indices((wd, wd))
    same_head = (r // HEAD_DIM) == (c // HEAD_DIM)
    bdmask = same_head
    bdtriu = same_head & ((r % HEAD_DIM) <= (c % HEAD_DIM))
    tril64 = np.tril(np.ones((CHUNK, CHUNK), bool))
    rr, cc = np.indices((LANES, 4 * wd))
    selexp = rr == 4 * (cc // wd) + (cc % wd) // HEAD_DIM
    eye64 = np.eye(CHUNK, dtype=bool)
    return [jnp.asarray(m, dtype=BF16) for m in (bdmask, bdtriu, tril64, selexp, eye64)]


def _mix_prompt(z, zt, p, bsz, seq, lb_rows, nseq):
    nlb = seq // lb_rows
    n_chunks = lb_rows // CHUNK
    row = lambda b, l: (b, l, 0)
    const2 = lambda b, l: (0, 0)
    st4 = lambda b, l: (b, 0, 0, 0)
    st3 = lambda b, l: (b, 0, 0)
    params = [p["bcol"], p["acol"], p["brow"], p["arow"], p["mnorm"], p["gnorm"], p["lnorm"],
              p["gcw"], p["scw"], p["wg"], p["bg"]] + _mix_consts()
    in_specs = [pl.BlockSpec((nseq, lb_rows, NP), row),
                pl.BlockSpec((nseq, 1, N_ROW_KINDS, n_chunks, W_GROUP), lambda b, l: (b, l, 0, 0, 0))]
    in_specs += [pl.BlockSpec(a.shape, const2) for a in params]
    hd = (N_HEADS, HEAD_DIM, HEAD_DIM)
    out_shape = [jax.ShapeDtypeStruct((bsz, seq, D_MODEL), BF16),
                 jax.ShapeDtypeStruct((bsz,) + hd, F32),
                 jax.ShapeDtypeStruct((bsz, N_HEADS, HEAD_DIM), F32),
                 jax.ShapeDtypeStruct((bsz, 1, N_HEADS), F32),
                 jax.ShapeDtypeStruct((bsz,) + hd, F32),
                 jax.ShapeDtypeStruct((bsz, 3, 3 * W_GROUP), F32),
                 jax.ShapeDtypeStruct((bsz, 2, W_GROUP), F32),
                 jax.ShapeDtypeStruct((bsz,) + hd, F32)]
    out_specs = [pl.BlockSpec((nseq, lb_rows, D_MODEL), row),
                 pl.BlockSpec((nseq,) + hd, st4),
                 pl.BlockSpec((nseq, N_HEADS, HEAD_DIM), st3),
                 pl.BlockSpec((nseq, 1, N_HEADS), st3),
                 pl.BlockSpec((nseq,) + hd, st4),
                 pl.BlockSpec((nseq, 3, 3 * W_GROUP), st3),
                 pl.BlockSpec((nseq, 2, W_GROUP), st3),
                 pl.BlockSpec((nseq,) + hd, st4)]
    vm = lambda *shape: pltpu.VMEM((nseq,) + shape, F32)
    scratch = [vm(HEAD_DIM, W_GROUP),
               vm(8, W_GROUP),
               vm(8, W_GROUP),
               vm(HEAD_DIM, W_GROUP),
               vm(HEAD_DIM, W_GROUP),
               vm(lb_rows + 8, 3 * W_GROUP),
               vm(lb_rows + 8, W_GROUP),
               vm(lb_rows, 3 * W_GROUP),
               vm(lb_rows, W_GROUP),
               vm(lb_rows, W_GROUP),
               vm(lb_rows, 5 * W_GROUP),
               vm(n_chunks, 8, W_GROUP)]
    return pl.pallas_call(
        functools.partial(_mix_kernel, lb_rows=lb_rows, nseq=nseq),
        grid=(bsz // nseq, nlb),
        in_specs=in_specs,
        out_specs=out_specs,
        out_shape=out_shape,
        scratch_shapes=scratch,
        compiler_params=_cparams(("parallel", "arbitrary")),
        name="mix_prompt",
    )(z, zt, *params)


def _inproj_s_kernel(x_ref, g_ref, wt_ref, zt_ref):
    w = wt_ref[...]
    n = w.shape[0]
    w1, w2 = _split2(w)
    h1, h2 = _split2(_rms(x_ref[...], g_ref[...]))
    r = _dot_nt(jnp.concatenate([w1, w2], axis=0), h1)
    zt_ref[...] = r[0:n] + r[n:] + _dot_nt(w1, h2)


def _inproj_s(x, g, wt_all, layer, n_blk=4):
    bn = x.shape[0]
    d_in = wt_all.shape[1]
    tn = d_in // n_blk
    return pl.pallas_call(
        _inproj_s_kernel,
        grid=(n_blk,),
        in_specs=[pl.BlockSpec((bn, D_MODEL), lambda j: (0, 0)),
                  pl.BlockSpec((1, D_MODEL), lambda j: (0, 0)),
                  pl.BlockSpec((None, tn, D_MODEL), lambda j: (layer, j, 0))],
        out_specs=pl.BlockSpec((tn, bn), lambda j: (j, 0)),
        out_shape=jax.ShapeDtypeStruct((d_in, bn), F32),
        compiler_params=_cparams(("parallel",)),
        name="inproj_s",
    )(x, g, wt_all)


def _spre_kernel(zt_ref, gbuf_ref, sbuf_ref, bcol_ref, acol_ref, gcw_ref, scw_ref, wg_ref, bg_ref,
                 qkvt_ref, glogt_ref, actt_ref, yc_ref, gbuf_out, sbuf_out):
    rows_t = lambda col, n: zt_ref[col:col + n, :].T
    small = rows_t(SM, LANES)
    lane = _iota2((1, LANES), 1)
    actt_ref[...] = _gate_act(small + bcol_ref[...], lane, -jnp.exp(acol_ref[...])).T
    glin = _mm3(small, wg_ref[...]) + bg_ref[...]
    glogt_ref[...] = (_logsigmoid(glin) * (1.0 / GLA_NORMALIZER)).T

    wq = 3 * W_GROUP
    u = rows_t(GQ, wq)
    b0, b1, b2 = gbuf_ref[:, 0:wq], gbuf_ref[:, wq:2 * wq], gbuf_ref[:, 2 * wq:]
    conv = b0 * gcw_ref[0:1, :] + b1 * gcw_ref[1:2, :] + b2 * gcw_ref[2:3, :] + u * gcw_ref[3:4, :]
    conv = _silu(conv)
    gbuf_out[:, 0:wq] = b1
    gbuf_out[:, wq:2 * wq] = b2
    gbuf_out[:, 2 * wq:] = u
    hsum = (_iota2((W_GROUP, W_GROUP), 0) // HEAD_DIM ==
            _iota2((W_GROUP, W_GROUP), 1) // HEAD_DIM).astype(BF16)

    def l2n(x):
        return x * lax.rsqrt(_sel_r(x * x, hsum) + EPS)

    qkvt_ref[0:W_GROUP, :] = (l2n(conv[:, 0:W_GROUP]) * Q_SCALE).T
    qkvt_ref[W_GROUP:2 * W_GROUP, :] = l2n(conv[:, W_GROUP:2 * W_GROUP]).T
    qkvt_ref[2 * W_GROUP:, :] = conv[:, 2 * W_GROUP:].T

    u2 = rows_t(SC, W_GROUP) * rows_t(SH, W_GROUP)
    s0, s1 = sbuf_ref[:, 0:W_GROUP], sbuf_ref[:, W_GROUP:]
    cu = s0 * scw_ref[0:1, :] + s1 * scw_ref[1:2, :] + u2 * scw_ref[2:3, :]
    yc_ref[...] = rows_t(SB, W_GROUP) * cu
    sbuf_out[:, 0:W_GROUP] = s1
    sbuf_out[:, W_GROUP:] = u2


def _sample_pre(zt, gbuf, sbuf, p):
    bn = zt.shape[1]
    sd = lambda s: jax.ShapeDtypeStruct(s, F32)
    return pl.pallas_call(
        _spre_kernel,
        out_shape=[sd((3 * W_GROUP, bn)), sd((W_GROUP, bn)), sd((LANES, bn)),
                   sd((bn, W_GROUP)), sd(gbuf.shape), sd(sbuf.shape)],
        compiler_params=pltpu.CompilerParams(vmem_limit_bytes=VMEM_LIMIT),
        name="sample_pre",
    )(zt, gbuf, sbuf, p["bcol"], p["acol"], p["gcw"], p["scw"], p["wg32"], p["bg"])


def _srec_kernel(zt_ref, qkvt_ref, glogt_ref, actt_ref, c_ref, n_ref, m_ref, sg_ref, sl_ref, norm_ref,
                 y_ref, c_out, n_out, m_out, sg_out, sl_out):
    h = pl.program_id(0)
    feat = lambda ref, col: ref[pl.ds(pl.multiple_of(col + h * HEAD_DIM, HEAD_DIM), HEAD_DIM), :]
    gate = lambda kind: actt_ref[pl.ds(4 * kind + h, 1), :]
    colsum = lambda x: jnp.sum(x, axis=0, keepdims=True)
    rows = range(HEAD_DIM)

    def headnorm(o, w):
        return o * lax.rsqrt(jnp.mean(o * o, axis=0, keepdims=True) + EPS) * w

    def contract(vecs, s_ref):
        accs = [v[0:1, :] * s_ref[0] for v in vecs]
        for d in rows[1:]:
            sd_ = s_ref[d]
            accs = [a + v[d:d + 1, :] * sd_ for a, v in zip(accs, vecs)]
        return accs

    mq, mk, mv, mo = feat(zt_ref, MQ), feat(zt_ref, MK) * Q_SCALE, feat(zt_ref, MV), feat(zt_ref, MO)
    ic, lf = gate(0), gate(1)
    nvec = n_ref[...]
    m_old = m_ref[pl.ds(h, 1), :]
    a = lf + m_old
    mt = jnp.maximum(a, ic)
    w_int = jnp.exp(a - mt)
    e_i = jnp.exp(ic - mt)
    s = colsum(mq * mk) * e_i
    (qc,) = contract([mq], c_ref)
    num = w_int * qc + s * mv
    den = w_int * colsum(mq * nvec) + s
    hh = num / jnp.maximum(jnp.abs(den), jnp.exp(-mt))
    kw = e_i * mk
    for d in rows:
        c_out[d] = w_int * c_ref[d] + kw[d:d + 1, :] * mv
    n_out[...] = w_int * nvec + kw
    m_out[...] = mt
    y_ref[0:HEAD_DIM, :] = _sigmoid(mo) * headnorm(hh, norm_ref[0])

    gq, gk, gv = feat(qkvt_ref, 0), feat(qkvt_ref, W_GROUP), feat(qkvt_ref, 2 * W_GROUP)
    beta = gate(3)
    eg = jnp.exp(gate(2))
    ks, qs = contract([gk, gq], sg_ref)
    vn = beta * gv - (beta * eg) * ks
    o = eg * qs + colsum(gq * gk) * vn
    for d in rows:
        sg_out[d] = eg * sg_ref[d] + gk[d:d + 1, :] * vn
    y_ref[HEAD_DIM:2 * HEAD_DIM, :] = headnorm(o, norm_ref[1]) * _silu(feat(zt_ref, GZ))

    lq, lk, lv = feat(zt_ref, LQ) * Q_SCALE, feat(zt_ref, LK), feat(zt_ref, LV)
    egk = jnp.exp(feat(glogt_ref, 0))
    (ql,) = contract([lq * egk], sl_ref)
    o = ql + colsum(lq * lk) * lv
    for d in rows:
        sl_out[d] = egk[d:d + 1, :] * sl_ref[d] + lk[d:d + 1, :] * lv
    y_ref[2 * HEAD_DIM:, :] = headnorm(o, norm_ref[2]) * _silu(feat(zt_ref, LR))


def _sample_rec(layer, zt, qkvt, glogt, actt, cst, nst, mst, sgst, slst, norms):
    bn = zt.shape[1]
    mat_in = pl.BlockSpec((None, None, HEAD_DIM, HEAD_DIM, bn), lambda h: (layer, h, 0, 0, 0))
    mat_out = pl.BlockSpec((None, HEAD_DIM, HEAD_DIM, bn), lambda h: (h, 0, 0, 0))
    full = lambda a: pl.BlockSpec(a.shape, lambda h: (0,) * a.ndim)
    sd = lambda *s: jax.ShapeDtypeStruct(s, F32)
    return pl.pallas_call(
        _srec_kernel,
        grid=(N_HEADS,),
        in_specs=[full(zt), full(qkvt), full(glogt), full(actt), mat_in,
                  pl.BlockSpec((None, None, HEAD_DIM, bn), lambda h: (layer, h, 0, 0)),
                  pl.BlockSpec((None, N_HEADS, bn), lambda h: (layer, 0, 0)),
                  mat_in, mat_in,
                  pl.BlockSpec((3, HEAD_DIM, bn), lambda h: (0, h, 0))],
        out_specs=[pl.BlockSpec((None, 3 * HEAD_DIM, bn), lambda h: (h, 0, 0)), mat_out,
                   pl.BlockSpec((None, HEAD_DIM, bn), lambda h: (h, 0, 0)),
                   pl.BlockSpec((None, 1, bn), lambda h: (h, 0, 0)), mat_out, mat_out],
        out_shape=[sd(N_HEADS, 3 * HEAD_DIM, bn), sd(N_HEADS, HEAD_DIM, HEAD_DIM, bn),
                   sd(N_HEADS, HEAD_DIM, bn), sd(N_HEADS, 1, bn),
                   sd(N_HEADS, HEAD_DIM, HEAD_DIM, bn), sd(N_HEADS, HEAD_DIM, HEAD_DIM, bn)],
        compiler_params=_cparams(("parallel",)),
        name="sample_rec",
    )(zt, qkvt, glogt, actt, cst, nst, mst, sgst, slst, norms)


def _outproj_s_kernel(y_ref, w_ref, x_ref, o_ref):
    o_ref[...] = x_ref[...] + _mm3(y_ref[...], w_ref[...])


def _outproj_s(y, w_all, layer, x):
    full = lambda a: pl.BlockSpec(a.shape, lambda i: (0, 0))
    return pl.pallas_call(
        _outproj_s_kernel,
        grid=(1,),
        in_specs=[full(y), pl.BlockSpec((None, D_MODEL, D_MODEL), lambda i: (layer, 0, 0)), full(x)],
        out_specs=full(x),
        out_shape=jax.ShapeDtypeStruct(x.shape, F32),
        compiler_params=_cparams(("arbitrary",)),
        name="outproj_s",
    )(y, w_all, x)


def _ffn_s_kernel(x_ref, g_ref, wg_ref, wu_ref, wd_ref, gf_ref, o_ref, h_ref, acc_ref, *, final_norm):
    j = pl.program_id(0)

    @pl.when(j == 0)
    def _():
        h_ref[...] = _rms(x_ref[...], g_ref[...])
        acc_ref[...] = jnp.zeros_like(acc_ref)

    h = h_ref[...]
    acc_ref[...] += _mm3(_silu(_mm3(h, wg_ref[...])) * _mm3(h, wu_ref[...]), wd_ref[...])

    @pl.when(j == pl.num_programs(0) - 1)
    def _():
        out = x_ref[...] + acc_ref[...]
        o_ref[...] = _rms(out, gf_ref[...]) if final_norm else out


def _ffn_s(x, g, wg, wu, wd, gf, tf, final_norm):
    t = x.shape[0]
    dff = wg.shape[1]
    full = lambda j: (0, 0)
    return pl.pallas_call(
        functools.partial(_ffn_s_kernel, final_norm=final_norm),
        grid=(dff // tf,),
        in_specs=[pl.BlockSpec((t, D_MODEL), full),
                  pl.BlockSpec((1, D_MODEL), full),
                  pl.BlockSpec((D_MODEL, tf), lambda j: (0, j)),
                  pl.BlockSpec((D_MODEL, tf), lambda j: (0, j)),
                  pl.BlockSpec((tf, D_MODEL), lambda j: (j, 0)),
                  pl.BlockSpec((1, D_MODEL), full)],
        out_specs=pl.BlockSpec((t, D_MODEL), full),
        out_shape=jax.ShapeDtypeStruct((t, D_MODEL), F32),
        scratch_shapes=[pltpu.VMEM((t, D_MODEL), F32), pltpu.VMEM((t, D_MODEL), F32)],
        compiler_params=_cparams(("arbitrary",)),
        name="ffn_s",
    )(x, g, wg, wu, wd, gf)


def _outproj_kernel(y_ref, w_ref, x_ref, o_ref):
    o_ref[...] = x_ref[...] + _dot(y_ref[...], w_ref[...].astype(BF16))


def _outproj(y, w_all, layer, x, tm):
    t = x.shape[0]
    return pl.pallas_call(
        _outproj_kernel,
        grid=(t // tm,),
        in_specs=[pl.BlockSpec((tm, D_MODEL), lambda i: (i, 0)),
                  pl.BlockSpec((None, D_MODEL, D_MODEL), lambda i: (layer, 0, 0)),
                  pl.BlockSpec((tm, D_MODEL), lambda i: (i, 0))],
        out_specs=pl.BlockSpec((tm, D_MODEL), lambda i: (i, 0)),
        out_shape=jax.ShapeDtypeStruct((t, D_MODEL), F32),
        compiler_params=_cparams(("parallel",)),
        name="outproj",
    )(y, w_all, x)


def _ffn_kernel(x_ref, y_ref, wo_ref, g_ref, wg_ref, wu_ref, wd_ref, gf_ref, o_ref,
                h_ref, acc_ref, xn_ref, *, final_norm):
    j = pl.program_id(1)

    @pl.when(j == 0)
    def _():
        xn = x_ref[...] + _dot(y_ref[...], wo_ref[...].astype(BF16))
        xn_ref[...] = xn
        h_ref[...] = _rms(xn, g_ref[...]).astype(BF16)
        acc_ref[...] = jnp.zeros_like(acc_ref)

    h = h_ref[...]
    act = _silu(_dot(h, wg_ref[...].astype(BF16))) * _dot(h, wu_ref[...].astype(BF16))
    acc_ref[...] += _dot(act.astype(BF16), wd_ref[...].astype(BF16))

    @pl.when(j == pl.num_programs(1) - 1)
    def _():
        out = xn_ref[...] + acc_ref[...]
        o_ref[...] = _rms(out, gf_ref[...]) if final_norm else out


def _ffn(x, y, wo_all, layer, g, wg_all, wu_all, wd_all, dense_idx, gf, tm, tf, final_norm):
    t = x.shape[0]
    dff = wg_all.shape[2]
    return pl.pallas_call(
        functools.partial(_ffn_kernel, final_norm=final_norm),
        grid=(t // tm, dff // tf),
        in_specs=[pl.BlockSpec((tm, D_MODEL), lambda i, j: (i, 0)),
                  pl.BlockSpec((tm, D_MODEL), lambda i, j: (i, 0)),
                  pl.BlockSpec((None, D_MODEL, D_MODEL), lambda i, j: (layer, 0, 0)),
                  pl.BlockSpec((1, D_MODEL), lambda i, j: (0, 0)),
                  pl.BlockSpec((None, D_MODEL, tf), lambda i, j: (dense_idx, 0, j)),
                  pl.BlockSpec((None, D_MODEL, tf), lambda i, j: (dense_idx, 0, j)),
                  pl.BlockSpec((None, tf, D_MODEL), lambda i, j: (dense_idx, j, 0)),
                  pl.BlockSpec((1, D_MODEL), lambda i, j: (0, 0))],
        out_specs=pl.BlockSpec((tm, D_MODEL), lambda i, j: (i, 0)),
        out_shape=jax.ShapeDtypeStruct((t, D_MODEL), F32),
        scratch_shapes=[pltpu.VMEM((tm, D_MODEL), BF16), pltpu.VMEM((tm, D_MODEL), F32),
                        pltpu.VMEM((tm, D_MODEL), F32)],
        compiler_params=_cparams(("parallel", "arbitrary")),
        name="ffn",
    )(x, y, wo_all, g, wg_all, wu_all, wd_all, gf)


def _top2_gates(logits):
    return _top2_select(logits)[0]


def _top2_select(logits):
    lane = _iota2((1, LANES), 1)
    valid = lane < N_EXPERTS
    logits = jnp.where(valid, logits, -jnp.inf)
    ex = jnp.exp(logits - jnp.max(logits, axis=-1, keepdims=True))
    probs = ex / jnp.sum(ex, axis=-1, keepdims=True)
    v1 = jnp.max(probs, axis=-1, keepdims=True)
    i1 = jnp.min(jnp.where(probs == v1, lane, LANES), axis=-1, keepdims=True)
    rest = jnp.where((lane == i1) | jnp.logical_not(valid), -1.0, probs)
    v2 = jnp.max(rest, axis=-1, keepdims=True)
    i2 = jnp.min(jnp.where(rest == v2, lane, LANES), axis=-1, keepdims=True)
    tot = v1 + v2
    gates = jnp.where(lane == i1, v1 / tot, 0.0) + jnp.where(lane == i2, v2 / tot, 0.0)
    return gates, ((lane == i1) | (lane == i2)).astype(F32)


def _moe_kernel(x_ref, g_ref, wr_ref, br_ref, wg_ref, wu_ref, wd_ref, gf_ref, o_ref,
                h_ref, acc_ref, gates_ref, *, final_norm, precise_router):
    e = pl.program_id(1)

    @pl.when(e == 0)
    def _():
        h = _rms(x_ref[...], g_ref[...])
        h_ref[...] = h.astype(BF16)
        acc_ref[...] = jnp.zeros_like(acc_ref)
        if precise_router:
            logits = _mm3(h, wr_ref[...])
        else:
            logits = _dot(h.astype(BF16), wr_ref[...].astype(BF16))
        gates_ref[...] = _top2_gates(logits + br_ref[...])

    h = h_ref[...]
    y = _dot((_silu(_dot(h, wg_ref[0])) * _dot(h, wu_ref[0])).astype(BF16), wd_ref[0])
    lane = _iota2((1, LANES), 1)
    ge = jnp.sum(jnp.where(lane == e, gates_ref[...], 0.0), axis=-1, keepdims=True)
    acc_ref[...] += ge * y

    @pl.when(e == pl.num_programs(1) - 1)
    def _():
        out = x_ref[...] + acc_ref[...]
        o_ref[...] = _rms(out, gf_ref[...]) if final_norm else out


def _moe(x, g, wr, br, wg, wu, wd, gf, tm, final_norm, precise_router):
    t = x.shape[0]
    fe = wg.shape[2]
    return pl.pallas_call(
        functools.partial(_moe_kernel, final_norm=final_norm, precise_router=precise_router),
        grid=(t // tm, N_EXPERTS),
        in_specs=[pl.BlockSpec((tm, D_MODEL), lambda i, e: (i, 0)),
                  pl.BlockSpec((1, D_MODEL), lambda i, e: (0, 0)),
                  pl.BlockSpec((D_MODEL, LANES), lambda i, e: (0, 0)),
                  pl.BlockSpec((1, LANES), lambda i, e: (0, 0)),
                  pl.BlockSpec((1, D_MODEL, fe), lambda i, e: (e, 0, 0)),
                  pl.BlockSpec((1, D_MODEL, fe), lambda i, e: (e, 0, 0)),
                  pl.BlockSpec((1, fe, D_MODEL), lambda i, e: (e, 0, 0)),
                  pl.BlockSpec((1, D_MODEL), lambda i, e: (0, 0))],
        out_specs=pl.BlockSpec((tm, D_MODEL), lambda i, e: (i, 0)),
        out_shape=jax.ShapeDtypeStruct((t, D_MODEL), F32),
        scratch_shapes=[pltpu.VMEM((tm, D_MODEL), BF16), pltpu.VMEM((tm, D_MODEL), F32),
                        pltpu.VMEM((tm, LANES), F32)],
        compiler_params=_cparams(("parallel", "arbitrary")),
        name="moe",
    )(x, g, wr, br, wg, wu, wd, gf)


def _moe_win_kernel(x_ref, g_ref, wr_ref, br_ref, tril_ref, wg_ref, wu_ref, wd_ref, gf_ref, o_ref,
                    h_ref, gate_ref, key_ref, keyt_ref, xg_ref, ys_ref, *, final_norm, cap, tc):
    e = pl.program_id(1)
    w_rows = x_ref.shape[0]
    lane = _iota2((1, LANES), 1)

    @pl.when(e == 0)
    def _():
        x = x_ref[...]
        hb = _rms(x, g_ref[...]).astype(BF16)
        h_ref[...] = hb
        o_ref[...] = x
        gates, sel = _top2_select(_dot(hb, wr_ref[...].astype(BF16)) + br_ref[...])
        rank = _dot(tril_ref[...], sel.astype(BF16))
        key = jnp.where(sel > 0.0, rank, -1.0)
        gate_ref[...] = gates
        key_ref[...] = key
        keyt_ref[...] = key.T

    key_r = keyt_ref[pl.ds(e, 1), :]
    pick_e = (_iota2((LANES, LANES), 0) == e).astype(BF16)
    key_c = _sel_r(key_ref[...], pick_e)
    gate_c = _sel_r(gate_ref[...], pick_e)
    gate_c = jnp.concatenate([gate_c] * (tc // LANES), axis=1)

    def expert(xg):
        act = (_silu(_dot(xg, wg_ref[0])) * _dot(xg, wu_ref[0])).astype(BF16)
        return _dot(act, wd_ref[0])

    n_tc = w_rows // tc
    win = LANES
    chunk_cnt = [jnp.sum((key_r[:, c * tc:(c + 1) * tc] >= 0.0).astype(F32), axis=-1,
                         keepdims=True)[0, 0].astype(jnp.int32) for c in range(n_tc)]
    starts, count = [], jnp.int32(0)
    for c in range(n_tc):
        starts.append(count)
        count = count + chunk_cnt[c]
    windowed = (count > 0) & (count <= cap)
    for c in range(n_tc):
        windowed = windowed & (chunk_cnt[c] <= win - 8)

    @pl.when(windowed)
    def _():
        xg_ref[...] = jnp.zeros_like(xg_ref)
        w_row = _iota2((win, tc), 0)
        w_col = _iota2((tc, 2 * win), 1) % win
        lo = [pl.multiple_of((starts[c] // 8) * 8, 8) for c in range(n_tc)]
        for c in range(n_tc):
            ts = slice(c * tc, (c + 1) * tc)
            gather = jnp.where((w_row + lo[c]).astype(F32) == key_r[:, ts], 1.0, 0.0).astype(BF16)
            xg_ref[pl.ds(lo[c], win), :] += _dot(gather, h_ref[ts, :])
        ys_ref[0:cap, :] = expert(xg_ref[0:cap, :].astype(BF16))
        ys_ref[cap:, :] = jnp.zeros((win, D_MODEL), F32)
        for c in range(n_tc):
            ts = slice(c * tc, (c + 1) * tc)
            y12 = jnp.concatenate(_split2(ys_ref[pl.ds(lo[c], win), :]), axis=0)
            kc = jnp.concatenate([key_c[ts, :]] * (2 * win // LANES), axis=1)
            scatter = jnp.where((w_col + lo[c]).astype(F32) == kc, 1.0, 0.0).astype(BF16)
            gc = jnp.concatenate([gate_c[ts, :]] * (D_MODEL // tc), axis=1)
            o_ref[ts, :] += gc * _dot(scatter, y12)

    @pl.when(jnp.logical_not(windowed) & (count > 0))
    def _():
        key_w = jnp.concatenate([key_c] * pl.cdiv(2 * cap, LANES), axis=1)[:, :2 * cap]
        row_id = _iota2((cap, w_rows), 0).astype(F32)
        col_id = (_iota2((w_rows, 2 * cap), 1) % cap).astype(F32)

        def trip(j, carry):
            base = (j * cap).astype(F32)
            gather = jnp.where(row_id + base == key_r, 1.0, 0.0).astype(BF16)
            y = expert(_dot(gather, h_ref[...]).astype(BF16))
            y12 = jnp.concatenate(_split2(y), axis=0)
            scatter = jnp.where(col_id + base == key_w, 1.0, 0.0).astype(BF16)
            for c0 in range(0, D_MODEL, tc):
                cs = slice(c0, c0 + tc)
                o_ref[:, cs] += gate_c * _dot(scatter, y12[:, cs])
            return carry

        lax.fori_loop(0, (count + cap - 1) // cap, trip, 0)

    if final_norm:
        @pl.when(e == pl.num_programs(1) - 1)
        def _():
            o_ref[...] = _rms(o_ref[...], gf_ref[...])


def _moe_win(x, g, wr, br, wg, wu, wd, gf, w_rows, final_norm):
    t = x.shape[0]
    fe = wg.shape[2]
    cap = max(16, int(w_rows / 4 + 3.5 * (w_rows * 3 / 16) ** 0.5) // 16 * 16)
    tc = min(w_rows, 256)
    tril = jnp.asarray(np.tril(np.ones((w_rows, w_rows), np.float32), -1), dtype=BF16)
    const = lambda i, e: (0, 0)
    return pl.pallas_call(
        functools.partial(_moe_win_kernel, final_norm=final_norm, cap=cap, tc=tc),
        grid=(t // w_rows, N_EXPERTS),
        in_specs=[pl.BlockSpec((w_rows, D_MODEL), lambda i, e: (i, 0)),
                  pl.BlockSpec((1, D_MODEL), const),
                  pl.BlockSpec((D_MODEL, LANES), const),
                  pl.BlockSpec((1, LANES), const),
                  pl.BlockSpec((w_rows, w_rows), const),
                  pl.BlockSpec((1, D_MODEL, fe), lambda i, e: (e, 0, 0)),
                  pl.BlockSpec((1, D_MODEL, fe), lambda i, e: (e, 0, 0)),
                  pl.BlockSpec((1, fe, D_MODEL), lambda i, e: (e, 0, 0)),
                  pl.BlockSpec((1, D_MODEL), const)],
        out_specs=pl.BlockSpec((w_rows, D_MODEL), lambda i, e: (i, 0)),
        out_shape=jax.ShapeDtypeStruct((t, D_MODEL), F32),
        scratch_shapes=[pltpu.VMEM((w_rows, D_MODEL), BF16),
                        pltpu.VMEM((w_rows, LANES), F32),
                        pltpu.VMEM((w_rows, LANES), F32), pltpu.VMEM((LANES, w_rows), F32),
                        pltpu.VMEM((cap + LANES, D_MODEL), F32),
                        pltpu.VMEM((cap + LANES, D_MODEL), F32)],
        compiler_params=_cparams(("parallel", "arbitrary")),
        name="moe_win",
    )(x, g, wr, br, tril, wg, wu, wd, gf)


def _prep_w_in(w):
    seg = lambda i: w[:, _OFF[i]:_OFF[i + 1]]
    small = jnp.concatenate([seg(i) for i in _SMALL], axis=1)
    pad = jnp.zeros((w.shape[0], LANES - N_SMALL), w.dtype)
    wfull = jnp.concatenate([seg(i) for i in _BIG] + [small, pad], axis=1).astype(BF16)
    gates = jnp.concatenate([seg(i) for i in _GATES], axis=1)
    return wfull, gates.T.astype(BF16)


def _regroup_zt(zt):
    seg = lambda i: zt[_OFF[i]:_OFF[i + 1], :]
    pad = jnp.zeros((LANES - N_SMALL, zt.shape[1]), zt.dtype)
    return jnp.concatenate([seg(i) for i in _BIG] + [seg(i) for i in _SMALL] + [pad], axis=0)


def _pad_lanes(v, start):
    out = jnp.zeros((1, LANES), F32)
    return lax.dynamic_update_slice(out, v.reshape(1, -1).astype(F32), (0, start))


def _layer_params(l, mlstm_b_i, mlstm_b_f, mlstm_norm, gdn_conv_w, gdn_a_log, gdn_dt_bias, gdn_norm,
                  sc_conv_w, gla_w_gate, gla_b_gate, gla_norm):
    bias = jnp.concatenate([mlstm_b_i[l], mlstm_b_f[l], gdn_dt_bias[l]]).astype(F32)
    alog = gdn_a_log[l].astype(F32)
    per_lane = lambda v: jnp.repeat(v, HEAD_DIM)
    zero = jnp.zeros((W_GROUP,), F32)
    brow = jnp.stack([per_lane(mlstm_b_i[l]), per_lane(mlstm_b_f[l]), per_lane(gdn_dt_bias[l]), zero])
    arow = jnp.stack([zero, zero, per_lane(alog), zero])
    wg32 = jnp.zeros((LANES, W_GROUP), F32).at[16:16 + GLA_RANK].set(gla_w_gate[l])
    return dict(bcol=_pad_lanes(bias, 0), acol=_pad_lanes(alog, 8), brow=brow, arow=arow,
                mnorm=mlstm_norm[l].reshape(1, -1), gnorm=gdn_norm[l].reshape(1, -1),
                lnorm=gla_norm[l].reshape(1, -1),
                gcw=gdn_conv_w[l].T, scw=sc_conv_w[l].T, wg=wg32.astype(BF16), wg32=wg32,
                bg=gla_b_gate[l].reshape(1, -1))


def _pick_tile(n, pref):
    for c in pref:
        if n % c == 0:
            return c
    return n


def kernel(x_prompt, x_sample, state_mlstm_C, state_mlstm_n, state_mlstm_m, state_gdn_S, state_gdn_conv,
           state_sc_conv, state_gla_S, w_in, g_mix, mlstm_b_i, mlstm_b_f, mlstm_norm, gdn_conv_w, gdn_a_log,
           gdn_dt_bias, gdn_norm, sc_conv_w, gla_w_gate, gla_b_gate, gla_norm, w_out, g_ffn, ffn_w_gate,
           ffn_w_up, ffn_w_down, moe_w_router, moe_b_router, moe_w_gate, moe_w_up, moe_w_down, g_final):
    depth = w_in.shape[0]
    bsz, seq, _ = x_prompt.shape
    bn = x_sample.shape[0]
    tp = bsz * seq
    assert x_sample.shape[1] == 1 and seq % CHUNK == 0

    xp = x_prompt.reshape(tp, D_MODEL)
    xs = x_sample.reshape(bn, D_MODEL)
    lb_rows = _pick_tile(seq, (128, 64))
    tm_p = _pick_tile(tp, (512, 256, 128, 64))
    tm_f = _pick_tile(tp, (1024, 512, 256, 128, 64))
    tm_e = _pick_tile(tp, (1024, 512, 256, 128))
    nseq = _pick_tile(bsz, (4, 2, 1))

    s_minor = (state_mlstm_C.transpose(0, 2, 3, 4, 1), state_mlstm_n.transpose(0, 2, 3, 1),
               state_mlstm_m.transpose(0, 2, 1), state_gdn_S.transpose(0, 2, 3, 4, 1),
               state_gla_S.transpose(0, 2, 3, 4, 1))
    w_in_t = w_in.transpose(0, 2, 1)
    gfin = g_final.reshape(1, -1)

    p_states = [[] for _ in range(7)]
    s_states = [[] for _ in range(7)]
    for l in range(depth):
        p = _layer_params(l, mlstm_b_i, mlstm_b_f, mlstm_norm, gdn_conv_w, gdn_a_log, gdn_dt_bias,
                          gdn_norm, sc_conv_w, gla_w_gate, gla_b_gate, gla_norm)
        wfull, wst = _prep_w_in(w_in[l])
        gm = g_mix[l].reshape(1, -1)
        gf = g_ffn[l].reshape(1, -1)
        last = l == depth - 1

        z, zt = _inproj(xp, gm, wfull, wst, tm_p, lb_rows)
        y, c1, n1, m1, sg1, gb1, sb1, sl1 = _mix_prompt(
            z.reshape(bsz, seq, NP), zt.reshape((bsz, seq // lb_rows) + zt.shape[1:]), p,
            bsz, seq, lb_rows, nseq)
        y = y.reshape(tp, D_MODEL)
        if l % 2:
            xp = _outproj(y, w_out, l, xp, tm_f)
        for lst, v in zip(p_states, (c1, n1, m1.reshape(bsz, N_HEADS), sg1, gb1, sb1, sl1)):
            lst.append(v)

        zt_s = _regroup_zt(_inproj_s(xs, gm, w_in_t, l))
        gbuf = state_gdn_conv[l].reshape(bn, -1)
        sbuf = state_sc_conv[l].reshape(bn, -1)
        qkvt, glogt, actt, yc, gbuf1, sbuf1 = _sample_pre(zt_s, gbuf, sbuf, p)
        norms = jnp.stack([jnp.broadcast_to(v[:, None], (W_GROUP, bn))
                           for v in (mlstm_norm[l], gdn_norm[l], gla_norm[l])])
        ysr, c2, n2, m2, sg2, sl2 = _sample_rec(l, zt_s, qkvt, glogt, actt, *s_minor, norms)
        yt = ysr.reshape(N_HEADS, 3, HEAD_DIM, bn).transpose(3, 1, 0, 2).reshape(bn, 3, W_GROUP)
        ymix = jnp.concatenate([yt[:, 0], yt[:, 1], yc, yt[:, 2]], axis=-1)
        xs = _outproj_s(ymix, w_out, l, xs)
        for lst, v in zip(s_states, (c2, n2, m2[:, 0], sg2, gbuf1.reshape(bn, 3, -1),
                                     sbuf1.reshape(bn, 2, -1), sl2)):
            lst.append(v)

        j = l // 2
        if l % 2 == 0:
            tf = _pick_tile(ffn_w_gate.shape[2], (256, 128))
            xp = _ffn(xp, y, w_out, l, gf, ffn_w_gate, ffn_w_up, ffn_w_down, j, gfin, tm_f, tf, last)
            xs = _ffn_s(xs, gf, ffn_w_gate[j], ffn_w_up[j], ffn_w_down[j], gfin, tf, last)
        else:
            wr = jnp.zeros((D_MODEL, LANES), F32).at[:, :N_EXPERTS].set(moe_w_router[j])
            br = _pad_lanes(moe_b_router[j], 0)
            wgt, wup, wdn = (moe_w_gate[j].astype(BF16), moe_w_up[j].astype(BF16),
                             moe_w_down[j].astype(BF16))
            xp = _moe_win(xp, gf, wr, br, wgt, wup, wdn, gfin, tm_e, last)
            xs = _moe(xs, gf, wr, br, wgt, wup, wdn, gfin, bn, last, True)

    y_prompt = xp.reshape(bsz, seq, D_MODEL)
    y_sample = xs.reshape(bn, 1, D_MODEL)
    sp = [jnp.stack(v) for v in p_states]
    ss = [jnp.stack(v) for v in s_states]
    for i, perm in ((0, (0, 4, 1, 2, 3)), (1, (0, 3, 1, 2)), (2, (0, 2, 1)), (3, (0, 4, 1, 2, 3)),
                    (6, (0, 4, 1, 2, 3))):
        ss[i] = ss[i].transpose(perm)
    return (y_prompt, y_sample, sp[0], ss[0], sp[1], ss[1], sp[2], ss[2], sp[3], ss[3],
            sp[4], ss[4], sp[5], ss[5], sp[6], ss[6])
```

```python
import functools

import numpy as np
import jax
import jax.numpy as jnp
from jax import lax
from jax.experimental import pallas as pl
from jax.experimental.pallas import tpu as pltpu

F32 = jnp.float32
BF16 = jnp.bfloat16

D_MODEL = 1024
W_GROUP = 256
N_HEADS = 4
HEAD_DIM = 64
CHUNK = 64
GLA_RANK = 16
GLA_NORMALIZER = 16.0
N_EXPERTS = 8
EPS = 1e-6
Q_SCALE = HEAD_DIM ** -0.5

VMEM_LIMIT = 56 * 1024 * 1024
LANES = 128

SPLIT_SIZES = ([W_GROUP] * 4 + [N_HEADS] * 2 + [W_GROUP] * 4 + [N_HEADS] * 2 +
               [W_GROUP] * 3 + [W_GROUP] * 4 + [GLA_RANK])
_OFF = np.concatenate([[0], np.cumsum(SPLIT_SIZES)])
_BIG = [0, 1, 2, 3, 6, 7, 8, 9, 12, 13, 14, 15, 16, 17, 18]
_SMALL = [4, 5, 10, 11, 19]
_GATES = [4, 5, 10, 11]
MQ, MK, MV, MO = 0, 256, 512, 768
GQ, GK, GV, GZ = 1024, 1280, 1536, 1792
SB, SC, SH = 2048, 2304, 2560
LQ, LK, LV, LR = 2816, 3072, 3328, 3584
SM = 3840
NP = SM + LANES
N_SMALL = 32
N_GATE_ROWS = 16
N_ROW_KINDS = 3
GX_I, GX_B, GX_G, GX_BETA, GX_PM = 0, 256, 512, 768, 1024
GLA_LEVELS = (32, 16, 8, 4, 2)


def _cparams(sem):
    return pltpu.CompilerParams(dimension_semantics=sem, vmem_limit_bytes=VMEM_LIMIT)


def _log1pexp_negabs(x):
    return jnp.log(1.0 + jnp.exp(-jnp.abs(x)))


def _softplus(x):
    return jnp.maximum(x, 0.0) + _log1pexp_negabs(x)


def _logsigmoid(x):
    return -(jnp.maximum(-x, 0.0) + _log1pexp_negabs(x))


def _sigmoid(x):
    return 1.0 / (1.0 + jnp.exp(-x))


def _silu(x):
    return x * _sigmoid(x)


def _rms(x, g):
    return x * lax.rsqrt(jnp.mean(x * x, axis=-1, keepdims=True) + EPS) * g


def _dot(a, b):
    return jnp.dot(a, b, preferred_element_type=F32)


def _dot_nt(a, b):
    return lax.dot_general(a, b, (((1,), (1,)), ((), ())), preferred_element_type=F32)


def _dot_tn(a, b):
    return lax.dot_general(a, b, (((0,), (0,)), ((), ())), preferred_element_type=F32)


def _split2(x):
    x1 = x.astype(BF16)
    return x1, (x - x1.astype(F32)).astype(BF16)


def _split3(x):
    x1 = x.astype(BF16)
    r1 = x - x1.astype(F32)
    x2 = r1.astype(BF16)
    x3 = (r1 - x2.astype(F32)).astype(BF16)
    return x1, x2, x3


def _sel(m, x):
    x1, x2, x3 = _split3(x)
    return _dot(m, x1) + _dot(m, x2) + _dot(m, x3)


def _sel_r(x, m):
    n = x.shape[0]
    if n % 8:
        x1, x2, x3 = _split3(x)
        return _dot(x1, m) + _dot(x2, m) + _dot(x3, m)
    r = _dot(jnp.concatenate(_split3(x), axis=0), m)
    return r[0:n] + r[n:2 * n] + r[2 * n:3 * n]


def _sel_nt(m, x):
    x1, x2, x3 = _split3(x)
    return _dot_nt(m, x1) + _dot_nt(m, x2) + _dot_nt(m, x3)


def _mm3(x, w):
    n = x.shape[0]
    x1, x2 = _split2(x)
    w1, w2 = _split2(w)
    r = _dot(jnp.concatenate([x1, x2], axis=0), w1)
    return r[0:n] + r[n:] + _dot(x1, w2)


def _iota2(shape, dim):
    return lax.broadcasted_iota(jnp.int32, shape, dim)


def _inproj_kernel(x_ref, g_ref, w_ref, wst_ref, z_ref, zt_ref):
    hb = _rms(x_ref[...], g_ref[...]).astype(BF16)
    z_ref[...] = _dot_nt(hb, w_ref[...])
    zt = _dot_nt(wst_ref[...], hb)
    n_blk, n_kind, n_chunks, _ = zt_ref.shape
    for b in range(n_blk):
        for k in range(n_kind):
            for c in range(n_chunks):
                t0 = (b * n_chunks + c) * CHUNK
                for h in range(N_HEADS):
                    zt_ref[b, k, c:c + 1, h * HEAD_DIM:(h + 1) * HEAD_DIM] = (
                        zt[k * 4 + h:k * 4 + h + 1, t0:t0 + CHUNK])


def _inproj(x, g, w, wst, tm, lb_rows):
    t = x.shape[0]
    n_chunks = lb_rows // CHUNK
    return pl.pallas_call(
        _inproj_kernel,
        grid=(t // tm,),
        in_specs=[pl.BlockSpec((tm, D_MODEL), lambda i: (i, 0)),
                  pl.BlockSpec((1, D_MODEL), lambda i: (0, 0)),
                  pl.BlockSpec((NP, D_MODEL), lambda i: (0, 0)),
                  pl.BlockSpec((N_GATE_ROWS, D_MODEL), lambda i: (0, 0))],
        out_specs=[pl.BlockSpec((tm, NP), lambda i: (i, 0)),
                   pl.BlockSpec((tm // lb_rows, N_ROW_KINDS, n_chunks, W_GROUP), lambda i: (i, 0, 0, 0))],
        out_shape=[jax.ShapeDtypeStruct((t, NP), F32),
                   jax.ShapeDtypeStruct((t // lb_rows, N_ROW_KINDS, n_chunks, W_GROUP), F32)],
        compiler_params=_cparams(("parallel",)),
        name="inproj",
    )(x, g, w, wst)


def _gate_act(pre, idx, neg_a):
    tail = _log1pexp_negabs(pre)
    lf = -(jnp.maximum(-pre, 0.0) + tail)
    dec = neg_a * (jnp.maximum(pre, 0.0) + tail)
    beta = _sigmoid(pre)
    return jnp.where(idx < 4, pre,
                     jnp.where(idx < 8, lf,
                               jnp.where(idx < 12, dec,
                                         jnp.where(idx < 16, beta, 0.0))))


def _mix_kernel(z_ref, zt_ref, bcol_ref, acol_ref, brow_ref, arow_ref, mnorm_ref, gnorm_ref,
                lnorm_ref, gcw_ref, scw_ref, wg_ref, bg_ref,
                bdmask_ref, bdtriu_ref, tril64_ref, selexp_ref, eye64_ref,
                y_ref, c_out, n_out, m_out, sg_out, gconv_out, sconv_out, sl_out,
                c_ref, n_ref, m_ref, sg_ref, slt_ref, xg_ref, xs_ref, qkv_ref, gcum_ref, glog_ref,
                gx_ref, rw_ref, *, lb_rows, nseq):
    lb = pl.program_id(1)
    nlb = pl.num_programs(1)
    n_chunks = lb_rows // CHUNK
    wd = W_GROUP
    seqs = range(nseq)

    @pl.when(lb == 0)
    def _():
        c_ref[...] = jnp.zeros_like(c_ref)
        n_ref[...] = jnp.zeros_like(n_ref)
        m_ref[...] = jnp.zeros_like(m_ref)
        sg_ref[...] = jnp.zeros_like(sg_ref)
        slt_ref[...] = jnp.zeros_like(slt_ref)
        for i in seqs:
            xg_ref[i, 0:8, :] = jnp.zeros((8, 3 * wd), F32)
            xs_ref[i, 0:8, :] = jnp.zeros((8, wd), F32)

    ti = _iota2((CHUNK, wd), 0)
    si = _iota2((CHUNK, wd), 1) % HEAD_DIM
    incl = ti >= si
    strict = ti > si
    eye = ti == si
    blk16 = (ti // 16) == (si // 16)
    blk32 = (ti // 32) == (si // 32)
    eye_f = eye.astype(F32)
    lvl_mask = {n: ((ti // (2 * n)) == (si // (2 * n))) & ((ti // n) > (si // n))
                for n in GLA_LEVELS + (1,)}
    bdmask = bdmask_ref[...]
    bdtriu = bdtriu_ref[...]
    tril64 = tril64_ref[...]
    head_of_lane = _iota2((1, wd), 1) // HEAD_DIM
    pos_of_lane = _iota2((8, wd), 1) % HEAD_DIM

    lane_half = [(_iota2((1, LANES), 1) // HEAD_DIM == j).astype(BF16) for j in (0, 1)]
    zero_tile = jnp.zeros((HEAD_DIM, LANES), BF16)

    def bd(x):
        xb = x.astype(BF16)
        blocks = []
        for h in range(N_HEADS):
            t = h // 2
            kept = xb[:, t * LANES:(t + 1) * LANES] * lane_half[h % 2]
            blocks.append(jnp.concatenate([kept, zero_tile] if t == 0 else [zero_tile, kept], axis=1))
        return jnp.concatenate(blocks, axis=0)

    def unbd(m):
        out = m[3 * HEAD_DIM:, :]
        for h in (2, 1, 0):
            out = jnp.where(head_of_lane == h, m[h * HEAD_DIM:(h + 1) * HEAD_DIM, :], out)
        return out

    def mm(a, bmat):
        return _dot(a.astype(BF16), bmat)

    def mm_nt(a, bmat):
        return _dot_nt(a.astype(BF16), bmat)

    def mm3_bd(a, b):
        a1, a2 = _split2(a)
        b1, b2 = _split2(b)
        r = _dot(jnp.concatenate([a1, a2], axis=0), bd(b1))
        return r[0:CHUNK] + r[CHUNK:] + _dot(a1, bd(b2))

    def hsum(x):
        x1, x2 = _split2(x)
        r = _dot(jnp.concatenate([x1, x2], axis=0), bdmask)
        return r[0:x.shape[0]] + r[x.shape[0]:]

    def headnorm(o, w):
        return o * lax.rsqrt(hsum(o * o) * (1.0 / HEAD_DIM) + EPS) * w

    lane = _iota2((1, LANES), 1)
    neg_a_col = -jnp.exp(acol_ref[...])
    neg_a_row = -jnp.exp(arow_ref[2:3, :])
    selexp = selexp_ref[...]
    eye64 = eye64_ref[...]

    def l2n(x):
        return x * lax.rsqrt(hsum(x * x) + EPS)

    rows_all = nseq * lb_rows
    merge = lambda x: x.reshape(rows_all, x.shape[-1])
    split = lambda x: x.reshape(nseq, lb_rows, x.shape[-1])
    small = merge(z_ref[:, :, SM:SM + LANES])
    act = _gate_act(small + bcol_ref[...], lane, neg_a_col)
    def chunk_cumsum(x):
        n_c, width = rows_all // CHUNK, x.shape[1]
        wide = jnp.concatenate([x[c * CHUNK:(c + 1) * CHUNK, :] for c in range(n_c)], axis=1)
        cs = _sel(tril64, wide)
        return jnp.concatenate([cs[:, c * width:(c + 1) * width] for c in range(n_c)], axis=0)

    csum = chunk_cumsum(act)
    comp = jnp.where((lane >= 4) & (lane < 12), csum, act)
    gx_ref[:, :, GX_I:GX_I + 4 * wd] = split(_sel_r(comp, selexp))
    glin = _dot(small.astype(BF16), wg_ref[...]) + bg_ref[...]
    glog = _logsigmoid(glin) * (1.0 / GLA_NORMALIZER)
    glog_ref[...] = split(glog)
    gcum_ref[...] = split(chunk_cumsum(glog))

    n_r = nseq * n_chunks
    trow = lambda k: jnp.concatenate([zt_ref[i, 0, k] for i in seqs], axis=0) + brow_ref[k:k + 1, :]
    ic_r = trow(0)
    lf_r = _logsigmoid(trow(1))
    dec_r = neg_a_row * _softplus(trow(2))
    cs_r = _sel_r(jnp.concatenate([lf_r, dec_r], axis=0), bdtriu)
    c_rows = ic_r - cs_r[0:n_r, :]
    g_rows = cs_r[n_r:, :]
    pm = jnp.concatenate([c_rows] * max(1, 8 // n_r), axis=0)
    for sh in (1, 2, 4, 8, 16, 32):
        pm = jnp.where(pos_of_lane[0:1, :] >= sh, jnp.maximum(pm, pltpu.roll(pm, sh, axis=1)), pm)
    for r in range(n_r):
        rw_ref[r // n_chunks, r % n_chunks, 0:1, :] = c_rows[r:r + 1, :]
        rw_ref[r // n_chunks, r % n_chunks, 1:2, :] = g_rows[r:r + 1, :]
    diag = jnp.concatenate(
        [jnp.where(eye, jnp.broadcast_to(pm[r:r + 1, :], (CHUNK, wd)), 0.0) for r in range(n_r)], axis=0)
    gx_ref[:, :, GX_PM:GX_PM + wd] = split(_sel_r(diag, bdmask))

    xg_ref[:, 8:, :] = z_ref[:, :, GQ:GQ + 3 * wd]
    conv = xg_ref[:, 5:5 + lb_rows, :] * gcw_ref[0:1, :]
    for j in range(1, 4):
        conv = conv + xg_ref[:, 5 + j:5 + j + lb_rows, :] * gcw_ref[j:j + 1, :]
    conv = merge(_silu(conv))
    qkv_ref[:, :, 0:wd] = split(l2n(conv[:, 0:wd]) * Q_SCALE)
    qkv_ref[:, :, wd:2 * wd] = split(l2n(conv[:, wd:2 * wd]))
    qkv_ref[:, :, 2 * wd:] = split(conv[:, 2 * wd:])

    xs_ref[:, 8:, :] = z_ref[:, :, SC:SC + wd] * z_ref[:, :, SH:SH + wd]
    cu = xs_ref[:, 6:6 + lb_rows, :] * scw_ref[0:1, :]
    for j in range(1, 3):
        cu = cu + xs_ref[:, 6 + j:6 + j + lb_rows, :] * scw_ref[j:j + 1, :]
    y_ref[:, :, 2 * wd:3 * wd] = (z_ref[:, :, SB:SB + wd] * cu).astype(y_ref.dtype)


    def chunk(c, carry):
        r0 = pl.multiple_of(c * CHUNK, CHUNK)
        rs = pl.ds(r0, CHUNK)
        each = lambda f, *xs: [f(*a) for a in zip(*xs)]
        zcol = lambda o: [z_ref[i, rs, o:o + wd] for i in seqs]
        gxcol = lambda o: [gx_ref[i, rs, o:o + wd] for i in seqs]
        last = lambda xs: [x[CHUNK - 1:CHUNK, :] for x in xs]
        tn_bd = lambda a, b: each(lambda x, y: unbd(_dot_tn(x.astype(BF16), y.astype(BF16))), a, b)
        mm3s = lambda a, b: each(mm3_bd, a, b)
        rw = [rw_ref[i, c, 0:2, :] for i in seqs]

        q = zcol(MQ)
        k = [x * Q_SCALE for x in zcol(MK)]
        v = zcol(MV)
        ic_e, b_e, pm_e = gxcol(GX_I), gxcol(GX_B), gxcol(GX_PM)
        m_old = [m_ref[i, 0:1, :] for i in seqs]
        cmat = [c_ref[i] for i in seqs]
        nrow = [n_ref[i, 0:1, :] for i in seqs]
        mx = each(jnp.maximum, m_old, pm_e)
        w_int = each(lambda a, b: jnp.exp(a - b), m_old, mx)
        p = each(lambda r, m: jnp.where(incl, jnp.exp(r[0:1, :] - m), 0.0), rw, mx)
        s = each(lambda a, b, c_: mm_nt(a, bd(b)) * c_, q, k, p)
        num = each(lambda w_, a, cm, s_, v_: w_ * mm(a, bd(cm)) + mm(s_, bd(v_)), w_int, q, cmat, s, v)
        den = each(lambda w_, a, n_, s_: hsum(w_ * (a * n_) + s_), w_int, q, nrow, s)
        hh = each(lambda n_, d_, b_, m: n_ / jnp.maximum(jnp.abs(d_), jnp.exp(-(b_ + m))), num, den, b_e, mx)
        mx_last = last(mx)
        wgk = each(lambda i_, b_, ml, k_: jnp.exp((i_ - b_) - ml) * k_, ic_e, b_e, mx_last, k)
        dec = each(lambda a, b: jnp.exp(a - b), m_old, mx_last)
        c_new = each(lambda d_, cm, u_: d_ * cm + u_, dec, cmat, tn_bd(wgk, v))
        for i in seqs:
            c_ref[i] = c_new[i]
            n_ref[i] = jnp.broadcast_to(dec[i] * nrow[i] + jnp.sum(wgk[i], axis=0, keepdims=True), (8, wd))
            m_ref[i] = jnp.broadcast_to(b_e[i][CHUNK - 1:CHUNK, :] + mx_last[i], (8, wd))
        y_a = each(lambda o_, h_: _sigmoid(o_) * headnorm(h_, mnorm_ref[...]), zcol(MO), hh)
        for i in seqs:
            y_ref[i, rs, 0:wd] = y_a[i].astype(y_ref.dtype)

        q = [qkv_ref[i, rs, 0:wd] for i in seqs]
        k = [qkv_ref[i, rs, wd:2 * wd] for i in seqs]
        v = [qkv_ref[i, rs, 2 * wd:] for i in seqs]
        g_e, beta = gxcol(GX_G), gxcol(GX_BETA)
        s_old = [sg_ref[i] for i in seqs]
        bdk = each(bd, k)
        lw = each(lambda g_, r: jnp.where(incl, jnp.exp(g_ - r[1:2, :]), 0.0), g_e, rw)
        amat = each(lambda b_, k_, bk, l_: jnp.where(strict, b_ * mm_nt(k_, bk) * l_, 0.0), beta, k, bdk, lw)
        egc = each(jnp.exp, g_e)
        ad = [jnp.where(blk16, a, 0.0) for a in amat]
        a2 = mm3s(ad, ad)
        a4 = mm3s(a2, a2)
        a8 = mm3s(a4, a4)
        tinv = [eye_f - a for a in ad]
        for apow in (a2, a4, a8):
            tinv = each(lambda t_, u_: t_ + u_, tinv, mm3s(tinv, apow))
        n1 = [jnp.where(blk32 & jnp.logical_not(blk16), a, 0.0) for a in amat]
        tinv = each(lambda t_, u_: t_ - u_, tinv, mm3s(mm3s(tinv, n1), tinv))
        n2 = [jnp.where(blk32, 0.0, a) for a in amat]
        tinv = each(lambda t_, u_: t_ - u_, tinv, mm3s(mm3s(tinv, n2), tinv))
        u = mm3s(tinv, each(lambda b_, v_: b_ * v_, beta, v))
        w = mm3s(tinv, each(lambda b_, e_, k_: (b_ * e_) * k_, beta, egc, k))
        bds = each(bd, s_old)
        vn = each(lambda u_, w_, bs: u_ - mm(w_, bs), u, w, bds)
        qk = each(lambda q_, bk, l_: mm_nt(q_, bk) * l_, q, bdk, lw)
        o = each(lambda q_, e_, bs, qk_, vn_: mm(q_ * e_, bs) + mm(qk_, bd(vn_)), q, egc, bds, qk, vn)
        g_last = last(g_e)
        kdec = each(lambda k_, gl, g_: k_ * jnp.exp(gl - g_), k, g_last, g_e)
        s_new = each(lambda gl, so, u_: jnp.exp(gl) * so + u_, g_last, s_old, tn_bd(kdec, vn))
        y_b = each(lambda o_, z_: headnorm(o_, gnorm_ref[...]) * _silu(z_), o, zcol(GZ))
        for i in seqs:
            sg_ref[i] = s_new[i]
            y_ref[i, rs, wd:2 * wd] = y_b[i].astype(y_ref.dtype)

        q = [x * Q_SCALE for x in zcol(LQ)]
        k = zcol(LK)
        v = zcol(LV)
        gk = [glog_ref[i, rs, :] for i in seqs]
        gcum = [gcum_ref[i, rs, :] for i in seqs]
        st_old = [slt_ref[i] for i in seqs]
        bdk = each(bd, k)
        amat = each(lambda q_, bk: jnp.where(eye, mm_nt(q_, bk), 0.0), q, bdk)
        amat = each(lambda a_, q_, g_, bk: a_ + jnp.where(lvl_mask[1], mm_nt(q_ * jnp.exp(g_), bk), 0.0),
                    amat, q, gk, bdk)
        g_end = gcum
        g_prev = [jnp.where(ti == 0, 0.0, pltpu.roll(g_, 1, axis=0)) for g_ in gcum]
        for n in GLA_LEVELS[::-1]:
            half = n // 2
            upper = (ti % n) >= half
            g_end = [jnp.where(upper, e_, pltpu.roll(e_, CHUNK - half, axis=0)) for e_ in g_end]
            g_prev = [jnp.where(upper, pltpu.roll(p_, half, axis=0), p_) for p_ in g_prev]
            qn = each(lambda q_, gc, gp: q_ * jnp.exp(gc - gp), q, gcum, g_prev)
            kn = each(lambda k_, ge, gc: k_ * jnp.exp(ge - gc), k, g_end, gcum)
            amat = each(lambda a_, q_, k_, m=lvl_mask[n]: a_ + jnp.where(m, mm_nt(q_, bd(k_)), 0.0),
                        amat, qn, kn)
        g_last = last(gcum)
        o = each(lambda q_, gc, st, a_, v_: mm_nt(q_ * jnp.exp(gc), bd(st)) + mm(a_, bd(v_)),
                 q, gcum, st_old, amat, v)
        kdec = each(lambda k_, gl, gc: k_ * jnp.exp(gl - gc), k, g_last, gcum)
        st_new = each(lambda st, gl, u_: st * jnp.exp(gl) + u_, st_old, g_last, tn_bd(v, kdec))
        y_d = each(lambda o_, z_: headnorm(o_, lnorm_ref[...]) * _silu(z_), o, zcol(LR))
        for i in seqs:
            slt_ref[i] = st_new[i]
            y_ref[i, rs, 3 * wd:] = y_d[i].astype(y_ref.dtype)
        return carry

    lax.fori_loop(0, n_chunks, chunk, 0)

    gtail = [xg_ref[i, lb_rows:lb_rows + 8, :] for i in seqs]
    stail = [xs_ref[i, lb_rows:lb_rows + 8, :] for i in seqs]
    for i in seqs:
        xg_ref[i, 0:8, :] = gtail[i]
        xs_ref[i, 0:8, :] = stail[i]

    @pl.when(lb == nlb - 1)
    def _():
        lane4 = _iota2((1, N_HEADS), 1)
        for i in seqs:
            gconv_out[i] = gtail[i][5:8, :]
            sconv_out[i] = stail[i][6:8, :]
            m_row = jnp.zeros((1, N_HEADS), F32)
            for h in range(N_HEADS):
                hs = slice(h * HEAD_DIM, (h + 1) * HEAD_DIM)
                c_out[i, h] = c_ref[i, :, hs]
                n_out[i, h:h + 1, :] = n_ref[i, 0:1, hs]
                m_row = jnp.where(lane4 == h, m_ref[i, 0:1, h * HEAD_DIM:h * HEAD_DIM + 1], m_row)
                sg_out[i, h] = sg_ref[i, :, hs]
                sl_out[i, h] = _sel_nt(eye64, slt_ref[i, :, hs])
            m_out[i] = m_row


def _mix_consts():
    wd = W_GROUP
    r, c = np.indices((wd, wd))
    same_head = (r // HEAD_DIM) == (c // HEAD_DIM)
    bdmask = same_head
    bdtriu = same_head & ((r % HEAD_DIM) <= (c % HEAD_DIM))
    tril64 = np.tril(np.ones((CHUNK, CHUNK), bool))
    rr, cc = np.indices((LANES, 4 * wd))
    selexp = rr == 4 * (cc // wd) + (cc % wd) // HEAD_DIM
    eye64 = np.eye(CHUNK, dtype=bool)
    return [jnp.asarray(m, dtype=BF16) for m in (bdmask, bdtriu, tril64, selexp, eye64)]


def _mix_prompt(z, zt, p, bsz, seq, lb_rows, nseq):
    nlb = seq // lb_rows
    n_chunks = lb_rows // CHUNK
    row = lambda b, l: (b, l, 0)
    const2 = lambda b, l: (0, 0)
    st4 = lambda b, l: (b, 0, 0, 0)
    st3 = lambda b, l: (b, 0, 0)
    params = [p["bcol"], p["acol"], p["brow"], p["arow"], p["mnorm"], p["gnorm"], p["lnorm"],
              p["gcw"], p["scw"], p["wg"], p["bg"]] + _mix_consts()
    in_specs = [pl.BlockSpec((nseq, lb_rows, NP), row),
                pl.BlockSpec((nseq, 1, N_ROW_KINDS, n_chunks, W_GROUP), lambda b, l: (b, l, 0, 0, 0))]
    in_specs += [pl.BlockSpec(a.shape, const2) for a in params]
    hd = (N_HEADS, HEAD_DIM, HEAD_DIM)
    out_shape = [jax.ShapeDtypeStruct((bsz, seq, D_MODEL), BF16),
                 jax.ShapeDtypeStruct((bsz,) + hd, F32),
                 jax.ShapeDtypeStruct((bsz, N_HEADS, HEAD_DIM), F32),
                 jax.ShapeDtypeStruct((bsz, 1, N_HEADS), F32),
                 jax.ShapeDtypeStruct((bsz,) + hd, F32),
                 jax.ShapeDtypeStruct((bsz, 3, 3 * W_GROUP), F32),
                 jax.ShapeDtypeStruct((bsz, 2, W_GROUP), F32),
                 jax.ShapeDtypeStruct((bsz,) + hd, F32)]
    out_specs = [pl.BlockSpec((nseq, lb_rows, D_MODEL), row),
                 pl.BlockSpec((nseq,) + hd, st4),
                 pl.BlockSpec((nseq, N_HEADS, HEAD_DIM), st3),
                 pl.BlockSpec((nseq, 1, N_HEADS), st3),
                 pl.BlockSpec((nseq,) + hd, st4),
                 pl.BlockSpec((nseq, 3, 3 * W_GROUP), st3),
                 pl.BlockSpec((nseq, 2, W_GROUP), st3),
                 pl.BlockSpec((nseq,) + hd, st4)]
    vm = lambda *shape: pltpu.VMEM((nseq,) + shape, F32)
    scratch = [vm(HEAD_DIM, W_GROUP),
               vm(8, W_GROUP),
               vm(8, W_GROUP),
               vm(HEAD_DIM, W_GROUP),
               vm(HEAD_DIM, W_GROUP),
               vm(lb_rows + 8, 3 * W_GROUP),
               vm(lb_rows + 8, W_GROUP),
               vm(lb_rows, 3 * W_GROUP),
               vm(lb_rows, W_GROUP),
               vm(lb_rows, W_GROUP),
               vm(lb_rows, 5 * W_GROUP),
               vm(n_chunks, 8, W_GROUP)]
    return pl.pallas_call(
        functools.partial(_mix_kernel, lb_rows=lb_rows, nseq=nseq),
        grid=(bsz // nseq, nlb),
        in_specs=in_specs,
        out_specs=out_specs,
        out_shape=out_shape,
        scratch_shapes=scratch,
        compiler_params=_cparams(("parallel", "arbitrary")),
        name="mix_prompt",
    )(z, zt, *params)


def _inproj_s_kernel(x_ref, g_ref, wt_ref, zt_ref):
    w = wt_ref[...]
    n = w.shape[0]
    w1, w2 = _split2(w)
    h1, h2 = _split2(_rms(x_ref[...], g_ref[...]))
    r = _dot_nt(jnp.concatenate([w1, w2], axis=0), h1)
    zt_ref[...] = r[0:n] + r[n:] + _dot_nt(w1, h2)


def _inproj_s(x, g, wt_all, layer, n_blk=4):
    bn = x.shape[0]
    d_in = wt_all.shape[1]
    tn = d_in // n_blk
    return pl.pallas_call(
        _inproj_s_kernel,
        grid=(n_blk,),
        in_specs=[pl.BlockSpec((bn, D_MODEL), lambda j: (0, 0)),
                  pl.BlockSpec((1, D_MODEL), lambda j: (0, 0)),
                  pl.BlockSpec((None, tn, D_MODEL), lambda j: (layer, j, 0))],
        out_specs=pl.BlockSpec((tn, bn), lambda j: (j, 0)),
        out_shape=jax.ShapeDtypeStruct((d_in, bn), F32),
        compiler_params=_cparams(("parallel",)),
        name="inproj_s",
    )(x, g, wt_all)


def _spre_kernel(zt_ref, gbuf_ref, sbuf_ref, bcol_ref, acol_ref, gcw_ref, scw_ref, wg_ref, bg_ref,
                 qkvt_ref, glogt_ref, actt_ref, yc_ref, gbuf_out, sbuf_out):
    rows_t = lambda col, n: zt_ref[col:col + n, :].T
    small = rows_t(SM, LANES)
    lane = _iota2((1, LANES), 1)
    actt_ref[...] = _gate_act(small + bcol_ref[...], lane, -jnp.exp(acol_ref[...])).T
    glin = _mm3(small, wg_ref[...]) + bg_ref[...]
    glogt_ref[...] = (_logsigmoid(glin) * (1.0 / GLA_NORMALIZER)).T

    wq = 3 * W_GROUP
    u = rows_t(GQ, wq)
    b0, b1, b2 = gbuf_ref[:, 0:wq], gbuf_ref[:, wq:2 * wq], gbuf_ref[:, 2 * wq:]
    conv = b0 * gcw_ref[0:1, :] + b1 * gcw_ref[1:2, :] + b2 * gcw_ref[2:3, :] + u * gcw_ref[3:4, :]
    conv = _silu(conv)
    gbuf_out[:, 0:wq] = b1
    gbuf_out[:, wq:2 * wq] = b2
    gbuf_out[:, 2 * wq:] = u
    hsum = (_iota2((W_GROUP, W_GROUP), 0) // HEAD_DIM ==
            _iota2((W_GROUP, W_GROUP), 1) // HEAD_DIM).astype(BF16)

    def l2n(x):
        return x * lax.rsqrt(_sel_r(x * x, hsum) + EPS)

    qkvt_ref[0:W_GROUP, :] = (l2n(conv[:, 0:W_GROUP]) * Q_SCALE).T
    qkvt_ref[W_GROUP:2 * W_GROUP, :] = l2n(conv[:, W_GROUP:2 * W_GROUP]).T
    qkvt_ref[2 * W_GROUP:, :] = conv[:, 2 * W_GROUP:].T

    u2 = rows_t(SC, W_GROUP) * rows_t(SH, W_GROUP)
    s0, s1 = sbuf_ref[:, 0:W_GROUP], sbuf_ref[:, W_GROUP:]
    cu = s0 * scw_ref[0:1, :] + s1 * scw_ref[1:2, :] + u2 * scw_ref[2:3, :]
    yc_ref[...] = rows_t(SB, W_GROUP) * cu
    sbuf_out[:, 0:W_GROUP] = s1
    sbuf_out[:, W_GROUP:] = u2


def _sample_pre(zt, gbuf, sbuf, p):
    bn = zt.shape[1]
    sd = lambda s: jax.ShapeDtypeStruct(s, F32)
    return pl.pallas_call(
        _spre_kernel,
        out_shape=[sd((3 * W_GROUP, bn)), sd((W_GROUP, bn)), sd((LANES, bn)),
                   sd((bn, W_GROUP)), sd(gbuf.shape), sd(sbuf.shape)],
        compiler_params=pltpu.CompilerParams(vmem_limit_bytes=VMEM_LIMIT),
        name="sample_pre",
    )(zt, gbuf, sbuf, p["bcol"], p["acol"], p["gcw"], p["scw"], p["wg32"], p["bg"])


def _srec_kernel(zt_ref, qkvt_ref, glogt_ref, actt_ref, c_ref, n_ref, m_ref, sg_ref, sl_ref, norm_ref,
                 y_ref, c_out, n_out, m_out, sg_out, sl_out):
    h = pl.program_id(0)
    feat = lambda ref, col: ref[pl.ds(pl.multiple_of(col + h * HEAD_DIM, HEAD_DIM), HEAD_DIM), :]
    gate = lambda kind: actt_ref[pl.ds(4 * kind + h, 1), :]
    colsum = lambda x: jnp.sum(x, axis=0, keepdims=True)
    rows = range(HEAD_DIM)

    def headnorm(o, w):
        return o * lax.rsqrt(jnp.mean(o * o, axis=0, keepdims=True) + EPS) * w

    def contract(vecs, s_ref):
        accs = [v[0:1, :] * s_ref[0] for v in vecs]
        for d in rows[1:]:
            sd_ = s_ref[d]
            accs = [a + v[d:d + 1, :] * sd_ for a, v in zip(accs, vecs)]
        return accs

    mq, mk, mv, mo = feat(zt_ref, MQ), feat(zt_ref, MK) * Q_SCALE, feat(zt_ref, MV), feat(zt_ref, MO)
    ic, lf = gate(0), gate(1)
    nvec = n_ref[...]
    m_old = m_ref[pl.ds(h, 1), :]
    a = lf + m_old
    mt = jnp.maximum(a, ic)
    w_int = jnp.exp(a - mt)
    e_i = jnp.exp(ic - mt)
    s = colsum(mq * mk) * e_i
    (qc,) = contract([mq], c_ref)
    num = w_int * qc + s * mv
    den = w_int * colsum(mq * nvec) + s
    hh = num / jnp.maximum(jnp.abs(den), jnp.exp(-mt))
    kw = e_i * mk
    for d in rows:
        c_out[d] = w_int * c_ref[d] + kw[d:d + 1, :] * mv
    n_out[...] = w_int * nvec + kw
    m_out[...] = mt
    y_ref[0:HEAD_DIM, :] = _sigmoid(mo) * headnorm(hh, norm_ref[0])

    gq, gk, gv = feat(qkvt_ref, 0), feat(qkvt_ref, W_GROUP), feat(qkvt_ref, 2 * W_GROUP)
    beta = gate(3)
    eg = jnp.exp(gate(2))
    ks, qs = contract([gk, gq], sg_ref)
    vn = beta * gv - (beta * eg) * ks
    o = eg * qs + colsum(gq * gk) * vn
    for d in rows:
        sg_out[d] = eg * sg_ref[d] + gk[d:d + 1, :] * vn
    y_ref[HEAD_DIM:2 * HEAD_DIM, :] = headnorm(o, norm_ref[1]) * _silu(feat(zt_ref, GZ))

    lq, lk, lv = feat(zt_ref, LQ) * Q_SCALE, feat(zt_ref, LK), feat(zt_ref, LV)
    egk = jnp.exp(feat(glogt_ref, 0))
    (ql,) = contract([lq * egk], sl_ref)
    o = ql + colsum(lq * lk) * lv
    for d in rows:
        sl_out[d] = egk[d:d + 1, :] * sl_ref[d] + lk[d:d + 1, :] * lv
    y_ref[2 * HEAD_DIM:, :] = headnorm(o, norm_ref[2]) * _silu(feat(zt_ref, LR))


def _sample_rec(layer, zt, qkvt, glogt, actt, cst, nst, mst, sgst, slst, norms):
    bn = zt.shape[1]
    mat_in = pl.BlockSpec((None, None, HEAD_DIM, HEAD_DIM, bn), lambda h: (layer, h, 0, 0, 0))
    mat_out = pl.BlockSpec((None, HEAD_DIM, HEAD_DIM, bn), lambda h: (h, 0, 0, 0))
    full = lambda a: pl.BlockSpec(a.shape, lambda h: (0,) * a.ndim)
    sd = lambda *s: jax.ShapeDtypeStruct(s, F32)
    return pl.pallas_call(
        _srec_kernel,
        grid=(N_HEADS,),
        in_specs=[full(zt), full(qkvt), full(glogt), full(actt), mat_in,
                  pl.BlockSpec((None, None, HEAD_DIM, bn), lambda h: (layer, h, 0, 0)),
                  pl.BlockSpec((None, N_HEADS, bn), lambda h: (layer, 0, 0)),
                  mat_in, mat_in,
                  pl.BlockSpec((3, HEAD_DIM, bn), lambda h: (0, h, 0))],
        out_specs=[pl.BlockSpec((None, 3 * HEAD_DIM, bn), lambda h: (h, 0, 0)), mat_out,
                   pl.BlockSpec((None, HEAD_DIM, bn), lambda h: (h, 0, 0)),
                   pl.BlockSpec((None, 1, bn), lambda h: (h, 0, 0)), mat_out, mat_out],
        out_shape=[sd(N_HEADS, 3 * HEAD_DIM, bn), sd(N_HEADS, HEAD_DIM, HEAD_DIM, bn),
                   sd(N_HEADS, HEAD_DIM, bn), sd(N_HEADS, 1, bn),
                   sd(N_HEADS, HEAD_DIM, HEAD_DIM, bn), sd(N_HEADS, HEAD_DIM, HEAD_DIM, bn)],
        compiler_params=_cparams(("parallel",)),
        name="sample_rec",
    )(zt, qkvt, glogt, actt, cst, nst, mst, sgst, slst, norms)


def _outproj_s_kernel(y_ref, w_ref, x_ref, o_ref):
    o_ref[...] = x_ref[...] + _mm3(y_ref[...], w_ref[...])


def _outproj_s(y, w_all, layer, x):
    full = lambda a: pl.BlockSpec(a.shape, lambda i: (0, 0))
    return pl.pallas_call(
        _outproj_s_kernel,
        grid=(1,),
        in_specs=[full(y), pl.BlockSpec((None, D_MODEL, D_MODEL), lambda i: (layer, 0, 0)), full(x)],
        out_specs=full(x),
        out_shape=jax.ShapeDtypeStruct(x.shape, F32),
        compiler_params=_cparams(("arbitrary",)),
        name="outproj_s",
    )(y, w_all, x)


def _ffn_s_kernel(x_ref, g_ref, wg_ref, wu_ref, wd_ref, gf_ref, o_ref, h_ref, acc_ref, *, final_norm):
    j = pl.program_id(0)

    @pl.when(j == 0)
    def _():
        h_ref[...] = _rms(x_ref[...], g_ref[...])
        acc_ref[...] = jnp.zeros_like(acc_ref)

    h = h_ref[...]
    acc_ref[...] += _mm3(_silu(_mm3(h, wg_ref[...])) * _mm3(h, wu_ref[...]), wd_ref[...])

    @pl.when(j == pl.num_programs(0) - 1)
    def _():
        out = x_ref[...] + acc_ref[...]
        o_ref[...] = _rms(out, gf_ref[...]) if final_norm else out


def _ffn_s(x, g, wg, wu, wd, gf, tf, final_norm):
    t = x.shape[0]
    dff = wg.shape[1]
    full = lambda j: (0, 0)
    return pl.pallas_call(
        functools.partial(_ffn_s_kernel, final_norm=final_norm),
        grid=(dff // tf,),
        in_specs=[pl.BlockSpec((t, D_MODEL), full),
                  pl.BlockSpec((1, D_MODEL), full),
                  pl.BlockSpec((D_MODEL, tf), lambda j: (0, j)),
                  pl.BlockSpec((D_MODEL, tf), lambda j: (0, j)),
                  pl.BlockSpec((tf, D_MODEL), lambda j: (j, 0)),
                  pl.BlockSpec((1, D_MODEL), full)],
        out_specs=pl.BlockSpec((t, D_MODEL), full),
        out_shape=jax.ShapeDtypeStruct((t, D_MODEL), F32),
        scratch_shapes=[pltpu.VMEM((t, D_MODEL), F32), pltpu.VMEM((t, D_MODEL), F32)],
        compiler_params=_cparams(("arbitrary",)),
        name="ffn_s",
    )(x, g, wg, wu, wd, gf)


def _outproj_kernel(y_ref, w_ref, x_ref, o_ref):
    o_ref[...] = x_ref[...] + _dot(y_ref[...], w_ref[...].astype(BF16))


def _outproj(y, w_all, layer, x, tm):
    t = x.shape[0]
    return pl.pallas_call(
        _outproj_kernel,
        grid=(t // tm,),
        in_specs=[pl.BlockSpec((tm, D_MODEL), lambda i: (i, 0)),
                  pl.BlockSpec((None, D_MODEL, D_MODEL), lambda i: (layer, 0, 0)),
                  pl.BlockSpec((tm, D_MODEL), lambda i: (i, 0))],
        out_specs=pl.BlockSpec((tm, D_MODEL), lambda i: (i, 0)),
        out_shape=jax.ShapeDtypeStruct((t, D_MODEL), F32),
        compiler_params=_cparams(("parallel",)),
        name="outproj",
    )(y, w_all, x)


def _ffn_kernel(x_ref, y_ref, wo_ref, g_ref, wg_ref, wu_ref, wd_ref, gf_ref, o_ref,
                h_ref, acc_ref, xn_ref, *, final_norm):
    j = pl.program_id(1)

    @pl.when(j == 0)
    def _():
        xn = x_ref[...] + _dot(y_ref[...], wo_ref[...].astype(BF16))
        xn_ref[...] = xn
        h_ref[...] = _rms(xn, g_ref[...]).astype(BF16)
        acc_ref[...] = jnp.zeros_like(acc_ref)

    h = h_ref[...]
    act = _silu(_dot(h, wg_ref[...].astype(BF16))) * _dot(h, wu_ref[...].astype(BF16))
    acc_ref[...] += _dot(act.astype(BF16), wd_ref[...].astype(BF16))

    @pl.when(j == pl.num_programs(1) - 1)
    def _():
        out = xn_ref[...] + acc_ref[...]
        o_ref[...] = _rms(out, gf_ref[...]) if final_norm else out


def _ffn(x, y, wo_all, layer, g, wg_all, wu_all, wd_all, dense_idx, gf, tm, tf, final_norm):
    t = x.shape[0]
    dff = wg_all.shape[2]
    return pl.pallas_call(
        functools.partial(_ffn_kernel, final_norm=final_norm),
        grid=(t // tm, dff // tf),
        in_specs=[pl.BlockSpec((tm, D_MODEL), lambda i, j: (i, 0)),
                  pl.BlockSpec((tm, D_MODEL), lambda i, j: (i, 0)),
                  pl.BlockSpec((None, D_MODEL, D_MODEL), lambda i, j: (layer, 0, 0)),
                  pl.BlockSpec((1, D_MODEL), lambda i, j: (0, 0)),
                  pl.BlockSpec((None, D_MODEL, tf), lambda i, j: (dense_idx, 0, j)),
                  pl.BlockSpec((None, D_MODEL, tf), lambda i, j: (dense_idx, 0, j)),
                  pl.BlockSpec((None, tf, D_MODEL), lambda i, j: (dense_idx, j, 0)),
                  pl.BlockSpec((1, D_MODEL), lambda i, j: (0, 0))],
        out_specs=pl.BlockSpec((tm, D_MODEL), lambda i, j: (i, 0)),
        out_shape=jax.ShapeDtypeStruct((t, D_MODEL), F32),
        scratch_shapes=[pltpu.VMEM((tm, D_MODEL), BF16), pltpu.VMEM((tm, D_MODEL), F32),
                        pltpu.VMEM((tm, D_MODEL), F32)],
        compiler_params=_cparams(("parallel", "arbitrary")),
        name="ffn",
    )(x, y, wo_all, g, wg_all, wu_all, wd_all, gf)


def _top2_gates(logits):
    return _top2_select(logits)[0]


def _top2_select(logits):
    lane = _iota2((1, LANES), 1)
    valid = lane < N_EXPERTS
    logits = jnp.where(valid, logits, -jnp.inf)
    ex = jnp.exp(logits - jnp.max(logits, axis=-1, keepdims=True))
    probs = ex / jnp.sum(ex, axis=-1, keepdims=True)
    v1 = jnp.max(probs, axis=-1, keepdims=True)
    i1 = jnp.min(jnp.where(probs == v1, lane, LANES), axis=-1, keepdims=True)
    rest = jnp.where((lane == i1) | jnp.logical_not(valid), -1.0, probs)
    v2 = jnp.max(rest, axis=-1, keepdims=True)
    i2 = jnp.min(jnp.where(rest == v2, lane, LANES), axis=-1, keepdims=True)
    tot = v1 + v2
    gates = jnp.where(lane == i1, v1 / tot, 0.0) + jnp.where(lane == i2, v2 / tot, 0.0)
    return gates, ((lane == i1) | (lane == i2)).astype(F32)


def _moe_kernel(x_ref, g_ref, wr_ref, br_ref, wg_ref, wu_ref, wd_ref, gf_ref, o_ref,
                h_ref, acc_ref, gates_ref, *, final_norm, precise_router):
    e = pl.program_id(1)

    @pl.when(e == 0)
    def _():
        h = _rms(x_ref[...], g_ref[...])
        h_ref[...] = h.astype(BF16)
        acc_ref[...] = jnp.zeros_like(acc_ref)
        if precise_router:
            logits = _mm3(h, wr_ref[...])
        else:
            logits = _dot(h.astype(BF16), wr_ref[...].astype(BF16))
        gates_ref[...] = _top2_gates(logits + br_ref[...])

    h = h_ref[...]
    y = _dot((_silu(_dot(h, wg_ref[0])) * _dot(h, wu_ref[0])).astype(BF16), wd_ref[0])
    lane = _iota2((1, LANES), 1)
    ge = jnp.sum(jnp.where(lane == e, gates_ref[...], 0.0), axis=-1, keepdims=True)
    acc_ref[...] += ge * y

    @pl.when(e == pl.num_programs(1) - 1)
    def _():
        out = x_ref[...] + acc_ref[...]
        o_ref[...] = _rms(out, gf_ref[...]) if final_norm else out


def _moe(x, g, wr, br, wg, wu, wd, gf, tm, final_norm, precise_router):
    t = x.shape[0]
    fe = wg.shape[2]
    return pl.pallas_call(
        functools.partial(_moe_kernel, final_norm=final_norm, precise_router=precise_router),
        grid=(t // tm, N_EXPERTS),
        in_specs=[pl.BlockSpec((tm, D_MODEL), lambda i, e: (i, 0)),
                  pl.BlockSpec((1, D_MODEL), lambda i, e: (0, 0)),
                  pl.BlockSpec((D_MODEL, LANES), lambda i, e: (0, 0)),
                  pl.BlockSpec((1, LANES), lambda i, e: (0, 0)),
                  pl.BlockSpec((1, D_MODEL, fe), lambda i, e: (e, 0, 0)),
                  pl.BlockSpec((1, D_MODEL, fe), lambda i, e: (e, 0, 0)),
                  pl.BlockSpec((1, fe, D_MODEL), lambda i, e: (e, 0, 0)),
                  pl.BlockSpec((1, D_MODEL), lambda i, e: (0, 0))],
        out_specs=pl.BlockSpec((tm, D_MODEL), lambda i, e: (i, 0)),
        out_shape=jax.ShapeDtypeStruct((t, D_MODEL), F32),
        scratch_shapes=[pltpu.VMEM((tm, D_MODEL), BF16), pltpu.VMEM((tm, D_MODEL), F32),
                        pltpu.VMEM((tm, LANES), F32)],
        compiler_params=_cparams(("parallel", "arbitrary")),
        name="moe",
    )(x, g, wr, br, wg, wu, wd, gf)


def _moe_win_kernel(x_ref, g_ref, wr_ref, br_ref, tril_ref, wg_ref, wu_ref, wd_ref, gf_ref, o_ref,
                    h_ref, gate_ref, key_ref, keyt_ref, xg_ref, ys_ref, *, final_norm, cap, fewer, tc):
    e = pl.program_id(1)
    w_rows = x_ref.shape[0]
    lane = _iota2((1, LANES), 1)

    @pl.when(e == 0)
    def _():
        x = x_ref[...]
        hb = _rms(x, g_ref[...]).astype(BF16)
        h_ref[...] = hb
        o_ref[...] = x
        gates, sel = _top2_select(_dot(hb, wr_ref[...].astype(BF16)) + br_ref[...])
        rank = _dot(tril_ref[...], sel.astype(BF16))
        key = jnp.where(sel > 0.0, rank, -1.0)
        gate_ref[...] = gates
        key_ref[...] = key
        keyt_ref[...] = key.T

    key_r = keyt_ref[pl.ds(e, 1), :]
    pick_e = (_iota2((LANES, LANES), 0) == e).astype(BF16)
    key_c = _sel_r(key_ref[...], pick_e)
    gate_c = _sel_r(gate_ref[...], pick_e)
    gate_c = jnp.concatenate([gate_c] * (tc // LANES), axis=1)

    def expert(xg):
        act = (_silu(_dot(xg, wg_ref[0])) * _dot(xg, wu_ref[0])).astype(BF16)
        return _dot(act, wd_ref[0])

    n_tc = w_rows // tc
    win = LANES
    chunk_cnt = [jnp.sum((key_r[:, c * tc:(c + 1) * tc] >= 0.0).astype(F32), axis=-1,
                         keepdims=True)[0, 0].astype(jnp.int32) for c in range(n_tc)]
    starts, count = [], jnp.int32(0)
    for c in range(n_tc):
        starts.append(count)
        count = count + chunk_cnt[c]
    windowed = (count > 0) & (count <= cap)
    for c in range(n_tc):
        windowed = windowed & (chunk_cnt[c] <= win - 8)

    @pl.when(windowed)
    def _():
        xg_ref[...] = jnp.zeros_like(xg_ref)
        w_row = _iota2((win, tc), 0)
        w_col = _iota2((tc, 2 * win), 1) % win
        lo = [pl.multiple_of((starts[c] // 8) * 8, 8) for c in range(n_tc)]
        for c in range(n_tc):
            ts = slice(c * tc, (c + 1) * tc)
            gather = jnp.where((w_row + lo[c]).astype(F32) == key_r[:, ts], 1.0, 0.0).astype(BF16)
            xg_ref[pl.ds(lo[c], win), :] += _dot(gather, h_ref[ts, :])
        def run_expert(rows):
            ys_ref[0:rows, :] = expert(xg_ref[0:rows, :].astype(BF16))
            ys_ref[rows:, :] = jnp.zeros((cap + win - rows, D_MODEL), F32)

        if fewer < cap:
            pl.when(count <= fewer)(lambda: run_expert(fewer))
            pl.when(count > fewer)(lambda: run_expert(cap))
        else:
            run_expert(cap)
        for c in range(n_tc):
            ts = slice(c * tc, (c + 1) * tc)
            y12 = jnp.concatenate(_split2(ys_ref[pl.ds(lo[c], win), :]), axis=0)
            kc = jnp.concatenate([key_c[ts, :]] * (2 * win // LANES), axis=1)
            scatter = jnp.where((w_col + lo[c]).astype(F32) == kc, 1.0, 0.0).astype(BF16)
            gc = jnp.concatenate([gate_c[ts, :]] * (D_MODEL // tc), axis=1)
            o_ref[ts, :] += gc * _dot(scatter, y12)

    @pl.when(jnp.logical_not(windowed) & (count > 0))
    def _():
        key_w = jnp.concatenate([key_c] * pl.cdiv(2 * cap, LANES), axis=1)[:, :2 * cap]
        row_id = _iota2((cap, w_rows), 0).astype(F32)
        col_id = (_iota2((w_rows, 2 * cap), 1) % cap).astype(F32)

        def trip(j, carry):
            base = (j * cap).astype(F32)
            gather = jnp.where(row_id + base == key_r, 1.0, 0.0).astype(BF16)
            y = expert(_dot(gather, h_ref[...]).astype(BF16))
            y12 = jnp.concatenate(_split2(y), axis=0)
            scatter = jnp.where(col_id + base == key_w, 1.0, 0.0).astype(BF16)
            for c0 in range(0, D_MODEL, tc):
                cs = slice(c0, c0 + tc)
                o_ref[:, cs] += gate_c * _dot(scatter, y12[:, cs])
            return carry

        lax.fori_loop(0, (count + cap - 1) // cap, trip, 0)

    if final_norm:
        @pl.when(e == pl.num_programs(1) - 1)
        def _():
            o_ref[...] = _rms(o_ref[...], gf_ref[...])


def _moe_win(x, g, wr, br, wg, wu, wd, gf, w_rows, final_norm):
    t = x.shape[0]
    fe = wg.shape[2]
    sigma = (w_rows * 3 / 16) ** 0.5
    cap = max(16, int(w_rows / 4 + 3.5 * sigma) // 16 * 16)
    fewer = max(16, int(w_rows / 4 + 1.2 * sigma) // 16 * 16)
    tc = min(w_rows, 256)
    tril = jnp.asarray(np.tril(np.ones((w_rows, w_rows), np.float32), -1), dtype=BF16)
    const = lambda i, e: (0, 0)
    return pl.pallas_call(
        functools.partial(_moe_win_kernel, final_norm=final_norm, cap=cap, fewer=fewer, tc=tc),
        grid=(t // w_rows, N_EXPERTS),
        in_specs=[pl.BlockSpec((w_rows, D_MODEL), lambda i, e: (i, 0)),
                  pl.BlockSpec((1, D_MODEL), const),
                  pl.BlockSpec((D_MODEL, LANES), const),
                  pl.BlockSpec((1, LANES), const),
                  pl.BlockSpec((w_rows, w_rows), const),
                  pl.BlockSpec((1, D_MODEL, fe), lambda i, e: (e, 0, 0)),
                  pl.BlockSpec((1, D_MODEL, fe), lambda i, e: (e, 0, 0)),
                  pl.BlockSpec((1, fe, D_MODEL), lambda i, e: (e, 0, 0)),
                  pl.BlockSpec((1, D_MODEL), const)],
        out_specs=pl.BlockSpec((w_rows, D_MODEL), lambda i, e: (i, 0)),
        out_shape=jax.ShapeDtypeStruct((t, D_MODEL), F32),
        scratch_shapes=[pltpu.VMEM((w_rows, D_MODEL), BF16),
                        pltpu.VMEM((w_rows, LANES), F32),
                        pltpu.VMEM((w_rows, LANES), F32), pltpu.VMEM((LANES, w_rows), F32),
                        pltpu.VMEM((cap + LANES, D_MODEL), F32),
                        pltpu.VMEM((cap + LANES, D_MODEL), F32)],
        compiler_params=_cparams(("parallel", "arbitrary")),
        name="moe_win",
    )(x, g, wr, br, tril, wg, wu, wd, gf)


def _prep_w_in(wt):
    seg = lambda i: wt[_OFF[i]:_OFF[i + 1], :]
    pad = jnp.zeros((LANES - N_SMALL, wt.shape[1]), wt.dtype)
    wfull = jnp.concatenate([seg(i) for i in _BIG] + [seg(i) for i in _SMALL] + [pad], axis=0)
    gates = jnp.concatenate([seg(i) for i in _GATES], axis=0)
    return wfull.astype(BF16), gates.astype(BF16)


def _regroup_zt(zt):
    seg = lambda i: zt[_OFF[i]:_OFF[i + 1], :]
    pad = jnp.zeros((LANES - N_SMALL, zt.shape[1]), zt.dtype)
    return jnp.concatenate([seg(i) for i in _BIG] + [seg(i) for i in _SMALL] + [pad], axis=0)


def _pad_lanes(v, start):
    out = jnp.zeros((1, LANES), F32)
    return lax.dynamic_update_slice(out, v.reshape(1, -1).astype(F32), (0, start))


def _layer_params(l, mlstm_b_i, mlstm_b_f, mlstm_norm, gdn_conv_w, gdn_a_log, gdn_dt_bias, gdn_norm,
                  sc_conv_w, gla_w_gate, gla_b_gate, gla_norm):
    bias = jnp.concatenate([mlstm_b_i[l], mlstm_b_f[l], gdn_dt_bias[l]]).astype(F32)
    alog = gdn_a_log[l].astype(F32)
    per_lane = lambda v: jnp.repeat(v, HEAD_DIM)
    zero = jnp.zeros((W_GROUP,), F32)
    brow = jnp.stack([per_lane(mlstm_b_i[l]), per_lane(mlstm_b_f[l]), per_lane(gdn_dt_bias[l]), zero])
    arow = jnp.stack([zero, zero, per_lane(alog), zero])
    wg32 = jnp.zeros((LANES, W_GROUP), F32).at[16:16 + GLA_RANK].set(gla_w_gate[l])
    return dict(bcol=_pad_lanes(bias, 0), acol=_pad_lanes(alog, 8), brow=brow, arow=arow,
                mnorm=mlstm_norm[l].reshape(1, -1), gnorm=gdn_norm[l].reshape(1, -1),
                lnorm=gla_norm[l].reshape(1, -1),
                gcw=gdn_conv_w[l].T, scw=sc_conv_w[l].T, wg=wg32.astype(BF16), wg32=wg32,
                bg=gla_b_gate[l].reshape(1, -1))


def _pick_tile(n, pref):
    for c in pref:
        if n % c == 0:
            return c
    return n


def kernel(x_prompt, x_sample, state_mlstm_C, state_mlstm_n, state_mlstm_m, state_gdn_S, state_gdn_conv,
           state_sc_conv, state_gla_S, w_in, g_mix, mlstm_b_i, mlstm_b_f, mlstm_norm, gdn_conv_w, gdn_a_log,
           gdn_dt_bias, gdn_norm, sc_conv_w, gla_w_gate, gla_b_gate, gla_norm, w_out, g_ffn, ffn_w_gate,
           ffn_w_up, ffn_w_down, moe_w_router, moe_b_router, moe_w_gate, moe_w_up, moe_w_down, g_final):
    depth = w_in.shape[0]
    bsz, seq, _ = x_prompt.shape
    bn = x_sample.shape[0]
    tp = bsz * seq
    assert x_sample.shape[1] == 1 and seq % CHUNK == 0

    xp = x_prompt.reshape(tp, D_MODEL)
    xs = x_sample.reshape(bn, D_MODEL)
    lb_rows = _pick_tile(seq, (128, 64))
    tm_p = _pick_tile(tp, (512, 256, 128, 64))
    tm_f = _pick_tile(tp, (1024, 512, 256, 128, 64))
    tm_e = _pick_tile(tp, (1024, 512, 256, 128))
    nseq = _pick_tile(bsz, (4, 2, 1))

    s_minor = (state_mlstm_C.transpose(0, 2, 3, 4, 1), state_mlstm_n.transpose(0, 2, 3, 1),
               state_mlstm_m.transpose(0, 2, 1), state_gdn_S.transpose(0, 2, 3, 4, 1),
               state_gla_S.transpose(0, 2, 3, 4, 1))
    w_in_t = w_in.transpose(0, 2, 1)
    gfin = g_final.reshape(1, -1)

    p_states = [[] for _ in range(7)]
    s_states = [[] for _ in range(7)]
    for l in range(depth):
        p = _layer_params(l, mlstm_b_i, mlstm_b_f, mlstm_norm, gdn_conv_w, gdn_a_log, gdn_dt_bias,
                          gdn_norm, sc_conv_w, gla_w_gate, gla_b_gate, gla_norm)
        wfull, wst = _prep_w_in(w_in_t[l])
        gm = g_mix[l].reshape(1, -1)
        gf = g_ffn[l].reshape(1, -1)
        last = l == depth - 1

        z, zt = _inproj(xp, gm, wfull, wst, tm_p, lb_rows)
        y, c1, n1, m1, sg1, gb1, sb1, sl1 = _mix_prompt(
            z.reshape(bsz, seq, NP), zt.reshape((bsz, seq // lb_rows) + zt.shape[1:]), p,
            bsz, seq, lb_rows, nseq)
        y = y.reshape(tp, D_MODEL)
        if l % 2:
            xp = _outproj(y, w_out, l, xp, tm_f)
        for lst, v in zip(p_states, (c1, n1, m1.reshape(bsz, N_HEADS), sg1, gb1, sb1, sl1)):
            lst.append(v)

        zt_s = _regroup_zt(_inproj_s(xs, gm, w_in_t, l))
        gbuf = state_gdn_conv[l].reshape(bn, -1)
        sbuf = state_sc_conv[l].reshape(bn, -1)
        qkvt, glogt, actt, yc, gbuf1, sbuf1 = _sample_pre(zt_s, gbuf, sbuf, p)
        norms = jnp.stack([jnp.broadcast_to(v[:, None], (W_GROUP, bn))
                           for v in (mlstm_norm[l], gdn_norm[l], gla_norm[l])])
        ysr, c2, n2, m2, sg2, sl2 = _sample_rec(l, zt_s, qkvt, glogt, actt, *s_minor, norms)
        yt = ysr.reshape(N_HEADS, 3, HEAD_DIM, bn).transpose(3, 1, 0, 2).reshape(bn, 3, W_GROUP)
        ymix = jnp.concatenate([yt[:, 0], yt[:, 1], yc, yt[:, 2]], axis=-1)
        xs = _outproj_s(ymix, w_out, l, xs)
        for lst, v in zip(s_states, (c2, n2, m2[:, 0], sg2, gbuf1.reshape(bn, 3, -1),
                                     sbuf1.reshape(bn, 2, -1), sl2)):
            lst.append(v)

        j = l // 2
        if l % 2 == 0:
            tf = _pick_tile(ffn_w_gate.shape[2], (256, 128))
            xp = _ffn(xp, y, w_out, l, gf, ffn_w_gate, ffn_w_up, ffn_w_down, j, gfin, tm_f, tf, last)
            xs = _ffn_s(xs, gf, ffn_w_gate[j], ffn_w_up[j], ffn_w_down[j], gfin, tf, last)
        else:
            wr = jnp.zeros((D_MODEL, LANES), F32).at[:, :N_EXPERTS].set(moe_w_router[j])
            br = _pad_lanes(moe_b_router[j], 0)
            wgt, wup, wdn = (moe_w_gate[j].astype(BF16), moe_w_up[j].astype(BF16),
                             moe_w_down[j].astype(BF16))
            xp = _moe_win(xp, gf, wr, br, wgt, wup, wdn, gfin, tm_e, last)
            xs = _moe(xs, gf, wr, br, wgt, wup, wdn, gfin, bn, last, True)

    y_prompt = xp.reshape(bsz, seq, D_MODEL)
    y_sample = xs.reshape(bn, 1, D_MODEL)
    sp = [jnp.stack(v) for v in p_states]
    ss = [jnp.stack(v) for v in s_states]
    for i, perm in ((0, (0, 4, 1, 2, 3)), (1, (0, 3, 1, 2)), (2, (0, 2, 1)), (3, (0, 4, 1, 2, 3)),
                    (6, (0, 4, 1, 2, 3))):
        ss[i] = ss[i].transpose(perm)
    return (y_prompt, y_sample, sp[0], ss[0], sp[1], ss[1], sp[2], ss[2], sp[3], ss[3],
            sp[4], ss[4], sp[5], ss[5], sp[6], ss[6])
```

```python
import functools

import numpy as np
import jax
import jax.numpy as jnp
from jax import lax
from jax.experimental import pallas as pl
from jax.experimental.pallas import tpu as pltpu

F32 = jnp.float32
BF16 = jnp.bfloat16

D_MODEL = 1024
W_GROUP = 256
N_HEADS = 4
HEAD_DIM = 64
CHUNK = 64
GLA_RANK = 16
GLA_NORMALIZER = 16.0
N_EXPERTS = 8
EPS = 1e-6
Q_SCALE = HEAD_DIM ** -0.5

VMEM_LIMIT = 56 * 1024 * 1024
LANES = 128

SPLIT_SIZES = ([W_GROUP] * 4 + [N_HEADS] * 2 + [W_GROUP] * 4 + [N_HEADS] * 2 +
               [W_GROUP] * 3 + [W_GROUP] * 4 + [GLA_RANK])
_OFF = np.concatenate([[0], np.cumsum(SPLIT_SIZES)])
_BIG = [0, 1, 2, 3, 6, 7, 8, 9, 12, 13, 14, 15, 16, 17, 18]
_SMALL = [4, 5, 10, 11, 19]
_GATES = [4, 5, 10, 11]
MQ, MK, MV, MO = 0, 256, 512, 768
GQ, GK, GV, GZ = 1024, 1280, 1536, 1792
SB, SC, SH = 2048, 2304, 2560
LQ, LK, LV, LR = 2816, 3072, 3328, 3584
SM = 3840
NP = SM + LANES
N_SMALL = 32
N_GATE_ROWS = 16
N_ROW_KINDS = 3
GX_I, GX_B, GX_G, GX_BETA, GX_PM = 0, 256, 512, 768, 1024
GLA_LEVELS = (32, 16, 8, 4, 2)


def _cparams(sem):
    return pltpu.CompilerParams(dimension_semantics=sem, vmem_limit_bytes=VMEM_LIMIT)


def _log1pexp_negabs(x):
    return jnp.log(1.0 + jnp.exp(-jnp.abs(x)))


def _softplus(x):
    return jnp.maximum(x, 0.0) + _log1pexp_negabs(x)


def _logsigmoid(x):
    return -(jnp.maximum(-x, 0.0) + _log1pexp_negabs(x))


def _sigmoid(x):
    return 1.0 / (1.0 + jnp.exp(-x))


def _silu(x):
    return x * _sigmoid(x)


def _rms(x, g):
    return x * lax.rsqrt(jnp.mean(x * x, axis=-1, keepdims=True) + EPS) * g


def _dot(a, b):
    return jnp.dot(a, b, preferred_element_type=F32)


def _dot_nt(a, b):
    return lax.dot_general(a, b, (((1,), (1,)), ((), ())), preferred_element_type=F32)


def _dot_tn(a, b):
    return lax.dot_general(a, b, (((0,), (0,)), ((), ())), preferred_element_type=F32)


def _split2(x):
    x1 = x.astype(BF16)
    return x1, (x - x1.astype(F32)).astype(BF16)


def _split3(x):
    x1 = x.astype(BF16)
    r1 = x - x1.astype(F32)
    x2 = r1.astype(BF16)
    x3 = (r1 - x2.astype(F32)).astype(BF16)
    return x1, x2, x3


def _sel(m, x):
    x1, x2, x3 = _split3(x)
    return _dot(m, x1) + _dot(m, x2) + _dot(m, x3)


def _sel_r(x, m):
    n = x.shape[0]
    if n % 8:
        x1, x2, x3 = _split3(x)
        return _dot(x1, m) + _dot(x2, m) + _dot(x3, m)
    r = _dot(jnp.concatenate(_split3(x), axis=0), m)
    return r[0:n] + r[n:2 * n] + r[2 * n:3 * n]


def _sel_nt(m, x):
    x1, x2, x3 = _split3(x)
    return _dot_nt(m, x1) + _dot_nt(m, x2) + _dot_nt(m, x3)


def _mm3(x, w):
    n = x.shape[0]
    x1, x2 = _split2(x)
    w1, w2 = _split2(w)
    r = _dot(jnp.concatenate([x1, x2], axis=0), w1)
    return r[0:n] + r[n:] + _dot(x1, w2)


def _iota2(shape, dim):
    return lax.broadcasted_iota(jnp.int32, shape, dim)


def _inproj_kernel(x_ref, g_ref, w_ref, wst_ref, z_ref, zt_ref):
    hb = _rms(x_ref[...], g_ref[...]).astype(BF16)
    z_ref[...] = _dot_nt(hb, w_ref[...])
    zt = _dot_nt(wst_ref[...], hb)
    n_blk, n_kind, n_chunks, _ = zt_ref.shape
    for b in range(n_blk):
        for k in range(n_kind):
            for c in range(n_chunks):
                t0 = (b * n_chunks + c) * CHUNK
                for h in range(N_HEADS):
                    zt_ref[b, k, c:c + 1, h * HEAD_DIM:(h + 1) * HEAD_DIM] = (
                        zt[k * 4 + h:k * 4 + h + 1, t0:t0 + CHUNK])


def _inproj(x, g, w, wst, tm, lb_rows):
    t = x.shape[0]
    n_chunks = lb_rows // CHUNK
    return pl.pallas_call(
        _inproj_kernel,
        grid=(t // tm,),
        in_specs=[pl.BlockSpec((tm, D_MODEL), lambda i: (i, 0)),
                  pl.BlockSpec((1, D_MODEL), lambda i: (0, 0)),
                  pl.BlockSpec((NP, D_MODEL), lambda i: (0, 0)),
                  pl.BlockSpec((N_GATE_ROWS, D_MODEL), lambda i: (0, 0))],
        out_specs=[pl.BlockSpec((tm, NP), lambda i: (i, 0)),
                   pl.BlockSpec((tm // lb_rows, N_ROW_KINDS, n_chunks, W_GROUP), lambda i: (i, 0, 0, 0))],
        out_shape=[jax.ShapeDtypeStruct((t, NP), F32),
                   jax.ShapeDtypeStruct((t // lb_rows, N_ROW_KINDS, n_chunks, W_GROUP), F32)],
        compiler_params=_cparams(("parallel",)),
        name="inproj",
    )(x, g, w, wst)


def _gate_act(pre, idx, neg_a):
    tail = _log1pexp_negabs(pre)
    lf = -(jnp.maximum(-pre, 0.0) + tail)
    dec = neg_a * (jnp.maximum(pre, 0.0) + tail)
    beta = _sigmoid(pre)
    return jnp.where(idx < 4, pre,
                     jnp.where(idx < 8, lf,
                               jnp.where(idx < 12, dec,
                                         jnp.where(idx < 16, beta, 0.0))))


def _mix_kernel(z_ref, zt_ref, bcol_ref, acol_ref, brow_ref, arow_ref, mnorm_ref, gnorm_ref,
                lnorm_ref, gcw_ref, scw_ref, wg_ref, bg_ref,
                bdmask_ref, bdtriu_ref, tril64_ref, selexp_ref, eye64_ref,
                y_ref, c_out, n_out, m_out, sg_out, gconv_out, sconv_out, sl_out,
                c_ref, n_ref, m_ref, sg_ref, slt_ref, xg_ref, xs_ref, qkv_ref, gcum_ref, glog_ref,
                gx_ref, rw_ref, *, lb_rows, nseq):
    lb = pl.program_id(1)
    nlb = pl.num_programs(1)
    n_chunks = lb_rows // CHUNK
    wd = W_GROUP
    seqs = range(nseq)

    @pl.when(lb == 0)
    def _():
        c_ref[...] = jnp.zeros_like(c_ref)
        n_ref[...] = jnp.zeros_like(n_ref)
        m_ref[...] = jnp.zeros_like(m_ref)
        sg_ref[...] = jnp.zeros_like(sg_ref)
        slt_ref[...] = jnp.zeros_like(slt_ref)
        for i in seqs:
            xg_ref[i, 0:8, :] = jnp.zeros((8, 3 * wd), F32)
            xs_ref[i, 0:8, :] = jnp.zeros((8, wd), F32)

    ti = _iota2((CHUNK, wd), 0)
    si = _iota2((CHUNK, wd), 1) % HEAD_DIM
    incl = ti >= si
    strict = ti > si
    eye = ti == si
    blk16 = (ti // 16) == (si // 16)
    blk32 = (ti // 32) == (si // 32)
    eye_f = eye.astype(F32)
    lvl_mask = {n: ((ti // (2 * n)) == (si // (2 * n))) & ((ti // n) > (si // n))
                for n in GLA_LEVELS + (1,)}
    bdmask = bdmask_ref[...]
    bdtriu = bdtriu_ref[...]
    tril64 = tril64_ref[...]
    head_of_lane = _iota2((1, wd), 1) // HEAD_DIM
    pos_of_lane = _iota2((8, wd), 1) % HEAD_DIM

    lane_half = [(_iota2((1, LANES), 1) // HEAD_DIM == j).astype(BF16) for j in (0, 1)]
    zero_tile = jnp.zeros((HEAD_DIM, LANES), BF16)

    def bd(x):
        xb = x.astype(BF16)
        blocks = []
        for h in range(N_HEADS):
            t = h // 2
            kept = xb[:, t * LANES:(t + 1) * LANES] * lane_half[h % 2]
            blocks.append(jnp.concatenate([kept, zero_tile] if t == 0 else [zero_tile, kept], axis=1))
        return jnp.concatenate(blocks, axis=0)

    def unbd(m):
        out = m[3 * HEAD_DIM:, :]
        for h in (2, 1, 0):
            out = jnp.where(head_of_lane == h, m[h * HEAD_DIM:(h + 1) * HEAD_DIM, :], out)
        return out

    def mm(a, bmat):
        return _dot(a.astype(BF16), bmat)

    def mm_nt(a, bmat):
        return _dot_nt(a.astype(BF16), bmat)

    def mm3_bd(a, b):
        a1, a2 = _split2(a)
        b1, b2 = _split2(b)
        r = _dot(jnp.concatenate([a1, a2], axis=0), bd(b1))
        return r[0:CHUNK] + r[CHUNK:] + _dot(a1, bd(b2))

    def hsum(x):
        x1, x2 = _split2(x)
        r = _dot(jnp.concatenate([x1, x2], axis=0), bdmask)
        return r[0:x.shape[0]] + r[x.shape[0]:]

    def headnorm(o, w):
        return o * lax.rsqrt(hsum(o * o) * (1.0 / HEAD_DIM) + EPS) * w

    lane = _iota2((1, LANES), 1)
    neg_a_col = -jnp.exp(acol_ref[...])
    neg_a_row = -jnp.exp(arow_ref[2:3, :])
    selexp = selexp_ref[...]
    eye64 = eye64_ref[...]

    def l2n(x):
        return x * lax.rsqrt(hsum(x * x) + EPS)

    rows_all = nseq * lb_rows
    merge = lambda x: x.reshape(rows_all, x.shape[-1])
    split = lambda x: x.reshape(nseq, lb_rows, x.shape[-1])
    small = merge(z_ref[:, :, SM:SM + LANES])
    act = _gate_act(small + bcol_ref[...], lane, neg_a_col)
    def chunk_cumsum(x):
        n_c, width = rows_all // CHUNK, x.shape[1]
        wide = jnp.concatenate([x[c * CHUNK:(c + 1) * CHUNK, :] for c in range(n_c)], axis=1)
        cs = _sel(tril64, wide)
        return jnp.concatenate([cs[:, c * width:(c + 1) * width] for c in range(n_c)], axis=0)

    csum = chunk_cumsum(act)
    comp = jnp.where((lane >= 4) & (lane < 12), csum, act)
    gx_ref[:, :, GX_I:GX_I + 4 * wd] = split(_sel_r(comp, selexp))
    glin = _dot(small.astype(BF16), wg_ref[...]) + bg_ref[...]
    glog = _logsigmoid(glin) * (1.0 / GLA_NORMALIZER)
    glog_ref[...] = split(glog)
    gcum_ref[...] = split(chunk_cumsum(glog))

    n_r = nseq * n_chunks
    trow = lambda k: jnp.concatenate([zt_ref[i, 0, k] for i in seqs], axis=0) + brow_ref[k:k + 1, :]
    ic_r = trow(0)
    lf_r = _logsigmoid(trow(1))
    dec_r = neg_a_row * _softplus(trow(2))
    cs_r = _sel_r(jnp.concatenate([lf_r, dec_r], axis=0), bdtriu)
    c_rows = ic_r - cs_r[0:n_r, :]
    g_rows = cs_r[n_r:, :]
    pm = jnp.concatenate([c_rows] * max(1, 8 // n_r), axis=0)
    for sh in (1, 2, 4, 8, 16, 32):
        pm = jnp.where(pos_of_lane[0:1, :] >= sh, jnp.maximum(pm, pltpu.roll(pm, sh, axis=1)), pm)
    for r in range(n_r):
        rw_ref[r // n_chunks, r % n_chunks, 0:1, :] = c_rows[r:r + 1, :]
        rw_ref[r // n_chunks, r % n_chunks, 1:2, :] = g_rows[r:r + 1, :]
    diag = jnp.concatenate(
        [jnp.where(eye, jnp.broadcast_to(pm[r:r + 1, :], (CHUNK, wd)), 0.0) for r in range(n_r)], axis=0)
    gx_ref[:, :, GX_PM:GX_PM + wd] = split(_sel_r(diag, bdmask))

    xg_ref[:, 8:, :] = z_ref[:, :, GQ:GQ + 3 * wd]
    conv = xg_ref[:, 5:5 + lb_rows, :] * gcw_ref[0:1, :]
    for j in range(1, 4):
        conv = conv + xg_ref[:, 5 + j:5 + j + lb_rows, :] * gcw_ref[j:j + 1, :]
    conv = merge(_silu(conv))
    qkv_ref[:, :, 0:wd] = split(l2n(conv[:, 0:wd]) * Q_SCALE)
    qkv_ref[:, :, wd:2 * wd] = split(l2n(conv[:, wd:2 * wd]))
    qkv_ref[:, :, 2 * wd:] = split(conv[:, 2 * wd:])

    xs_ref[:, 8:, :] = z_ref[:, :, SC:SC + wd] * z_ref[:, :, SH:SH + wd]
    cu = xs_ref[:, 6:6 + lb_rows, :] * scw_ref[0:1, :]
    for j in range(1, 3):
        cu = cu + xs_ref[:, 6 + j:6 + j + lb_rows, :] * scw_ref[j:j + 1, :]
    y_ref[:, :, 2 * wd:3 * wd] = (z_ref[:, :, SB:SB + wd] * cu).astype(y_ref.dtype)


    def chunk(c, carry):
        rs = pl.ds(c * CHUNK, CHUNK)
        each = lambda f, *xs: [f(*a) for a in zip(*xs)]
        zcol = lambda o: [z_ref[i, rs, o:o + wd] for i in seqs]
        gxcol = lambda o: [gx_ref[i, rs, o:o + wd] for i in seqs]
        last = lambda xs: [x[CHUNK - 1:CHUNK, :] for x in xs]
        tn_bd = lambda a, b: each(lambda x, y: unbd(_dot_tn(x.astype(BF16), y.astype(BF16))), a, b)
        mm3s = lambda a, b: each(mm3_bd, a, b)
        rw = [rw_ref[i, c, 0:2, :] for i in seqs]

        q = zcol(MQ)
        k = [x * Q_SCALE for x in zcol(MK)]
        v = zcol(MV)
        ic_e, b_e, pm_e = gxcol(GX_I), gxcol(GX_B), gxcol(GX_PM)
        m_old = [m_ref[i, 0:1, :] for i in seqs]
        cmat = [c_ref[i] for i in seqs]
        nrow = [n_ref[i, 0:1, :] for i in seqs]
        mx = each(jnp.maximum, m_old, pm_e)
        w_int = each(lambda a, b: jnp.exp(a - b), m_old, mx)
        p = each(lambda r, m: jnp.where(incl, jnp.exp(r[0:1, :] - m), 0.0), rw, mx)
        s = each(lambda a, b, c_: mm_nt(a, bd(b)) * c_, q, k, p)
        num = each(lambda w_, a, cm, s_, v_: w_ * mm(a, bd(cm)) + mm(s_, bd(v_)), w_int, q, cmat, s, v)
        den = each(lambda w_, a, n_, s_: hsum(w_ * (a * n_) + s_), w_int, q, nrow, s)
        hh = each(lambda n_, d_, b_, m: n_ / jnp.maximum(jnp.abs(d_), jnp.exp(-(b_ + m))), num, den, b_e, mx)
        mx_last = last(mx)
        wgk = each(lambda i_, b_, ml, k_: jnp.exp((i_ - b_) - ml) * k_, ic_e, b_e, mx_last, k)
        dec = each(lambda a, b: jnp.exp(a - b), m_old, mx_last)
        c_new = each(lambda d_, cm, u_: d_ * cm + u_, dec, cmat, tn_bd(wgk, v))
        for i in seqs:
            c_ref[i] = c_new[i]
            n_ref[i] = jnp.broadcast_to(dec[i] * nrow[i] + jnp.sum(wgk[i], axis=0, keepdims=True), (8, wd))
            m_ref[i] = jnp.broadcast_to(b_e[i][CHUNK - 1:CHUNK, :] + mx_last[i], (8, wd))
        y_a = each(lambda o_, h_: _sigmoid(o_) * headnorm(h_, mnorm_ref[...]), zcol(MO), hh)
        for i in seqs:
            y_ref[i, rs, 0:wd] = y_a[i].astype(y_ref.dtype)

        q = [qkv_ref[i, rs, 0:wd] for i in seqs]
        k = [qkv_ref[i, rs, wd:2 * wd] for i in seqs]
        v = [qkv_ref[i, rs, 2 * wd:] for i in seqs]
        g_e, beta = gxcol(GX_G), gxcol(GX_BETA)
        s_old = [sg_ref[i] for i in seqs]
        bdk = each(bd, k)
        lw = each(lambda g_, r: jnp.where(incl, jnp.exp(g_ - r[1:2, :]), 0.0), g_e, rw)
        amat = each(lambda b_, k_, bk, l_: jnp.where(strict, b_ * mm_nt(k_, bk) * l_, 0.0), beta, k, bdk, lw)
        egc = each(jnp.exp, g_e)
        ad = [jnp.where(blk16, a, 0.0) for a in amat]
        a2 = mm3s(ad, ad)
        a4 = mm3s(a2, a2)
        a8 = mm3s(a4, a4)
        tinv = [eye_f - a for a in ad]
        for apow in (a2, a4, a8):
            tinv = each(lambda t_, u_: t_ + u_, tinv, mm3s(tinv, apow))
        n1 = [jnp.where(blk32 & jnp.logical_not(blk16), a, 0.0) for a in amat]
        tinv = each(lambda t_, u_: t_ - u_, tinv, mm3s(mm3s(tinv, n1), tinv))
        n2 = [jnp.where(blk32, 0.0, a) for a in amat]
        tinv = each(lambda t_, u_: t_ - u_, tinv, mm3s(mm3s(tinv, n2), tinv))
        u = mm3s(tinv, each(lambda b_, v_: b_ * v_, beta, v))
        w = mm3s(tinv, each(lambda b_, e_, k_: (b_ * e_) * k_, beta, egc, k))
        bds = each(bd, s_old)
        vn = each(lambda u_, w_, bs: u_ - mm(w_, bs), u, w, bds)
        qk = each(lambda q_, bk, l_: mm_nt(q_, bk) * l_, q, bdk, lw)
        o = each(lambda q_, e_, bs, qk_, vn_: mm(q_ * e_, bs) + mm(qk_, bd(vn_)), q, egc, bds, qk, vn)
        g_last = last(g_e)
        kdec = each(lambda k_, gl, g_: k_ * jnp.exp(gl - g_), k, g_last, g_e)
        s_new = each(lambda gl, so, u_: jnp.exp(gl) * so + u_, g_last, s_old, tn_bd(kdec, vn))
        y_b = each(lambda o_, z_: headnorm(o_, gnorm_ref[...]) * _silu(z_), o, zcol(GZ))
        for i in seqs:
            sg_ref[i] = s_new[i]
            y_ref[i, rs, wd:2 * wd] = y_b[i].astype(y_ref.dtype)

        q = [x * Q_SCALE for x in zcol(LQ)]
        k = zcol(LK)
        v = zcol(LV)
        gk = [glog_ref[i, rs, :] for i in seqs]
        gcum = [gcum_ref[i, rs, :] for i in seqs]
        st_old = [slt_ref[i] for i in seqs]
        bdk = each(bd, k)
        amat = each(lambda q_, bk: jnp.where(eye, mm_nt(q_, bk), 0.0), q, bdk)
        amat = each(lambda a_, q_, g_, bk: a_ + jnp.where(lvl_mask[1], mm_nt(q_ * jnp.exp(g_), bk), 0.0),
                    amat, q, gk, bdk)
        g_end = gcum
        g_prev = [jnp.where(ti == 0, 0.0, pltpu.roll(g_, 1, axis=0)) for g_ in gcum]
        for n in GLA_LEVELS[::-1]:
            half = n // 2
            upper = (ti % n) >= half
            g_end = [jnp.where(upper, e_, pltpu.roll(e_, CHUNK - half, axis=0)) for e_ in g_end]
            g_prev = [jnp.where(upper, pltpu.roll(p_, half, axis=0), p_) for p_ in g_prev]
            qn = each(lambda q_, gc, gp: q_ * jnp.exp(gc - gp), q, gcum, g_prev)
            kn = each(lambda k_, ge, gc: k_ * jnp.exp(ge - gc), k, g_end, gcum)
            amat = each(lambda a_, q_, k_, m=lvl_mask[n]: a_ + jnp.where(m, mm_nt(q_, bd(k_)), 0.0),
                        amat, qn, kn)
        g_last = last(gcum)
        o = each(lambda q_, gc, st, a_, v_: mm_nt(q_ * jnp.exp(gc), bd(st)) + mm(a_, bd(v_)),
                 q, gcum, st_old, amat, v)
        kdec = each(lambda k_, gl, gc: k_ * jnp.exp(gl - gc), k, g_last, gcum)
        st_new = each(lambda st, gl, u_: st * jnp.exp(gl) + u_, st_old, g_last, tn_bd(v, kdec))
        y_d = each(lambda o_, z_: headnorm(o_, lnorm_ref[...]) * _silu(z_), o, zcol(LR))
        for i in seqs:
            slt_ref[i] = st_new[i]
            y_ref[i, rs, 3 * wd:] = y_d[i].astype(y_ref.dtype)
        return carry

    for c in range(n_chunks):
        chunk(c, 0)

    gtail = [xg_ref[i, lb_rows:lb_rows + 8, :] for i in seqs]
    stail = [xs_ref[i, lb_rows:lb_rows + 8, :] for i in seqs]
    for i in seqs:
        xg_ref[i, 0:8, :] = gtail[i]
        xs_ref[i, 0:8, :] = stail[i]

    @pl.when(lb == nlb - 1)
    def _():
        lane4 = _iota2((1, N_HEADS), 1)
        for i in seqs:
            gconv_out[i] = gtail[i][5:8, :]
            sconv_out[i] = stail[i][6:8, :]
            m_row = jnp.zeros((1, N_HEADS), F32)
            for h in range(N_HEADS):
                hs = slice(h * HEAD_DIM, (h + 1) * HEAD_DIM)
                c_out[i, h] = c_ref[i, :, hs]
                n_out[i, h:h + 1, :] = n_ref[i, 0:1, hs]
                m_row = jnp.where(lane4 == h, m_ref[i, 0:1, h * HEAD_DIM:h * HEAD_DIM + 1], m_row)
                sg_out[i, h] = sg_ref[i, :, hs]
                sl_out[i, h] = _sel_nt(eye64, slt_ref[i, :, hs])
            m_out[i] = m_row


def _mix_consts():
    wd = W_GROUP
    r, c = np.indices((wd, wd))
    same_head = (r // HEAD_DIM) == (c // HEAD_DIM)
    bdmask = same_head
    bdtriu = same_head & ((r % HEAD_DIM) <= (c % HEAD_DIM))
    tril64 = np.tril(np.ones((CHUNK, CHUNK), bool))
    rr, cc = np.indices((LANES, 4 * wd))
    selexp = rr == 4 * (cc // wd) + (cc % wd) // HEAD_DIM
    eye64 = np.eye(CHUNK, dtype=bool)
    return [jnp.asarray(m, dtype=BF16) for m in (bdmask, bdtriu, tril64, selexp, eye64)]


def _mix_prompt(z, zt, p, bsz, seq, lb_rows, nseq):
    nlb = seq // lb_rows
    n_chunks = lb_rows // CHUNK
    row = lambda b, l: (b, l, 0)
    const2 = lambda b, l: (0, 0)
    st4 = lambda b, l: (b, 0, 0, 0)
    st3 = lambda b, l: (b, 0, 0)
    params = [p["bcol"], p["acol"], p["brow"], p["arow"], p["mnorm"], p["gnorm"], p["lnorm"],
              p["gcw"], p["scw"], p["wg"], p["bg"]] + _mix_consts()
    in_specs = [pl.BlockSpec((nseq, lb_rows, NP), row),
                pl.BlockSpec((nseq, 1, N_ROW_KINDS, n_chunks, W_GROUP), lambda b, l: (b, l, 0, 0, 0))]
    in_specs += [pl.BlockSpec(a.shape, const2) for a in params]
    hd = (N_HEADS, HEAD_DIM, HEAD_DIM)
    out_shape = [jax.ShapeDtypeStruct((bsz, seq, D_MODEL), BF16),
                 jax.ShapeDtypeStruct((bsz,) + hd, F32),
                 jax.ShapeDtypeStruct((bsz, N_HEADS, HEAD_DIM), F32),
                 jax.ShapeDtypeStruct((bsz, 1, N_HEADS), F32),
                 jax.ShapeDtypeStruct((bsz,) + hd, F32),
                 jax.ShapeDtypeStruct((bsz, 3, 3 * W_GROUP), F32),
                 jax.ShapeDtypeStruct((bsz, 2, W_GROUP), F32),
                 jax.ShapeDtypeStruct((bsz,) + hd, F32)]
    out_specs = [pl.BlockSpec((nseq, lb_rows, D_MODEL), row),
                 pl.BlockSpec((nseq,) + hd, st4),
                 pl.BlockSpec((nseq, N_HEADS, HEAD_DIM), st3),
                 pl.BlockSpec((nseq, 1, N_HEADS), st3),
                 pl.BlockSpec((nseq,) + hd, st4),
                 pl.BlockSpec((nseq, 3, 3 * W_GROUP), st3),
                 pl.BlockSpec((nseq, 2, W_GROUP), st3),
                 pl.BlockSpec((nseq,) + hd, st4)]
    vm = lambda *shape: pltpu.VMEM((nseq,) + shape, F32)
    scratch = [vm(HEAD_DIM, W_GROUP),
               vm(8, W_GROUP),
               vm(8, W_GROUP),
               vm(HEAD_DIM, W_GROUP),
               vm(HEAD_DIM, W_GROUP),
               vm(lb_rows + 8, 3 * W_GROUP),
               vm(lb_rows + 8, W_GROUP),
               vm(lb_rows, 3 * W_GROUP),
               vm(lb_rows, W_GROUP),
               vm(lb_rows, W_GROUP),
               vm(lb_rows, 5 * W_GROUP),
               vm(n_chunks, 8, W_GROUP)]
    return pl.pallas_call(
        functools.partial(_mix_kernel, lb_rows=lb_rows, nseq=nseq),
        grid=(bsz // nseq, nlb),
        in_specs=in_specs,
        out_specs=out_specs,
        out_shape=out_shape,
        scratch_shapes=scratch,
        compiler_params=_cparams(("parallel", "arbitrary")),
        name="mix_prompt",
    )(z, zt, *params)


def _inproj_s_kernel(x_ref, g_ref, wt_ref, zt_ref):
    w = wt_ref[...]
    n = w.shape[0]
    w1, w2 = _split2(w)
    h1, h2 = _split2(_rms(x_ref[...], g_ref[...]))
    r = _dot_nt(jnp.concatenate([w1, w2], axis=0), h1)
    zt_ref[...] = r[0:n] + r[n:] + _dot_nt(w1, h2)


def _inproj_s(x, g, wt_all, layer, n_blk=4):
    bn = x.shape[0]
    d_in = wt_all.shape[1]
    tn = d_in // n_blk
    return pl.pallas_call(
        _inproj_s_kernel,
        grid=(n_blk,),
        in_specs=[pl.BlockSpec((bn, D_MODEL), lambda j: (0, 0)),
                  pl.BlockSpec((1, D_MODEL), lambda j: (0, 0)),
                  pl.BlockSpec((None, tn, D_MODEL), lambda j: (layer, j, 0))],
        out_specs=pl.BlockSpec((tn, bn), lambda j: (j, 0)),
        out_shape=jax.ShapeDtypeStruct((d_in, bn), F32),
        compiler_params=_cparams(("parallel",)),
        name="inproj_s",
    )(x, g, wt_all)


def _spre_kernel(zt_ref, gbuf_ref, sbuf_ref, bcol_ref, acol_ref, gcw_ref, scw_ref, wg_ref, bg_ref,
                 qkvt_ref, glogt_ref, actt_ref, yc_ref, gbuf_out, sbuf_out):
    rows_t = lambda col, n: zt_ref[col:col + n, :].T
    small = rows_t(SM, LANES)
    lane = _iota2((1, LANES), 1)
    actt_ref[...] = _gate_act(small + bcol_ref[...], lane, -jnp.exp(acol_ref[...])).T
    glin = _mm3(small, wg_ref[...]) + bg_ref[...]
    glogt_ref[...] = (_logsigmoid(glin) * (1.0 / GLA_NORMALIZER)).T

    wq = 3 * W_GROUP
    u = rows_t(GQ, wq)
    b0, b1, b2 = gbuf_ref[:, 0:wq], gbuf_ref[:, wq:2 * wq], gbuf_ref[:, 2 * wq:]
    conv = b0 * gcw_ref[0:1, :] + b1 * gcw_ref[1:2, :] + b2 * gcw_ref[2:3, :] + u * gcw_ref[3:4, :]
    conv = _silu(conv)
    gbuf_out[:, 0:wq] = b1
    gbuf_out[:, wq:2 * wq] = b2
    gbuf_out[:, 2 * wq:] = u
    hsum = (_iota2((W_GROUP, W_GROUP), 0) // HEAD_DIM ==
            _iota2((W_GROUP, W_GROUP), 1) // HEAD_DIM).astype(BF16)

    def l2n(x):
        return x * lax.rsqrt(_sel_r(x * x, hsum) + EPS)

    qkvt_ref[0:W_GROUP, :] = (l2n(conv[:, 0:W_GROUP]) * Q_SCALE).T
    qkvt_ref[W_GROUP:2 * W_GROUP, :] = l2n(conv[:, W_GROUP:2 * W_GROUP]).T
    qkvt_ref[2 * W_GROUP:, :] = conv[:, 2 * W_GROUP:].T

    u2 = rows_t(SC, W_GROUP) * rows_t(SH, W_GROUP)
    s0, s1 = sbuf_ref[:, 0:W_GROUP], sbuf_ref[:, W_GROUP:]
    cu = s0 * scw_ref[0:1, :] + s1 * scw_ref[1:2, :] + u2 * scw_ref[2:3, :]
    yc_ref[...] = rows_t(SB, W_GROUP) * cu
    sbuf_out[:, 0:W_GROUP] = s1
    sbuf_out[:, W_GROUP:] = u2


def _sample_pre(zt, gbuf, sbuf, p):
    bn = zt.shape[1]
    sd = lambda s: jax.ShapeDtypeStruct(s, F32)
    return pl.pallas_call(
        _spre_kernel,
        out_shape=[sd((3 * W_GROUP, bn)), sd((W_GROUP, bn)), sd((LANES, bn)),
                   sd((bn, W_GROUP)), sd(gbuf.shape), sd(sbuf.shape)],
        compiler_params=pltpu.CompilerParams(vmem_limit_bytes=VMEM_LIMIT),
        name="sample_pre",
    )(zt, gbuf, sbuf, p["bcol"], p["acol"], p["gcw"], p["scw"], p["wg32"], p["bg"])


def _srec_kernel(zt_ref, qkvt_ref, glogt_ref, actt_ref, c_ref, n_ref, m_ref, sg_ref, sl_ref, norm_ref,
                 y_ref, c_out, n_out, m_out, sg_out, sl_out):
    h = pl.program_id(0)
    feat = lambda ref, col: ref[pl.ds(pl.multiple_of(col + h * HEAD_DIM, HEAD_DIM), HEAD_DIM), :]
    gate = lambda kind: actt_ref[pl.ds(4 * kind + h, 1), :]
    colsum = lambda x: jnp.sum(x, axis=0, keepdims=True)
    rows = range(HEAD_DIM)

    def headnorm(o, w):
        return o * lax.rsqrt(jnp.mean(o * o, axis=0, keepdims=True) + EPS) * w

    def contract(vecs, s_ref):
        accs = [v[0:1, :] * s_ref[0] for v in vecs]
        for d in rows[1:]:
            sd_ = s_ref[d]
            accs = [a + v[d:d + 1, :] * sd_ for a, v in zip(accs, vecs)]
        return accs

    mq, mk, mv, mo = feat(zt_ref, MQ), feat(zt_ref, MK) * Q_SCALE, feat(zt_ref, MV), feat(zt_ref, MO)
    ic, lf = gate(0), gate(1)
    nvec = n_ref[...]
    m_old = m_ref[pl.ds(h, 1), :]
    a = lf + m_old
    mt = jnp.maximum(a, ic)
    w_int = jnp.exp(a - mt)
    e_i = jnp.exp(ic - mt)
    s = colsum(mq * mk) * e_i
    (qc,) = contract([mq], c_ref)
    num = w_int * qc + s * mv
    den = w_int * colsum(mq * nvec) + s
    hh = num / jnp.maximum(jnp.abs(den), jnp.exp(-mt))
    kw = e_i * mk
    for d in rows:
        c_out[d] = w_int * c_ref[d] + kw[d:d + 1, :] * mv
    n_out[...] = w_int * nvec + kw
    m_out[...] = mt
    y_ref[0:HEAD_DIM, :] = _sigmoid(mo) * headnorm(hh, norm_ref[0])

    gq, gk, gv = feat(qkvt_ref, 0), feat(qkvt_ref, W_GROUP), feat(qkvt_ref, 2 * W_GROUP)
    beta = gate(3)
    eg = jnp.exp(gate(2))
    ks, qs = contract([gk, gq], sg_ref)
    vn = beta * gv - (beta * eg) * ks
    o = eg * qs + colsum(gq * gk) * vn
    for d in rows:
        sg_out[d] = eg * sg_ref[d] + gk[d:d + 1, :] * vn
    y_ref[HEAD_DIM:2 * HEAD_DIM, :] = headnorm(o, norm_ref[1]) * _silu(feat(zt_ref, GZ))

    lq, lk, lv = feat(zt_ref, LQ) * Q_SCALE, feat(zt_ref, LK), feat(zt_ref, LV)
    egk = jnp.exp(feat(glogt_ref, 0))
    (ql,) = contract([lq * egk], sl_ref)
    o = ql + colsum(lq * lk) * lv
    for d in rows:
        sl_out[d] = egk[d:d + 1, :] * sl_ref[d] + lk[d:d + 1, :] * lv
    y_ref[2 * HEAD_DIM:, :] = headnorm(o, norm_ref[2]) * _silu(feat(zt_ref, LR))


def _sample_rec(layer, zt, qkvt, glogt, actt, cst, nst, mst, sgst, slst, norms):
    bn = zt.shape[1]
    mat_in = pl.BlockSpec((None, None, HEAD_DIM, HEAD_DIM, bn), lambda h: (layer, h, 0, 0, 0))
    mat_out = pl.BlockSpec((None, HEAD_DIM, HEAD_DIM, bn), lambda h: (h, 0, 0, 0))
    full = lambda a: pl.BlockSpec(a.shape, lambda h: (0,) * a.ndim)
    sd = lambda *s: jax.ShapeDtypeStruct(s, F32)
    return pl.pallas_call(
        _srec_kernel,
        grid=(N_HEADS,),
        in_specs=[full(zt), full(qkvt), full(glogt), full(actt), mat_in,
                  pl.BlockSpec((None, None, HEAD_DIM, bn), lambda h: (layer, h, 0, 0)),
                  pl.BlockSpec((None, N_HEADS, bn), lambda h: (layer, 0, 0)),
                  mat_in, mat_in,
                  pl.BlockSpec((3, HEAD_DIM, bn), lambda h: (0, h, 0))],
        out_specs=[pl.BlockSpec((None, 3 * HEAD_DIM, bn), lambda h: (h, 0, 0)), mat_out,
                   pl.BlockSpec((None, HEAD_DIM, bn), lambda h: (h, 0, 0)),
                   pl.BlockSpec((None, 1, bn), lambda h: (h, 0, 0)), mat_out, mat_out],
        out_shape=[sd(N_HEADS, 3 * HEAD_DIM, bn), sd(N_HEADS, HEAD_DIM, HEAD_DIM, bn),
                   sd(N_HEADS, HEAD_DIM, bn), sd(N_HEADS, 1, bn),
                   sd(N_HEADS, HEAD_DIM, HEAD_DIM, bn), sd(N_HEADS, HEAD_DIM, HEAD_DIM, bn)],
        compiler_params=_cparams(("parallel",)),
        name="sample_rec",
    )(zt, qkvt, glogt, actt, cst, nst, mst, sgst, slst, norms)


def _outproj_s_kernel(y_ref, w_ref, x_ref, o_ref):
    o_ref[...] = x_ref[...] + _mm3(y_ref[...], w_ref[...])


def _outproj_s(y, w_all, layer, x):
    full = lambda a: pl.BlockSpec(a.shape, lambda i: (0, 0))
    return pl.pallas_call(
        _outproj_s_kernel,
        grid=(1,),
        in_specs=[full(y), pl.BlockSpec((None, D_MODEL, D_MODEL), lambda i: (layer, 0, 0)), full(x)],
        out_specs=full(x),
        out_shape=jax.ShapeDtypeStruct(x.shape, F32),
        compiler_params=_cparams(("arbitrary",)),
        name="outproj_s",
    )(y, w_all, x)


def _ffn_s_kernel(x_ref, g_ref, wg_ref, wu_ref, wd_ref, gf_ref, o_ref, h_ref, acc_ref, *, final_norm):
    j = pl.program_id(0)

    @pl.when(j == 0)
    def _():
        h_ref[...] = _rms(x_ref[...], g_ref[...])
        acc_ref[...] = jnp.zeros_like(acc_ref)

    h = h_ref[...]
    acc_ref[...] += _mm3(_silu(_mm3(h, wg_ref[...])) * _mm3(h, wu_ref[...]), wd_ref[...])

    @pl.when(j == pl.num_programs(0) - 1)
    def _():
        out = x_ref[...] + acc_ref[...]
        o_ref[...] = _rms(out, gf_ref[...]) if final_norm else out


def _ffn_s(x, g, wg, wu, wd, gf, tf, final_norm):
    t = x.shape[0]
    dff = wg.shape[1]
    full = lambda j: (0, 0)
    return pl.pallas_call(
        functools.partial(_ffn_s_kernel, final_norm=final_norm),
        grid=(dff // tf,),
        in_specs=[pl.BlockSpec((t, D_MODEL), full),
                  pl.BlockSpec((1, D_MODEL), full),
                  pl.BlockSpec((D_MODEL, tf), lambda j: (0, j)),
                  pl.BlockSpec((D_MODEL, tf), lambda j: (0, j)),
                  pl.BlockSpec((tf, D_MODEL), lambda j: (j, 0)),
                  pl.BlockSpec((1, D_MODEL), full)],
        out_specs=pl.BlockSpec((t, D_MODEL), full),
        out_shape=jax.ShapeDtypeStruct((t, D_MODEL), F32),
        scratch_shapes=[pltpu.VMEM((t, D_MODEL), F32), pltpu.VMEM((t, D_MODEL), F32)],
        compiler_params=_cparams(("arbitrary",)),
        name="ffn_s",
    )(x, g, wg, wu, wd, gf)


def _outproj_kernel(y_ref, w_ref, x_ref, o_ref):
    o_ref[...] = x_ref[...] + _dot(y_ref[...], w_ref[...].astype(BF16))


def _outproj(y, w_all, layer, x, tm):
    t = x.shape[0]
    return pl.pallas_call(
        _outproj_kernel,
        grid=(t // tm,),
        in_specs=[pl.BlockSpec((tm, D_MODEL), lambda i: (i, 0)),
                  pl.BlockSpec((None, D_MODEL, D_MODEL), lambda i: (layer, 0, 0)),
                  pl.BlockSpec((tm, D_MODEL), lambda i: (i, 0))],
        out_specs=pl.BlockSpec((tm, D_MODEL), lambda i: (i, 0)),
        out_shape=jax.ShapeDtypeStruct((t, D_MODEL), F32),
        compiler_params=_cparams(("parallel",)),
        name="outproj",
    )(y, w_all, x)


def _ffn_kernel(x_ref, y_ref, wo_ref, g_ref, wg_ref, wu_ref, wd_ref, gf_ref, o_ref,
                h_ref, acc_ref, xn_ref, *, final_norm):
    j = pl.program_id(1)

    @pl.when(j == 0)
    def _():
        xn = x_ref[...] + _dot(y_ref[...], wo_ref[...].astype(BF16))
        xn_ref[...] = xn
        h_ref[...] = _rms(xn, g_ref[...]).astype(BF16)
        acc_ref[...] = jnp.zeros_like(acc_ref)

    h = h_ref[...]
    act = _silu(_dot(h, wg_ref[...].astype(BF16))) * _dot(h, wu_ref[...].astype(BF16))
    acc_ref[...] += _dot(act.astype(BF16), wd_ref[...].astype(BF16))

    @pl.when(j == pl.num_programs(1) - 1)
    def _():
        out = xn_ref[...] + acc_ref[...]
        o_ref[...] = _rms(out, gf_ref[...]) if final_norm else out


def _ffn(x, y, wo_all, layer, g, wg_all, wu_all, wd_all, dense_idx, gf, tm, tf, final_norm):
    t = x.shape[0]
    dff = wg_all.shape[2]
    return pl.pallas_call(
        functools.partial(_ffn_kernel, final_norm=final_norm),
        grid=(t // tm, dff // tf),
        in_specs=[pl.BlockSpec((tm, D_MODEL), lambda i, j: (i, 0)),
                  pl.BlockSpec((tm, D_MODEL), lambda i, j: (i, 0)),
                  pl.BlockSpec((None, D_MODEL, D_MODEL), lambda i, j: (layer, 0, 0)),
                  pl.BlockSpec((1, D_MODEL), lambda i, j: (0, 0)),
                  pl.BlockSpec((None, D_MODEL, tf), lambda i, j: (dense_idx, 0, j)),
                  pl.BlockSpec((None, D_MODEL, tf), lambda i, j: (dense_idx, 0, j)),
                  pl.BlockSpec((None, tf, D_MODEL), lambda i, j: (dense_idx, j, 0)),
                  pl.BlockSpec((1, D_MODEL), lambda i, j: (0, 0))],
        out_specs=pl.BlockSpec((tm, D_MODEL), lambda i, j: (i, 0)),
        out_shape=jax.ShapeDtypeStruct((t, D_MODEL), F32),
        scratch_shapes=[pltpu.VMEM((tm, D_MODEL), BF16), pltpu.VMEM((tm, D_MODEL), F32),
                        pltpu.VMEM((tm, D_MODEL), F32)],
        compiler_params=_cparams(("parallel", "arbitrary")),
        name="ffn",
    )(x, y, wo_all, g, wg_all, wu_all, wd_all, gf)


def _top2_gates(logits):
    return _top2_select(logits)[0]


def _top2_select(logits):
    lane = _iota2((1, LANES), 1)
    valid = lane < N_EXPERTS
    logits = jnp.where(valid, logits, -jnp.inf)
    ex = jnp.exp(logits - jnp.max(logits, axis=-1, keepdims=True))
    probs = ex / jnp.sum(ex, axis=-1, keepdims=True)
    v1 = jnp.max(probs, axis=-1, keepdims=True)
    i1 = jnp.min(jnp.where(probs == v1, lane, LANES), axis=-1, keepdims=True)
    rest = jnp.where((lane == i1) | jnp.logical_not(valid), -1.0, probs)
    v2 = jnp.max(rest, axis=-1, keepdims=True)
    i2 = jnp.min(jnp.where(rest == v2, lane, LANES), axis=-1, keepdims=True)
    tot = v1 + v2
    gates = jnp.where(lane == i1, v1 / tot, 0.0) + jnp.where(lane == i2, v2 / tot, 0.0)
    return gates, ((lane == i1) | (lane == i2)).astype(F32)


def _moe_kernel(x_ref, g_ref, wr_ref, br_ref, wg_ref, wu_ref, wd_ref, gf_ref, o_ref,
                h_ref, acc_ref, gates_ref, *, final_norm, precise_router):
    e = pl.program_id(1)

    @pl.when(e == 0)
    def _():
        h = _rms(x_ref[...], g_ref[...])
        h_ref[...] = h.astype(BF16)
        acc_ref[...] = jnp.zeros_like(acc_ref)
        if precise_router:
            logits = _mm3(h, wr_ref[...])
        else:
            logits = _dot(h.astype(BF16), wr_ref[...].astype(BF16))
        gates_ref[...] = _top2_gates(logits + br_ref[...])

    h = h_ref[...]
    y = _dot((_silu(_dot(h, wg_ref[0])) * _dot(h, wu_ref[0])).astype(BF16), wd_ref[0])
    lane = _iota2((1, LANES), 1)
    ge = jnp.sum(jnp.where(lane == e, gates_ref[...], 0.0), axis=-1, keepdims=True)
    acc_ref[...] += ge * y

    @pl.when(e == pl.num_programs(1) - 1)
    def _():
        out = x_ref[...] + acc_ref[...]
        o_ref[...] = _rms(out, gf_ref[...]) if final_norm else out


def _moe(x, g, wr, br, wg, wu, wd, gf, tm, final_norm, precise_router):
    t = x.shape[0]
    fe = wg.shape[2]
    return pl.pallas_call(
        functools.partial(_moe_kernel, final_norm=final_norm, precise_router=precise_router),
        grid=(t // tm, N_EXPERTS),
        in_specs=[pl.BlockSpec((tm, D_MODEL), lambda i, e: (i, 0)),
                  pl.BlockSpec((1, D_MODEL), lambda i, e: (0, 0)),
                  pl.BlockSpec((D_MODEL, LANES), lambda i, e: (0, 0)),
                  pl.BlockSpec((1, LANES), lambda i, e: (0, 0)),
                  pl.BlockSpec((1, D_MODEL, fe), lambda i, e: (e, 0, 0)),
                  pl.BlockSpec((1, D_MODEL, fe), lambda i, e: (e, 0, 0)),
                  pl.BlockSpec((1, fe, D_MODEL), lambda i, e: (e, 0, 0)),
                  pl.BlockSpec((1, D_MODEL), lambda i, e: (0, 0))],
        out_specs=pl.BlockSpec((tm, D_MODEL), lambda i, e: (i, 0)),
        out_shape=jax.ShapeDtypeStruct((t, D_MODEL), F32),
        scratch_shapes=[pltpu.VMEM((tm, D_MODEL), BF16), pltpu.VMEM((tm, D_MODEL), F32),
                        pltpu.VMEM((tm, LANES), F32)],
        compiler_params=_cparams(("parallel", "arbitrary")),
        name="moe",
    )(x, g, wr, br, wg, wu, wd, gf)


def _moe_win_kernel(x_ref, g_ref, wr_ref, br_ref, tril_ref, wg_ref, wu_ref, wd_ref, gf_ref, o_ref,
                    h_ref, gate_ref, key_ref, keyt_ref, xg_ref, ys_ref, *, final_norm, cap, fewer, tc):
    e = pl.program_id(1)
    w_rows = x_ref.shape[0]
    lane = _iota2((1, LANES), 1)

    @pl.when(e == 0)
    def _():
        x = x_ref[...]
        hb = _rms(x, g_ref[...]).astype(BF16)
        h_ref[...] = hb
        o_ref[...] = x
        gates, sel = _top2_select(_dot(hb, wr_ref[...].astype(BF16)) + br_ref[...])
        rank = _dot(tril_ref[...], sel.astype(BF16))
        key = jnp.where(sel > 0.0, rank, -1.0)
        gate_ref[...] = gates
        key_ref[...] = key
        keyt_ref[...] = key.T

    key_r = keyt_ref[pl.ds(e, 1), :]
    pick_e = (_iota2((LANES, LANES), 0) == e).astype(BF16)
    key_c = _sel_r(key_ref[...], pick_e)
    gate_c = _sel_r(gate_ref[...], pick_e)
    gate_c = jnp.concatenate([gate_c] * (tc // LANES), axis=1)

    def expert(xg):
        act = (_silu(_dot(xg, wg_ref[0])) * _dot(xg, wu_ref[0])).astype(BF16)
        return _dot(act, wd_ref[0])

    n_tc = w_rows // tc
    win = LANES
    chunk_cnt = [jnp.sum((key_r[:, c * tc:(c + 1) * tc] >= 0.0).astype(F32), axis=-1,
                         keepdims=True)[0, 0].astype(jnp.int32) for c in range(n_tc)]
    starts, count = [], jnp.int32(0)
    for c in range(n_tc):
        starts.append(count)
        count = count + chunk_cnt[c]
    windowed = (count > 0) & (count <= cap)
    for c in range(n_tc):
        windowed = windowed & (chunk_cnt[c] <= win - 8)

    @pl.when(windowed)
    def _():
        xg_ref[...] = jnp.zeros_like(xg_ref)
        w_row = _iota2((win, tc), 0)
        w_col = _iota2((tc, 2 * win), 1) % win
        lo = [pl.multiple_of((starts[c] // 8) * 8, 8) for c in range(n_tc)]
        for c in range(n_tc):
            ts = slice(c * tc, (c + 1) * tc)
            gather = jnp.where((w_row + lo[c]).astype(F32) == key_r[:, ts], 1.0, 0.0).astype(BF16)
            xg_ref[pl.ds(lo[c], win), :] += _dot(gather, h_ref[ts, :])
        def run_expert(rows):
            ys_ref[0:rows, :] = expert(xg_ref[0:rows, :].astype(BF16))
            ys_ref[rows:, :] = jnp.zeros((cap + win - rows, D_MODEL), F32)

        if fewer < cap:
            pl.when(count <= fewer)(lambda: run_expert(fewer))
            pl.when(count > fewer)(lambda: run_expert(cap))
        else:
            run_expert(cap)
        for c in range(n_tc):
            ts = slice(c * tc, (c + 1) * tc)
            y12 = jnp.concatenate(_split2(ys_ref[pl.ds(lo[c], win), :]), axis=0)
            kc = jnp.concatenate([key_c[ts, :]] * (2 * win // LANES), axis=1)
            scatter = jnp.where((w_col + lo[c]).astype(F32) == kc, 1.0, 0.0).astype(BF16)
            gc = jnp.concatenate([gate_c[ts, :]] * (D_MODEL // tc), axis=1)
            o_ref[ts, :] += gc * _dot(scatter, y12)

    @pl.when(jnp.logical_not(windowed) & (count > 0))
    def _():
        key_w = jnp.concatenate([key_c] * pl.cdiv(2 * cap, LANES), axis=1)[:, :2 * cap]
        row_id = _iota2((cap, w_rows), 0).astype(F32)
        col_id = (_iota2((w_rows, 2 * cap), 1) % cap).astype(F32)

        def trip(j, carry):
            base = (j * cap).astype(F32)
            gather = jnp.where(row_id + base == key_r, 1.0, 0.0).astype(BF16)
            y = expert(_dot(gather, h_ref[...]).astype(BF16))
            y12 = jnp.concatenate(_split2(y), axis=0)
            scatter = jnp.where(col_id + base == key_w, 1.0, 0.0).astype(BF16)
            for c0 in range(0, D_MODEL, tc):
                cs = slice(c0, c0 + tc)
                o_ref[:, cs] += gate_c * _dot(scatter, y12[:, cs])
            return carry

        lax.fori_loop(0, (count + cap - 1) // cap, trip, 0)

    if final_norm:
        @pl.when(e == pl.num_programs(1) - 1)
        def _():
            o_ref[...] = _rms(o_ref[...], gf_ref[...])


def _moe_win(x, g, wr, br, wg, wu, wd, gf, w_rows, final_norm):
    t = x.shape[0]
    fe = wg.shape[2]
    sigma = (w_rows * 3 / 16) ** 0.5
    cap = max(16, int(w_rows / 4 + 3.5 * sigma) // 16 * 16)
    fewer = max(16, int(w_rows / 4 + 1.2 * sigma) // 16 * 16)
    tc = min(w_rows, 256)
    tril = jnp.asarray(np.tril(np.ones((w_rows, w_rows), np.float32), -1), dtype=BF16)
    const = lambda i, e: (0, 0)
    return pl.pallas_call(
        functools.partial(_moe_win_kernel, final_norm=final_norm, cap=cap, fewer=fewer, tc=tc),
        grid=(t // w_rows, N_EXPERTS),
        in_specs=[pl.BlockSpec((w_rows, D_MODEL), lambda i, e: (i, 0)),
                  pl.BlockSpec((1, D_MODEL), const),
                  pl.BlockSpec((D_MODEL, LANES), const),
                  pl.BlockSpec((1, LANES), const),
                  pl.BlockSpec((w_rows, w_rows), const),
                  pl.BlockSpec((1, D_MODEL, fe), lambda i, e: (e, 0, 0)),
                  pl.BlockSpec((1, D_MODEL, fe), lambda i, e: (e, 0, 0)),
                  pl.BlockSpec((1, fe, D_MODEL), lambda i, e: (e, 0, 0)),
                  pl.BlockSpec((1, D_MODEL), const)],
        out_specs=pl.BlockSpec((w_rows, D_MODEL), lambda i, e: (i, 0)),
        out_shape=jax.ShapeDtypeStruct((t, D_MODEL), F32),
        scratch_shapes=[pltpu.VMEM((w_rows, D_MODEL), BF16),
                        pltpu.VMEM((w_rows, LANES), F32),
                        pltpu.VMEM((w_rows, LANES), F32), pltpu.VMEM((LANES, w_rows), F32),
                        pltpu.VMEM((cap + LANES, D_MODEL), F32),
                        pltpu.VMEM((cap + LANES, D_MODEL), F32)],
        compiler_params=_cparams(("parallel", "arbitrary")),
        name="moe_win",
    )(x, g, wr, br, tril, wg, wu, wd, gf)


def _prep_w_in(wt):
    seg = lambda i: wt[_OFF[i]:_OFF[i + 1], :]
    pad = jnp.zeros((LANES - N_SMALL, wt.shape[1]), wt.dtype)
    wfull = jnp.concatenate([seg(i) for i in _BIG] + [seg(i) for i in _SMALL] + [pad], axis=0)
    gates = jnp.concatenate([seg(i) for i in _GATES], axis=0)
    return wfull.astype(BF16), gates.astype(BF16)


def _regroup_zt(zt):
    seg = lambda i: zt[_OFF[i]:_OFF[i + 1], :]
    pad = jnp.zeros((LANES - N_SMALL, zt.shape[1]), zt.dtype)
    return jnp.concatenate([seg(i) for i in _BIG] + [seg(i) for i in _SMALL] + [pad], axis=0)


def _pad_lanes(v, start):
    out = jnp.zeros((1, LANES), F32)
    return lax.dynamic_update_slice(out, v.reshape(1, -1).astype(F32), (0, start))


def _layer_params(l, mlstm_b_i, mlstm_b_f, mlstm_norm, gdn_conv_w, gdn_a_log, gdn_dt_bias, gdn_norm,
                  sc_conv_w, gla_w_gate, gla_b_gate, gla_norm):
    bias = jnp.concatenate([mlstm_b_i[l], mlstm_b_f[l], gdn_dt_bias[l]]).astype(F32)
    alog = gdn_a_log[l].astype(F32)
    per_lane = lambda v: jnp.repeat(v, HEAD_DIM)
    zero = jnp.zeros((W_GROUP,), F32)
    brow = jnp.stack([per_lane(mlstm_b_i[l]), per_lane(mlstm_b_f[l]), per_lane(gdn_dt_bias[l]), zero])
    arow = jnp.stack([zero, zero, per_lane(alog), zero])
    wg32 = jnp.zeros((LANES, W_GROUP), F32).at[16:16 + GLA_RANK].set(gla_w_gate[l])
    return dict(bcol=_pad_lanes(bias, 0), acol=_pad_lanes(alog, 8), brow=brow, arow=arow,
                mnorm=mlstm_norm[l].reshape(1, -1), gnorm=gdn_norm[l].reshape(1, -1),
                lnorm=gla_norm[l].reshape(1, -1),
                gcw=gdn_conv_w[l].T, scw=sc_conv_w[l].T, wg=wg32.astype(BF16), wg32=wg32,
                bg=gla_b_gate[l].reshape(1, -1))


def _pick_tile(n, pref):
    for c in pref:
        if n % c == 0:
            return c
    return n


def kernel(x_prompt, x_sample, state_mlstm_C, state_mlstm_n, state_mlstm_m, state_gdn_S, state_gdn_conv,
           state_sc_conv, state_gla_S, w_in, g_mix, mlstm_b_i, mlstm_b_f, mlstm_norm, gdn_conv_w, gdn_a_log,
           gdn_dt_bias, gdn_norm, sc_conv_w, gla_w_gate, gla_b_gate, gla_norm, w_out, g_ffn, ffn_w_gate,
           ffn_w_up, ffn_w_down, moe_w_router, moe_b_router, moe_w_gate, moe_w_up, moe_w_down, g_final):
    depth = w_in.shape[0]
    bsz, seq, _ = x_prompt.shape
    bn = x_sample.shape[0]
    tp = bsz * seq
    assert x_sample.shape[1] == 1 and seq % CHUNK == 0

    xp = x_prompt.reshape(tp, D_MODEL)
    xs = x_sample.reshape(bn, D_MODEL)
    lb_rows = _pick_tile(seq, (128, 64))
    tm_p = _pick_tile(tp, (512, 256, 128, 64))
    tm_f = _pick_tile(tp, (1024, 512, 256, 128, 64))
    tm_e = _pick_tile(tp, (1024, 512, 256, 128))
    nseq = _pick_tile(bsz, (4, 2, 1))

    s_minor = (state_mlstm_C.transpose(0, 2, 3, 4, 1), state_mlstm_n.transpose(0, 2, 3, 1),
               state_mlstm_m.transpose(0, 2, 1), state_gdn_S.transpose(0, 2, 3, 4, 1),
               state_gla_S.transpose(0, 2, 3, 4, 1))
    w_in_t = w_in.transpose(0, 2, 1)
    gfin = g_final.reshape(1, -1)

    p_states = [[] for _ in range(7)]
    s_states = [[] for _ in range(7)]
    for l in range(depth):
        p = _layer_params(l, mlstm_b_i, mlstm_b_f, mlstm_norm, gdn_conv_w, gdn_a_log, gdn_dt_bias,
                          gdn_norm, sc_conv_w, gla_w_gate, gla_b_gate, gla_norm)
        wfull, wst = _prep_w_in(w_in_t[l])
        gm = g_mix[l].reshape(1, -1)
        gf = g_ffn[l].reshape(1, -1)
        last = l == depth - 1

        z, zt = _inproj(xp, gm, wfull, wst, tm_p, lb_rows)
        y, c1, n1, m1, sg1, gb1, sb1, sl1 = _mix_prompt(
            z.reshape(bsz, seq, NP), zt.reshape((bsz, seq // lb_rows) + zt.shape[1:]), p,
            bsz, seq, lb_rows, nseq)
        y = y.reshape(tp, D_MODEL)
        if l % 2:
            xp = _outproj(y, w_out, l, xp, tm_f)
        for lst, v in zip(p_states, (c1, n1, m1.reshape(bsz, N_HEADS), sg1, gb1, sb1, sl1)):
            lst.append(v)

        zt_s = _regroup_zt(_inproj_s(xs, gm, w_in_t, l))
        gbuf = state_gdn_conv[l].reshape(bn, -1)
        sbuf = state_sc_conv[l].reshape(bn, -1)
        qkvt, glogt, actt, yc, gbuf1, sbuf1 = _sample_pre(zt_s, gbuf, sbuf, p)
        norms = jnp.stack([jnp.broadcast_to(v[:, None], (W_GROUP, bn))
                           for v in (mlstm_norm[l], gdn_norm[l], gla_norm[l])])
        ysr, c2, n2, m2, sg2, sl2 = _sample_rec(l, zt_s, qkvt, glogt, actt, *s_minor, norms)
        yt = ysr.reshape(N_HEADS, 3, HEAD_DIM, bn).transpose(3, 1, 0, 2).reshape(bn, 3, W_GROUP)
        ymix = jnp.concatenate([yt[:, 0], yt[:, 1], yc, yt[:, 2]], axis=-1)
        xs = _outproj_s(ymix, w_out, l, xs)
        for lst, v in zip(s_states, (c2, n2, m2[:, 0], sg2, gbuf1.reshape(bn, 3, -1),
                                     sbuf1.reshape(bn, 2, -1), sl2)):
            lst.append(v)

        j = l // 2
        if l % 2 == 0:
            tf = _pick_tile(ffn_w_gate.shape[2], (256, 128))
            xp = _ffn(xp, y, w_out, l, gf, ffn_w_gate, ffn_w_up, ffn_w_down, j, gfin, tm_f, tf, last)
            xs = _ffn_s(xs, gf, ffn_w_gate[j], ffn_w_up[j], ffn_w_down[j], gfin, tf, last)
        else:
            wr = jnp.zeros((D_MODEL, LANES), F32).at[:, :N_EXPERTS].set(moe_w_router[j])
            br = _pad_lanes(moe_b_router[j], 0)
            wgt, wup, wdn = (moe_w_gate[j].astype(BF16), moe_w_up[j].astype(BF16),
                             moe_w_down[j].astype(BF16))
            xp = _moe_win(xp, gf, wr, br, wgt, wup, wdn, gfin, tm_e, last)
            xs = _moe(xs, gf, wr, br, wgt, wup, wdn, gfin, bn, last, True)

    y_prompt = xp.reshape(bsz, seq, D_MODEL)
    y_sample = xs.reshape(bn, 1, D_MODEL)
    sp = [jnp.stack(v) for v in p_states]
    ss = [jnp.stack(v) for v in s_states]
    for i, perm in ((0, (0, 4, 1, 2, 3)), (1, (0, 3, 1, 2)), (2, (0, 2, 1)), (3, (0, 4, 1, 2, 3)),
                    (6, (0, 4, 1, 2, 3))):
        ss[i] = ss[i].transpose(perm)
    return (y_prompt, y_sample, sp[0], ss[0], sp[1], ss[1], sp[2], ss[2], sp[3], ss[3],
            sp[4], ss[4], sp[5], ss[5], sp[6], ss[6])
```

```python
import functools

import numpy as np
import jax
import jax.numpy as jnp
from jax import lax
from jax.experimental import pallas as pl
from jax.experimental.pallas import tpu as pltpu

F32 = jnp.float32
BF16 = jnp.bfloat16

D_MODEL = 1024
W_GROUP = 256
N_HEADS = 4
HEAD_DIM = 64
CHUNK = 64
GLA_RANK = 16
GLA_NORMALIZER = 16.0
N_EXPERTS = 8
EPS = 1e-6
Q_SCALE = HEAD_DIM ** -0.5

VMEM_LIMIT = 56 * 1024 * 1024
LANES = 128

SPLIT_SIZES = ([W_GROUP] * 4 + [N_HEADS] * 2 + [W_GROUP] * 4 + [N_HEADS] * 2 +
               [W_GROUP] * 3 + [W_GROUP] * 4 + [GLA_RANK])
_OFF = np.concatenate([[0], np.cumsum(SPLIT_SIZES)])
_BIG = [0, 1, 2, 3, 6, 7, 8, 9, 12, 13, 14, 15, 16, 17, 18]
_SMALL = [4, 5, 10, 11, 19]
_GATES = [4, 5, 10, 11]
MQ, MK, MV, MO = 0, 256, 512, 768
GQ, GK, GV, GZ = 1024, 1280, 1536, 1792
SB, SC, SH = 2048, 2304, 2560
LQ, LK, LV, LR = 2816, 3072, 3328, 3584
SM = 3840
NP = SM + LANES
N_SMALL = 32
N_GATE_ROWS = 16
N_ROW_KINDS = 3
GX_I, GX_B, GX_G, GX_BETA, GX_PM = 0, 256, 512, 768, 1024
GLA_LEVELS = (32, 16, 8, 4, 2)


def _cparams(sem):
    return pltpu.CompilerParams(dimension_semantics=sem, vmem_limit_bytes=VMEM_LIMIT)


def _log1pexp_negabs(x):
    return jnp.log(1.0 + jnp.exp(-jnp.abs(x)))


def _softplus(x):
    return jnp.maximum(x, 0.0) + _log1pexp_negabs(x)


def _logsigmoid(x):
    return -(jnp.maximum(-x, 0.0) + _log1pexp_negabs(x))


def _sigmoid(x):
    return 1.0 / (1.0 + jnp.exp(-x))


def _silu(x):
    return x * _sigmoid(x)


def _rms(x, g):
    return x * lax.rsqrt(jnp.mean(x * x, axis=-1, keepdims=True) + EPS) * g


def _dot(a, b):
    return jnp.dot(a, b, preferred_element_type=F32)


def _dot_nt(a, b):
    return lax.dot_general(a, b, (((1,), (1,)), ((), ())), preferred_element_type=F32)


def _dot_tn(a, b):
    return lax.dot_general(a, b, (((0,), (0,)), ((), ())), preferred_element_type=F32)


def _split2(x):
    x1 = x.astype(BF16)
    return x1, (x - x1.astype(F32)).astype(BF16)


def _split3(x):
    x1 = x.astype(BF16)
    r1 = x - x1.astype(F32)
    x2 = r1.astype(BF16)
    x3 = (r1 - x2.astype(F32)).astype(BF16)
    return x1, x2, x3


def _sel(m, x):
    x1, x2, x3 = _split3(x)
    return _dot(m, x1) + _dot(m, x2) + _dot(m, x3)


def _sel_r(x, m):
    n = x.shape[0]
    if n % 8:
        x1, x2, x3 = _split3(x)
        return _dot(x1, m) + _dot(x2, m) + _dot(x3, m)
    r = _dot(jnp.concatenate(_split3(x), axis=0), m)
    return r[0:n] + r[n:2 * n] + r[2 * n:3 * n]


def _sel_nt(m, x):
    x1, x2, x3 = _split3(x)
    return _dot_nt(m, x1) + _dot_nt(m, x2) + _dot_nt(m, x3)


def _mm3(x, w):
    n = x.shape[0]
    x1, x2 = _split2(x)
    w1, w2 = _split2(w)
    r = _dot(jnp.concatenate([x1, x2], axis=0), w1)
    return r[0:n] + r[n:] + _dot(x1, w2)


def _iota2(shape, dim):
    return lax.broadcasted_iota(jnp.int32, shape, dim)


def _inproj_kernel(x_ref, g_ref, w_ref, wst_ref, z_ref, zt_ref):
    hb = _rms(x_ref[...], g_ref[...]).astype(BF16)
    z_ref[...] = _dot_nt(hb, w_ref[...])
    zt = _dot_nt(wst_ref[...], hb)
    n_blk, n_kind, n_chunks, _ = zt_ref.shape
    for b in range(n_blk):
        for k in range(n_kind):
            for c in range(n_chunks):
                t0 = (b * n_chunks + c) * CHUNK
                for h in range(N_HEADS):
                    zt_ref[b, k, c:c + 1, h * HEAD_DIM:(h + 1) * HEAD_DIM] = (
                        zt[k * 4 + h:k * 4 + h + 1, t0:t0 + CHUNK])


def _inproj(x, g, w, wst, tm, lb_rows):
    t = x.shape[0]
    n_chunks = lb_rows // CHUNK
    return pl.pallas_call(
        _inproj_kernel,
        grid=(t // tm,),
        in_specs=[pl.BlockSpec((tm, D_MODEL), lambda i: (i, 0)),
                  pl.BlockSpec((1, D_MODEL), lambda i: (0, 0)),
                  pl.BlockSpec((NP, D_MODEL), lambda i: (0, 0)),
                  pl.BlockSpec((N_GATE_ROWS, D_MODEL), lambda i: (0, 0))],
        out_specs=[pl.BlockSpec((tm, NP), lambda i: (i, 0)),
                   pl.BlockSpec((tm // lb_rows, N_ROW_KINDS, n_chunks, W_GROUP), lambda i: (i, 0, 0, 0))],
        out_shape=[jax.ShapeDtypeStruct((t, NP), F32),
                   jax.ShapeDtypeStruct((t // lb_rows, N_ROW_KINDS, n_chunks, W_GROUP), F32)],
        compiler_params=_cparams(("parallel",)),
        name="inproj",
    )(x, g, w, wst)


def _gate_act(pre, idx, neg_a):
    tail = _log1pexp_negabs(pre)
    lf = -(jnp.maximum(-pre, 0.0) + tail)
    dec = neg_a * (jnp.maximum(pre, 0.0) + tail)
    beta = _sigmoid(pre)
    return jnp.where(idx < 4, pre,
                     jnp.where(idx < 8, lf,
                               jnp.where(idx < 12, dec,
                                         jnp.where(idx < 16, beta, 0.0))))


def _mix_kernel(z_ref, zt_ref, bcol_ref, acol_ref, brow_ref, arow_ref, mnorm_ref, gnorm_ref,
                lnorm_ref, gcw_ref, scw_ref, wg_ref, bg_ref,
                bdmask_ref, bdtriu_ref, tril64_ref, selexp_ref, eye64_ref,
                y_ref, c_out, n_out, m_out, sg_out, gconv_out, sconv_out, sl_out,
                c_ref, n_ref, m_ref, sg_ref, slt_ref, xg_ref, xs_ref, qkv_ref, gcum_ref, glog_ref,
                gx_ref, rw_ref, *, lb_rows, nseq):
    lb = pl.program_id(1)
    nlb = pl.num_programs(1)
    n_chunks = lb_rows // CHUNK
    wd = W_GROUP
    seqs = range(nseq)

    @pl.when(lb == 0)
    def _():
        c_ref[...] = jnp.zeros_like(c_ref)
        n_ref[...] = jnp.zeros_like(n_ref)
        m_ref[...] = jnp.zeros_like(m_ref)
        sg_ref[...] = jnp.zeros_like(sg_ref)
        slt_ref[...] = jnp.zeros_like(slt_ref)
        for i in seqs:
            xg_ref[i, 0:8, :] = jnp.zeros((8, 3 * wd), F32)
            xs_ref[i, 0:8, :] = jnp.zeros((8, wd), F32)

    ti = _iota2((CHUNK, wd), 0)
    si = _iota2((CHUNK, wd), 1) % HEAD_DIM
    incl = ti >= si
    strict = ti > si
    eye = ti == si
    blk16 = (ti // 16) == (si // 16)
    blk32 = (ti // 32) == (si // 32)
    eye_f = eye.astype(F32)
    lvl_mask = {n: ((ti // (2 * n)) == (si // (2 * n))) & ((ti // n) > (si // n))
                for n in GLA_LEVELS + (1,)}
    bdmask = bdmask_ref[...]
    bdtriu = bdtriu_ref[...]
    tril64 = tril64_ref[...]
    head_of_lane = _iota2((1, wd), 1) // HEAD_DIM
    pos_of_lane = _iota2((8, wd), 1) % HEAD_DIM

    lane_half = [(_iota2((1, LANES), 1) // HEAD_DIM == j).astype(BF16) for j in (0, 1)]
    zero_tile = jnp.zeros((HEAD_DIM, LANES), BF16)

    def bd(x):
        xb = x.astype(BF16)
        blocks = []
        for h in range(N_HEADS):
            t = h // 2
            kept = xb[:, t * LANES:(t + 1) * LANES] * lane_half[h % 2]
            blocks.append(jnp.concatenate([kept, zero_tile] if t == 0 else [zero_tile, kept], axis=1))
        return jnp.concatenate(blocks, axis=0)

    def unbd(m):
        out = m[3 * HEAD_DIM:, :]
        for h in (2, 1, 0):
            out = jnp.where(head_of_lane == h, m[h * HEAD_DIM:(h + 1) * HEAD_DIM, :], out)
        return out

    def mm(a, bmat):
        return _dot(a.astype(BF16), bmat)

    def mm_nt(a, bmat):
        return _dot_nt(a.astype(BF16), bmat)

    def mm3_bd(a, b):
        a1, a2 = _split2(a)
        b1, b2 = _split2(b)
        r = _dot(jnp.concatenate([a1, a2], axis=0), bd(b1))
        return r[0:CHUNK] + r[CHUNK:] + _dot(a1, bd(b2))

    def hsum(x):
        x1, x2 = _split2(x)
        r = _dot(jnp.concatenate([x1, x2], axis=0), bdmask)
        return r[0:x.shape[0]] + r[x.shape[0]:]

    def headnorm(o, w):
        return o * lax.rsqrt(hsum(o * o) * (1.0 / HEAD_DIM) + EPS) * w

    lane = _iota2((1, LANES), 1)
    neg_a_col = -jnp.exp(acol_ref[...])
    neg_a_row = -jnp.exp(arow_ref[2:3, :])
    selexp = selexp_ref[...]
    eye64 = eye64_ref[...]

    def l2n(x):
        return x * lax.rsqrt(hsum(x * x) + EPS)

    rows_all = nseq * lb_rows
    merge = lambda x: x.reshape(rows_all, x.shape[-1])
    split = lambda x: x.reshape(nseq, lb_rows, x.shape[-1])
    small = merge(z_ref[:, :, SM:SM + LANES])
    act = _gate_act(small + bcol_ref[...], lane, neg_a_col)
    def chunk_cumsum(x):
        n_c, width = rows_all // CHUNK, x.shape[1]
        wide = jnp.concatenate([x[c * CHUNK:(c + 1) * CHUNK, :] for c in range(n_c)], axis=1)
        cs = _sel(tril64, wide)
        return jnp.concatenate([cs[:, c * width:(c + 1) * width] for c in range(n_c)], axis=0)

    csum = chunk_cumsum(act)
    comp = jnp.where((lane >= 4) & (lane < 12), csum, act)
    gx_ref[:, :, GX_I:GX_I + 4 * wd] = split(_sel_r(comp, selexp))
    glin = _dot(small.astype(BF16), wg_ref[...]) + bg_ref[...]
    glog = _logsigmoid(glin) * (1.0 / GLA_NORMALIZER)
    glog_ref[...] = split(glog)
    gcum_ref[...] = split(chunk_cumsum(glog))

    n_r = nseq * n_chunks
    trow = lambda k: jnp.concatenate([zt_ref[i, 0, k] for i in seqs], axis=0) + brow_ref[k:k + 1, :]
    ic_r = trow(0)
    lf_r = _logsigmoid(trow(1))
    dec_r = neg_a_row * _softplus(trow(2))
    cs_r = _sel_r(jnp.concatenate([lf_r, dec_r], axis=0), bdtriu)
    c_rows = ic_r - cs_r[0:n_r, :]
    g_rows = cs_r[n_r:, :]
    pm = jnp.concatenate([c_rows] * max(1, 8 // n_r), axis=0)
    for sh in (1, 2, 4, 8, 16, 32):
        pm = jnp.where(pos_of_lane[0:1, :] >= sh, jnp.maximum(pm, pltpu.roll(pm, sh, axis=1)), pm)
    for r in range(n_r):
        rw_ref[r // n_chunks, r % n_chunks, 0:1, :] = c_rows[r:r + 1, :]
        rw_ref[r // n_chunks, r % n_chunks, 1:2, :] = g_rows[r:r + 1, :]
    diag = jnp.concatenate(
        [jnp.where(eye, jnp.broadcast_to(pm[r:r + 1, :], (CHUNK, wd)), 0.0) for r in range(n_r)], axis=0)
    gx_ref[:, :, GX_PM:GX_PM + wd] = split(_sel_r(diag, bdmask))

    xg_ref[:, 8:, :] = z_ref[:, :, GQ:GQ + 3 * wd]
    conv = xg_ref[:, 5:5 + lb_rows, :] * gcw_ref[0:1, :]
    for j in range(1, 4):
        conv = conv + xg_ref[:, 5 + j:5 + j + lb_rows, :] * gcw_ref[j:j + 1, :]
    conv = merge(_silu(conv))
    qkv_ref[:, :, 0:wd] = split(l2n(conv[:, 0:wd]) * Q_SCALE)
    qkv_ref[:, :, wd:2 * wd] = split(l2n(conv[:, wd:2 * wd]))
    qkv_ref[:, :, 2 * wd:] = split(conv[:, 2 * wd:])

    xs_ref[:, 8:, :] = z_ref[:, :, SC:SC + wd] * z_ref[:, :, SH:SH + wd]
    cu = xs_ref[:, 6:6 + lb_rows, :] * scw_ref[0:1, :]
    for j in range(1, 3):
        cu = cu + xs_ref[:, 6 + j:6 + j + lb_rows, :] * scw_ref[j:j + 1, :]
    y_ref[:, :, 2 * wd:3 * wd] = (z_ref[:, :, SB:SB + wd] * cu).astype(y_ref.dtype)


    def chunk(c, carry):
        rs = pl.ds(c * CHUNK, CHUNK)
        each = lambda f, *xs: [f(*a) for a in zip(*xs)]
        zcol = lambda o: [z_ref[i, rs, o:o + wd] for i in seqs]
        gxcol = lambda o: [gx_ref[i, rs, o:o + wd] for i in seqs]
        last = lambda xs: [x[CHUNK - 1:CHUNK, :] for x in xs]
        tn_bd = lambda a, b: each(lambda x, y: unbd(_dot_tn(x.astype(BF16), y.astype(BF16))), a, b)
        mm3s = lambda a, b: each(mm3_bd, a, b)
        rw = [rw_ref[i, c, 0:2, :] for i in seqs]

        q = zcol(MQ)
        k = [x * Q_SCALE for x in zcol(MK)]
        v = zcol(MV)
        ic_e, b_e, pm_e = gxcol(GX_I), gxcol(GX_B), gxcol(GX_PM)
        m_old = [m_ref[i, 0:1, :] for i in seqs]
        cmat = [c_ref[i] for i in seqs]
        nrow = [n_ref[i, 0:1, :] for i in seqs]
        mx = each(jnp.maximum, m_old, pm_e)
        w_int = each(lambda a, b: jnp.exp(a - b), m_old, mx)
        p = each(lambda r, m: jnp.where(incl, jnp.exp(r[0:1, :] - m), 0.0), rw, mx)
        s = each(lambda a, b, c_: mm_nt(a, bd(b)) * c_, q, k, p)
        num = each(lambda w_, a, cm, s_, v_: w_ * mm(a, bd(cm)) + mm(s_, bd(v_)), w_int, q, cmat, s, v)
        den = each(lambda w_, a, n_, s_: hsum(w_ * (a * n_) + s_), w_int, q, nrow, s)
        hh = each(lambda n_, d_, b_, m: n_ / jnp.maximum(jnp.abs(d_), jnp.exp(-(b_ + m))), num, den, b_e, mx)
        mx_last = last(mx)
        wgk = each(lambda i_, b_, ml, k_: jnp.exp((i_ - b_) - ml) * k_, ic_e, b_e, mx_last, k)
        dec = each(lambda a, b: jnp.exp(a - b), m_old, mx_last)
        c_new = each(lambda d_, cm, u_: d_ * cm + u_, dec, cmat, tn_bd(wgk, v))
        for i in seqs:
            c_ref[i] = c_new[i]
            n_ref[i] = jnp.broadcast_to(dec[i] * nrow[i] + jnp.sum(wgk[i], axis=0, keepdims=True), (8, wd))
            m_ref[i] = jnp.broadcast_to(b_e[i][CHUNK - 1:CHUNK, :] + mx_last[i], (8, wd))
        y_a = each(lambda o_, h_: _sigmoid(o_) * headnorm(h_, mnorm_ref[...]), zcol(MO), hh)
        for i in seqs:
            y_ref[i, rs, 0:wd] = y_a[i].astype(y_ref.dtype)

        q = [qkv_ref[i, rs, 0:wd] for i in seqs]
        k = [qkv_ref[i, rs, wd:2 * wd] for i in seqs]
        v = [qkv_ref[i, rs, 2 * wd:] for i in seqs]
        g_e, beta = gxcol(GX_G), gxcol(GX_BETA)
        s_old = [sg_ref[i] for i in seqs]
        bdk = each(bd, k)
        lw = each(lambda g_, r: jnp.where(incl, jnp.exp(g_ - r[1:2, :]), 0.0), g_e, rw)
        amat = each(lambda b_, k_, bk, l_: jnp.where(strict, b_ * mm_nt(k_, bk) * l_, 0.0), beta, k, bdk, lw)
        egc = each(jnp.exp, g_e)
        ad = [jnp.where(blk16, a, 0.0) for a in amat]
        a2 = mm3s(ad, ad)
        a4 = mm3s(a2, a2)
        a8 = mm3s(a4, a4)
        tinv = [eye_f - a for a in ad]
        for apow in (a2, a4, a8):
            tinv = each(lambda t_, u_: t_ + u_, tinv, mm3s(tinv, apow))
        n1 = [jnp.where(blk32 & jnp.logical_not(blk16), a, 0.0) for a in amat]
        tinv = each(lambda t_, u_: t_ - u_, tinv, mm3s(mm3s(tinv, n1), tinv))
        n2 = [jnp.where(blk32, 0.0, a) for a in amat]
        tinv = each(lambda t_, u_: t_ - u_, tinv, mm3s(mm3s(tinv, n2), tinv))
        u = mm3s(tinv, each(lambda b_, v_: b_ * v_, beta, v))
        w = mm3s(tinv, each(lambda b_, e_, k_: (b_ * e_) * k_, beta, egc, k))
        bds = each(bd, s_old)
        vn = each(lambda u_, w_, bs: u_ - mm(w_, bs), u, w, bds)
        qk = each(lambda q_, bk, l_: mm_nt(q_, bk) * l_, q, bdk, lw)
        o = each(lambda q_, e_, bs, qk_, vn_: mm(q_ * e_, bs) + mm(qk_, bd(vn_)), q, egc, bds, qk, vn)
        g_last = last(g_e)
        kdec = each(lambda k_, gl, g_: k_ * jnp.exp(gl - g_), k, g_last, g_e)
        s_new = each(lambda gl, so, u_: jnp.exp(gl) * so + u_, g_last, s_old, tn_bd(kdec, vn))
        y_b = each(lambda o_, z_: headnorm(o_, gnorm_ref[...]) * _silu(z_), o, zcol(GZ))
        for i in seqs:
            sg_ref[i] = s_new[i]
            y_ref[i, rs, wd:2 * wd] = y_b[i].astype(y_ref.dtype)

        q = [x * Q_SCALE for x in zcol(LQ)]
        k = zcol(LK)
        v = zcol(LV)
        gk = [glog_ref[i, rs, :] for i in seqs]
        gcum = [gcum_ref[i, rs, :] for i in seqs]
        st_old = [slt_ref[i] for i in seqs]
        bdk = each(bd, k)
        amat = each(lambda q_, bk: jnp.where(eye, mm_nt(q_, bk), 0.0), q, bdk)
        amat = each(lambda a_, q_, g_, bk: a_ + jnp.where(lvl_mask[1], mm_nt(q_ * jnp.exp(g_), bk), 0.0),
                    amat, q, gk, bdk)
        g_end = gcum
        g_prev = [jnp.where(ti == 0, 0.0, pltpu.roll(g_, 1, axis=0)) for g_ in gcum]
        for n in GLA_LEVELS[::-1]:
            half = n // 2
            upper = (ti % n) >= half
            g_end = [jnp.where(upper, e_, pltpu.roll(e_, CHUNK - half, axis=0)) for e_ in g_end]
            g_prev = [jnp.where(upper, pltpu.roll(p_, half, axis=0), p_) for p_ in g_prev]
            qn = each(lambda q_, gc, gp: q_ * jnp.exp(gc - gp), q, gcum, g_prev)
            kn = each(lambda k_, ge, gc: k_ * jnp.exp(ge - gc), k, g_end, gcum)
            amat = each(lambda a_, q_, k_, m=lvl_mask[n]: a_ + jnp.where(m, mm_nt(q_, bd(k_)), 0.0),
                        amat, qn, kn)
        g_last = last(gcum)
        o = each(lambda q_, gc, st, a_, v_: mm_nt(q_ * jnp.exp(gc), bd(st)) + mm(a_, bd(v_)),
                 q, gcum, st_old, amat, v)
        kdec = each(lambda k_, gl, gc: k_ * jnp.exp(gl - gc), k, g_last, gcum)
        st_new = each(lambda st, gl, u_: st * jnp.exp(gl) + u_, st_old, g_last, tn_bd(v, kdec))
        y_d = each(lambda o_, z_: headnorm(o_, lnorm_ref[...]) * _silu(z_), o, zcol(LR))
        for i in seqs:
            slt_ref[i] = st_new[i]
            y_ref[i, rs, 3 * wd:] = y_d[i].astype(y_ref.dtype)
        return carry

    for c in range(n_chunks):
        chunk(c, 0)

    gtail = [xg_ref[i, lb_rows:lb_rows + 8, :] for i in seqs]
    stail = [xs_ref[i, lb_rows:lb_rows + 8, :] for i in seqs]
    for i in seqs:
        xg_ref[i, 0:8, :] = gtail[i]
        xs_ref[i, 0:8, :] = stail[i]

    @pl.when(lb == nlb - 1)
    def _():
        lane4 = _iota2((1, N_HEADS), 1)
        for i in seqs:
            gconv_out[i] = gtail[i][5:8, :]
            sconv_out[i] = stail[i][6:8, :]
            m_row = jnp.zeros((1, N_HEADS), F32)
            for h in range(N_HEADS):
                hs = slice(h * HEAD_DIM, (h + 1) * HEAD_DIM)
                c_out[i, h] = c_ref[i, :, hs]
                n_out[i, h:h + 1, :] = n_ref[i, 0:1, hs]
                m_row = jnp.where(lane4 == h, m_ref[i, 0:1, h * HEAD_DIM:h * HEAD_DIM + 1], m_row)
                sg_out[i, h] = sg_ref[i, :, hs]
                sl_out[i, h] = _sel_nt(eye64, slt_ref[i, :, hs])
            m_out[i] = m_row


def _mix_consts():
    wd = W_GROUP
    r, c = np.indices((wd, wd))
    same_head = (r // HEAD_DIM) == (c // HEAD_DIM)
    bdmask = same_head
    bdtriu = same_head & ((r % HEAD_DIM) <= (c % HEAD_DIM))
    tril64 = np.tril(np.ones((CHUNK, CHUNK), bool))
    rr, cc = np.indices((LANES, 4 * wd))
    selexp = rr == 4 * (cc // wd) + (cc % wd) // HEAD_DIM
    eye64 = np.eye(CHUNK, dtype=bool)
    return [jnp.asarray(m, dtype=BF16) for m in (bdmask, bdtriu, tril64, selexp, eye64)]


def _mix_prompt(z, zt, p, bsz, seq, lb_rows, nseq):
    nlb = seq // lb_rows
    n_chunks = lb_rows // CHUNK
    row = lambda b, l: (b, l, 0)
    const2 = lambda b, l: (0, 0)
    st4 = lambda b, l: (b, 0, 0, 0)
    st3 = lambda b, l: (b, 0, 0)
    params = [p["bcol"], p["acol"], p["brow"], p["arow"], p["mnorm"], p["gnorm"], p["lnorm"],
              p["gcw"], p["scw"], p["wg"], p["bg"]] + _mix_consts()
    in_specs = [pl.BlockSpec((nseq, lb_rows, NP), row),
                pl.BlockSpec((nseq, 1, N_ROW_KINDS, n_chunks, W_GROUP), lambda b, l: (b, l, 0, 0, 0))]
    in_specs += [pl.BlockSpec(a.shape, const2) for a in params]
    hd = (N_HEADS, HEAD_DIM, HEAD_DIM)
    out_shape = [jax.ShapeDtypeStruct((bsz, seq, D_MODEL), BF16),
                 jax.ShapeDtypeStruct((bsz,) + hd, F32),
                 jax.ShapeDtypeStruct((bsz, N_HEADS, HEAD_DIM), F32),
                 jax.ShapeDtypeStruct((bsz, 1, N_HEADS), F32),
                 jax.ShapeDtypeStruct((bsz,) + hd, F32),
                 jax.ShapeDtypeStruct((bsz, 3, 3 * W_GROUP), F32),
                 jax.ShapeDtypeStruct((bsz, 2, W_GROUP), F32),
                 jax.ShapeDtypeStruct((bsz,) + hd, F32)]
    out_specs = [pl.BlockSpec((nseq, lb_rows, D_MODEL), row),
                 pl.BlockSpec((nseq,) + hd, st4),
                 pl.BlockSpec((nseq, N_HEADS, HEAD_DIM), st3),
                 pl.BlockSpec((nseq, 1, N_HEADS), st3),
                 pl.BlockSpec((nseq,) + hd, st4),
                 pl.BlockSpec((nseq, 3, 3 * W_GROUP), st3),
                 pl.BlockSpec((nseq, 2, W_GROUP), st3),
                 pl.BlockSpec((nseq,) + hd, st4)]
    vm = lambda *shape: pltpu.VMEM((nseq,) + shape, F32)
    scratch = [vm(HEAD_DIM, W_GROUP),
               vm(8, W_GROUP),
               vm(8, W_GROUP),
               vm(HEAD_DIM, W_GROUP),
               vm(HEAD_DIM, W_GROUP),
               vm(lb_rows + 8, 3 * W_GROUP),
               vm(lb_rows + 8, W_GROUP),
               vm(lb_rows, 3 * W_GROUP),
               vm(lb_rows, W_GROUP),
               vm(lb_rows, W_GROUP),
               vm(lb_rows, 5 * W_GROUP),
               vm(n_chunks, 8, W_GROUP)]
    return pl.pallas_call(
        functools.partial(_mix_kernel, lb_rows=lb_rows, nseq=nseq),
        grid=(bsz // nseq, nlb),
        in_specs=in_specs,
        out_specs=out_specs,
        out_shape=out_shape,
        scratch_shapes=scratch,
        compiler_params=_cparams(("parallel", "arbitrary")),
        name="mix_prompt",
    )(z, zt, *params)


def _inproj_s_kernel(x_ref, g_ref, wt_ref, zt_ref):
    w = wt_ref[...]
    n = w.shape[0]
    w1, w2 = _split2(w)
    h1, h2 = _split2(_rms(x_ref[...], g_ref[...]))
    r = _dot_nt(jnp.concatenate([w1, w2], axis=0), h1)
    zt_ref[...] = r[0:n] + r[n:] + _dot_nt(w1, h2)


def _inproj_s(x, g, wt_all, layer, n_blk=4):
    bn = x.shape[0]
    d_in = wt_all.shape[1]
    tn = d_in // n_blk
    return pl.pallas_call(
        _inproj_s_kernel,
        grid=(n_blk,),
        in_specs=[pl.BlockSpec((bn, D_MODEL), lambda j: (0, 0)),
                  pl.BlockSpec((1, D_MODEL), lambda j: (0, 0)),
                  pl.BlockSpec((None, tn, D_MODEL), lambda j: (layer, j, 0))],
        out_specs=pl.BlockSpec((tn, bn), lambda j: (j, 0)),
        out_shape=jax.ShapeDtypeStruct((d_in, bn), F32),
        compiler_params=_cparams(("parallel",)),
        name="inproj_s",
    )(x, g, wt_all)


def _spre_kernel(zt_ref, gbuf_ref, sbuf_ref, bcol_ref, acol_ref, gcw_ref, scw_ref, wg_ref, bg_ref,
                 qkvt_ref, glogt_ref, actt_ref, yc_ref, gbuf_out, sbuf_out):
    rows_t = lambda col, n: zt_ref[col:col + n, :].T
    small = rows_t(SM, LANES)
    lane = _iota2((1, LANES), 1)
    actt_ref[...] = _gate_act(small + bcol_ref[...], lane, -jnp.exp(acol_ref[...])).T
    glin = _mm3(small, wg_ref[...]) + bg_ref[...]
    glogt_ref[...] = (_logsigmoid(glin) * (1.0 / GLA_NORMALIZER)).T

    wq = 3 * W_GROUP
    u = rows_t(GQ, wq)
    b0, b1, b2 = gbuf_ref[:, 0:wq], gbuf_ref[:, wq:2 * wq], gbuf_ref[:, 2 * wq:]
    conv = b0 * gcw_ref[0:1, :] + b1 * gcw_ref[1:2, :] + b2 * gcw_ref[2:3, :] + u * gcw_ref[3:4, :]
    conv = _silu(conv)
    gbuf_out[:, 0:wq] = b1
    gbuf_out[:, wq:2 * wq] = b2
    gbuf_out[:, 2 * wq:] = u
    hsum = (_iota2((W_GROUP, W_GROUP), 0) // HEAD_DIM ==
            _iota2((W_GROUP, W_GROUP), 1) // HEAD_DIM).astype(BF16)

    def l2n(x):
        return x * lax.rsqrt(_sel_r(x * x, hsum) + EPS)

    qkvt_ref[0:W_GROUP, :] = (l2n(conv[:, 0:W_GROUP]) * Q_SCALE).T
    qkvt_ref[W_GROUP:2 * W_GROUP, :] = l2n(conv[:, W_GROUP:2 * W_GROUP]).T
    qkvt_ref[2 * W_GROUP:, :] = conv[:, 2 * W_GROUP:].T

    u2 = rows_t(SC, W_GROUP) * rows_t(SH, W_GROUP)
    s0, s1 = sbuf_ref[:, 0:W_GROUP], sbuf_ref[:, W_GROUP:]
    cu = s0 * scw_ref[0:1, :] + s1 * scw_ref[1:2, :] + u2 * scw_ref[2:3, :]
    yc_ref[...] = rows_t(SB, W_GROUP) * cu
    sbuf_out[:, 0:W_GROUP] = s1
    sbuf_out[:, W_GROUP:] = u2


def _sample_pre(zt, gbuf, sbuf, p):
    bn = zt.shape[1]
    sd = lambda s: jax.ShapeDtypeStruct(s, F32)
    return pl.pallas_call(
        _spre_kernel,
        out_shape=[sd((3 * W_GROUP, bn)), sd((W_GROUP, bn)), sd((LANES, bn)),
                   sd((bn, W_GROUP)), sd(gbuf.shape), sd(sbuf.shape)],
        compiler_params=pltpu.CompilerParams(vmem_limit_bytes=VMEM_LIMIT),
        name="sample_pre",
    )(zt, gbuf, sbuf, p["bcol"], p["acol"], p["gcw"], p["scw"], p["wg32"], p["bg"])


def _srec_kernel(zt_ref, qkvt_ref, glogt_ref, actt_ref, c_ref, n_ref, m_ref, sg_ref, sl_ref, norm_ref,
                 y_ref, c_out, n_out, m_out, sg_out, sl_out):
    h = pl.program_id(0)
    feat = lambda ref, col: ref[pl.ds(pl.multiple_of(col + h * HEAD_DIM, HEAD_DIM), HEAD_DIM), :]
    gate = lambda kind: actt_ref[pl.ds(4 * kind + h, 1), :]
    colsum = lambda x: jnp.sum(x, axis=0, keepdims=True)
    rows = range(HEAD_DIM)

    def headnorm(o, w):
        return o * lax.rsqrt(jnp.mean(o * o, axis=0, keepdims=True) + EPS) * w

    def contract(vecs, s_ref):
        accs = [v[0:1, :] * s_ref[0] for v in vecs]
        for d in rows[1:]:
            sd_ = s_ref[d]
            accs = [a + v[d:d + 1, :] * sd_ for a, v in zip(accs, vecs)]
        return accs

    mq, mk, mv, mo = feat(zt_ref, MQ), feat(zt_ref, MK) * Q_SCALE, feat(zt_ref, MV), feat(zt_ref, MO)
    ic, lf = gate(0), gate(1)
    nvec = n_ref[...]
    m_old = m_ref[pl.ds(h, 1), :]
    a = lf + m_old
    mt = jnp.maximum(a, ic)
    w_int = jnp.exp(a - mt)
    e_i = jnp.exp(ic - mt)
    s = colsum(mq * mk) * e_i
    (qc,) = contract([mq], c_ref)
    num = w_int * qc + s * mv
    den = w_int * colsum(mq * nvec) + s
    hh = num / jnp.maximum(jnp.abs(den), jnp.exp(-mt))
    kw = e_i * mk
    for d in rows:
        c_out[d] = w_int * c_ref[d] + kw[d:d + 1, :] * mv
    n_out[...] = w_int * nvec + kw
    m_out[...] = mt
    y_ref[0:HEAD_DIM, :] = _sigmoid(mo) * headnorm(hh, norm_ref[0])

    gq, gk, gv = feat(qkvt_ref, 0), feat(qkvt_ref, W_GROUP), feat(qkvt_ref, 2 * W_GROUP)
    beta = gate(3)
    eg = jnp.exp(gate(2))
    ks, qs = contract([gk, gq], sg_ref)
    vn = beta * gv - (beta * eg) * ks
    o = eg * qs + colsum(gq * gk) * vn
    for d in rows:
        sg_out[d] = eg * sg_ref[d] + gk[d:d + 1, :] * vn
    y_ref[HEAD_DIM:2 * HEAD_DIM, :] = headnorm(o, norm_ref[1]) * _silu(feat(zt_ref, GZ))

    lq, lk, lv = feat(zt_ref, LQ) * Q_SCALE, feat(zt_ref, LK), feat(zt_ref, LV)
    egk = jnp.exp(feat(glogt_ref, 0))
    (ql,) = contract([lq * egk], sl_ref)
    o = ql + colsum(lq * lk) * lv
    for d in rows:
        sl_out[d] = egk[d:d + 1, :] * sl_ref[d] + lk[d:d + 1, :] * lv
    y_ref[2 * HEAD_DIM:, :] = headnorm(o, norm_ref[2]) * _silu(feat(zt_ref, LR))


def _sample_rec(layer, zt, qkvt, glogt, actt, cst, nst, mst, sgst, slst, norms):
    bn = zt.shape[1]
    mat_in = pl.BlockSpec((None, None, HEAD_DIM, HEAD_DIM, bn), lambda h: (layer, h, 0, 0, 0))
    mat_out = pl.BlockSpec((None, HEAD_DIM, HEAD_DIM, bn), lambda h: (h, 0, 0, 0))
    full = lambda a: pl.BlockSpec(a.shape, lambda h: (0,) * a.ndim)
    sd = lambda *s: jax.ShapeDtypeStruct(s, F32)
    return pl.pallas_call(
        _srec_kernel,
        grid=(N_HEADS,),
        in_specs=[full(zt), full(qkvt), full(glogt), full(actt), mat_in,
                  pl.BlockSpec((None, None, HEAD_DIM, bn), lambda h: (layer, h, 0, 0)),
                  pl.BlockSpec((None, N_HEADS, bn), lambda h: (layer, 0, 0)),
                  mat_in, mat_in,
                  pl.BlockSpec((3, HEAD_DIM, bn), lambda h: (0, h, 0))],
        out_specs=[pl.BlockSpec((None, 3 * HEAD_DIM, bn), lambda h: (h, 0, 0)), mat_out,
                   pl.BlockSpec((None, HEAD_DIM, bn), lambda h: (h, 0, 0)),
                   pl.BlockSpec((None, 1, bn), lambda h: (h, 0, 0)), mat_out, mat_out],
        out_shape=[sd(N_HEADS, 3 * HEAD_DIM, bn), sd(N_HEADS, HEAD_DIM, HEAD_DIM, bn),
                   sd(N_HEADS, HEAD_DIM, bn), sd(N_HEADS, 1, bn),
                   sd(N_HEADS, HEAD_DIM, HEAD_DIM, bn), sd(N_HEADS, HEAD_DIM, HEAD_DIM, bn)],
        compiler_params=_cparams(("parallel",)),
        name="sample_rec",
    )(zt, qkvt, glogt, actt, cst, nst, mst, sgst, slst, norms)


def _outproj_s_kernel(y_ref, w_ref, x_ref, o_ref):
    o_ref[...] = x_ref[...] + _mm3(y_ref[...], w_ref[...])


def _outproj_s(y, w_all, layer, x):
    full = lambda a: pl.BlockSpec(a.shape, lambda i: (0, 0))
    return pl.pallas_call(
        _outproj_s_kernel,
        grid=(1,),
        in_specs=[full(y), pl.BlockSpec((None, D_MODEL, D_MODEL), lambda i: (layer, 0, 0)), full(x)],
        out_specs=full(x),
        out_shape=jax.ShapeDtypeStruct(x.shape, F32),
        compiler_params=_cparams(("arbitrary",)),
        name="outproj_s",
    )(y, w_all, x)


def _ffn_s_kernel(x_ref, g_ref, wg_ref, wu_ref, wd_ref, gf_ref, o_ref, h_ref, acc_ref, *, final_norm):
    j = pl.program_id(0)

    @pl.when(j == 0)
    def _():
        h_ref[...] = _rms(x_ref[...], g_ref[...])
        acc_ref[...] = jnp.zeros_like(acc_ref)

    h = h_ref[...]
    acc_ref[...] += _mm3(_silu(_mm3(h, wg_ref[...])) * _mm3(h, wu_ref[...]), wd_ref[...])

    @pl.when(j == pl.num_programs(0) - 1)
    def _():
        out = x_ref[...] + acc_ref[...]
        o_ref[...] = _rms(out, gf_ref[...]) if final_norm else out


def _ffn_s(x, g, wg, wu, wd, gf, tf, final_norm):
    t = x.shape[0]
    dff = wg.shape[1]
    full = lambda j: (0, 0)
    return pl.pallas_call(
        functools.partial(_ffn_s_kernel, final_norm=final_norm),
        grid=(dff // tf,),
        in_specs=[pl.BlockSpec((t, D_MODEL), full),
                  pl.BlockSpec((1, D_MODEL), full),
                  pl.BlockSpec((D_MODEL, tf), lambda j: (0, j)),
                  pl.BlockSpec((D_MODEL, tf), lambda j: (0, j)),
                  pl.BlockSpec((tf, D_MODEL), lambda j: (j, 0)),
                  pl.BlockSpec((1, D_MODEL), full)],
        out_specs=pl.BlockSpec((t, D_MODEL), full),
        out_shape=jax.ShapeDtypeStruct((t, D_MODEL), F32),
        scratch_shapes=[pltpu.VMEM((t, D_MODEL), F32), pltpu.VMEM((t, D_MODEL), F32)],
        compiler_params=_cparams(("arbitrary",)),
        name="ffn_s",
    )(x, g, wg, wu, wd, gf)


def _outproj_kernel(y_ref, w_ref, x_ref, o_ref):
    o_ref[...] = x_ref[...] + _dot(y_ref[...], w_ref[...].astype(BF16))


def _outproj(y, w_all, layer, x, tm):
    t = x.shape[0]
    return pl.pallas_call(
        _outproj_kernel,
        grid=(t // tm,),
        in_specs=[pl.BlockSpec((tm, D_MODEL), lambda i: (i, 0)),
                  pl.BlockSpec((None, D_MODEL, D_MODEL), lambda i: (layer, 0, 0)),
                  pl.BlockSpec((tm, D_MODEL), lambda i: (i, 0))],
        out_specs=pl.BlockSpec((tm, D_MODEL), lambda i: (i, 0)),
        out_shape=jax.ShapeDtypeStruct((t, D_MODEL), F32),
        compiler_params=_cparams(("parallel",)),
        name="outproj",
    )(y, w_all, x)


def _ffn_kernel(x_ref, y_ref, wo_ref, g_ref, wg_ref, wu_ref, wd_ref, gf_ref, o_ref,
                h_ref, acc_ref, xn_ref, *, final_norm):
    j = pl.program_id(1)

    @pl.when(j == 0)
    def _():
        xn = x_ref[...] + _dot(y_ref[...], wo_ref[...].astype(BF16))
        xn_ref[...] = xn
        h_ref[...] = _rms(xn, g_ref[...]).astype(BF16)
        acc_ref[...] = jnp.zeros_like(acc_ref)

    h = h_ref[...]
    act = _silu(_dot(h, wg_ref[...].astype(BF16))) * _dot(h, wu_ref[...].astype(BF16))
    acc_ref[...] += _dot(act.astype(BF16), wd_ref[...].astype(BF16))

    @pl.when(j == pl.num_programs(1) - 1)
    def _():
        out = xn_ref[...] + acc_ref[...]
        o_ref[...] = _rms(out, gf_ref[...]) if final_norm else out


def _ffn(x, y, wo_all, layer, g, wg_all, wu_all, wd_all, dense_idx, gf, tm, tf, final_norm):
    t = x.shape[0]
    dff = wg_all.shape[2]
    return pl.pallas_call(
        functools.partial(_ffn_kernel, final_norm=final_norm),
        grid=(t // tm, dff // tf),
        in_specs=[pl.BlockSpec((tm, D_MODEL), lambda i, j: (i, 0)),
                  pl.BlockSpec((tm, D_MODEL), lambda i, j: (i, 0)),
                  pl.BlockSpec((None, D_MODEL, D_MODEL), lambda i, j: (layer, 0, 0)),
                  pl.BlockSpec((1, D_MODEL), lambda i, j: (0, 0)),
                  pl.BlockSpec((None, D_MODEL, tf), lambda i, j: (dense_idx, 0, j)),
                  pl.BlockSpec((None, D_MODEL, tf), lambda i, j: (dense_idx, 0, j)),
                  pl.BlockSpec((None, tf, D_MODEL), lambda i, j: (dense_idx, j, 0)),
                  pl.BlockSpec((1, D_MODEL), lambda i, j: (0, 0))],
        out_specs=pl.BlockSpec((tm, D_MODEL), lambda i, j: (i, 0)),
        out_shape=jax.ShapeDtypeStruct((t, D_MODEL), F32),
        scratch_shapes=[pltpu.VMEM((tm, D_MODEL), BF16), pltpu.VMEM((tm, D_MODEL), F32),
                        pltpu.VMEM((tm, D_MODEL), F32)],
        compiler_params=_cparams(("parallel", "arbitrary")),
        name="ffn",
    )(x, y, wo_all, g, wg_all, wu_all, wd_all, gf)


def _top2_gates(logits):
    return _top2_select(logits)[0]


def _top2_select(logits):
    lane = _iota2((1, LANES), 1)
    valid = lane < N_EXPERTS
    logits = jnp.where(valid, logits, -jnp.inf)
    ex = jnp.exp(logits - jnp.max(logits, axis=-1, keepdims=True))
    probs = ex / jnp.sum(ex, axis=-1, keepdims=True)
    v1 = jnp.max(probs, axis=-1, keepdims=True)
    i1 = jnp.min(jnp.where(probs == v1, lane, LANES), axis=-1, keepdims=True)
    rest = jnp.where((lane == i1) | jnp.logical_not(valid), -1.0, probs)
    v2 = jnp.max(rest, axis=-1, keepdims=True)
    i2 = jnp.min(jnp.where(rest == v2, lane, LANES), axis=-1, keepdims=True)
    tot = v1 + v2
    gates = jnp.where(lane == i1, v1 / tot, 0.0) + jnp.where(lane == i2, v2 / tot, 0.0)
    return gates, ((lane == i1) | (lane == i2)).astype(F32)


def _moe_kernel(x_ref, g_ref, wr_ref, br_ref, wg_ref, wu_ref, wd_ref, gf_ref, o_ref,
                h_ref, acc_ref, gates_ref, *, final_norm, precise_router):
    e = pl.program_id(1)

    @pl.when(e == 0)
    def _():
        h = _rms(x_ref[...], g_ref[...])
        h_ref[...] = h.astype(BF16)
        acc_ref[...] = jnp.zeros_like(acc_ref)
        if precise_router:
            logits = _mm3(h, wr_ref[...])
        else:
            logits = _dot(h.astype(BF16), wr_ref[...].astype(BF16))
        gates_ref[...] = _top2_gates(logits + br_ref[...])

    h = h_ref[...]
    y = _dot((_silu(_dot(h, wg_ref[0])) * _dot(h, wu_ref[0])).astype(BF16), wd_ref[0])
    lane = _iota2((1, LANES), 1)
    ge = jnp.sum(jnp.where(lane == e, gates_ref[...], 0.0), axis=-1, keepdims=True)
    acc_ref[...] += ge * y

    @pl.when(e == pl.num_programs(1) - 1)
    def _():
        out = x_ref[...] + acc_ref[...]
        o_ref[...] = _rms(out, gf_ref[...]) if final_norm else out


def _moe(x, g, wr, br, wg, wu, wd, gf, tm, final_norm, precise_router):
    t = x.shape[0]
    fe = wg.shape[2]
    return pl.pallas_call(
        functools.partial(_moe_kernel, final_norm=final_norm, precise_router=precise_router),
        grid=(t // tm, N_EXPERTS),
        in_specs=[pl.BlockSpec((tm, D_MODEL), lambda i, e: (i, 0)),
                  pl.BlockSpec((1, D_MODEL), lambda i, e: (0, 0)),
                  pl.BlockSpec((D_MODEL, LANES), lambda i, e: (0, 0)),
                  pl.BlockSpec((1, LANES), lambda i, e: (0, 0)),
                  pl.BlockSpec((1, D_MODEL, fe), lambda i, e: (e, 0, 0)),
                  pl.BlockSpec((1, D_MODEL, fe), lambda i, e: (e, 0, 0)),
                  pl.BlockSpec((1, fe, D_MODEL), lambda i, e: (e, 0, 0)),
                  pl.BlockSpec((1, D_MODEL), lambda i, e: (0, 0))],
        out_specs=pl.BlockSpec((tm, D_MODEL), lambda i, e: (i, 0)),
        out_shape=jax.ShapeDtypeStruct((t, D_MODEL), F32),
        scratch_shapes=[pltpu.VMEM((tm, D_MODEL), BF16), pltpu.VMEM((tm, D_MODEL), F32),
                        pltpu.VMEM((tm, LANES), F32)],
        compiler_params=_cparams(("parallel", "arbitrary")),
        name="moe",
    )(x, g, wr, br, wg, wu, wd, gf)


def _moe_win_kernel(x_ref, y_ref, wo_ref, g_ref, wr_ref, br_ref, tril_ref, wg_ref, wu_ref, wd_ref, gf_ref, o_ref,
                    h_ref, gate_ref, key_ref, keyt_ref, xg_ref, ys_ref, *, final_norm, cap, fewer, tc):
    e = pl.program_id(1)
    w_rows = x_ref.shape[0]
    lane = _iota2((1, LANES), 1)

    @pl.when(e == 0)
    def _():
        x = x_ref[...] + _dot(y_ref[...], wo_ref[...])
        hb = _rms(x, g_ref[...]).astype(BF16)
        h_ref[...] = hb
        o_ref[...] = x
        gates, sel = _top2_select(_dot(hb, wr_ref[...].astype(BF16)) + br_ref[...])
        rank = _dot(tril_ref[...], sel.astype(BF16))
        key = jnp.where(sel > 0.0, rank, -1.0)
        gate_ref[...] = gates
        key_ref[...] = key
        keyt_ref[...] = key.T

    key_r = keyt_ref[pl.ds(e, 1), :]
    pick_e = (_iota2((LANES, LANES), 0) == e).astype(BF16)
    key_c = _sel_r(key_ref[...], pick_e)
    gate_c = _sel_r(gate_ref[...], pick_e)
    gate_c = jnp.concatenate([gate_c] * (tc // LANES), axis=1)

    def expert(xg):
        act = (_silu(_dot(xg, wg_ref[0])) * _dot(xg, wu_ref[0])).astype(BF16)
        return _dot(act, wd_ref[0])

    n_tc = w_rows // tc
    win = LANES
    chunk_cnt = [jnp.sum((key_r[:, c * tc:(c + 1) * tc] >= 0.0).astype(F32), axis=-1,
                         keepdims=True)[0, 0].astype(jnp.int32) for c in range(n_tc)]
    starts, count = [], jnp.int32(0)
    for c in range(n_tc):
        starts.append(count)
        count = count + chunk_cnt[c]
    windowed = (count > 0) & (count <= cap)
    for c in range(n_tc):
        windowed = windowed & (chunk_cnt[c] <= win - 8)

    @pl.when(windowed)
    def _():
        xg_ref[...] = jnp.zeros_like(xg_ref)
        w_row = _iota2((win, tc), 0)
        w_col = _iota2((tc, 2 * win), 1) % win
        lo = [pl.multiple_of((starts[c] // 8) * 8, 8) for c in range(n_tc)]
        for c in range(n_tc):
            ts = slice(c * tc, (c + 1) * tc)
            gather = jnp.where((w_row + lo[c]).astype(F32) == key_r[:, ts], 1.0, 0.0).astype(BF16)
            xg_ref[pl.ds(lo[c], win), :] += _dot(gather, h_ref[ts, :])
        def run_expert(rows):
            ys_ref[0:rows, :] = expert(xg_ref[0:rows, :].astype(BF16))
            ys_ref[rows:, :] = jnp.zeros((cap + win - rows, D_MODEL), F32)

        if fewer < cap:
            pl.when(count <= fewer)(lambda: run_expert(fewer))
            pl.when(count > fewer)(lambda: run_expert(cap))
        else:
            run_expert(cap)
        for c in range(n_tc):
            ts = slice(c * tc, (c + 1) * tc)
            y12 = jnp.concatenate(_split2(ys_ref[pl.ds(lo[c], win), :]), axis=0)
            kc = jnp.concatenate([key_c[ts, :]] * (2 * win // LANES), axis=1)
            scatter = jnp.where((w_col + lo[c]).astype(F32) == kc, 1.0, 0.0).astype(BF16)
            gc = jnp.concatenate([gate_c[ts, :]] * (D_MODEL // tc), axis=1)
            o_ref[ts, :] += gc * _dot(scatter, y12)

    @pl.when(jnp.logical_not(windowed) & (count > 0))
    def _():
        key_w = jnp.concatenate([key_c] * pl.cdiv(2 * cap, LANES), axis=1)[:, :2 * cap]
        row_id = _iota2((cap, w_rows), 0).astype(F32)
        col_id = (_iota2((w_rows, 2 * cap), 1) % cap).astype(F32)

        def trip(j, carry):
            base = (j * cap).astype(F32)
            gather = jnp.where(row_id + base == key_r, 1.0, 0.0).astype(BF16)
            y = expert(_dot(gather, h_ref[...]).astype(BF16))
            y12 = jnp.concatenate(_split2(y), axis=0)
            scatter = jnp.where(col_id + base == key_w, 1.0, 0.0).astype(BF16)
            for c0 in range(0, D_MODEL, tc):
                cs = slice(c0, c0 + tc)
                o_ref[:, cs] += gate_c * _dot(scatter, y12[:, cs])
            return carry

        lax.fori_loop(0, (count + cap - 1) // cap, trip, 0)

    if final_norm:
        @pl.when(e == pl.num_programs(1) - 1)
        def _():
            o_ref[...] = _rms(o_ref[...], gf_ref[...])


def _moe_win(x, y, wo, g, wr, br, wg, wu, wd, gf, w_rows, final_norm):
    t = x.shape[0]
    fe = wg.shape[2]
    sigma = (w_rows * 3 / 16) ** 0.5
    cap = max(16, int(w_rows / 4 + 3.5 * sigma) // 16 * 16)
    fewer = max(16, int(w_rows / 4 + 1.2 * sigma) // 16 * 16)
    tc = min(w_rows, 256)
    tril = jnp.asarray(np.tril(np.ones((w_rows, w_rows), np.float32), -1), dtype=BF16)
    const = lambda i, e: (0, 0)
    return pl.pallas_call(
        functools.partial(_moe_win_kernel, final_norm=final_norm, cap=cap, fewer=fewer, tc=tc),
        grid=(t // w_rows, N_EXPERTS),
        in_specs=[pl.BlockSpec((w_rows, D_MODEL), lambda i, e: (i, 0)),
                  pl.BlockSpec((w_rows, D_MODEL), lambda i, e: (i, 0)),
                  pl.BlockSpec((D_MODEL, D_MODEL), const),
                  pl.BlockSpec((1, D_MODEL), const),
                  pl.BlockSpec((D_MODEL, LANES), const),
                  pl.BlockSpec((1, LANES), const),
                  pl.BlockSpec((w_rows, w_rows), const),
                  pl.BlockSpec((1, D_MODEL, fe), lambda i, e: (e, 0, 0)),
                  pl.BlockSpec((1, D_MODEL, fe), lambda i, e: (e, 0, 0)),
                  pl.BlockSpec((1, fe, D_MODEL), lambda i, e: (e, 0, 0)),
                  pl.BlockSpec((1, D_MODEL), const)],
        out_specs=pl.BlockSpec((w_rows, D_MODEL), lambda i, e: (i, 0)),
        out_shape=jax.ShapeDtypeStruct((t, D_MODEL), F32),
        scratch_shapes=[pltpu.VMEM((w_rows, D_MODEL), BF16),
                        pltpu.VMEM((w_rows, LANES), F32),
                        pltpu.VMEM((w_rows, LANES), F32), pltpu.VMEM((LANES, w_rows), F32),
                        pltpu.VMEM((cap + LANES, D_MODEL), F32),
                        pltpu.VMEM((cap + LANES, D_MODEL), F32)],
        compiler_params=_cparams(("parallel", "arbitrary")),
        name="moe_win",
    )(x, y, wo, g, wr, br, tril, wg, wu, wd, gf)


def _prep_w_in(wt):
    seg = lambda i: wt[_OFF[i]:_OFF[i + 1], :]
    pad = jnp.zeros((LANES - N_SMALL, wt.shape[1]), wt.dtype)
    wfull = jnp.concatenate([seg(i) for i in _BIG] + [seg(i) for i in _SMALL] + [pad], axis=0)
    gates = jnp.concatenate([seg(i) for i in _GATES], axis=0)
    return wfull.astype(BF16), gates.astype(BF16)


def _regroup_zt(zt):
    seg = lambda i: zt[_OFF[i]:_OFF[i + 1], :]
    pad = jnp.zeros((LANES - N_SMALL, zt.shape[1]), zt.dtype)
    return jnp.concatenate([seg(i) for i in _BIG] + [seg(i) for i in _SMALL] + [pad], axis=0)


def _pad_lanes(v, start):
    out = jnp.zeros((1, LANES), F32)
    return lax.dynamic_update_slice(out, v.reshape(1, -1).astype(F32), (0, start))


def _layer_params(l, mlstm_b_i, mlstm_b_f, mlstm_norm, gdn_conv_w, gdn_a_log, gdn_dt_bias, gdn_norm,
                  sc_conv_w, gla_w_gate, gla_b_gate, gla_norm):
    bias = jnp.concatenate([mlstm_b_i[l], mlstm_b_f[l], gdn_dt_bias[l]]).astype(F32)
    alog = gdn_a_log[l].astype(F32)
    per_lane = lambda v: jnp.repeat(v, HEAD_DIM)
    zero = jnp.zeros((W_GROUP,), F32)
    brow = jnp.stack([per_lane(mlstm_b_i[l]), per_lane(mlstm_b_f[l]), per_lane(gdn_dt_bias[l]), zero])
    arow = jnp.stack([zero, zero, per_lane(alog), zero])
    wg32 = jnp.zeros((LANES, W_GROUP), F32).at[16:16 + GLA_RANK].set(gla_w_gate[l])
    return dict(bcol=_pad_lanes(bias, 0), acol=_pad_lanes(alog, 8), brow=brow, arow=arow,
                mnorm=mlstm_norm[l].reshape(1, -1), gnorm=gdn_norm[l].reshape(1, -1),
                lnorm=gla_norm[l].reshape(1, -1),
                gcw=gdn_conv_w[l].T, scw=sc_conv_w[l].T, wg=wg32.astype(BF16), wg32=wg32,
                bg=gla_b_gate[l].reshape(1, -1))


def _pick_tile(n, pref):
    for c in pref:
        if n % c == 0:
            return c
    return n


def kernel(x_prompt, x_sample, state_mlstm_C, state_mlstm_n, state_mlstm_m, state_gdn_S, state_gdn_conv,
           state_sc_conv, state_gla_S, w_in, g_mix, mlstm_b_i, mlstm_b_f, mlstm_norm, gdn_conv_w, gdn_a_log,
           gdn_dt_bias, gdn_norm, sc_conv_w, gla_w_gate, gla_b_gate, gla_norm, w_out, g_ffn, ffn_w_gate,
           ffn_w_up, ffn_w_down, moe_w_router, moe_b_router, moe_w_gate, moe_w_up, moe_w_down, g_final):
    depth = w_in.shape[0]
    bsz, seq, _ = x_prompt.shape
    bn = x_sample.shape[0]
    tp = bsz * seq
    assert x_sample.shape[1] == 1 and seq % CHUNK == 0

    xp = x_prompt.reshape(tp, D_MODEL)
    xs = x_sample.reshape(bn, D_MODEL)
    lb_rows = _pick_tile(seq, (128, 64))
    tm_p = _pick_tile(tp, (512, 256, 128, 64))
    tm_f = _pick_tile(tp, (1024, 512, 256, 128, 64))
    tm_e = _pick_tile(tp, (1024, 512, 256, 128))
    nseq = _pick_tile(bsz, (4, 2, 1))

    s_minor = (state_mlstm_C.transpose(0, 2, 3, 4, 1), state_mlstm_n.transpose(0, 2, 3, 1),
               state_mlstm_m.transpose(0, 2, 1), state_gdn_S.transpose(0, 2, 3, 4, 1),
               state_gla_S.transpose(0, 2, 3, 4, 1))
    w_in_t = w_in.transpose(0, 2, 1)
    gfin = g_final.reshape(1, -1)

    p_states = [[] for _ in range(7)]
    s_states = [[] for _ in range(7)]
    for l in range(depth):
        p = _layer_params(l, mlstm_b_i, mlstm_b_f, mlstm_norm, gdn_conv_w, gdn_a_log, gdn_dt_bias,
                          gdn_norm, sc_conv_w, gla_w_gate, gla_b_gate, gla_norm)
        wfull, wst = _prep_w_in(w_in_t[l])
        gm = g_mix[l].reshape(1, -1)
        gf = g_ffn[l].reshape(1, -1)
        last = l == depth - 1

        z, zt = _inproj(xp, gm, wfull, wst, tm_p, lb_rows)
        y, c1, n1, m1, sg1, gb1, sb1, sl1 = _mix_prompt(
            z.reshape(bsz, seq, NP), zt.reshape((bsz, seq // lb_rows) + zt.shape[1:]), p,
            bsz, seq, lb_rows, nseq)
        y = y.reshape(tp, D_MODEL)
        for lst, v in zip(p_states, (c1, n1, m1.reshape(bsz, N_HEADS), sg1, gb1, sb1, sl1)):
            lst.append(v)

        zt_s = _regroup_zt(_inproj_s(xs, gm, w_in_t, l))
        gbuf = state_gdn_conv[l].reshape(bn, -1)
        sbuf = state_sc_conv[l].reshape(bn, -1)
        qkvt, glogt, actt, yc, gbuf1, sbuf1 = _sample_pre(zt_s, gbuf, sbuf, p)
        norms = jnp.stack([jnp.broadcast_to(v[:, None], (W_GROUP, bn))
                           for v in (mlstm_norm[l], gdn_norm[l], gla_norm[l])])
        ysr, c2, n2, m2, sg2, sl2 = _sample_rec(l, zt_s, qkvt, glogt, actt, *s_minor, norms)
        yt = ysr.reshape(N_HEADS, 3, HEAD_DIM, bn).transpose(3, 1, 0, 2).reshape(bn, 3, W_GROUP)
        ymix = jnp.concatenate([yt[:, 0], yt[:, 1], yc, yt[:, 2]], axis=-1)
        xs = _outproj_s(ymix, w_out, l, xs)
        for lst, v in zip(s_states, (c2, n2, m2[:, 0], sg2, gbuf1.reshape(bn, 3, -1),
                                     sbuf1.reshape(bn, 2, -1), sl2)):
            lst.append(v)

        j = l // 2
        if l % 2 == 0:
            tf = _pick_tile(ffn_w_gate.shape[2], (256, 128))
            xp = _ffn(xp, y, w_out, l, gf, ffn_w_gate, ffn_w_up, ffn_w_down, j, gfin, tm_f, tf, last)
            xs = _ffn_s(xs, gf, ffn_w_gate[j], ffn_w_up[j], ffn_w_down[j], gfin, tf, last)
        else:
            wr = jnp.zeros((D_MODEL, LANES), F32).at[:, :N_EXPERTS].set(moe_w_router[j])
            br = _pad_lanes(moe_b_router[j], 0)
            wgt, wup, wdn = (moe_w_gate[j].astype(BF16), moe_w_up[j].astype(BF16),
                             moe_w_down[j].astype(BF16))
            xp = _moe_win(xp, y, w_out[l].astype(BF16), gf, wr, br, wgt, wup, wdn, gfin, tm_e, last)
            xs = _moe(xs, gf, wr, br, wgt, wup, wdn, gfin, bn, last, True)

    y_prompt = xp.reshape(bsz, seq, D_MODEL)
    y_sample = xs.reshape(bn, 1, D_MODEL)
    sp = [jnp.stack(v) for v in p_states]
    ss = [jnp.stack(v) for v in s_states]
    for i, perm in ((0, (0, 4, 1, 2, 3)), (1, (0, 3, 1, 2)), (2, (0, 2, 1)), (3, (0, 4, 1, 2, 3)),
                    (6, (0, 4, 1, 2, 3))):
        ss[i] = ss[i].transpose(perm)
    return (y_prompt, y_sample, sp[0], ss[0], sp[1], ss[1], sp[2], ss[2], sp[3], ss[3],
            sp[4], ss[4], sp[5], ss[5], sp[6], ss[6])
```
